```python
import jax, jax.numpy as jnp
from jax import lax
import numpy as np

D_MODEL = 1024
BATCH = 8
SEQ = 8192
DEPTH = 2

CHUNK = 64
N_MIXERS = 2
N_CONV_LAYERS = (DEPTH + 1) // 2
N_MLA_LAYERS = DEPTH // 2
D_FF = 2816
FFN_RES_WEIGHT = 0.5
CONV_WIDTH = 31
N_HEADS = 8
QK_NOPE = 128
QK_ROPE = 64
V_HEAD = 128
Q_LORA = 512
KV_LORA = 256
ROPE_THETA = 10000.0
Q_BLOCK = 128
RMS_EPS = 1e-6
POS_OFFSET_MAX = 65536

kernel_name = "hybrid_conformer_conv_mla_macaron"


def rmsnorm(x, g):
    xf = x.astype(jnp.float32)
    y = xf * lax.rsqrt(jnp.mean(xf * xf, axis=-1, keepdims=True) + RMS_EPS) * g.astype(jnp.float32)
    return y.astype(x.dtype)


def swiglu_ffn(h, w1, w3, w2):
    return (jax.nn.silu(h @ w1) * (h @ w3)) @ w2


def conv_module(h, w_pw1, w_dw, g_norm, w_pw2):
    u = h @ w_pw1
    a, b = jnp.split(u, 2, axis=-1)
    u = a * jax.nn.sigmoid(b)
    u = lax.conv_general_dilated(
        u, w_dw[:, None, :].astype(u.dtype), window_strides=(1,),
        padding=[(CONV_WIDTH - 1, 0)],
        dimension_numbers=("NWC", "WIO", "NWC"),
        feature_group_count=D_MODEL)
    u = jax.nn.silu(rmsnorm(u, g_norm))
    return u @ w_pw2


def apply_rope(x, cos, sin):
    half = x.shape[-1] // 2
    xf = x.astype(jnp.float32)
    x1, x2 = xf[..., :half], xf[..., half:]
    out = jnp.concatenate([x1 * cos - x2 * sin, x2 * cos + x1 * sin], axis=-1)
    return out.astype(x.dtype)


def mla(h, positions, w_a, g_q, g_kv, w_uq, w_ukv, w_o):
    a = h @ w_a
    c_q = rmsnorm(a[..., :Q_LORA], g_q)
    c_kv = rmsnorm(a[..., Q_LORA:Q_LORA + KV_LORA], g_kv)
    k_rope = a[..., Q_LORA + KV_LORA:]
    q = jnp.einsum("bsc,chd->bshd", c_q, w_uq)
    q_nope, q_rope = q[..., :QK_NOPE], q[..., QK_NOPE:]
    kv = jnp.einsum("bsc,chd->bshd", c_kv, w_ukv)
    k_nope, v = kv[..., :QK_NOPE], kv[..., QK_NOPE:]

    inv_freq = ROPE_THETA ** (-2.0 * jnp.arange(QK_ROPE // 2, dtype=jnp.float32) / QK_ROPE)
    ang = positions.astype(jnp.float32)[..., None] * inv_freq
    cos, sin = jnp.cos(ang), jnp.sin(ang)
    q_rope = apply_rope(q_rope, cos[:, :, None, :], sin[:, :, None, :])
    k_rope = apply_rope(k_rope, cos, sin)

    seq = h.shape[1]
    scale = (QK_NOPE + QK_ROPE) ** -0.5
    chunk_id = jnp.arange(seq) // CHUNK
    outs = []
    for blk in range(seq // Q_BLOCK):
        q0, q1 = blk * Q_BLOCK, (blk + 1) * Q_BLOCK
        s = (jnp.einsum("bqhd,bkhd->bhqk", q_nope[:, q0:q1], k_nope[:, :q1],
                        preferred_element_type=jnp.float32)
             + jnp.einsum("bqhr,bkr->bhqk", q_rope[:, q0:q1], k_rope[:, :q1],
                          preferred_element_type=jnp.float32)) * scale
        mask = chunk_id[q0:q1, None] >= chunk_id[None, :q1]
        s = jnp.where(mask[None, None], s, -jnp.inf)
        p = jax.nn.softmax(s, axis=-1).astype(v.dtype)
        outs.append(jnp.einsum("bhqk,bkhd->bqhd", p, v[:, :q1]))
    o = jnp.concatenate(outs, axis=1)
    return o.reshape(o.shape[0], seq, N_HEADS * V_HEAD) @ w_o


def _fwd_setup_inputs(seed: int = 0) -> dict:
    key = jax.random.key(seed)
    ks = jax.random.split(key, 24)
    f32 = jnp.float32

    def w(k, shape, fan_in):
        return jax.random.normal(k, shape, f32) * (fan_in ** -0.5)

    def gain(k, shape):
        return 1.0 + 0.02 * jax.random.normal(k, shape, f32)

    x = jax.random.normal(ks[0], (BATCH, SEQ, D_MODEL), f32)
    offset = jax.random.randint(ks[1], (BATCH, 1), 0, POS_OFFSET_MAX, dtype=jnp.int32)
    positions = offset + jnp.arange(SEQ, dtype=jnp.int32)[None, :]
    return {
        "x": x,
        "positions": positions,
        "ffn_norm1": gain(ks[2], (DEPTH, D_MODEL)),
        "ffn1_w1": w(ks[3], (DEPTH, D_MODEL, D_FF), D_MODEL),
        "ffn1_w3": w(ks[4], (DEPTH, D_MODEL, D_FF), D_MODEL),
        "ffn1_w2": w(ks[5], (DEPTH, D_FF, D_MODEL), D_FF),
        "mix_norm": gain(ks[6], (DEPTH, D_MODEL)),
        "ffn_norm2": gain(ks[7], (DEPTH, D_MODEL)),
        "ffn2_w1": w(ks[8], (DEPTH, D_MODEL, D_FF), D_MODEL),
        "ffn2_w3": w(ks[9], (DEPTH, D_MODEL, D_FF), D_MODEL),
        "ffn2_w2": w(ks[10], (DEPTH, D_FF, D_MODEL), D_FF),
        "conv_w_pw1": w(ks[11], (N_CONV_LAYERS, D_MODEL, 2 * D_MODEL), D_MODEL),
        "conv_w_dw": w(ks[12], (N_CONV_LAYERS, CONV_WIDTH, D_MODEL), CONV_WIDTH),
        "conv_norm": gain(ks[13], (N_CONV_LAYERS, D_MODEL)),
        "conv_w_pw2": w(ks[14], (N_CONV_LAYERS, D_MODEL, D_MODEL), D_MODEL),
        "mla_w_a": w(ks[15], (N_MLA_LAYERS, D_MODEL, Q_LORA + KV_LORA + QK_ROPE), D_MODEL),
        "mla_q_norm": gain(ks[16], (N_MLA_LAYERS, Q_LORA)),
        "mla_kv_norm": gain(ks[17], (N_MLA_LAYERS, KV_LORA)),
        "mla_w_uq": w(ks[18], (N_MLA_LAYERS, Q_LORA, N_HEADS, QK_NOPE + QK_ROPE), Q_LORA),
        "mla_w_ukv": w(ks[19], (N_MLA_LAYERS, KV_LORA, N_HEADS, QK_NOPE + V_HEAD), KV_LORA),
        "mla_w_o": w(ks[20], (N_MLA_LAYERS, N_HEADS * V_HEAD, D_MODEL), N_HEADS * V_HEAD),
        "final_norm": gain(ks[21], (D_MODEL,)),
    }


def _fwd_reference(x, positions, ffn_norm1, ffn1_w1, ffn1_w3, ffn1_w2, mix_norm, ffn_norm2,
              ffn2_w1, ffn2_w3, ffn2_w2, conv_w_pw1, conv_w_dw, conv_norm, conv_w_pw2,
              mla_w_a, mla_q_norm, mla_kv_norm, mla_w_uq, mla_w_ukv, mla_w_o, final_norm):
    h = x
    for i in range(DEPTH):
        h = h + FFN_RES_WEIGHT * swiglu_ffn(rmsnorm(h, ffn_norm1[i]), ffn1_w1[i], ffn1_w3[i], ffn1_w2[i])
        m = rmsnorm(h, mix_norm[i])
        j = i // N_MIXERS
        if i % N_MIXERS == 0:
            h = h + conv_module(m, conv_w_pw1[j], conv_w_dw[j], conv_norm[j], conv_w_pw2[j])
        else:
            h = h + mla(m, positions, mla_w_a[j], mla_q_norm[j], mla_kv_norm[j],
                        mla_w_uq[j], mla_w_ukv[j], mla_w_o[j])
        h = h + FFN_RES_WEIGHT * swiglu_ffn(rmsnorm(h, ffn_norm2[i]), ffn2_w1[i], ffn2_w3[i], ffn2_w2[i])
    return rmsnorm(h, final_norm)


import jax as _jax
import jax.numpy as _jnp

TWIN_FORMAT = 'train_step'
FWD_PARAMS = ['x', 'positions', 'ffn_norm1', 'ffn1_w1', 'ffn1_w3', 'ffn1_w2', 'mix_norm', 'ffn_norm2', 'ffn2_w1', 'ffn2_w3', 'ffn2_w2', 'conv_w_pw1', 'conv_w_dw', 'conv_norm', 'conv_w_pw2', 'mla_w_a', 'mla_q_norm', 'mla_kv_norm', 'mla_w_uq', 'mla_w_ukv', 'mla_w_o', 'final_norm']
TWIN_WEIGHTS = ['ffn_norm1', 'ffn1_w1', 'ffn1_w3', 'ffn1_w2', 'mix_norm', 'ffn_norm2', 'ffn2_w1', 'ffn2_w3', 'ffn2_w2', 'conv_w_pw1', 'conv_w_dw', 'conv_norm', 'conv_w_pw2', 'mla_w_a', 'mla_q_norm', 'mla_kv_norm', 'mla_w_uq', 'mla_w_ukv', 'mla_w_o', 'final_norm']
TWIN_DIFF_INPUT = 'x'
TWIN_INPUTS = ['x', 'positions', 'ffn_norm1', 'ffn1_w1', 'ffn1_w3', 'ffn1_w2', 'mix_norm', 'ffn_norm2', 'ffn2_w1', 'ffn2_w3', 'ffn2_w2', 'conv_w_pw1', 'conv_w_dw', 'conv_norm', 'conv_w_pw2', 'mla_w_a', 'mla_q_norm', 'mla_kv_norm', 'mla_w_uq', 'mla_w_ukv', 'mla_w_o', 'final_norm', 'loss_target', 'm_ffn_norm1', 'm_ffn1_w1', 'm_ffn1_w3', 'm_ffn1_w2', 'm_mix_norm', 'm_ffn_norm2', 'm_ffn2_w1', 'm_ffn2_w3', 'm_ffn2_w2', 'm_conv_w_pw1', 'm_conv_w_dw', 'm_conv_norm', 'm_conv_w_pw2', 'm_mla_w_a', 'm_mla_q_norm', 'm_mla_kv_norm', 'm_mla_w_uq', 'm_mla_w_ukv', 'm_mla_w_o', 'm_final_norm', 'v_ffn_norm1', 'v_ffn1_w1', 'v_ffn1_w3', 'v_ffn1_w2', 'v_mix_norm', 'v_ffn_norm2', 'v_ffn2_w1', 'v_ffn2_w3', 'v_ffn2_w2', 'v_conv_w_pw1', 'v_conv_w_dw', 'v_conv_norm', 'v_conv_w_pw2', 'v_mla_w_a', 'v_mla_q_norm', 'v_mla_kv_norm', 'v_mla_w_uq', 'v_mla_w_ukv', 'v_mla_w_o', 'v_final_norm']
TWIN_OUTPUTS = ['loss', 'grad_x', 'grad_ffn_norm1', 'grad_ffn1_w1', 'grad_ffn1_w3', 'grad_ffn1_w2', 'grad_mix_norm', 'grad_ffn_norm2', 'grad_ffn2_w1', 'grad_ffn2_w3', 'grad_ffn2_w2', 'grad_conv_w_pw1', 'grad_conv_w_dw', 'grad_conv_norm', 'grad_conv_w_pw2', 'grad_mla_w_a', 'grad_mla_q_norm', 'grad_mla_kv_norm', 'grad_mla_w_uq', 'grad_mla_w_ukv', 'grad_mla_w_o', 'grad_final_norm', 'delta_ffn_norm1', 'delta_ffn1_w1', 'delta_ffn1_w3', 'delta_ffn1_w2', 'delta_mix_norm', 'delta_ffn_norm2', 'delta_ffn2_w1', 'delta_ffn2_w3', 'delta_ffn2_w2', 'delta_conv_w_pw1', 'delta_conv_w_dw', 'delta_conv_norm', 'delta_conv_w_pw2', 'delta_mla_w_a', 'delta_mla_q_norm', 'delta_mla_kv_norm', 'delta_mla_w_uq', 'delta_mla_w_ukv', 'delta_mla_w_o', 'delta_final_norm', 'new_m_ffn_norm1', 'new_m_ffn1_w1', 'new_m_ffn1_w3', 'new_m_ffn1_w2', 'new_m_mix_norm', 'new_m_ffn_norm2', 'new_m_ffn2_w1', 'new_m_ffn2_w3', 'new_m_ffn2_w2', 'new_m_conv_w_pw1', 'new_m_conv_w_dw', 'new_m_conv_norm', 'new_m_conv_w_pw2', 'new_m_mla_w_a', 'new_m_mla_q_norm', 'new_m_mla_kv_norm', 'new_m_mla_w_uq', 'new_m_mla_w_ukv', 'new_m_mla_w_o', 'new_m_final_norm', 'new_v_ffn_norm1', 'new_v_ffn1_w1', 'new_v_ffn1_w3', 'new_v_ffn1_w2', 'new_v_mix_norm', 'new_v_ffn_norm2', 'new_v_ffn2_w1', 'new_v_ffn2_w3', 'new_v_ffn2_w2', 'new_v_conv_w_pw1', 'new_v_conv_w_dw', 'new_v_conv_norm', 'new_v_conv_w_pw2', 'new_v_mla_w_a', 'new_v_mla_q_norm', 'new_v_mla_kv_norm', 'new_v_mla_w_uq', 'new_v_mla_w_ukv', 'new_v_mla_w_o', 'new_v_final_norm']
TWIN_LEAF_KINDS = {'loss': 'loss', 'grad_x': 'grad_x', 'grad_ffn_norm1': 'grad_w', 'grad_ffn1_w1': 'grad_w', 'grad_ffn1_w3': 'grad_w', 'grad_ffn1_w2': 'grad_w', 'grad_mix_norm': 'grad_w', 'grad_ffn_norm2': 'grad_w', 'grad_ffn2_w1': 'grad_w', 'grad_ffn2_w3': 'grad_w', 'grad_ffn2_w2': 'grad_w', 'grad_conv_w_pw1': 'grad_w', 'grad_conv_w_dw': 'grad_w', 'grad_conv_norm': 'grad_w', 'grad_conv_w_pw2': 'grad_w', 'grad_mla_w_a': 'grad_w', 'grad_mla_q_norm': 'grad_w', 'grad_mla_kv_norm': 'grad_w', 'grad_mla_w_uq': 'grad_w', 'grad_mla_w_ukv': 'grad_w', 'grad_mla_w_o': 'grad_w', 'grad_final_norm': 'grad_w', 'delta_ffn_norm1': 'delta_w', 'delta_ffn1_w1': 'delta_w', 'delta_ffn1_w3': 'delta_w', 'delta_ffn1_w2': 'delta_w', 'delta_mix_norm': 'delta_w', 'delta_ffn_norm2': 'delta_w', 'delta_ffn2_w1': 'delta_w', 'delta_ffn2_w3': 'delta_w', 'delta_ffn2_w2': 'delta_w', 'delta_conv_w_pw1': 'delta_w', 'delta_conv_w_dw': 'delta_w', 'delta_conv_norm': 'delta_w', 'delta_conv_w_pw2': 'delta_w', 'delta_mla_w_a': 'delta_w', 'delta_mla_q_norm': 'delta_w', 'delta_mla_kv_norm': 'delta_w', 'delta_mla_w_uq': 'delta_w', 'delta_mla_w_ukv': 'delta_w', 'delta_mla_w_o': 'delta_w', 'delta_final_norm': 'delta_w', 'new_m_ffn_norm1': 'new_m', 'new_m_ffn1_w1': 'new_m', 'new_m_ffn1_w3': 'new_m', 'new_m_ffn1_w2': 'new_m', 'new_m_mix_norm': 'new_m', 'new_m_ffn_norm2': 'new_m', 'new_m_ffn2_w1': 'new_m', 'new_m_ffn2_w3': 'new_m', 'new_m_ffn2_w2': 'new_m', 'new_m_conv_w_pw1': 'new_m', 'new_m_conv_w_dw': 'new_m', 'new_m_conv_norm': 'new_m', 'new_m_conv_w_pw2': 'new_m', 'new_m_mla_w_a': 'new_m', 'new_m_mla_q_norm': 'new_m', 'new_m_mla_kv_norm': 'new_m', 'new_m_mla_w_uq': 'new_m', 'new_m_mla_w_ukv': 'new_m', 'new_m_mla_w_o': 'new_m', 'new_m_final_norm': 'new_m', 'new_v_ffn_norm1': 'new_v', 'new_v_ffn1_w1': 'new_v', 'new_v_ffn1_w3': 'new_v', 'new_v_ffn1_w2': 'new_v', 'new_v_mix_norm': 'new_v', 'new_v_ffn_norm2': 'new_v', 'new_v_ffn2_w1': 'new_v', 'new_v_ffn2_w3': 'new_v', 'new_v_ffn2_w2': 'new_v', 'new_v_conv_w_pw1': 'new_v', 'new_v_conv_w_dw': 'new_v', 'new_v_conv_norm': 'new_v', 'new_v_conv_w_pw2': 'new_v', 'new_v_mla_w_a': 'new_v', 'new_v_mla_q_norm': 'new_v', 'new_v_mla_kv_norm': 'new_v', 'new_v_mla_w_uq': 'new_v', 'new_v_mla_w_ukv': 'new_v', 'new_v_mla_w_o': 'new_v', 'new_v_final_norm': 'new_v'}


def _forward(args):
    return _fwd_reference(*[args[k] for k in FWD_PARAMS])


def _output_shape():
    out = _jax.eval_shape(lambda: _forward(_fwd_setup_inputs(0)))
    return out.shape, out.dtype

N_MICROBATCH = 1
ADAM_LR = 0.001
ADAM_B1 = 0.9
ADAM_B2 = 0.999
ADAM_EPS = 1e-08
ADAM_WD = 0.01
ADAM_STEP = 10
PER_EXAMPLE_BATCH_AXIS = {'x': 0, 'positions': 0, 'loss_target': 0}
SHARED_INPUTS = []
_WEIGHT_DTYPES = {'ffn_norm1': _jnp.float32, 'ffn1_w1': _jnp.float32, 'ffn1_w3': _jnp.float32, 'ffn1_w2': _jnp.float32, 'mix_norm': _jnp.float32, 'ffn_norm2': _jnp.float32, 'ffn2_w1': _jnp.float32, 'ffn2_w3': _jnp.float32, 'ffn2_w2': _jnp.float32, 'conv_w_pw1': _jnp.float32, 'conv_w_dw': _jnp.float32, 'conv_norm': _jnp.float32, 'conv_w_pw2': _jnp.float32, 'mla_w_a': _jnp.float32, 'mla_q_norm': _jnp.float32, 'mla_kv_norm': _jnp.float32, 'mla_w_uq': _jnp.float32, 'mla_w_ukv': _jnp.float32, 'mla_w_o': _jnp.float32, 'final_norm': _jnp.float32}
MOMENT_SCALE = {'ffn_norm1': 1.035771e-01, 'ffn1_w1': 4.447469e-02, 'ffn1_w3': 4.310026e-02, 'ffn1_w2': 7.130828e-02, 'mix_norm': 1.155081e-01, 'ffn_norm2': 8.631854e-02, 'ffn2_w1': 3.747056e-02, 'ffn2_w3': 3.627037e-02, 'ffn2_w2': 6.018287e-02, 'conv_w_pw1': 1.078779e-01, 'conv_w_dw': 1.418902e-01, 'conv_norm': 1.813406e-01, 'conv_w_pw2': 1.417576e-01, 'mla_w_a': 6.585145e-02, 'mla_q_norm': 3.931015e-02, 'mla_kv_norm': 1.057072e-01, 'mla_w_uq': 2.317963e-02, 'mla_w_ukv': 3.613707e-02, 'mla_w_o': 4.658749e-02, 'final_norm': 6.404033e+01}


def _to_microbatches(a, axis):
    t = _jnp.moveaxis(a, axis, 0)
    t = t.reshape((N_MICROBATCH, t.shape[0] // N_MICROBATCH) + t.shape[1:])
    return _jnp.moveaxis(t, 1, axis + 1)


def setup_inputs(seed: int = 0) -> dict:
    inp = _fwd_setup_inputs(seed)
    key = _jax.random.fold_in(_jax.random.key(seed), 7919)
    shape, _ = _output_shape()
    out = dict(inp)
    out["loss_target"] = _jax.random.normal(_jax.random.fold_in(key, 0), shape, _jnp.float32)
    for i, name in enumerate(TWIN_WEIGHTS):
        w = inp[name].astype(_jnp.float32)
        if MOMENT_SCALE is None:
            s = _jnp.sqrt(_jnp.mean(_jnp.square(w)) + 1e-30)
        else:
            s = MOMENT_SCALE[name]
        km, kv = _jax.random.split(_jax.random.fold_in(key, i + 1))
        out[name] = w
        out["m_" + name] = s * _jax.random.normal(km, w.shape, _jnp.float32)
        out["v_" + name] = (s * s) * _jax.random.uniform(kv, w.shape, _jnp.float32, 0.5, 1.5)
    if N_MICROBATCH > 1:
        for name, axis in PER_EXAMPLE_BATCH_AXIS.items():
            out[name] = _to_microbatches(out[name], axis)
    return {'x': out['x'], 'positions': out['positions'], 'ffn_norm1': out['ffn_norm1'], 'ffn1_w1': out['ffn1_w1'], 'ffn1_w3': out['ffn1_w3'], 'ffn1_w2': out['ffn1_w2'], 'mix_norm': out['mix_norm'], 'ffn_norm2': out['ffn_norm2'], 'ffn2_w1': out['ffn2_w1'], 'ffn2_w3': out['ffn2_w3'], 'ffn2_w2': out['ffn2_w2'], 'conv_w_pw1': out['conv_w_pw1'], 'conv_w_dw': out['conv_w_dw'], 'conv_norm': out['conv_norm'], 'conv_w_pw2': out['conv_w_pw2'], 'mla_w_a': out['mla_w_a'], 'mla_q_norm': out['mla_q_norm'], 'mla_kv_norm': out['mla_kv_norm'], 'mla_w_uq': out['mla_w_uq'], 'mla_w_ukv': out['mla_w_ukv'], 'mla_w_o': out['mla_w_o'], 'final_norm': out['final_norm'], 'loss_target': out['loss_target'], 'm_ffn_norm1': out['m_ffn_norm1'], 'm_ffn1_w1': out['m_ffn1_w1'], 'm_ffn1_w3': out['m_ffn1_w3'], 'm_ffn1_w2': out['m_ffn1_w2'], 'm_mix_norm': out['m_mix_norm'], 'm_ffn_norm2': out['m_ffn_norm2'], 'm_ffn2_w1': out['m_ffn2_w1'], 'm_ffn2_w3': out['m_ffn2_w3'], 'm_ffn2_w2': out['m_ffn2_w2'], 'm_conv_w_pw1': out['m_conv_w_pw1'], 'm_conv_w_dw': out['m_conv_w_dw'], 'm_conv_norm': out['m_conv_norm'], 'm_conv_w_pw2': out['m_conv_w_pw2'], 'm_mla_w_a': out['m_mla_w_a'], 'm_mla_q_norm': out['m_mla_q_norm'], 'm_mla_kv_norm': out['m_mla_kv_norm'], 'm_mla_w_uq': out['m_mla_w_uq'], 'm_mla_w_ukv': out['m_mla_w_ukv'], 'm_mla_w_o': out['m_mla_w_o'], 'm_final_norm': out['m_final_norm'], 'v_ffn_norm1': out['v_ffn_norm1'], 'v_ffn1_w1': out['v_ffn1_w1'], 'v_ffn1_w3': out['v_ffn1_w3'], 'v_ffn1_w2': out['v_ffn1_w2'], 'v_mix_norm': out['v_mix_norm'], 'v_ffn_norm2': out['v_ffn_norm2'], 'v_ffn2_w1': out['v_ffn2_w1'], 'v_ffn2_w3': out['v_ffn2_w3'], 'v_ffn2_w2': out['v_ffn2_w2'], 'v_conv_w_pw1': out['v_conv_w_pw1'], 'v_conv_w_dw': out['v_conv_w_dw'], 'v_conv_norm': out['v_conv_norm'], 'v_conv_w_pw2': out['v_conv_w_pw2'], 'v_mla_w_a': out['v_mla_w_a'], 'v_mla_q_norm': out['v_mla_q_norm'], 'v_mla_kv_norm': out['v_mla_kv_norm'], 'v_mla_w_uq': out['v_mla_w_uq'], 'v_mla_w_ukv': out['v_mla_w_ukv'], 'v_mla_w_o': out['v_mla_w_o'], 'v_final_norm': out['v_final_norm']}


def _loss(weights, diff, rest, loss_target):
    with _jax.named_scope("forward"):
        args = {**rest, TWIN_DIFF_INPUT: diff, **{k: w.astype(_WEIGHT_DTYPES[k]) for k, w in weights.items()}}
        y = _forward(args)
    with _jax.named_scope("loss_head"):
        err = _jnp.square(y.astype(_jnp.float32) - loss_target)
        return 0.5 * _jnp.sum(_jnp.mean(err, axis=-1)) if err.ndim else 0.5 * err


def _adamw(w, g, m, v):
    m = ADAM_B1 * m + (1.0 - ADAM_B1) * g
    v = ADAM_B2 * v + (1.0 - ADAM_B2) * _jnp.square(g)
    m_hat = m / (1.0 - ADAM_B1 ** ADAM_STEP)
    v_hat = v / (1.0 - ADAM_B2 ** ADAM_STEP)
    delta = -ADAM_LR * (m_hat / (_jnp.sqrt(v_hat) + ADAM_EPS) + ADAM_WD * w)
    return delta, m, v


def reference(x, positions, ffn_norm1, ffn1_w1, ffn1_w3, ffn1_w2, mix_norm, ffn_norm2, ffn2_w1, ffn2_w3, ffn2_w2, conv_w_pw1, conv_w_dw, conv_norm, conv_w_pw2, mla_w_a, mla_q_norm, mla_kv_norm, mla_w_uq, mla_w_ukv, mla_w_o, final_norm, loss_target, m_ffn_norm1, m_ffn1_w1, m_ffn1_w3, m_ffn1_w2, m_mix_norm, m_ffn_norm2, m_ffn2_w1, m_ffn2_w3, m_ffn2_w2, m_conv_w_pw1, m_conv_w_dw, m_conv_norm, m_conv_w_pw2, m_mla_w_a, m_mla_q_norm, m_mla_kv_norm, m_mla_w_uq, m_mla_w_ukv, m_mla_w_o, m_final_norm, v_ffn_norm1, v_ffn1_w1, v_ffn1_w3, v_ffn1_w2, v_mix_norm, v_ffn_norm2, v_ffn2_w1, v_ffn2_w3, v_ffn2_w2, v_conv_w_pw1, v_conv_w_dw, v_conv_norm, v_conv_w_pw2, v_mla_w_a, v_mla_q_norm, v_mla_kv_norm, v_mla_w_uq, v_mla_w_ukv, v_mla_w_o, v_final_norm):
    given = dict(x=x, positions=positions, ffn_norm1=ffn_norm1, ffn1_w1=ffn1_w1, ffn1_w3=ffn1_w3, ffn1_w2=ffn1_w2, mix_norm=mix_norm, ffn_norm2=ffn_norm2, ffn2_w1=ffn2_w1, ffn2_w3=ffn2_w3, ffn2_w2=ffn2_w2, conv_w_pw1=conv_w_pw1, conv_w_dw=conv_w_dw, conv_norm=conv_norm, conv_w_pw2=conv_w_pw2, mla_w_a=mla_w_a, mla_q_norm=mla_q_norm, mla_kv_norm=mla_kv_norm, mla_w_uq=mla_w_uq, mla_w_ukv=mla_w_ukv, mla_w_o=mla_w_o, final_norm=final_norm, loss_target=loss_target, m_ffn_norm1=m_ffn_norm1, m_ffn1_w1=m_ffn1_w1, m_ffn1_w3=m_ffn1_w3, m_ffn1_w2=m_ffn1_w2, m_mix_norm=m_mix_norm, m_ffn_norm2=m_ffn_norm2, m_ffn2_w1=m_ffn2_w1, m_ffn2_w3=m_ffn2_w3, m_ffn2_w2=m_ffn2_w2, m_conv_w_pw1=m_conv_w_pw1, m_conv_w_dw=m_conv_w_dw, m_conv_norm=m_conv_norm, m_conv_w_pw2=m_conv_w_pw2, m_mla_w_a=m_mla_w_a, m_mla_q_norm=m_mla_q_norm, m_mla_kv_norm=m_mla_kv_norm, m_mla_w_uq=m_mla_w_uq, m_mla_w_ukv=m_mla_w_ukv, m_mla_w_o=m_mla_w_o, m_final_norm=m_final_norm, v_ffn_norm1=v_ffn_norm1, v_ffn1_w1=v_ffn1_w1, v_ffn1_w3=v_ffn1_w3, v_ffn1_w2=v_ffn1_w2, v_mix_norm=v_mix_norm, v_ffn_norm2=v_ffn_norm2, v_ffn2_w1=v_ffn2_w1, v_ffn2_w3=v_ffn2_w3, v_ffn2_w2=v_ffn2_w2, v_conv_w_pw1=v_conv_w_pw1, v_conv_w_dw=v_conv_w_dw, v_conv_norm=v_conv_norm, v_conv_w_pw2=v_conv_w_pw2, v_mla_w_a=v_mla_w_a, v_mla_q_norm=v_mla_q_norm, v_mla_kv_norm=v_mla_kv_norm, v_mla_w_uq=v_mla_w_uq, v_mla_w_ukv=v_mla_w_ukv, v_mla_w_o=v_mla_w_o, v_final_norm=v_final_norm)
    weights = {n: given[n] for n in TWIN_WEIGHTS}
    shared = {n: given[n] for n in SHARED_INPUTS}
    per_example = {n: given[n] for n in ['x', 'positions']}
    grad_fn = _jax.value_and_grad(_loss, argnums=(0, 1))

    def one_microbatch(ex, loss_target):
        ex = dict(ex)
        diff = ex.pop(TWIN_DIFF_INPUT)
        return grad_fn(weights, diff, {**shared, **ex}, loss_target)

    if N_MICROBATCH == 1:
        loss, (grad_w, grad_x) = one_microbatch(per_example, given["loss_target"])
    else:
        def body(carry, xs):
            loss_sum, grad_sum = carry
            l_k, (gw_k, gx_k) = one_microbatch(xs[0], xs[1])
            with _jax.named_scope("update"):
                return (loss_sum + l_k, _jax.tree.map(_jnp.add, grad_sum, gw_k)), gx_k

        init = (_jnp.zeros((), _jnp.float32), _jax.tree.map(_jnp.zeros_like, weights))
        (loss, grad_w), grad_x = _jax.lax.scan(body, init, (per_example, given["loss_target"]))
    with _jax.named_scope("update"):
        delta_w, new_m, new_v = {}, {}, {}
        for n in TWIN_WEIGHTS:
            delta_w[n], new_m[n], new_v[n] = _adamw(weights[n], grad_w[n], given["m_" + n], given["v_" + n])
    return (loss, grad_x, *[grad_w[n] for n in TWIN_WEIGHTS], *[delta_w[n] for n in TWIN_WEIGHTS],
            *[new_m[n] for n in TWIN_WEIGHTS], *[new_v[n] for n in TWIN_WEIGHTS])
```

```python
import functools

import jax
import jax.numpy as jnp
from jax import lax
from jax.experimental import pallas as pl
from jax.experimental.pallas import tpu as pltpu

F32 = jnp.float32
MXU_DTYPE = jnp.bfloat16
WIRE_DTYPE = jnp.bfloat16

N_DEV = 8
D_MODEL = 1024
D_FF = 2816
N_HEADS = 8
QK_NOPE = 128
QK_ROPE = 64
V_HEAD = 128
Q_LORA = 512
KV_LORA = 256
HEAD_PAD = 256
A_PAD = Q_LORA + KV_LORA + 128
CONV_WIDTH = 31
CONV_HALO = 32
CHUNK_SHIFT = 6
ROPE_THETA = 10000.0
RMS_EPS = 1e-6
ATTN_SCALE = (QK_NOPE + QK_ROPE) ** -0.5
FFN_RES_WEIGHT = 0.5
ADAM_LR = 0.001
ADAM_B1 = 0.9
ADAM_B2 = 0.999
ADAM_EPS = 1e-08
ADAM_WD = 0.01
ADAM_STEP = 10

PACK_W = 1024
PACK_ROW_TILE = 512
VMEM_LIMIT = 56 << 20


def _cparams(*sem):
    return pltpu.CompilerParams(dimension_semantics=sem, vmem_limit_bytes=VMEM_LIMIT)


def _dot(a, b):
    return lax.dot_general(a, b, (((1,), (0,)), ((), ())), preferred_element_type=F32)


def _dot_nt(a, b):
    return lax.dot_general(a, b, (((1,), (1,)), ((), ())), preferred_element_type=F32)


def _dot_tn(a, b):
    return lax.dot_general(a, b, (((0,), (0,)), ((), ())), preferred_element_type=F32)


def _resident(shape):
    return pl.BlockSpec(shape, lambda *_: (0,) * len(shape), pipeline_mode=pl.Buffered(1))


def _rows(tm, n):
    return pl.BlockSpec((tm, n), lambda t, *_: (t, 0))


def _rms(x, g):
    r = lax.rsqrt(jnp.mean(x * x, axis=-1, keepdims=True) + RMS_EPS)
    return x * r * g


def _rms_bwd(x, g, dy):
    r = lax.rsqrt(jnp.mean(x * x, axis=-1, keepdims=True) + RMS_EPS)
    xr = x * r
    dg = jnp.sum(dy * xr, axis=0, keepdims=True)
    u = dy * g
    dx = r * (u - xr * jnp.mean(u * xr, axis=-1, keepdims=True))
    return dx, dg


def _sigmoid(x):
    return 1.0 / (1.0 + jnp.exp(-x))


def _exchange(big, small, per_peer, name):
    rows, width = big.shape[-2:]
    srows = small.shape[0]

    def body(big_ref, small_ref, obig_ref, osmall_ref, send_big, recv_big, send_small, recv_small, local_sem):
        x, y, c = lax.axis_index("x"), lax.axis_index("y"), lax.axis_index("c")
        me = 4 * x + 2 * y + c

        def big_src(p):
            return big_ref.at[p] if per_peer else big_ref

        own_big = pltpu.make_async_copy(big_src(me), obig_ref.at[me], local_sem.at[0])
        own_small = pltpu.make_async_copy(small_ref, osmall_ref.at[me], local_sem.at[1])
        own_big.start()
        own_small.start()
        sends, recvs = [], []
        for k in range(1, N_DEV):
            px, py, pc = x ^ ((k >> 2) & 1), y ^ ((k >> 1) & 1), c ^ (k & 1)
            p = 4 * px + 2 * py + pc
            peer = dict(device_id=(px, py, pc), device_id_type=pl.DeviceIdType.MESH)
            sends.append(pltpu.make_async_remote_copy(
                src_ref=big_src(p), dst_ref=obig_ref.at[me], send_sem=send_big.at[k - 1], recv_sem=recv_big.at[k - 1], **peer))
            sends.append(pltpu.make_async_remote_copy(
                src_ref=small_ref, dst_ref=osmall_ref.at[me], send_sem=send_small.at[k - 1], recv_sem=recv_small.at[k - 1], **peer))
            recvs.append(pltpu.make_async_remote_copy(
                src_ref=big_src(p), dst_ref=obig_ref.at[p], send_sem=send_big.at[k - 1], recv_sem=recv_big.at[k - 1], **peer))
            recvs.append(pltpu.make_async_remote_copy(
                src_ref=small_ref, dst_ref=osmall_ref.at[p], send_sem=send_small.at[k - 1], recv_sem=recv_small.at[k - 1], **peer))
        for cp in sends:
            cp.start()
        for cp in recvs:
            cp.wait_recv()
        for cp in sends:
            cp.wait_send()
        own_big.wait()
        own_small.wait()

    hbm = pl.BlockSpec(memory_space=pltpu.HBM)
    vmem = pl.BlockSpec(memory_space=pltpu.VMEM)
    return pl.pallas_call(
        body,
        name=name,
        out_shape=(jax.ShapeDtypeStruct((N_DEV, rows, width), big.dtype), jax.ShapeDtypeStruct((N_DEV, srows, width), small.dtype)),
        in_specs=[hbm, vmem],
        out_specs=(hbm, vmem),
        scratch_shapes=[pltpu.SemaphoreType.DMA((N_DEV - 1,))] * 4 + [pltpu.SemaphoreType.DMA((2,))],
        compiler_params=pltpu.CompilerParams(vmem_limit_bytes=VMEM_LIMIT),
    )(big, small)


def _ffn_tiles(t, tm=512):
    return min(tm, t), 1408


def _ffn_fwd(h, g, w1, w3, w2):
    t = h.shape[0]
    tm, tf = _ffn_tiles(t)
    nf = D_FF // tf

    def body(h_ref, g_ref, w1_ref, w3_ref, w2_ref, ho_ref, n_ref, a_ref, b_ref, n_scr, acc):
        f = pl.program_id(1)

        @pl.when(f == 0)
        def _():
            n_scr[...] = _rms(h_ref[...], g_ref[...]).astype(MXU_DTYPE)
            n_ref[...] = n_scr[...]
            acc[...] = jnp.zeros_like(acc)

        n = n_scr[...]
        a = _dot(n, w1_ref[...])
        b = _dot(n, w3_ref[...])
        a_ref[...] = a.astype(MXU_DTYPE)
        b_ref[...] = b.astype(MXU_DTYPE)
        s = (a * _sigmoid(a)) * b
        acc[...] += _dot(s.astype(MXU_DTYPE), w2_ref[...])

        @pl.when(f == nf - 1)
        def _():
            ho_ref[...] = h_ref[...] + FFN_RES_WEIGHT * acc[...]

    return pl.pallas_call(
        body,
        name="ffn_fwd",
        grid=(t // tm, nf),
        in_specs=[
            _rows(tm, D_MODEL),
            _resident((1, D_MODEL)),
            pl.BlockSpec((D_MODEL, tf), lambda i, f: (0, f)),
            pl.BlockSpec((D_MODEL, tf), lambda i, f: (0, f)),
            pl.BlockSpec((tf, D_MODEL), lambda i, f: (f, 0)),
        ],
        out_specs=[
            _rows(tm, D_MODEL),
            _rows(tm, D_MODEL),
            pl.BlockSpec((tm, tf), lambda i, f: (i, f)),
            pl.BlockSpec((tm, tf), lambda i, f: (i, f)),
        ],
        out_shape=[
            jax.ShapeDtypeStruct((t, D_MODEL), F32),
            jax.ShapeDtypeStruct((t, D_MODEL), MXU_DTYPE),
            jax.ShapeDtypeStruct((t, D_FF), MXU_DTYPE),
            jax.ShapeDtypeStruct((t, D_FF), MXU_DTYPE),
        ],
        scratch_shapes=[pltpu.VMEM((tm, D_MODEL), MXU_DTYPE), pltpu.VMEM((tm, D_MODEL), F32)],
        compiler_params=_cparams("parallel", "arbitrary"),
    )(h, g, w1, w3, w2)


def _ffn_bwd(dout, h, g, a, b, w1, w3, w2):
    t = h.shape[0]
    tm, tf = _ffn_tiles(t, tm=256)
    nf = D_FF // tf

    def body(do_ref, h_ref, g_ref, a_ref, b_ref, w1_ref, w3_ref, w2_ref, dh_ref, dob_ref, s_ref, da_ref, db_ref, dg_ref, dob_scr, acc):
        i, f = pl.program_id(0), pl.program_id(1)

        @pl.when(f == 0)
        def _():
            dob_scr[...] = (FFN_RES_WEIGHT * do_ref[...]).astype(MXU_DTYPE)
            dob_ref[...] = dob_scr[...]
            acc[...] = jnp.zeros_like(acc)

        @pl.when((f == 0) & (i == 0))
        def _():
            dg_ref[...] = jnp.zeros_like(dg_ref)

        ds = _dot_nt(dob_scr[...], w2_ref[...])
        av = a_ref[...].astype(F32)
        bv = b_ref[...].astype(F32)
        sig = _sigmoid(av)
        sil = av * sig
        s_ref[...] = (sil * bv).astype(MXU_DTYPE)
        da = (ds * bv * (sig * (1.0 + av * (1.0 - sig)))).astype(MXU_DTYPE)
        db = (ds * sil).astype(MXU_DTYPE)
        da_ref[...] = da
        db_ref[...] = db
        acc[...] += _dot_nt(da, w1_ref[...]) + _dot_nt(db, w3_ref[...])

        @pl.when(f == nf - 1)
        def _():
            dx, dg = _rms_bwd(h_ref[...], g_ref[...], acc[...])
            dh_ref[...] = do_ref[...] + dx
            dg_ref[...] += dg

    return pl.pallas_call(
        body,
        name="ffn_bwd",
        grid=(t // tm, nf),
        in_specs=[
            _rows(tm, D_MODEL),
            _rows(tm, D_MODEL),
            _resident((1, D_MODEL)),
            pl.BlockSpec((tm, tf), lambda i, f: (i, f)),
            pl.BlockSpec((tm, tf), lambda i, f: (i, f)),
            pl.BlockSpec((D_MODEL, tf), lambda i, f: (0, f)),
            pl.BlockSpec((D_MODEL, tf), lambda i, f: (0, f)),
            pl.BlockSpec((tf, D_MODEL), lambda i, f: (f, 0)),
        ],
        out_specs=[
            _rows(tm, D_MODEL),
            _rows(tm, D_MODEL),
            pl.BlockSpec((tm, tf), lambda i, f: (i, f)),
            pl.BlockSpec((tm, tf), lambda i, f: (i, f)),
            pl.BlockSpec((tm, tf), lambda i, f: (i, f)),
            pl.BlockSpec((1, D_MODEL), lambda i, f: (0, 0)),
        ],
        out_shape=[
            jax.ShapeDtypeStruct((t, D_MODEL), F32),
            jax.ShapeDtypeStruct((t, D_MODEL), MXU_DTYPE),
            jax.ShapeDtypeStruct((t, D_FF), MXU_DTYPE),
            jax.ShapeDtypeStruct((t, D_FF), MXU_DTYPE),
            jax.ShapeDtypeStruct((t, D_FF), MXU_DTYPE),
            jax.ShapeDtypeStruct((1, D_MODEL), F32),
        ],
        scratch_shapes=[pltpu.VMEM((tm, D_MODEL), MXU_DTYPE), pltpu.VMEM((tm, D_MODEL), F32)],
        compiler_params=_cparams("arbitrary", "arbitrary"),
    )(dout, h, g, a, b, w1, w3, w2)


def _wgrad(x, y, name):
    t, m = x.shape
    n = y.shape[1]
    tk = min(1024, t)
    bm = m if m <= 1408 else m // 2
    bn = n if n <= 1408 else n // 2
    nk = t // tk

    def body(x_ref, y_ref, o_ref, acc):
        k = pl.program_id(2)

        @pl.when(k == 0)
        def _():
            acc[...] = jnp.zeros_like(acc)

        acc[...] += _dot_tn(x_ref[...].astype(MXU_DTYPE), y_ref[...].astype(MXU_DTYPE))

        @pl.when(k == nk - 1)
        def _():
            o_ref[...] = acc[...]

    return pl.pallas_call(
        body,
        name=name,
        grid=(m // bm, n // bn, nk),
        in_specs=[pl.BlockSpec((tk, bm), lambda i, j, k: (k, i)), pl.BlockSpec((tk, bn), lambda i, j, k: (k, j))],
        out_specs=pl.BlockSpec((bm, bn), lambda i, j, k: (i, j)),
        out_shape=jax.ShapeDtypeStruct((m, n), F32),
        scratch_shapes=[pltpu.VMEM((bm, bn), F32)],
        compiler_params=_cparams("parallel", "parallel", "arbitrary"),
    )(x, y)


def _conv_pre(h, g, w_pw1):
    t = h.shape[0]
    tm = min(512, t)

    def body(h_ref, g_ref, w_ref, n_ref, u_ref, glu_ref):
        n = _rms(h_ref[...], g_ref[...]).astype(MXU_DTYPE)
        n_ref[...] = n
        u = _dot(n, w_ref[...])
        u_ref[...] = u
        glu_ref[...] = u[:, :D_MODEL] * _sigmoid(u[:, D_MODEL:])

    return pl.pallas_call(
        body,
        name="conv_pre",
        grid=(t // tm,),
        in_specs=[_rows(tm, D_MODEL), _resident((1, D_MODEL)), _resident((D_MODEL, 2 * D_MODEL))],
        out_specs=[_rows(tm, D_MODEL), _rows(tm, 2 * D_MODEL), _rows(tm, D_MODEL)],
        out_shape=[
            jax.ShapeDtypeStruct((t, D_MODEL), MXU_DTYPE),
            jax.ShapeDtypeStruct((t, 2 * D_MODEL), F32),
            jax.ShapeDtypeStruct((t, D_MODEL), F32),
        ],
        compiler_params=_cparams("parallel"),
    )(h, g, w_pw1)


def _conv_main(glu, w_dw, g, w_pw2, h):
    t = h.shape[0]
    tm = min(512, t)
    per = tm // CONV_HALO

    def body(glu_ref, halo_ref, w_ref, g_ref, w2_ref, h_ref, ho_ref, c_ref, sw_ref, ext):
        i = pl.program_id(0)
        ext[pl.ds(0, CONV_HALO), :] = jnp.where(i == 0, 0.0, halo_ref[...])
        ext[pl.ds(CONV_HALO, tm), :] = glu_ref[...]
        acc = jnp.zeros((tm, D_MODEL), F32)
        for k in range(CONV_WIDTH):
            acc = acc + ext[pl.ds(CONV_HALO - (CONV_WIDTH - 1) + k, tm), :] * w_ref[pl.ds(k, 1), :]
        c_ref[...] = acc
        y = _rms(acc, g_ref[...])
        sw = (y * _sigmoid(y)).astype(MXU_DTYPE)
        sw_ref[...] = sw
        ho_ref[...] = h_ref[...] + _dot(sw, w2_ref[...])

    return pl.pallas_call(
        body,
        name="conv_main",
        grid=(t // tm,),
        in_specs=[
            _rows(tm, D_MODEL),
            pl.BlockSpec((CONV_HALO, D_MODEL), lambda i: (jnp.maximum(i * per - 1, 0), 0)),
            _resident((CONV_WIDTH, D_MODEL)),
            _resident((1, D_MODEL)),
            _resident((D_MODEL, D_MODEL)),
            _rows(tm, D_MODEL),
        ],
        out_specs=[_rows(tm, D_MODEL), _rows(tm, D_MODEL), _rows(tm, D_MODEL)],
        out_shape=[
            jax.ShapeDtypeStruct((t, D_MODEL), F32),
            jax.ShapeDtypeStruct((t, D_MODEL), F32),
            jax.ShapeDtypeStruct((t, D_MODEL), MXU_DTYPE),
        ],
        scratch_shapes=[pltpu.VMEM((tm + CONV_HALO, D_MODEL), F32)],
        compiler_params=_cparams("parallel"),
    )(glu, glu, w_dw, g, w_pw2, h)


def _conv_bwd_post(dout, c, g, w_pw2):
    t = dout.shape[0]
    tm = min(512, t)

    def body(do_ref, c_ref, g_ref, w2_ref, dc_ref, dg_ref):
        @pl.when(pl.program_id(0) == 0)
        def _():
            dg_ref[...] = jnp.zeros_like(dg_ref)

        dsw = _dot_nt(do_ref[...].astype(MXU_DTYPE), w2_ref[...])
        cv = c_ref[...]
        y = _rms(cv, g_ref[...])
        sig = _sigmoid(y)
        dy = dsw * (sig * (1.0 + y * (1.0 - sig)))
        dc, dg = _rms_bwd(cv, g_ref[...], dy)
        dc_ref[...] = dc
        dg_ref[...] += dg

    return pl.pallas_call(
        body,
        name="conv_bwd_post",
        grid=(t // tm,),
        in_specs=[_rows(tm, D_MODEL), _rows(tm, D_MODEL), _resident((1, D_MODEL)), _resident((D_MODEL, D_MODEL))],
        out_specs=[_rows(tm, D_MODEL), pl.BlockSpec((1, D_MODEL), lambda i: (0, 0))],
        out_shape=[jax.ShapeDtypeStruct((t, D_MODEL), F32), jax.ShapeDtypeStruct((1, D_MODEL), F32)],
        compiler_params=_cparams("arbitrary"),
    )(dout, c, g, w_pw2)


def _conv_bwd_pre(dc, glu, u, w_dw, w_pw1, h, g, dout):
    t = h.shape[0]
    tm = min(512, t)
    per = tm // CONV_HALO
    nt = t // tm
    last_halo = t // CONV_HALO - 1

    def body(dc_ref, dcn_ref, glu_ref, gluh_ref, u_ref, w_ref, w1_ref, h_ref, g_ref, do_ref,
             dh_ref, du_ref, dw_ref, dg_ref, dext, gext):
        i = pl.program_id(0)

        @pl.when(i == 0)
        def _():
            dw_ref[...] = jnp.zeros_like(dw_ref)
            dg_ref[...] = jnp.zeros_like(dg_ref)

        dext[pl.ds(0, tm), :] = dc_ref[...]
        dext[pl.ds(tm, CONV_HALO), :] = jnp.where(i == nt - 1, 0.0, dcn_ref[...])
        gext[pl.ds(0, CONV_HALO), :] = jnp.where(i == 0, 0.0, gluh_ref[...])
        gext[pl.ds(CONV_HALO, tm), :] = glu_ref[...]
        dcv = dc_ref[...]
        dglu = jnp.zeros((tm, D_MODEL), F32)
        for k in range(CONV_WIDTH):
            dglu = dglu + dext[pl.ds(CONV_WIDTH - 1 - k, tm), :] * w_ref[pl.ds(k, 1), :]
            prod = dcv * gext[pl.ds(CONV_HALO - (CONV_WIDTH - 1) + k, tm), :]
            dw_ref[pl.ds(k, 1), :] += jnp.sum(prod, axis=0, keepdims=True)
        uv = u_ref[...]
        av, bv = uv[:, :D_MODEL], uv[:, D_MODEL:]
        sig = _sigmoid(bv)
        du_ref[:, :D_MODEL] = (dglu * sig).astype(MXU_DTYPE)
        du_ref[:, D_MODEL:] = (dglu * av * (sig * (1.0 - sig))).astype(MXU_DTYPE)
        dn = _dot_nt(du_ref[...], w1_ref[...])
        dx, dg = _rms_bwd(h_ref[...], g_ref[...], dn)
        dh_ref[...] = do_ref[...] + dx
        dg_ref[...] += dg

    return pl.pallas_call(
        body,
        name="conv_bwd_pre",
        grid=(nt,),
        in_specs=[
            _rows(tm, D_MODEL),
            pl.BlockSpec((CONV_HALO, D_MODEL), lambda i: (jnp.minimum((i + 1) * per, last_halo), 0)),
            _rows(tm, D_MODEL),
            pl.BlockSpec((CONV_HALO, D_MODEL), lambda i: (jnp.maximum(i * per - 1, 0), 0)),
            _rows(tm, 2 * D_MODEL),
            _resident((CONV_WIDTH, D_MODEL)),
            _resident((D_MODEL, 2 * D_MODEL)),
            _rows(tm, D_MODEL),
            _resident((1, D_MODEL)),
            _rows(tm, D_MODEL),
        ],
        out_specs=[
            _rows(tm, D_MODEL),
            _rows(tm, 2 * D_MODEL),
            pl.BlockSpec((CONV_HALO, D_MODEL), lambda i: (0, 0)),
            pl.BlockSpec((1, D_MODEL), lambda i: (0, 0)),
        ],
        out_shape=[
            jax.ShapeDtypeStruct((t, D_MODEL), F32),
            jax.ShapeDtypeStruct((t, 2 * D_MODEL), MXU_DTYPE),
            jax.ShapeDtypeStruct((CONV_HALO, D_MODEL), F32),
            jax.ShapeDtypeStruct((1, D_MODEL), F32),
        ],
        scratch_shapes=[pltpu.VMEM((tm + CONV_HALO, D_MODEL), F32), pltpu.VMEM((tm + CONV_HALO, D_MODEL), F32)],
        compiler_params=_cparams("arbitrary"),
    )(dc, dc, glu, glu, u, w_dw, w_pw1, h, g, dout)


def _rope(s, cs, sa, sb):
    return s * cs - pltpu.roll(s, 96, 1) * sa + pltpu.roll(s, 32, 1) * sb


def _unrope(d, cs, sa, sb):
    return d * cs + pltpu.roll(d, 96, 1) * sa - pltpu.roll(d, 32, 1) * sb


def _mla_pre(h, g, w_a, g_q, g_kv, w_uq, w_uk, w_uv, cs, sa, sb):
    t = h.shape[0]
    tm = min(512, t)
    qw = N_HEADS * HEAD_PAD
    vw = N_HEADS * V_HEAD

    def body(h_ref, g_ref, wa_ref, gq_ref, gkv_ref, wuq_ref, wuk_ref, wuv_ref, cs_ref, sa_ref, sb_ref,
             n_ref, a_ref, cq_ref, ckv_ref, q_ref, k_ref, v_ref):
        n = _rms(h_ref[...], g_ref[...]).astype(MXU_DTYPE)
        n_ref[...] = n
        a = _dot(n, wa_ref[...])
        a_ref[...] = a
        cq = _rms(a[:, :Q_LORA], gq_ref[...]).astype(MXU_DTYPE)
        ckv = _rms(a[:, Q_LORA:Q_LORA + KV_LORA], gkv_ref[...]).astype(MXU_DTYPE)
        cq_ref[...] = cq
        ckv_ref[...] = ckv
        q = _dot(cq, wuq_ref[...])
        kn = _dot(ckv, wuk_ref[...])
        v_ref[...] = _dot(ckv, wuv_ref[...]).astype(MXU_DTYPE)
        cs_, sa_, sb_ = cs_ref[...], sa_ref[...], sb_ref[...]
        k_rot = _rope(a[:, Q_LORA + KV_LORA:], cs_, sa_, sb_).astype(MXU_DTYPE)
        for hd in range(N_HEADS):
            lo = hd * HEAD_PAD
            q_ref[:, lo:lo + QK_NOPE] = q[:, lo:lo + QK_NOPE].astype(MXU_DTYPE)
            q_ref[:, lo + QK_NOPE:lo + HEAD_PAD] = _rope(q[:, lo + QK_NOPE:lo + HEAD_PAD], cs_, sa_, sb_).astype(MXU_DTYPE)
            k_ref[:, lo:lo + QK_NOPE] = kn[:, hd * QK_NOPE:(hd + 1) * QK_NOPE].astype(MXU_DTYPE)
            k_ref[:, lo + QK_NOPE:lo + HEAD_PAD] = k_rot

    return pl.pallas_call(
        body,
        name="mla_pre",
        grid=(t // tm,),
        in_specs=[
            _rows(tm, D_MODEL),
            _resident((1, D_MODEL)),
            _resident((D_MODEL, A_PAD)),
            _resident((1, Q_LORA)),
            _resident((1, KV_LORA)),
            _resident((Q_LORA, qw)),
            _resident((KV_LORA, vw)),
            _resident((KV_LORA, vw)),
            _rows(tm, 128),
            _rows(tm, 128),
            _rows(tm, 128),
        ],
        out_specs=[_rows(tm, D_MODEL), _rows(tm, A_PAD), _rows(tm, Q_LORA), _rows(tm, KV_LORA), _rows(tm, qw), _rows(tm, qw), _rows(tm, vw)],
        out_shape=[
            jax.ShapeDtypeStruct((t, D_MODEL), MXU_DTYPE),
            jax.ShapeDtypeStruct((t, A_PAD), F32),
            jax.ShapeDtypeStruct((t, Q_LORA), MXU_DTYPE),
            jax.ShapeDtypeStruct((t, KV_LORA), MXU_DTYPE),
            jax.ShapeDtypeStruct((t, qw), MXU_DTYPE),
            jax.ShapeDtypeStruct((t, qw), MXU_DTYPE),
            jax.ShapeDtypeStruct((t, vw), MXU_DTYPE),
        ],
        compiler_params=_cparams("parallel"),
    )(h, g, w_a, g_q, g_kv, w_uq, w_uk, w_uv, cs, sa, sb)


def _attn_tile(t):
    return min(1024, t)


def _chunk_mask(qi, ki, tq, tk):
    rows = qi * tq + lax.broadcasted_iota(jnp.int32, (tq, tk), 0)
    cols = ki * tk + lax.broadcasted_iota(jnp.int32, (tq, tk), 1)
    return (rows >> CHUNK_SHIFT) >= (cols >> CHUNK_SHIFT)


def _flash_fwd(q, k, v):
    t = q.shape[0]
    tq = tk = _attn_tile(t)
    nk = t // tk

    def body(q_ref, k_ref, v_ref, o_ref, lse_ref, m_s, l_s, acc):
        qi, ki = pl.program_id(1), pl.program_id(2)

        @pl.when(ki == 0)
        def _():
            m_s[...] = jnp.full_like(m_s, -jnp.inf)
            l_s[...] = jnp.zeros_like(l_s)
            acc[...] = jnp.zeros_like(acc)

        @pl.when(ki <= qi)
        def _():
            s = _dot_nt(q_ref[...], k_ref[...]) * ATTN_SCALE
            s = jnp.where(_chunk_mask(qi, ki, tq, tk), s, -jnp.inf)
            m_prev = m_s[...]
            m_new = jnp.maximum(m_prev, jnp.max(s, axis=1, keepdims=True))
            alpha = jnp.exp(m_prev - m_new)
            p = jnp.exp(s - m_new)
            l_s[...] = alpha * l_s[...] + jnp.sum(p, axis=1, keepdims=True)
            acc[...] = alpha * acc[...] + _dot(p.astype(MXU_DTYPE), v_ref[...])
            m_s[...] = m_new

        @pl.when(ki == nk - 1)
        def _():
            o_ref[...] = acc[...] / l_s[...]
            lse_ref[...] = jnp.broadcast_to(m_s[...] + jnp.log(l_s[...]), (tq, V_HEAD))

    return pl.pallas_call(
        body,
        name="flash_fwd",
        grid=(N_HEADS, t // tq, nk),
        in_specs=[
            pl.BlockSpec((tq, HEAD_PAD), lambda h, i, j: (i, h)),
            pl.BlockSpec((tk, HEAD_PAD), lambda h, i, j: (jnp.minimum(i, j), h)),
            pl.BlockSpec((tk, V_HEAD), lambda h, i, j: (jnp.minimum(i, j), h)),
        ],
        out_specs=[pl.BlockSpec((tq, V_HEAD), lambda h, i, j: (i, h)), pl.BlockSpec((tq, V_HEAD), lambda h, i, j: (i, h))],
        out_shape=[jax.ShapeDtypeStruct((t, N_HEADS * V_HEAD), F32), jax.ShapeDtypeStruct((t, N_HEADS * V_HEAD), F32)],
        scratch_shapes=[pltpu.VMEM((tq, 1), F32), pltpu.VMEM((tq, 1), F32), pltpu.VMEM((tq, V_HEAD), F32)],
        compiler_params=_cparams("parallel", "parallel", "arbitrary"),
    )(q, k, v)


def _flash_bwd(q, k, v, do, lse, delta):
    t = q.shape[0]
    tq = tk = _attn_tile(t)
    nq = t // tq

    def body(q_ref, k_ref, v_ref, do_ref, lse_ref, dl_ref, dq_ref, dk_ref, dv_ref, dk_acc, dv_acc):
        kj, qi = pl.program_id(1), pl.program_id(2)

        @pl.when((kj == 0) & (qi == 0))
        def _():
            dq_ref[...] = jnp.zeros_like(dq_ref)

        @pl.when(qi == 0)
        def _():
            dk_acc[...] = jnp.zeros_like(dk_acc)
            dv_acc[...] = jnp.zeros_like(dv_acc)

        @pl.when(qi >= kj)
        def _():
            qv, kv, dov = q_ref[...], k_ref[...], do_ref[...]
            s = _dot_nt(qv, kv) * ATTN_SCALE
            s = jnp.where(_chunk_mask(qi, kj, tq, tk), s, -jnp.inf)
            p = jnp.exp(s - lse_ref[:, :1])
            dp = _dot_nt(dov, v_ref[...])
            ds = (p * (dp - dl_ref[:, :1]) * ATTN_SCALE).astype(MXU_DTYPE)
            dv_acc[...] += _dot_tn(p.astype(MXU_DTYPE), dov)
            dk_acc[...] += _dot_tn(ds, qv)
            rows = pl.ds(pl.multiple_of(qi * tq, tq), tq)
            dq_ref[rows, :] += _dot(ds, kv)

        @pl.when(qi == nq - 1)
        def _():
            dk_ref[...] = dk_acc[...]
            dv_ref[...] = dv_acc[...]

    return pl.pallas_call(
        body,
        name="flash_bwd",
        grid=(N_HEADS, t // tk, nq),
        in_specs=[
            pl.BlockSpec((tq, HEAD_PAD), lambda h, j, i: (jnp.maximum(i, j), h)),
            pl.BlockSpec((tk, HEAD_PAD), lambda h, j, i: (j, h)),
            pl.BlockSpec((tk, V_HEAD), lambda h, j, i: (j, h)),
            pl.BlockSpec((tq, V_HEAD), lambda h, j, i: (jnp.maximum(i, j), h)),
            pl.BlockSpec((tq, V_HEAD), lambda h, j, i: (jnp.maximum(i, j), h)),
            pl.BlockSpec((tq, V_HEAD), lambda h, j, i: (jnp.maximum(i, j), h)),
        ],
        out_specs=[
            pl.BlockSpec((t, HEAD_PAD), lambda h, j, i: (0, h)),
            pl.BlockSpec((tk, HEAD_PAD), lambda h, j, i: (j, h)),
            pl.BlockSpec((tk, V_HEAD), lambda h, j, i: (j, h)),
        ],
        out_shape=[
            jax.ShapeDtypeStruct((t, N_HEADS * HEAD_PAD), F32),
            jax.ShapeDtypeStruct((t, N_HEADS * HEAD_PAD), F32),
            jax.ShapeDtypeStruct((t, N_HEADS * V_HEAD), F32),
        ],
        scratch_shapes=[pltpu.VMEM((tk, HEAD_PAD), F32), pltpu.VMEM((tk, V_HEAD), F32)],
        compiler_params=_cparams("arbitrary", "arbitrary", "arbitrary"),
    )(q, k, v, do, lse, delta)


def _attn_out(o, w_o, h):
    t = h.shape[0]
    tm = min(512, t)

    def body(o_ref, w_ref, h_ref, ho_ref):
        ho_ref[...] = h_ref[...] + _dot(o_ref[...].astype(MXU_DTYPE), w_ref[...])

    return pl.pallas_call(
        body,
        name="attn_out",
        grid=(t // tm,),
        in_specs=[_rows(tm, D_MODEL), _resident((D_MODEL, D_MODEL)), _rows(tm, D_MODEL)],
        out_specs=_rows(tm, D_MODEL),
        out_shape=jax.ShapeDtypeStruct((t, D_MODEL), F32),
        compiler_params=_cparams("parallel"),
    )(o, w_o, h)


def _attn_out_bwd(dout, o, w_o):
    t = dout.shape[0]
    tm = min(512, t)

    def body(d_ref, o_ref, w_ref, do_ref, dl_ref):
        do = _dot_nt(d_ref[...].astype(MXU_DTYPE), w_ref[...])
        do_ref[...] = do.astype(MXU_DTYPE)
        prod = do * o_ref[...]
        for hd in range(N_HEADS):
            lanes = slice(hd * V_HEAD, (hd + 1) * V_HEAD)
            dl_ref[:, lanes] = jnp.broadcast_to(jnp.sum(prod[:, lanes], axis=1, keepdims=True), (tm, V_HEAD))

    return pl.pallas_call(
        body,
        name="attn_out_bwd",
        grid=(t // tm,),
        in_specs=[_rows(tm, D_MODEL), _rows(tm, D_MODEL), _resident((D_MODEL, D_MODEL))],
        out_specs=[_rows(tm, D_MODEL), _rows(tm, D_MODEL)],
        out_shape=[jax.ShapeDtypeStruct((t, D_MODEL), MXU_DTYPE), jax.ShapeDtypeStruct((t, D_MODEL), F32)],
        compiler_params=_cparams("parallel"),
    )(dout, o, w_o)


def _mla_bwd_pre(dq, dk, dv, a, h, dout, g, g_q, g_kv, w_a, w_uq, w_uk, w_uv, cs, sa, sb):
    t = h.shape[0]
    tm = min(512, t)
    qw = N_HEADS * HEAD_PAD
    vw = N_HEADS * V_HEAD

    def body(dq_ref, dk_ref, dv_ref, a_ref, h_ref, do_ref, g_ref, gq_ref, gkv_ref, wa_ref, wuq_ref, wuk_ref, wuv_ref,
             cs_ref, sa_ref, sb_ref, dh_ref, dqp_ref, dkn_ref, dvb_ref, da_ref, dg_ref, dgq_ref, dgkv_ref):
        @pl.when(pl.program_id(0) == 0)
        def _():
            dg_ref[...] = jnp.zeros_like(dg_ref)
            dgq_ref[...] = jnp.zeros_like(dgq_ref)
            dgkv_ref[...] = jnp.zeros_like(dgkv_ref)

        cs_, sa_, sb_ = cs_ref[...], sa_ref[...], sb_ref[...]
        slab = jnp.zeros((tm, 128), F32)
        for hd in range(N_HEADS):
            lo = hd * HEAD_PAD
            dqp_ref[:, lo:lo + QK_NOPE] = dq_ref[:, lo:lo + QK_NOPE].astype(MXU_DTYPE)
            dqp_ref[:, lo + QK_NOPE:lo + HEAD_PAD] = _unrope(dq_ref[:, lo + QK_NOPE:lo + HEAD_PAD], cs_, sa_, sb_).astype(MXU_DTYPE)
            dkn_ref[:, hd * QK_NOPE:(hd + 1) * QK_NOPE] = dk_ref[:, lo:lo + QK_NOPE].astype(MXU_DTYPE)
            slab = slab + dk_ref[:, lo + QK_NOPE:lo + HEAD_PAD]
        dvb_ref[...] = dv_ref[...].astype(MXU_DTYPE)
        dcq = _dot_nt(dqp_ref[...], wuq_ref[...])
        dckv = _dot_nt(dkn_ref[...], wuk_ref[...]) + _dot_nt(dvb_ref[...], wuv_ref[...])
        av = a_ref[...]
        daq, dgq = _rms_bwd(av[:, :Q_LORA], gq_ref[...], dcq)
        dakv, dgkv = _rms_bwd(av[:, Q_LORA:Q_LORA + KV_LORA], gkv_ref[...], dckv)
        da_ref[:, :Q_LORA] = daq.astype(MXU_DTYPE)
        da_ref[:, Q_LORA:Q_LORA + KV_LORA] = dakv.astype(MXU_DTYPE)
        da_ref[:, Q_LORA + KV_LORA:] = _unrope(slab, cs_, sa_, sb_).astype(MXU_DTYPE)
        dn = _dot_nt(da_ref[...], wa_ref[...])
        dx, dg = _rms_bwd(h_ref[...], g_ref[...], dn)
        dh_ref[...] = do_ref[...] + dx
        dg_ref[...] += dg
        dgq_ref[...] += dgq
        dgkv_ref[...] += dgkv

    def const(n):
        return pl.BlockSpec((1, n), lambda i: (0, 0))

    return pl.pallas_call(
        body,
        name="mla_bwd_pre",
        grid=(t // tm,),
        in_specs=[
            _rows(tm, qw), _rows(tm, qw), _rows(tm, vw), _rows(tm, A_PAD), _rows(tm, D_MODEL), _rows(tm, D_MODEL),
            _resident((1, D_MODEL)), _resident((1, Q_LORA)), _resident((1, KV_LORA)),
            _resident((D_MODEL, A_PAD)), _resident((Q_LORA, qw)), _resident((KV_LORA, vw)), _resident((KV_LORA, vw)),
            _rows(tm, 128), _rows(tm, 128), _rows(tm, 128),
        ],
        out_specs=[_rows(tm, D_MODEL), _rows(tm, qw), _rows(tm, vw), _rows(tm, vw), _rows(tm, A_PAD),
                   const(D_MODEL), const(Q_LORA), const(KV_LORA)],
        out_shape=[
            jax.ShapeDtypeStruct((t, D_MODEL), F32),
            jax.ShapeDtypeStruct((t, qw), MXU_DTYPE),
            jax.ShapeDtypeStruct((t, vw), MXU_DTYPE),
            jax.ShapeDtypeStruct((t, vw), MXU_DTYPE),
            jax.ShapeDtypeStruct((t, A_PAD), MXU_DTYPE),
            jax.ShapeDtypeStruct((1, D_MODEL), F32),
            jax.ShapeDtypeStruct((1, Q_LORA), F32),
            jax.ShapeDtypeStruct((1, KV_LORA), F32),
        ],
        compiler_params=_cparams("arbitrary"),
    )(dq, dk, dv, a, h, dout, g, g_q, g_kv, w_a, w_uq, w_uk, w_uv, cs, sa, sb)


def _loss_head(h, g, target):
    t = h.shape[0]
    tm = min(512, t)

    def body(h_ref, g_ref, t_ref, sq_ref, dh_ref, dg_ref):
        @pl.when(pl.program_id(0) == 0)
        def _():
            sq_ref[...] = jnp.zeros_like(sq_ref)
            dg_ref[...] = jnp.zeros_like(dg_ref)

        x = h_ref[...]
        err = _rms(x, g_ref[...]) - t_ref[...]
        sq_ref[...] += jnp.sum(err * err, axis=0, keepdims=True)
        dx, dg = _rms_bwd(x, g_ref[...], err * (1.0 / D_MODEL))
        dh_ref[...] = dx
        dg_ref[...] += dg

    return pl.pallas_call(
        body,
        name="loss_head",
        grid=(t // tm,),
        in_specs=[_rows(tm, D_MODEL), _resident((1, D_MODEL)), _rows(tm, D_MODEL)],
        out_specs=[pl.BlockSpec((1, D_MODEL), lambda i: (0, 0)), _rows(tm, D_MODEL), pl.BlockSpec((1, D_MODEL), lambda i: (0, 0))],
        out_shape=[jax.ShapeDtypeStruct((1, D_MODEL), F32), jax.ShapeDtypeStruct((t, D_MODEL), F32), jax.ShapeDtypeStruct((1, D_MODEL), F32)],
        compiler_params=_cparams("arbitrary"),
    )(h, g, target)


def _adamw(parts, w, m, v, name):
    rows, width = w.shape
    tr = min(PACK_ROW_TILE // 2, rows)

    def body(p_ref, w_ref, m_ref, v_ref, g_ref, d_ref, mo_ref, vo_ref):
        g = p_ref[0].astype(F32)
        for q in range(1, N_DEV):
            g = g + p_ref[q].astype(F32)
        g_ref[...] = g
        m_new = ADAM_B1 * m_ref[...] + (1.0 - ADAM_B1) * g
        v_new = ADAM_B2 * v_ref[...] + (1.0 - ADAM_B2) * (g * g)
        m_hat = m_new / (1.0 - ADAM_B1 ** ADAM_STEP)
        v_hat = v_new / (1.0 - ADAM_B2 ** ADAM_STEP)
        d_ref[...] = -ADAM_LR * (m_hat / (jnp.sqrt(v_hat) + ADAM_EPS) + ADAM_WD * w_ref[...])
        mo_ref[...] = m_new
        vo_ref[...] = v_new

    blk = pl.BlockSpec((tr, width), lambda i: (i, 0))
    return pl.pallas_call(
        body,
        name=name,
        grid=(rows // tr,),
        in_specs=[pl.BlockSpec((N_DEV, tr, width), lambda i: (0, i, 0)), blk, blk, blk],
        out_specs=[blk, blk, blk, blk],
        out_shape=[jax.ShapeDtypeStruct((rows, width), F32)] * 4,
        compiler_params=_cparams("parallel"),
    )(parts, w, m, v)


_SHARD_AXIS = {
    "ffn1_w1": 2, "ffn1_w3": 2, "ffn1_w2": 1, "ffn2_w1": 2, "ffn2_w3": 2, "ffn2_w2": 1,
    "conv_w_pw1": 2, "conv_w_dw": 2, "conv_w_pw2": 1, "mla_w_a": 1, "mla_q_norm": 1, "mla_kv_norm": 1,
    "mla_w_uq": 1, "mla_w_ukv": 1, "mla_w_o": 1,
}
_SHARDED = tuple(_SHARD_AXIS)
_REPLICATED = ("ffn_norm1", "mix_norm", "ffn_norm2", "conv_norm", "final_norm")
_F32_GATHERED = ("conv_w_dw", "mla_q_norm", "mla_kv_norm")
_SMALL_ROWS = 8


def _pack(shards, names, dtype, rows=None):
    flat = jnp.concatenate([shards[n].reshape(-1).astype(dtype) for n in names])
    if rows is None:
        rows = -(-flat.shape[0] // (PACK_W * PACK_ROW_TILE)) * PACK_ROW_TILE
    return jnp.pad(flat, (0, rows * PACK_W - flat.shape[0])).reshape(rows, PACK_W)


def _unpack(packed, like, names):
    lead = packed.shape[:-2]
    flat = packed.reshape(lead + (-1,))
    out, off = {}, 0
    for n in names:
        size = 1
        for s in like[n]:
            size *= s
        out[n] = flat[..., off:off + size].reshape(lead + tuple(like[n]))
        off += size
    return out


def _join(gathered, axis):
    g = jnp.moveaxis(gathered, 0, axis)
    return g.reshape(g.shape[:axis] + (g.shape[axis] * g.shape[axis + 1],) + g.shape[axis + 2:])


def _split(full, axis):
    s = full.shape
    return jnp.moveaxis(full.reshape(s[:axis] + (N_DEV, s[axis] // N_DEV) + s[axis + 1:]), axis, 0)


def kernel(x, positions, ffn_norm1, ffn1_w1, ffn1_w3, ffn1_w2, mix_norm, ffn_norm2, ffn2_w1, ffn2_w3, ffn2_w2, conv_w_pw1, conv_w_dw, conv_norm, conv_w_pw2, mla_w_a, mla_q_norm, mla_kv_norm, mla_w_uq, mla_w_ukv, mla_w_o, final_norm, loss_target, m_ffn_norm1, m_ffn1_w1, m_ffn1_w3, m_ffn1_w2, m_mix_norm, m_ffn_norm2, m_ffn2_w1, m_ffn2_w3, m_ffn2_w2, m_conv_w_pw1, m_conv_w_dw, m_conv_norm, m_conv_w_pw2, m_mla_w_a, m_mla_q_norm, m_mla_kv_norm, m_mla_w_uq, m_mla_w_ukv, m_mla_w_o, m_final_norm, v_ffn_norm1, v_ffn1_w1, v_ffn1_w3, v_ffn1_w2, v_mix_norm, v_ffn_norm2, v_ffn2_w1, v_ffn2_w3, v_ffn2_w2, v_conv_w_pw1, v_conv_w_dw, v_conv_norm, v_conv_w_pw2, v_mla_w_a, v_mla_q_norm, v_mla_kv_norm, v_mla_w_uq, v_mla_w_ukv, v_mla_w_o, v_final_norm):
    args = dict(locals())
    w_sh = {n: args[n] for n in _SHARDED}
    m_sh = {n: args["m_" + n] for n in _SHARDED}
    v_sh = {n: args["v_" + n] for n in _SHARDED}
    shard_shape = {n: w_sh[n].shape for n in _SHARDED}
    f32_shape = {n: shard_shape[n] for n in _F32_GATHERED}
    rep_shape = {n: args[n].shape for n in _REPLICATED}

    w_pack = _pack(w_sh, _SHARDED, F32)
    gathered, gathered_f32 = _exchange(
        w_pack.astype(WIRE_DTYPE), _pack(w_sh, _F32_GATHERED, F32, rows=_SMALL_ROWS), per_peer=False, name="weight_all_gather")
    full = {n: _join(p, _SHARD_AXIS[n]) for n, p in _unpack(gathered, shard_shape, _SHARDED).items()}
    full.update({n: _join(p, _SHARD_AXIS[n]) for n, p in _unpack(gathered_f32, f32_shape, _F32_GATHERED).items()})
    mx = lambda a: a.astype(MXU_DTYPE)
    w_dw = full["conv_w_dw"][0]
    w_pw1, w_pw2 = mx(full["conv_w_pw1"][0]), mx(full["conv_w_pw2"][0])
    w_a = mx(jnp.pad(full["mla_w_a"][0], ((0, 0), (0, A_PAD - (Q_LORA + KV_LORA + QK_ROPE)))))
    g_q, g_kv = full["mla_q_norm"], full["mla_kv_norm"]
    w_uq = mx(jnp.pad(full["mla_w_uq"][0], ((0, 0), (0, 0), (0, HEAD_PAD - QK_NOPE - QK_ROPE))).reshape(Q_LORA, N_HEADS * HEAD_PAD))
    w_uk = mx(full["mla_w_ukv"][0][:, :, :QK_NOPE].reshape(KV_LORA, N_HEADS * QK_NOPE))
    w_uv = mx(full["mla_w_ukv"][0][:, :, QK_NOPE:].reshape(KV_LORA, N_HEADS * V_HEAD))
    w_o = mx(full["mla_w_o"][0])

    inv_freq = ROPE_THETA ** (-2.0 * jnp.arange(QK_ROPE // 2, dtype=F32) / QK_ROPE)
    ang = positions[0].astype(F32)[:, None] * inv_freq
    cos, sin, zero = jnp.cos(ang), jnp.sin(ang), jnp.zeros_like(ang)
    cs = jnp.concatenate([cos, cos, zero, zero], axis=1)
    sa = jnp.concatenate([sin, zero, zero, zero], axis=1)
    sb = jnp.concatenate([zero, sin, zero, zero], axis=1)

    def ffn_w(prefix, layer):
        return mx(full[prefix + "_w1"][layer]), mx(full[prefix + "_w3"][layer]), mx(full[prefix + "_w2"][layer])

    h0 = x[0]
    saved = []
    h = h0
    for layer in range(2):
        f1 = ffn_w("ffn1", layer)
        h_in = h
        h, n1, a1, b1 = _ffn_fwd(h_in, ffn_norm1[layer:layer + 1], *f1)
        h_mix = h
        if layer == 0:
            n_m, u, glu = _conv_pre(h_mix, mix_norm[0:1], w_pw1)
            h, c, sw = _conv_main(glu, w_dw, conv_norm, w_pw2, h_mix)
            mixer = (n_m, u, glu, c, sw)
        else:
            n_m, a_lat, cq, ckv, q, k, v = _mla_pre(h_mix, mix_norm[1:2], w_a, g_q, g_kv, w_uq, w_uk, w_uv, cs, sa, sb)
            o, lse = _flash_fwd(q, k, v)
            h = _attn_out(o, w_o, h_mix)
            mixer = (n_m, a_lat, cq, ckv, q, k, v, o, lse)
        f2 = ffn_w("ffn2", layer)
        h_mid = h
        h, n2, a2, b2 = _ffn_fwd(h_mid, ffn_norm2[layer:layer + 1], *f2)
        saved.append((h_in, f1, n1, a1, b1, h_mix, mixer, h_mid, f2, n2, a2, b2))

    sq, dh, dg_final = _loss_head(h, final_norm[None, :], loss_target[0])
    loss = lax.psum(0.5 / D_MODEL * jnp.sum(sq), ("x", "y", "c"))

    grads = {}
    gain = {}

    def ffn_backward(prefix, norm_name, layer, dh, h_in, weights, n, a, b, gain_vec):
        dh, dob, s, da, db, dg = _ffn_bwd(dh, h_in, gain_vec, a, b, *weights)
        gain[(norm_name, layer)] = dg
        grads[(prefix + "_w1", layer)] = _wgrad(n, da, "wgrad_ffn_in")
        grads[(prefix + "_w3", layer)] = _wgrad(n, db, "wgrad_ffn_in")
        grads[(prefix + "_w2", layer)] = _wgrad(s, dob, "wgrad_ffn_out")
        return dh

    for layer in (1, 0):
        h_in, f1, n1, a1, b1, h_mix, mixer, h_mid, f2, n2, a2, b2 = saved[layer]
        dh = ffn_backward("ffn2", "ffn_norm2", layer, dh, h_mid, f2, n2, a2, b2, ffn_norm2[layer:layer + 1])
        if layer == 0:
            n_m, u, glu, c, sw = mixer
            dc, gain[("conv_norm", 0)] = _conv_bwd_post(dh, c, conv_norm, w_pw2)
            grads[("conv_w_pw2", 0)] = _wgrad(sw, dh, "wgrad_conv_pw2")
            dh, du, d_dw, gain[("mix_norm", 0)] = _conv_bwd_pre(dc, glu, u, w_dw, w_pw1, h_mix, mix_norm[0:1], dh)
            grads[("conv_w_pw1", 0)] = _wgrad(n_m, du, "wgrad_conv_pw1")
            grads[("conv_w_dw", 0)] = d_dw[:CONV_WIDTH]
        else:
            n_m, a_lat, cq, ckv, q, k, v, o, lse = mixer
            do, delta = _attn_out_bwd(dh, o, w_o)
            grads[("mla_w_o", 0)] = _wgrad(o, dh, "wgrad_attn_out")
            dq, dk, dv = _flash_bwd(q, k, v, do, lse, delta)
            dh, dqp, dkn, dvb, da_lat, gain[("mix_norm", 1)], d_gq, d_gkv = _mla_bwd_pre(
                dq, dk, dv, a_lat, h_mix, dh, mix_norm[1:2], g_q, g_kv, w_a, w_uq, w_uk, w_uv, cs, sa, sb)
            grads[("mla_w_a", 0)] = _wgrad(n_m, da_lat, "wgrad_mla_a")[:, :Q_LORA + KV_LORA + QK_ROPE]
            d_uq = _wgrad(cq, dqp, "wgrad_mla_uq").reshape(Q_LORA, N_HEADS, HEAD_PAD)
            grads[("mla_w_uq", 0)] = d_uq[:, :, :QK_NOPE + QK_ROPE]
            d_uk = _wgrad(ckv, dkn, "wgrad_mla_ukv").reshape(KV_LORA, N_HEADS, QK_NOPE)
            d_uv = _wgrad(ckv, dvb, "wgrad_mla_ukv").reshape(KV_LORA, N_HEADS, V_HEAD)
            grads[("mla_w_ukv", 0)] = jnp.concatenate([d_uk, d_uv], axis=-1)
            grads[("mla_q_norm", 0)] = d_gq[0]
            grads[("mla_kv_norm", 0)] = d_gkv[0]
        dh = ffn_backward("ffn1", "ffn_norm1", layer, dh, h_in, f1, n1, a1, b1, ffn_norm1[layer:layer + 1])
    grad_x = dh[None]

    def full_grad(n):
        layers = 2 if n.startswith("ffn") else 1
        return jnp.stack([grads[(n, layer)] for layer in range(layers)])

    by_dest = {n: _split(full_grad(n), _SHARD_AXIS[n]) for n in _SHARDED}
    rows = w_pack.shape[0]
    g_pack = jnp.stack([_pack({n: by_dest[n][p] for n in _SHARDED}, _SHARDED, WIRE_DTYPE, rows=rows) for p in range(N_DEV)])
    gain_rows = jnp.concatenate([
        gain[("ffn_norm1", 0)], gain[("ffn_norm1", 1)], gain[("mix_norm", 0)], gain[("mix_norm", 1)],
        gain[("ffn_norm2", 0)], gain[("ffn_norm2", 1)], gain[("conv_norm", 0)], dg_final])
    parts, gain_parts = _exchange(g_pack, gain_rows, per_peer=True, name="grad_exchange")

    big = _adamw(parts, w_pack, _pack(m_sh, _SHARDED, F32, rows=rows), _pack(v_sh, _SHARDED, F32, rows=rows), "adamw_sharded")
    rep = {n: args[n] for n in _REPLICATED}
    rep_m = {n: args["m_" + n] for n in _REPLICATED}
    rep_v = {n: args["v_" + n] for n in _REPLICATED}
    small = _adamw(gain_parts, _pack(rep, _REPLICATED, F32, rows=_SMALL_ROWS), _pack(rep_m, _REPLICATED, F32, rows=_SMALL_ROWS),
                   _pack(rep_v, _REPLICATED, F32, rows=_SMALL_ROWS), "adamw_replicated")

    order = ("ffn_norm1", "ffn1_w1", "ffn1_w3", "ffn1_w2", "mix_norm", "ffn_norm2", "ffn2_w1", "ffn2_w3", "ffn2_w2",
             "conv_w_pw1", "conv_w_dw", "conv_norm", "conv_w_pw2", "mla_w_a", "mla_q_norm", "mla_kv_norm",
             "mla_w_uq", "mla_w_ukv", "mla_w_o", "final_norm")
    outs = [loss, grad_x]
    for kind in range(4):
        leaves = _unpack(big[kind], shard_shape, _SHARDED)
        leaves.update(_unpack(small[kind], rep_shape, _REPLICATED))
        outs.extend(leaves[n] for n in order)
    return tuple(outs)
```

```python
import jax
import jax.numpy as jnp
from jax import lax
from jax.experimental import pallas as pl
from jax.experimental.pallas import tpu as pltpu

F32 = jnp.float32
MXU_DTYPE = jnp.bfloat16
WIRE_DTYPE = jnp.bfloat16

N_DEV = 8
D_MODEL = 1024
FF_SHARD = 352
FF_SHARD_PAD = 384
D_FF_PAD = N_DEV * FF_SHARD_PAD
N_HEADS = 8
QK_NOPE = 128
QK_ROPE = 64
V_HEAD = 128
Q_LORA = 512
KV_LORA = 256
HEAD_PAD = 256
A_WIDTH = Q_LORA + KV_LORA + QK_ROPE
A_PAD = Q_LORA + KV_LORA + 128
CONV_WIDTH = 31
CONV_HALO = 32
CHUNK_SHIFT = 6
ROPE_THETA = 10000.0
RMS_EPS = 1e-6
ATTN_SCALE = (QK_NOPE + QK_ROPE) ** -0.5
FFN_RES_WEIGHT = 0.5
ADAM_LR = 0.001
ADAM_B1 = 0.9
ADAM_B2 = 0.999
ADAM_EPS = 1e-08
ADAM_WD = 0.01
ADAM_STEP = 10

SMALL_ROWS = 16
VMEM_LIMIT = 56 << 20


def _cparams(*sem):
    return pltpu.CompilerParams(dimension_semantics=sem, vmem_limit_bytes=VMEM_LIMIT)


def _dot(a, b):
    return lax.dot_general(a, b, (((1,), (0,)), ((), ())), preferred_element_type=F32)


def _dot_nt(a, b):
    return lax.dot_general(a, b, (((1,), (1,)), ((), ())), preferred_element_type=F32)


def _dot_tn(a, b):
    return lax.dot_general(a, b, (((0,), (0,)), ((), ())), preferred_element_type=F32)


def _resident(shape, index=None):
    fixed = index if index is not None else (0,) * len(shape)
    return pl.BlockSpec(shape, lambda *_: fixed, pipeline_mode=pl.Buffered(1))


def _rows(tm, n):
    return pl.BlockSpec((tm, n), lambda t, *_: (t, 0))


def _rms(x, g):
    r = lax.rsqrt(jnp.mean(x * x, axis=-1, keepdims=True) + RMS_EPS)
    return x * r * g


def _rms_bwd(x, g, dy):
    r = lax.rsqrt(jnp.mean(x * x, axis=-1, keepdims=True) + RMS_EPS)
    xr = x * r
    dg = jnp.sum(dy * xr, axis=0, keepdims=True)
    u = dy * g
    dx = r * (u - xr * jnp.mean(u * xr, axis=-1, keepdims=True))
    return dx, dg


def _sigmoid(x):
    return 1.0 / (1.0 + jnp.exp(-x))


def _exchange(streams, outs, name):
    n, n_out = len(streams), len(outs)
    out_of = [st[1] for st in streams]
    sends_of = [st[2] for st in streams]
    lands_of = [st[3] for st in streams]

    def body(*refs):
        srcs, dsts = refs[:n], refs[n:n + n_out]
        send_sem, recv_sem, local_sem = refs[n + n_out:]
        x, y, c = lax.axis_index("x"), lax.axis_index("y"), lax.axis_index("c")
        me = 4 * x + 2 * y + c

        def src(e, p):
            return sends_of[e](srcs[e], p)

        def dst(e, q):
            return lands_of[e](dsts[out_of[e]], q)

        own = [pltpu.make_async_copy(src(e, me), dst(e, me), local_sem.at[e]) for e in range(n)]
        for cp in own:
            cp.start()
        sends, recvs = [], []
        for k in range(1, N_DEV):
            px, py, pc = x ^ ((k >> 2) & 1), y ^ ((k >> 1) & 1), c ^ (k & 1)
            p = 4 * px + 2 * py + pc
            peer = dict(device_id=(px, py, pc), device_id_type=pl.DeviceIdType.MESH)
            for e in range(n):
                sem = dict(send_sem=send_sem.at[e * (N_DEV - 1) + k - 1], recv_sem=recv_sem.at[e * (N_DEV - 1) + k - 1])
                sends.append(pltpu.make_async_remote_copy(src_ref=src(e, p), dst_ref=dst(e, me), **sem, **peer))
                recvs.append(pltpu.make_async_remote_copy(src_ref=src(e, p), dst_ref=dst(e, p), **sem, **peer))
        for cp in sends:
            cp.start()
        for cp in recvs:
            cp.wait_recv()
        for cp in sends:
            cp.wait_send()
        for cp in own:
            cp.wait()

    hbm = pl.BlockSpec(memory_space=pltpu.HBM)
    return pl.pallas_call(
        body,
        name=name,
        out_shape=tuple(outs),
        in_specs=[hbm] * n,
        out_specs=tuple([hbm] * n_out),
        scratch_shapes=[pltpu.SemaphoreType.DMA((n * (N_DEV - 1),)), pltpu.SemaphoreType.DMA((n * (N_DEV - 1),)),
                        pltpu.SemaphoreType.DMA((n,))],
        compiler_params=pltpu.CompilerParams(vmem_limit_bytes=VMEM_LIMIT),
    )(*[st[0] for st in streams])


def _whole(ref, _):
    return ref


def _slot(ref, q):
    return ref.at[q]


def _lane_block(width):
    def pick(ref, q):
        idx = (slice(None),) * (len(ref.shape) - 1) + (pl.ds(pl.multiple_of(q * width, 128), width),)
        return ref.at[idx]
    return pick


def _row_block(height):
    def pick(ref, q):
        idx = (slice(None),) * (len(ref.shape) - 2) + (pl.ds(pl.multiple_of(q * height, 16), height), slice(None))
        return ref.at[idx]
    return pick


def _block_of(axis_from_end, size):
    return _lane_block(size) if axis_from_end == 1 else _row_block(size)


def _all_gather(shards, name):
    streams, outs = [], []
    for i, (shard, axis) in enumerate(shards):
        shape = list(shard.shape)
        if axis is None:
            shape, land = [N_DEV] + shape, _slot
        else:
            land = _block_of(axis, shape[-axis])
            shape[-axis] *= N_DEV
        streams.append((shard, i, _whole, land))
        outs.append(jax.ShapeDtypeStruct(tuple(shape), shard.dtype))
    return _exchange(streams, outs, name)


FFN_TM = 512
FFN_TF = 768


def _ffn_fwd(h, g, w13, w2, layer):
    t = h.shape[0]
    tm, tf = min(FFN_TM, t), FFN_TF
    nf = D_FF_PAD // tf

    def body(h_ref, g_ref, w1_ref, w3_ref, w2_ref, ho_ref, n_ref, a_ref, b_ref, n_scr, acc):
        f = pl.program_id(1)

        @pl.when(f == 0)
        def _():
            n_scr[...] = _rms(h_ref[...], g_ref[...]).astype(MXU_DTYPE)
            n_ref[...] = n_scr[...]
            acc[...] = jnp.zeros_like(acc)

        n = n_scr[...]
        a = _dot(n, w1_ref[...])
        b = _dot(n, w3_ref[...])
        a_ref[...] = a.astype(MXU_DTYPE)
        b_ref[...] = b.astype(MXU_DTYPE)
        s = (a * _sigmoid(a)) * b
        acc[...] += _dot(s.astype(MXU_DTYPE), w2_ref[...])

        @pl.when(f == nf - 1)
        def _():
            ho_ref[...] = h_ref[...] + FFN_RES_WEIGHT * acc[...]

    return pl.pallas_call(
        body,
        name="ffn_fwd",
        grid=(t // tm, nf),
        in_specs=[
            _rows(tm, D_MODEL),
            _resident((1, D_MODEL)),
            pl.BlockSpec((None, None, D_MODEL, tf), lambda i, f: (layer, 0, 0, f)),
            pl.BlockSpec((None, None, D_MODEL, tf), lambda i, f: (layer, 1, 0, f)),
            pl.BlockSpec((None, tf, D_MODEL), lambda i, f: (layer, f, 0)),
        ],
        out_specs=[
            _rows(tm, D_MODEL),
            _rows(tm, D_MODEL),
            pl.BlockSpec((tm, tf), lambda i, f: (i, f)),
            pl.BlockSpec((tm, tf), lambda i, f: (i, f)),
        ],
        out_shape=[
            jax.ShapeDtypeStruct((t, D_MODEL), F32),
            jax.ShapeDtypeStruct((t, D_MODEL), MXU_DTYPE),
            jax.ShapeDtypeStruct((t, D_FF_PAD), MXU_DTYPE),
            jax.ShapeDtypeStruct((t, D_FF_PAD), MXU_DTYPE),
        ],
        scratch_shapes=[pltpu.VMEM((tm, D_MODEL), MXU_DTYPE), pltpu.VMEM((tm, D_MODEL), F32)],
        compiler_params=_cparams("parallel", "arbitrary"),
    )(h, g, w13, w13, w2)


def _ffn_bwd(dout, h, g, a, b, w13, w2, layer):
    t = h.shape[0]
    tm, tf = min(FFN_TM, t), FFN_TF
    nf = D_FF_PAD // tf

    def body(do_ref, h_ref, g_ref, a_ref, b_ref, w1_ref, w3_ref, w2_ref, dh_ref, dob_ref, s_ref, dab_ref, dg_ref, dob_scr, acc):
        i, f = pl.program_id(0), pl.program_id(1)

        @pl.when(f == 0)
        def _():
            dob_scr[...] = (FFN_RES_WEIGHT * do_ref[...]).astype(MXU_DTYPE)
            dob_ref[...] = dob_scr[...]
            acc[...] = jnp.zeros_like(acc)

        @pl.when((f == 0) & (i == 0))
        def _():
            dg_ref[...] = jnp.zeros_like(dg_ref)

        ds = _dot_nt(dob_scr[...], w2_ref[...])
        av = a_ref[...].astype(F32)
        bv = b_ref[...].astype(F32)
        sig = _sigmoid(av)
        sil = av * sig
        s_ref[...] = (sil * bv).astype(MXU_DTYPE)
        da = (ds * bv * (sig * (1.0 + av * (1.0 - sig)))).astype(MXU_DTYPE)
        db = (ds * sil).astype(MXU_DTYPE)
        dab_ref[0] = da
        dab_ref[1] = db
        acc[...] += _dot_nt(da, w1_ref[...]) + _dot_nt(db, w3_ref[...])

        @pl.when(f == nf - 1)
        def _():
            dx, dg = _rms_bwd(h_ref[...], g_ref[...], acc[...])
            dh_ref[...] = do_ref[...] + dx
            dg_ref[...] += dg

    return pl.pallas_call(
        body,
        name="ffn_bwd",
        grid=(t // tm, nf),
        in_specs=[
            _rows(tm, D_MODEL),
            _rows(tm, D_MODEL),
            _resident((1, D_MODEL)),
            pl.BlockSpec((tm, tf), lambda i, f: (i, f)),
            pl.BlockSpec((tm, tf), lambda i, f: (i, f)),
            pl.BlockSpec((None, None, D_MODEL, tf), lambda i, f: (layer, 0, 0, f)),
            pl.BlockSpec((None, None, D_MODEL, tf), lambda i, f: (layer, 1, 0, f)),
            pl.BlockSpec((None, tf, D_MODEL), lambda i, f: (layer, f, 0)),
        ],
        out_specs=[
            _rows(tm, D_MODEL),
            _rows(tm, D_MODEL),
            pl.BlockSpec((tm, tf), lambda i, f: (i, f)),
            pl.BlockSpec((2, tm, tf), lambda i, f: (0, i, f)),
            pl.BlockSpec((1, D_MODEL), lambda i, f: (0, 0)),
        ],
        out_shape=[
            jax.ShapeDtypeStruct((t, D_MODEL), F32),
            jax.ShapeDtypeStruct((t, D_MODEL), MXU_DTYPE),
            jax.ShapeDtypeStruct((t, D_FF_PAD), MXU_DTYPE),
            jax.ShapeDtypeStruct((2, t, D_FF_PAD), MXU_DTYPE),
            jax.ShapeDtypeStruct((1, D_MODEL), F32),
        ],
        scratch_shapes=[pltpu.VMEM((tm, D_MODEL), MXU_DTYPE), pltpu.VMEM((tm, D_MODEL), F32)],
        compiler_params=_cparams("arbitrary", "arbitrary"),
    )(dout, h, g, a, b, w13, w13, w2)


def _wgrad(x, y, name, out_dtype=WIRE_DTYPE):
    t, m = x.shape
    grouped = y.ndim == 3
    groups = y.shape[0] if grouped else 1
    n = y.shape[-1]
    tk = min(1024, t)
    bm = m if m <= 1536 else m // 2
    bn = n if n <= 1536 else n // 2
    nk = t // tk

    def body(x_ref, y_ref, o_ref, acc):
        k = pl.program_id(3)

        @pl.when(k == 0)
        def _():
            acc[...] = jnp.zeros_like(acc)

        acc[...] += _dot_tn(x_ref[...].astype(MXU_DTYPE), y_ref[...].astype(MXU_DTYPE))

        @pl.when(k == nk - 1)
        def _():
            o_ref[...] = acc[...].astype(out_dtype)

    if grouped:
        y_spec = pl.BlockSpec((None, tk, bn), lambda g, i, j, k: (g, k, j))
        o_spec = pl.BlockSpec((None, bm, bn), lambda g, i, j, k: (g, i, j))
        o_shape = jax.ShapeDtypeStruct((groups, m, n), out_dtype)
    else:
        y_spec = pl.BlockSpec((tk, bn), lambda g, i, j, k: (k, j))
        o_spec = pl.BlockSpec((bm, bn), lambda g, i, j, k: (i, j))
        o_shape = jax.ShapeDtypeStruct((m, n), out_dtype)
    return pl.pallas_call(
        body,
        name=name,
        grid=(groups, m // bm, n // bn, nk),
        in_specs=[pl.BlockSpec((tk, bm), lambda g, i, j, k: (k, i)), y_spec],
        out_specs=o_spec,
        out_shape=o_shape,
        scratch_shapes=[pltpu.VMEM((bm, bn), F32)],
        compiler_params=_cparams("parallel", "parallel", "parallel", "arbitrary"),
    )(x, y)


def _conv_pre(h, g, w_pw1):
    t = h.shape[0]
    tm = min(512, t)

    def body(h_ref, g_ref, w_ref, n_ref, u_ref, glu_ref):
        n = _rms(h_ref[...], g_ref[...]).astype(MXU_DTYPE)
        n_ref[...] = n
        u = _dot(n, w_ref[...])
        u_ref[...] = u
        glu_ref[...] = u[:, :D_MODEL] * _sigmoid(u[:, D_MODEL:])

    return pl.pallas_call(
        body,
        name="conv_pre",
        grid=(t // tm,),
        in_specs=[_rows(tm, D_MODEL), _resident((1, D_MODEL)), _resident((D_MODEL, 2 * D_MODEL))],
        out_specs=[_rows(tm, D_MODEL), _rows(tm, 2 * D_MODEL), _rows(tm, D_MODEL)],
        out_shape=[
            jax.ShapeDtypeStruct((t, D_MODEL), MXU_DTYPE),
            jax.ShapeDtypeStruct((t, 2 * D_MODEL), F32),
            jax.ShapeDtypeStruct((t, D_MODEL), F32),
        ],
        compiler_params=_cparams("parallel"),
    )(h, g, w_pw1)


def _conv_main(glu, w_dw, g, w_pw2, h):
    t = h.shape[0]
    tm = min(512, t)
    per = tm // CONV_HALO

    def body(glu_ref, halo_ref, w_ref, g_ref, w2_ref, h_ref, ho_ref, c_ref, sw_ref, ext):
        i = pl.program_id(0)
        ext[pl.ds(0, CONV_HALO), :] = jnp.where(i == 0, 0.0, halo_ref[...])
        ext[pl.ds(CONV_HALO, tm), :] = glu_ref[...]
        acc = jnp.zeros((tm, D_MODEL), F32)
        for k in range(CONV_WIDTH):
            acc = acc + ext[pl.ds(CONV_HALO - (CONV_WIDTH - 1) + k, tm), :] * w_ref[pl.ds(k, 1), :]
        c_ref[...] = acc
        y = _rms(acc, g_ref[...])
        sw = (y * _sigmoid(y)).astype(MXU_DTYPE)
        sw_ref[...] = sw
        ho_ref[...] = h_ref[...] + _dot(sw, w2_ref[...])

    return pl.pallas_call(
        body,
        name="conv_main",
        grid=(t // tm,),
        in_specs=[
            _rows(tm, D_MODEL),
            pl.BlockSpec((CONV_HALO, D_MODEL), lambda i: (jnp.maximum(i * per - 1, 0), 0)),
            _resident((CONV_WIDTH, D_MODEL)),
            _resident((1, D_MODEL)),
            _resident((D_MODEL, D_MODEL)),
            _rows(tm, D_MODEL),
        ],
        out_specs=[_rows(tm, D_MODEL), _rows(tm, D_MODEL), _rows(tm, D_MODEL)],
        out_shape=[
            jax.ShapeDtypeStruct((t, D_MODEL), F32),
            jax.ShapeDtypeStruct((t, D_MODEL), F32),
            jax.ShapeDtypeStruct((t, D_MODEL), MXU_DTYPE),
        ],
        scratch_shapes=[pltpu.VMEM((tm + CONV_HALO, D_MODEL), F32)],
        compiler_params=_cparams("parallel"),
    )(glu, glu, w_dw, g, w_pw2, h)


def _conv_bwd_post(dout, c, g, w_pw2):
    t = dout.shape[0]
    tm = min(512, t)

    def body(do_ref, c_ref, g_ref, w2_ref, dc_ref, dg_ref):
        @pl.when(pl.program_id(0) == 0)
        def _():
            dg_ref[...] = jnp.zeros_like(dg_ref)

        dsw = _dot_nt(do_ref[...].astype(MXU_DTYPE), w2_ref[...])
        cv = c_ref[...]
        y = _rms(cv, g_ref[...])
        sig = _sigmoid(y)
        dy = dsw * (sig * (1.0 + y * (1.0 - sig)))
        dc, dg = _rms_bwd(cv, g_ref[...], dy)
        dc_ref[...] = dc
        dg_ref[...] += dg

    return pl.pallas_call(
        body,
        name="conv_bwd_post",
        grid=(t // tm,),
        in_specs=[_rows(tm, D_MODEL), _rows(tm, D_MODEL), _resident((1, D_MODEL)), _resident((D_MODEL, D_MODEL))],
        out_specs=[_rows(tm, D_MODEL), pl.BlockSpec((1, D_MODEL), lambda i: (0, 0))],
        out_shape=[jax.ShapeDtypeStruct((t, D_MODEL), F32), jax.ShapeDtypeStruct((1, D_MODEL), F32)],
        compiler_params=_cparams("arbitrary"),
    )(dout, c, g, w_pw2)


def _conv_bwd_pre(dc, glu, u, w_dw, w_pw1, h, g, dout):
    t = h.shape[0]
    tm = min(512, t)
    per = tm // CONV_HALO
    nt = t // tm
    last_halo = t // CONV_HALO - 1

    def body(dc_ref, dcn_ref, glu_ref, gluh_ref, u_ref, w_ref, w1_ref, h_ref, g_ref, do_ref,
             dh_ref, du_ref, dw_ref, dg_ref, dext, gext):
        i = pl.program_id(0)

        @pl.when(i == 0)
        def _():
            dw_ref[...] = jnp.zeros_like(dw_ref)
            dg_ref[...] = jnp.zeros_like(dg_ref)

        dext[pl.ds(0, tm), :] = dc_ref[...]
        dext[pl.ds(tm, CONV_HALO), :] = jnp.where(i == nt - 1, 0.0, dcn_ref[...])
        gext[pl.ds(0, CONV_HALO), :] = jnp.where(i == 0, 0.0, gluh_ref[...])
        gext[pl.ds(CONV_HALO, tm), :] = glu_ref[...]
        dcv = dc_ref[...]
        dglu = jnp.zeros((tm, D_MODEL), F32)
        for k in range(CONV_WIDTH):
            dglu = dglu + dext[pl.ds(CONV_WIDTH - 1 - k, tm), :] * w_ref[pl.ds(k, 1), :]
            prod = dcv * gext[pl.ds(CONV_HALO - (CONV_WIDTH - 1) + k, tm), :]
            dw_ref[pl.ds(k, 1), :] += jnp.sum(prod, axis=0, keepdims=True)
        uv = u_ref[...]
        av, bv = uv[:, :D_MODEL], uv[:, D_MODEL:]
        sig = _sigmoid(bv)
        du_ref[:, :D_MODEL] = (dglu * sig).astype(MXU_DTYPE)
        du_ref[:, D_MODEL:] = (dglu * av * (sig * (1.0 - sig))).astype(MXU_DTYPE)
        dn = _dot_nt(du_ref[...], w1_ref[...])
        dx, dg = _rms_bwd(h_ref[...], g_ref[...], dn)
        dh_ref[...] = do_ref[...] + dx
        dg_ref[...] += dg

    return pl.pallas_call(
        body,
        name="conv_bwd_pre",
        grid=(nt,),
        in_specs=[
            _rows(tm, D_MODEL),
            pl.BlockSpec((CONV_HALO, D_MODEL), lambda i: (jnp.minimum((i + 1) * per, last_halo), 0)),
            _rows(tm, D_MODEL),
            pl.BlockSpec((CONV_HALO, D_MODEL), lambda i: (jnp.maximum(i * per - 1, 0), 0)),
            _rows(tm, 2 * D_MODEL),
            _resident((CONV_WIDTH, D_MODEL)),
            _resident((D_MODEL, 2 * D_MODEL)),
            _rows(tm, D_MODEL),
            _resident((1, D_MODEL)),
            _rows(tm, D_MODEL),
        ],
        out_specs=[
            _rows(tm, D_MODEL),
            _rows(tm, 2 * D_MODEL),
            pl.BlockSpec((CONV_HALO, D_MODEL), lambda i: (0, 0)),
            pl.BlockSpec((1, D_MODEL), lambda i: (0, 0)),
        ],
        out_shape=[
            jax.ShapeDtypeStruct((t, D_MODEL), F32),
            jax.ShapeDtypeStruct((t, 2 * D_MODEL), MXU_DTYPE),
            jax.ShapeDtypeStruct((CONV_HALO, D_MODEL), F32),
            jax.ShapeDtypeStruct((1, D_MODEL), F32),
        ],
        scratch_shapes=[pltpu.VMEM((tm + CONV_HALO, D_MODEL), F32), pltpu.VMEM((tm + CONV_HALO, D_MODEL), F32)],
        compiler_params=_cparams("arbitrary"),
    )(dc, dc, glu, glu, u, w_dw, w_pw1, h, g, dout)


def _rope(s, cs, sa, sb):
    return s * cs - pltpu.roll(s, 96, 1) * sa + pltpu.roll(s, 32, 1) * sb


def _unrope(d, cs, sa, sb):
    return d * cs + pltpu.roll(d, 96, 1) * sa - pltpu.roll(d, 32, 1) * sb


def _mla_pre(h, g, w_a, g_q, g_kv, w_uq, w_ukv, cs, sa, sb):
    t = h.shape[0]
    tm = min(512, t)
    qw = N_HEADS * HEAD_PAD
    vw = N_HEADS * V_HEAD

    def body(h_ref, g_ref, wa_ref, gq_ref, gkv_ref, wuq_ref, wukv_ref, cs_ref, sa_ref, sb_ref,
             n_ref, a_ref, cq_ref, ckv_ref, q_ref, k_ref, v_ref):
        n = _rms(h_ref[...], g_ref[...]).astype(MXU_DTYPE)
        n_ref[...] = n
        a = _dot(n, wa_ref[...])
        a_ref[...] = a
        cq = _rms(a[:, :Q_LORA], gq_ref[...]).astype(MXU_DTYPE)
        ckv = _rms(a[:, Q_LORA:Q_LORA + KV_LORA], gkv_ref[...]).astype(MXU_DTYPE)
        cq_ref[...] = cq
        ckv_ref[...] = ckv
        q = _dot(cq, wuq_ref[...])
        kv = _dot(ckv, wukv_ref[...])
        cs_, sa_, sb_ = cs_ref[...], sa_ref[...], sb_ref[...]
        k_rot = _rope(a[:, Q_LORA + KV_LORA:], cs_, sa_, sb_).astype(MXU_DTYPE)
        for hd in range(N_HEADS):
            lo = hd * HEAD_PAD
            q_ref[:, lo:lo + QK_NOPE] = q[:, lo:lo + QK_NOPE].astype(MXU_DTYPE)
            q_ref[:, lo + QK_NOPE:lo + HEAD_PAD] = _rope(q[:, lo + QK_NOPE:lo + HEAD_PAD], cs_, sa_, sb_).astype(MXU_DTYPE)
            k_ref[:, lo:lo + QK_NOPE] = kv[:, lo:lo + QK_NOPE].astype(MXU_DTYPE)
            k_ref[:, lo + QK_NOPE:lo + HEAD_PAD] = k_rot
            v_ref[:, hd * V_HEAD:(hd + 1) * V_HEAD] = kv[:, lo + QK_NOPE:lo + HEAD_PAD].astype(MXU_DTYPE)

    return pl.pallas_call(
        body,
        name="mla_pre",
        grid=(t // tm,),
        in_specs=[
            _rows(tm, D_MODEL),
            _resident((1, D_MODEL)),
            _resident((D_MODEL, A_PAD)),
            _resident((1, Q_LORA)),
            _resident((1, KV_LORA)),
            _resident((Q_LORA, qw)),
            _resident((KV_LORA, qw)),
            _rows(tm, 128),
            _rows(tm, 128),
            _rows(tm, 128),
        ],
        out_specs=[_rows(tm, D_MODEL), _rows(tm, A_PAD), _rows(tm, Q_LORA), _rows(tm, KV_LORA), _rows(tm, qw), _rows(tm, qw), _rows(tm, vw)],
        out_shape=[
            jax.ShapeDtypeStruct((t, D_MODEL), MXU_DTYPE),
            jax.ShapeDtypeStruct((t, A_PAD), F32),
            jax.ShapeDtypeStruct((t, Q_LORA), MXU_DTYPE),
            jax.ShapeDtypeStruct((t, KV_LORA), MXU_DTYPE),
            jax.ShapeDtypeStruct((t, qw), MXU_DTYPE),
            jax.ShapeDtypeStruct((t, qw), MXU_DTYPE),
            jax.ShapeDtypeStruct((t, vw), MXU_DTYPE),
        ],
        compiler_params=_cparams("parallel"),
    )(h, g, w_a, g_q, g_kv, w_uq, w_ukv, cs, sa, sb)


def _attn_tile(t):
    return min(1024, t)


def _chunk_mask(qi, ki, tq, tk):
    rows = qi * tq + lax.broadcasted_iota(jnp.int32, (tq, tk), 0)
    cols = ki * tk + lax.broadcasted_iota(jnp.int32, (tq, tk), 1)
    return (rows >> CHUNK_SHIFT) >= (cols >> CHUNK_SHIFT)


def _flash_fwd(q, k, v):
    t = q.shape[0]
    tq = tk = _attn_tile(t)
    nk = t // tk

    def body(q_ref, k_ref, v_ref, o_ref, lse_ref, m_s, l_s, acc):
        qi, ki = pl.program_id(1), pl.program_id(2)

        @pl.when(ki == 0)
        def _():
            m_s[...] = jnp.full_like(m_s, -jnp.inf)
            l_s[...] = jnp.zeros_like(l_s)
            acc[...] = jnp.zeros_like(acc)

        @pl.when(ki <= qi)
        def _():
            s = _dot_nt(q_ref[...], k_ref[...]) * ATTN_SCALE
            s = jnp.where(_chunk_mask(qi, ki, tq, tk), s, -jnp.inf)
            m_prev = m_s[...]
            m_new = jnp.maximum(m_prev, jnp.max(s, axis=1, keepdims=True))
            alpha = jnp.exp(m_prev - m_new)
            p = jnp.exp(s - m_new)
            l_s[...] = alpha * l_s[...] + jnp.sum(p, axis=1, keepdims=True)
            acc[...] = alpha * acc[...] + _dot(p.astype(MXU_DTYPE), v_ref[...])
            m_s[...] = m_new

        @pl.when(ki == nk - 1)
        def _():
            o_ref[...] = acc[...] / l_s[...]
            lse_ref[...] = jnp.broadcast_to(m_s[...] + jnp.log(l_s[...]), (tq, V_HEAD))

    return pl.pallas_call(
        body,
        name="flash_fwd",
        grid=(N_HEADS, t // tq, nk),
        in_specs=[
            pl.BlockSpec((tq, HEAD_PAD), lambda h, i, j: (i, h)),
            pl.BlockSpec((tk, HEAD_PAD), lambda h, i, j: (jnp.minimum(i, j), h)),
            pl.BlockSpec((tk, V_HEAD), lambda h, i, j: (jnp.minimum(i, j), h)),
        ],
        out_specs=[pl.BlockSpec((tq, V_HEAD), lambda h, i, j: (i, h)), pl.BlockSpec((tq, V_HEAD), lambda h, i, j: (i, h))],
        out_shape=[jax.ShapeDtypeStruct((t, N_HEADS * V_HEAD), F32), jax.ShapeDtypeStruct((t, N_HEADS * V_HEAD), F32)],
        scratch_shapes=[pltpu.VMEM((tq, 1), F32), pltpu.VMEM((tq, 1), F32), pltpu.VMEM((tq, V_HEAD), F32)],
        compiler_params=_cparams("parallel", "parallel", "arbitrary"),
    )(q, k, v)


def _flash_bwd(q, k, v, do, lse, delta):
    t = q.shape[0]
    tq = tk = _attn_tile(t)
    nq = t // tq

    def body(q_ref, k_ref, v_ref, do_ref, lse_ref, dl_ref, dq_ref, dk_ref, dv_ref, dk_acc, dv_acc):
        kj, qi = pl.program_id(1), pl.program_id(2)

        @pl.when((kj == 0) & (qi == 0))
        def _():
            dq_ref[...] = jnp.zeros_like(dq_ref)

        @pl.when(qi == 0)
        def _():
            dk_acc[...] = jnp.zeros_like(dk_acc)
            dv_acc[...] = jnp.zeros_like(dv_acc)

        @pl.when(qi >= kj)
        def _():
            qv, kv, dov = q_ref[...], k_ref[...], do_ref[...]
            s = _dot_nt(qv, kv) * ATTN_SCALE
            s = jnp.where(_chunk_mask(qi, kj, tq, tk), s, -jnp.inf)
            p = jnp.exp(s - lse_ref[:, :1])
            dp = _dot_nt(dov, v_ref[...])
            ds = (p * (dp - dl_ref[:, :1]) * ATTN_SCALE).astype(MXU_DTYPE)
            dv_acc[...] += _dot_tn(p.astype(MXU_DTYPE), dov)
            dk_acc[...] += _dot_tn(ds, qv)
            rows = pl.ds(pl.multiple_of(qi * tq, tq), tq)
            dq_ref[rows, :] += _dot(ds, kv)

        @pl.when(qi == nq - 1)
        def _():
            dk_ref[...] = dk_acc[...]
            dv_ref[...] = dv_acc[...]

    return pl.pallas_call(
        body,
        name="flash_bwd",
        grid=(N_HEADS, t // tk, nq),
        in_specs=[
            pl.BlockSpec((tq, HEAD_PAD), lambda h, j, i: (jnp.maximum(i, j), h)),
            pl.BlockSpec((tk, HEAD_PAD), lambda h, j, i: (j, h)),
            pl.BlockSpec((tk, V_HEAD), lambda h, j, i: (j, h)),
            pl.BlockSpec((tq, V_HEAD), lambda h, j, i: (jnp.maximum(i, j), h)),
            pl.BlockSpec((tq, V_HEAD), lambda h, j, i: (jnp.maximum(i, j), h)),
            pl.BlockSpec((tq, V_HEAD), lambda h, j, i: (jnp.maximum(i, j), h)),
        ],
        out_specs=[
            pl.BlockSpec((t, HEAD_PAD), lambda h, j, i: (0, h)),
            pl.BlockSpec((tk, HEAD_PAD), lambda h, j, i: (j, h)),
            pl.BlockSpec((tk, V_HEAD), lambda h, j, i: (j, h)),
        ],
        out_shape=[
            jax.ShapeDtypeStruct((t, N_HEADS * HEAD_PAD), F32),
            jax.ShapeDtypeStruct((t, N_HEADS * HEAD_PAD), F32),
            jax.ShapeDtypeStruct((t, N_HEADS * V_HEAD), F32),
        ],
        scratch_shapes=[pltpu.VMEM((tk, HEAD_PAD), F32), pltpu.VMEM((tk, V_HEAD), F32)],
        compiler_params=_cparams("arbitrary", "arbitrary", "arbitrary"),
    )(q, k, v, do, lse, delta)


def _attn_out(o, w_o, h):
    t = h.shape[0]
    tm = min(512, t)

    def body(o_ref, w_ref, h_ref, ho_ref):
        ho_ref[...] = h_ref[...] + _dot(o_ref[...].astype(MXU_DTYPE), w_ref[...])

    return pl.pallas_call(
        body,
        name="attn_out",
        grid=(t // tm,),
        in_specs=[_rows(tm, D_MODEL), _resident((D_MODEL, D_MODEL)), _rows(tm, D_MODEL)],
        out_specs=_rows(tm, D_MODEL),
        out_shape=jax.ShapeDtypeStruct((t, D_MODEL), F32),
        compiler_params=_cparams("parallel"),
    )(o, w_o, h)


def _attn_out_bwd(dout, o, w_o):
    t = dout.shape[0]
    tm = min(512, t)

    def body(d_ref, o_ref, w_ref, do_ref, dl_ref):
        do = _dot_nt(d_ref[...].astype(MXU_DTYPE), w_ref[...])
        do_ref[...] = do.astype(MXU_DTYPE)
        prod = do * o_ref[...]
        for hd in range(N_HEADS):
            lanes = slice(hd * V_HEAD, (hd + 1) * V_HEAD)
            dl_ref[:, lanes] = jnp.broadcast_to(jnp.sum(prod[:, lanes], axis=1, keepdims=True), (tm, V_HEAD))

    return pl.pallas_call(
        body,
        name="attn_out_bwd",
        grid=(t // tm,),
        in_specs=[_rows(tm, D_MODEL), _rows(tm, D_MODEL), _resident((D_MODEL, D_MODEL))],
        out_specs=[_rows(tm, D_MODEL), _rows(tm, D_MODEL)],
        out_shape=[jax.ShapeDtypeStruct((t, D_MODEL), MXU_DTYPE), jax.ShapeDtypeStruct((t, D_MODEL), F32)],
        compiler_params=_cparams("parallel"),
    )(dout, o, w_o)


def _mla_bwd_pre(dq, dk, dv, a, h, dout, g, g_q, g_kv, w_a, w_uq, w_ukv, cs, sa, sb):
    t = h.shape[0]
    tm = min(512, t)
    qw = N_HEADS * HEAD_PAD
    vw = N_HEADS * V_HEAD

    def body(dq_ref, dk_ref, dv_ref, a_ref, h_ref, do_ref, g_ref, gq_ref, gkv_ref, wa_ref, wuq_ref, wukv_ref,
             cs_ref, sa_ref, sb_ref, dh_ref, dqp_ref, dkv_ref, da_ref, dg_ref, dgq_ref, dgkv_ref):
        @pl.when(pl.program_id(0) == 0)
        def _():
            dg_ref[...] = jnp.zeros_like(dg_ref)
            dgq_ref[...] = jnp.zeros_like(dgq_ref)
            dgkv_ref[...] = jnp.zeros_like(dgkv_ref)

        cs_, sa_, sb_ = cs_ref[...], sa_ref[...], sb_ref[...]
        slab = jnp.zeros((tm, 128), F32)
        for hd in range(N_HEADS):
            lo = hd * HEAD_PAD
            dqp_ref[:, lo:lo + QK_NOPE] = dq_ref[:, lo:lo + QK_NOPE].astype(MXU_DTYPE)
            dqp_ref[:, lo + QK_NOPE:lo + HEAD_PAD] = _unrope(dq_ref[:, lo + QK_NOPE:lo + HEAD_PAD], cs_, sa_, sb_).astype(MXU_DTYPE)
            dkv_ref[:, lo:lo + QK_NOPE] = dk_ref[:, lo:lo + QK_NOPE].astype(MXU_DTYPE)
            dkv_ref[:, lo + QK_NOPE:lo + HEAD_PAD] = dv_ref[:, hd * V_HEAD:(hd + 1) * V_HEAD].astype(MXU_DTYPE)
            slab = slab + dk_ref[:, lo + QK_NOPE:lo + HEAD_PAD]
        dcq = _dot_nt(dqp_ref[...], wuq_ref[...])
        dckv = _dot_nt(dkv_ref[...], wukv_ref[...])
        av = a_ref[...]
        daq, dgq = _rms_bwd(av[:, :Q_LORA], gq_ref[...], dcq)
        dakv, dgkv = _rms_bwd(av[:, Q_LORA:Q_LORA + KV_LORA], gkv_ref[...], dckv)
        da_ref[:, :Q_LORA] = daq.astype(MXU_DTYPE)
        da_ref[:, Q_LORA:Q_LORA + KV_LORA] = dakv.astype(MXU_DTYPE)
        da_ref[:, Q_LORA + KV_LORA:] = _unrope(slab, cs_, sa_, sb_).astype(MXU_DTYPE)
        dn = _dot_nt(da_ref[...], wa_ref[...])
        dx, dg = _rms_bwd(h_ref[...], g_ref[...], dn)
        dh_ref[...] = do_ref[...] + dx
        dg_ref[...] += dg
        dgq_ref[...] += dgq
        dgkv_ref[...] += dgkv

    def const(n):
        return pl.BlockSpec((1, n), lambda i: (0, 0))

    return pl.pallas_call(
        body,
        name="mla_bwd_pre",
        grid=(t // tm,),
        in_specs=[
            _rows(tm, qw), _rows(tm, qw), _rows(tm, vw), _rows(tm, A_PAD), _rows(tm, D_MODEL), _rows(tm, D_MODEL),
            _resident((1, D_MODEL)), _resident((1, Q_LORA)), _resident((1, KV_LORA)),
            _resident((D_MODEL, A_PAD)), _resident((Q_LORA, qw)), _resident((KV_LORA, qw)),
            _rows(tm, 128), _rows(tm, 128), _rows(tm, 128),
        ],
        out_specs=[_rows(tm, D_MODEL), _rows(tm, qw), _rows(tm, qw), _rows(tm, A_PAD),
                   const(D_MODEL), const(Q_LORA), const(KV_LORA)],
        out_shape=[
            jax.ShapeDtypeStruct((t, D_MODEL), F32),
            jax.ShapeDtypeStruct((t, qw), MXU_DTYPE),
            jax.ShapeDtypeStruct((t, qw), MXU_DTYPE),
            jax.ShapeDtypeStruct((t, A_PAD), MXU_DTYPE),
            jax.ShapeDtypeStruct((1, D_MODEL), F32),
            jax.ShapeDtypeStruct((1, Q_LORA), F32),
            jax.ShapeDtypeStruct((1, KV_LORA), F32),
        ],
        compiler_params=_cparams("arbitrary"),
    )(dq, dk, dv, a, h, dout, g, g_q, g_kv, w_a, w_uq, w_ukv, cs, sa, sb)


def _loss_head(h, g, target):
    t = h.shape[0]
    tm = min(512, t)

    def body(h_ref, g_ref, t_ref, sq_ref, dh_ref, dg_ref):
        @pl.when(pl.program_id(0) == 0)
        def _():
            sq_ref[...] = jnp.zeros_like(sq_ref)
            dg_ref[...] = jnp.zeros_like(dg_ref)

        x = h_ref[...]
        err = _rms(x, g_ref[...]) - t_ref[...]
        sq_ref[...] += jnp.sum(err * err, axis=0, keepdims=True)
        dx, dg = _rms_bwd(x, g_ref[...], err * (1.0 / D_MODEL))
        dh_ref[...] = dx
        dg_ref[...] += dg

    return pl.pallas_call(
        body,
        name="loss_head",
        grid=(t // tm,),
        in_specs=[_rows(tm, D_MODEL), _resident((1, D_MODEL)), _rows(tm, D_MODEL)],
        out_specs=[pl.BlockSpec((1, D_MODEL), lambda i: (0, 0)), _rows(tm, D_MODEL), pl.BlockSpec((1, D_MODEL), lambda i: (0, 0))],
        out_shape=[jax.ShapeDtypeStruct((1, D_MODEL), F32), jax.ShapeDtypeStruct((t, D_MODEL), F32), jax.ShapeDtypeStruct((1, D_MODEL), F32)],
        compiler_params=_cparams("arbitrary"),
    )(h, g, target)


def _adamw(parts, w, m, v, name, tr, row_offset=0):
    layers, rows, width = w.shape
    pwidth = parts.shape[-1]
    off = row_offset // tr

    def body(p_ref, w_ref, m_ref, v_ref, g_ref, d_ref, mo_ref, vo_ref):
        def part(q):
            return p_ref[q, :, pl.ds(0, width)].astype(F32)

        g = part(0)
        for q in range(1, N_DEV):
            g = g + part(q)
        g_ref[...] = g
        m_new = ADAM_B1 * m_ref[...] + (1.0 - ADAM_B1) * g
        v_new = ADAM_B2 * v_ref[...] + (1.0 - ADAM_B2) * (g * g)
        m_hat = m_new / (1.0 - ADAM_B1 ** ADAM_STEP)
        v_hat = v_new / (1.0 - ADAM_B2 ** ADAM_STEP)
        d_ref[...] = -ADAM_LR * (m_hat / (jnp.sqrt(v_hat) + ADAM_EPS) + ADAM_WD * w_ref[...])
        mo_ref[...] = m_new
        vo_ref[...] = v_new

    blk = pl.BlockSpec((None, tr, width), lambda l, i: (l, i, 0))
    return pl.pallas_call(
        body,
        name=name,
        grid=(layers, rows // tr),
        in_specs=[pl.BlockSpec((N_DEV, None, tr, pwidth), lambda l, i: (0, l, off + i, 0)), blk, blk, blk],
        out_specs=[blk, blk, blk, blk],
        out_shape=[jax.ShapeDtypeStruct((layers, rows, width), F32)] * 4,
        compiler_params=_cparams("parallel", "parallel"),
    )(parts, w, m, v)


_GAINS = ("ffn_norm1", "mix_norm", "ffn_norm2", "conv_norm", "final_norm")
_TINY = ("conv_w_dw", "mla_q_norm", "mla_kv_norm")


def _small_pack(parts):
    flat = jnp.concatenate([p.reshape(-1).astype(F32) for p in parts])
    return jnp.pad(flat, (0, SMALL_ROWS * D_MODEL - flat.shape[0])).reshape(SMALL_ROWS, D_MODEL)


def _small_unpack(packed, shapes):
    flat, out, off = packed.reshape(-1), [], 0
    for s in shapes:
        size = 1
        for d in s:
            size *= d
        out.append(flat[off:off + size].reshape(s))
        off += size
    return out


def _pad_to(a, axis, size):
    pad = [(0, 0)] * a.ndim
    pad[axis] = (0, size - a.shape[axis])
    return jnp.pad(a, pad)


def kernel(x, positions, ffn_norm1, ffn1_w1, ffn1_w3, ffn1_w2, mix_norm, ffn_norm2, ffn2_w1, ffn2_w3, ffn2_w2, conv_w_pw1, conv_w_dw, conv_norm, conv_w_pw2, mla_w_a, mla_q_norm, mla_kv_norm, mla_w_uq, mla_w_ukv, mla_w_o, final_norm, loss_target, m_ffn_norm1, m_ffn1_w1, m_ffn1_w3, m_ffn1_w2, m_mix_norm, m_ffn_norm2, m_ffn2_w1, m_ffn2_w3, m_ffn2_w2, m_conv_w_pw1, m_conv_w_dw, m_conv_norm, m_conv_w_pw2, m_mla_w_a, m_mla_q_norm, m_mla_kv_norm, m_mla_w_uq, m_mla_w_ukv, m_mla_w_o, m_final_norm, v_ffn_norm1, v_ffn1_w1, v_ffn1_w3, v_ffn1_w2, v_mix_norm, v_ffn_norm2, v_ffn2_w1, v_ffn2_w3, v_ffn2_w2, v_conv_w_pw1, v_conv_w_dw, v_conv_norm, v_conv_w_pw2, v_mla_w_a, v_mla_q_norm, v_mla_kv_norm, v_mla_w_uq, v_mla_w_ukv, v_mla_w_o, v_final_norm):
    args = dict(locals())
    wire = lambda a: a.astype(WIRE_DTYPE)

    def ffn_in_shard(w1, w3):
        return wire(_pad_to(jnp.stack([w1, w3], axis=1), 3, FF_SHARD_PAD))

    def uq_pad(a):
        return _pad_to(a, 3, HEAD_PAD).reshape(1, a.shape[1], N_HEADS * HEAD_PAD)

    ukv_rows = lambda a: a.reshape(1, a.shape[1], N_HEADS * HEAD_PAD)
    tiny_shapes = [conv_w_dw.shape, mla_q_norm.shape, mla_kv_norm.shape]
    gathered = _all_gather([
        (ffn_in_shard(ffn1_w1, ffn1_w3), 1),
        (wire(_pad_to(ffn1_w2, 1, FF_SHARD_PAD)), 2),
        (ffn_in_shard(ffn2_w1, ffn2_w3), 1),
        (wire(_pad_to(ffn2_w2, 1, FF_SHARD_PAD)), 2),
        (wire(conv_w_pw1[0]), 1),
        (wire(conv_w_pw2[0]), 2),
        (wire(_pad_to(mla_w_a[0], 1, A_PAD)), 2),
        (wire(uq_pad(mla_w_uq)[0]), 2),
        (wire(ukv_rows(mla_w_ukv)[0]), 2),
        (wire(mla_w_o[0]), 2),
        (_small_pack([conv_w_dw, mla_q_norm, mla_kv_norm]), None),
    ], "weight_all_gather")
    w13 = {"ffn1": gathered[0], "ffn2": gathered[2]}
    w2 = {"ffn1": gathered[1], "ffn2": gathered[3]}
    w_pw1, w_pw2, w_a, w_uq, w_ukv, w_o, tiny = gathered[4:]
    tiny = [_small_unpack(tiny[q], tiny_shapes) for q in range(N_DEV)]
    w_dw = jnp.concatenate([t_[0][0] for t_ in tiny], axis=1)
    g_q = jnp.concatenate([t_[1] for t_ in tiny], axis=1)
    g_kv = jnp.concatenate([t_[2] for t_ in tiny], axis=1)

    inv_freq = ROPE_THETA ** (-2.0 * jnp.arange(QK_ROPE // 2, dtype=F32) / QK_ROPE)
    ang = positions[0].astype(F32)[:, None] * inv_freq
    cos, sin, zero = jnp.cos(ang), jnp.sin(ang), jnp.zeros_like(ang)
    cs = jnp.concatenate([cos, cos, zero, zero], axis=1)
    sa = jnp.concatenate([sin, zero, zero, zero], axis=1)
    sb = jnp.concatenate([zero, sin, zero, zero], axis=1)

    saved = []
    h = x[0]
    for layer in range(2):
        h_in = h
        h, n1, a1, b1 = _ffn_fwd(h_in, ffn_norm1[layer:layer + 1], w13["ffn1"], w2["ffn1"], layer)
        h_mix = h
        if layer == 0:
            n_m, u, glu = _conv_pre(h_mix, mix_norm[0:1], w_pw1)
            h, c, sw = _conv_main(glu, w_dw, conv_norm, w_pw2, h_mix)
            mixer = (n_m, u, glu, c, sw)
        else:
            n_m, a_lat, cq, ckv, q, k, v = _mla_pre(h_mix, mix_norm[1:2], w_a, g_q, g_kv, w_uq, w_ukv, cs, sa, sb)
            o, lse = _flash_fwd(q, k, v)
            h = _attn_out(o, w_o, h_mix)
            mixer = (n_m, a_lat, cq, ckv, q, k, v, o, lse)
        h_mid = h
        h, n2, a2, b2 = _ffn_fwd(h_mid, ffn_norm2[layer:layer + 1], w13["ffn2"], w2["ffn2"], layer)
        saved.append((h_in, n1, a1, b1, h_mix, mixer, h_mid, n2, a2, b2))

    sq, dh, dg_final = _loss_head(h, final_norm[None, :], loss_target[0])
    loss = lax.psum(0.5 / D_MODEL * jnp.sum(sq), ("x", "y", "c"))

    gain = {}
    g13, g2 = {}, {}

    def ffn_backward(prefix, norm, layer, dh, h_in, n, a, b):
        dh, dob, s, dab, gain[(norm, layer)] = _ffn_bwd(dh, h_in, args[norm][layer:layer + 1], a, b, w13[prefix], w2[prefix], layer)
        g13[(prefix, layer)] = _wgrad(n, dab, "wgrad_ffn_in")
        g2[(prefix, layer)] = _wgrad(s, dob, "wgrad_ffn_out")
        return dh

    for layer in (1, 0):
        h_in, n1, a1, b1, h_mix, mixer, h_mid, n2, a2, b2 = saved[layer]
        dh = ffn_backward("ffn2", "ffn_norm2", layer, dh, h_mid, n2, a2, b2)
        if layer == 0:
            n_m, u, glu, c, sw = mixer
            dc, gain[("conv_norm", 0)] = _conv_bwd_post(dh, c, conv_norm, w_pw2)
            d_pw2 = _wgrad(sw, dh, "wgrad_conv_pw2")
            dh, du, d_dw, gain[("mix_norm", 0)] = _conv_bwd_pre(dc, glu, u, w_dw, w_pw1, h_mix, mix_norm[0:1], dh)
            d_pw1 = _wgrad(n_m, du, "wgrad_conv_pw1")
        else:
            n_m, a_lat, cq, ckv, q, k, v, o, lse = mixer
            do, delta = _attn_out_bwd(dh, o, w_o)
            d_o = _wgrad(o, dh, "wgrad_attn_out")
            dq, dk, dv = _flash_bwd(q, k, v, do, lse, delta)
            dh, dqp, dkv, da_lat, gain[("mix_norm", 1)], d_gq, d_gkv = _mla_bwd_pre(
                dq, dk, dv, a_lat, h_mix, dh, mix_norm[1:2], g_q, g_kv, w_a, w_uq, w_ukv, cs, sa, sb)
            d_a = _wgrad(n_m, da_lat, "wgrad_mla_a")
            d_uq = _wgrad(cq, dqp, "wgrad_mla_uq")
            d_ukv = _wgrad(ckv, dkv, "wgrad_mla_ukv")
        dh = ffn_backward("ffn1", "ffn_norm1", layer, dh, h_in, n1, a1, b1)
    grad_x = dh[None]

    gain_rows = jnp.concatenate([
        gain[("ffn_norm1", 0)], gain[("ffn_norm1", 1)], gain[("mix_norm", 0)], gain[("mix_norm", 1)],
        gain[("ffn_norm2", 0)], gain[("ffn_norm2", 1)], gain[("conv_norm", 0)], dg_final])
    dw_by_dest = jnp.moveaxis(d_dw[:CONV_WIDTH].reshape(CONV_WIDTH, N_DEV, -1), 1, 0)
    small_by_dest = jnp.stack([
        _small_pack([gain_rows, dw_by_dest[p], d_gq.reshape(N_DEV, -1)[p], d_gkv.reshape(N_DEV, -1)[p]]) for p in range(N_DEV)])

    def layer_slot(layer):
        return lambda ref, q: ref.at[q, layer]

    streams, outs = [], []

    def scatter(full, axis, lead=(N_DEV,), land=_slot, new_out=True):
        shape = list(full.shape)
        shape[-axis] //= N_DEV
        if new_out:
            outs.append(jax.ShapeDtypeStruct(tuple(lead) + tuple(shape), full.dtype))
        streams.append((full, len(outs) - 1, _block_of(axis, shape[-axis]), land))

    for prefix in ("ffn1", "ffn2"):
        for layer in range(2):
            scatter(g13[(prefix, layer)], 1, lead=(N_DEV, 2), land=layer_slot(layer), new_out=layer == 0)
        for layer in range(2):
            scatter(g2[(prefix, layer)], 2, lead=(N_DEV, 2), land=layer_slot(layer), new_out=layer == 0)
    scatter(d_pw1, 1)
    for full in (d_pw2, d_a, d_uq, d_ukv, d_o):
        scatter(full, 2)
    outs.append(jax.ShapeDtypeStruct((N_DEV, SMALL_ROWS, D_MODEL), F32))
    streams.append((small_by_dest, len(outs) - 1, _slot, _slot))
    p13_1, p2_1, p13_2, p2_2, p_pw1, p_pw2, p_a, p_uq, p_ukv, p_o, p_small = _exchange(streams, outs, "grad_exchange")

    results = {}

    def adam(name, parts, tr, row_offset=0, view=lambda a: a):
        res = _adamw(parts, view(args[name]), view(args["m_" + name]), view(args["v_" + name]), "adamw_" + name, tr, row_offset)
        results[name] = res
        return res

    lead = lambda a: a[:, None]
    for prefix, p13, p2 in (("ffn1", p13_1, p2_1), ("ffn2", p13_2, p2_2)):
        p13 = p13.reshape(N_DEV, 2, 2 * D_MODEL, FF_SHARD_PAD)
        adam(prefix + "_w1", p13, 256)
        adam(prefix + "_w3", p13, 256, row_offset=D_MODEL)
        adam(prefix + "_w2", p2, FF_SHARD)
    adam("conv_w_pw1", lead(p_pw1), 256)
    adam("conv_w_pw2", lead(p_pw2), 128)
    adam("mla_w_a", lead(p_a), 128)
    adam("mla_w_uq", lead(p_uq), 64, view=uq_pad)
    adam("mla_w_ukv", lead(p_ukv), 32, view=ukv_rows)
    adam("mla_w_o", lead(p_o), 128)
    results["mla_w_uq"] = [r.reshape(1, -1, N_HEADS, HEAD_PAD)[..., :QK_NOPE + QK_ROPE] for r in results["mla_w_uq"]]
    results["mla_w_ukv"] = [r.reshape(mla_w_ukv.shape) for r in results["mla_w_ukv"]]

    small_names = _GAINS + _TINY
    small_shapes = [args[n].shape for n in small_names]
    pack_small = lambda prefix: _small_pack([args[prefix + n] for n in small_names])[None]
    small = _adamw(lead(p_small), pack_small(""), pack_small("m_"), pack_small("v_"), "adamw_small", SMALL_ROWS)
    for kind in range(4):
        for n, leaf in zip(small_names, _small_unpack(small[kind][0], small_shapes)):
            results.setdefault(n, [None] * 4)[kind] = leaf

    order = ("ffn_norm1", "ffn1_w1", "ffn1_w3", "ffn1_w2", "mix_norm", "ffn_norm2", "ffn2_w1", "ffn2_w3", "ffn2_w2",
             "conv_w_pw1", "conv_w_dw", "conv_norm", "conv_w_pw2", "mla_w_a", "mla_q_norm", "mla_kv_norm",
             "mla_w_uq", "mla_w_ukv", "mla_w_o", "final_norm")
    outputs = [loss, grad_x]
    for kind in range(4):
        outputs.extend(results[n][kind] for n in order)
    return tuple(outputs)
```

```python
import jax
import jax.numpy as jnp
from jax import lax
from jax.experimental import pallas as pl
from jax.experimental.pallas import tpu as pltpu

F32 = jnp.float32
MXU_DTYPE = jnp.bfloat16
WIRE_DTYPE = jnp.bfloat16

N_DEV = 8
D_MODEL = 1024
FF_SHARD = 352
FF_SHARD_PAD = 384
D_FF_PAD = N_DEV * FF_SHARD_PAD
N_HEADS = 8
QK_NOPE = 128
QK_ROPE = 64
V_HEAD = 128
Q_LORA = 512
KV_LORA = 256
HEAD_PAD = 256
A_WIDTH = Q_LORA + KV_LORA + QK_ROPE
A_PAD = Q_LORA + KV_LORA + 128
CONV_WIDTH = 31
CONV_HALO = 32
CHUNK_SHIFT = 6
ROPE_THETA = 10000.0
RMS_EPS = 1e-6
ATTN_SCALE = (QK_NOPE + QK_ROPE) ** -0.5
FFN_RES_WEIGHT = 0.5
ADAM_LR = 0.001
ADAM_B1 = 0.9
ADAM_B2 = 0.999
ADAM_EPS = 1e-08
ADAM_WD = 0.01
ADAM_STEP = 10

SMALL_ROWS = 16
VMEM_LIMIT = 56 << 20


def _cparams(*sem):
    return pltpu.CompilerParams(dimension_semantics=sem, vmem_limit_bytes=VMEM_LIMIT)


def _dot(a, b):
    return lax.dot_general(a, b, (((1,), (0,)), ((), ())), preferred_element_type=F32)


def _dot_nt(a, b):
    return lax.dot_general(a, b, (((1,), (1,)), ((), ())), preferred_element_type=F32)


def _dot_tn(a, b):
    return lax.dot_general(a, b, (((0,), (0,)), ((), ())), preferred_element_type=F32)


def _resident(shape, index=None):
    fixed = index if index is not None else (0,) * len(shape)
    return pl.BlockSpec(shape, lambda *_: fixed, pipeline_mode=pl.Buffered(1))


def _rows(tm, n):
    return pl.BlockSpec((tm, n), lambda t, *_: (t, 0))


def _rms(x, g):
    r = lax.rsqrt(jnp.mean(x * x, axis=-1, keepdims=True) + RMS_EPS)
    return x * r * g


def _rms_bwd(x, g, dy):
    r = lax.rsqrt(jnp.mean(x * x, axis=-1, keepdims=True) + RMS_EPS)
    xr = x * r
    dg = jnp.sum(dy * xr, axis=0, keepdims=True)
    u = dy * g
    dx = r * (u - xr * jnp.mean(u * xr, axis=-1, keepdims=True))
    return dx, dg


def _sigmoid(x):
    return 1.0 / (1.0 + jnp.exp(-x))


def _me():
    return 4 * lax.axis_index("x") + 2 * lax.axis_index("y") + lax.axis_index("c")


def _peer(k):
    px, py, pc = lax.axis_index("x") ^ ((k >> 2) & 1), lax.axis_index("y") ^ ((k >> 1) & 1), lax.axis_index("c") ^ (k & 1)
    return (px, py, pc), 4 * px + 2 * py + pc


def _stream_copies(stream, src_ref, dst_ref, send_sem, recv_sem, base):
    _, _, send, land = stream
    me = _me()
    outgoing, incoming = [], []
    for k in range(1, N_DEV):
        pos, p = _peer(k)
        common = dict(send_sem=send_sem.at[base + k - 1], recv_sem=recv_sem.at[base + k - 1], device_id=pos, device_id_type=pl.DeviceIdType.MESH)
        outgoing.append(pltpu.make_async_remote_copy(src_ref=send(src_ref, p), dst_ref=land(dst_ref, me), **common))
        incoming.append(pltpu.make_async_remote_copy(src_ref=send(src_ref, p), dst_ref=land(dst_ref, p), **common))
    return outgoing, incoming


_HBM = pl.BlockSpec(memory_space=pltpu.HBM)
_SEM = pl.BlockSpec(memory_space=pltpu.SEMAPHORE)


def _exchange(streams, outs, name):
    n, n_out = len(streams), len(outs)

    def body(*refs):
        srcs, dsts = refs[:n], refs[n:n + n_out]
        send_sem, recv_sem, local_sem = refs[n + n_out:]
        me = _me()
        own = [pltpu.make_async_copy(st[2](srcs[e], me), st[3](dsts[st[1]], me), local_sem.at[e]) for e, st in enumerate(streams)]
        for cp in own:
            cp.start()
        sends, recvs = [], []
        for e, st in enumerate(streams):
            outgoing, incoming = _stream_copies(st, srcs[e], dsts[st[1]], send_sem, recv_sem, e * (N_DEV - 1))
            sends += outgoing
            recvs += incoming
        for cp in sends:
            cp.start()
        for cp in recvs:
            cp.wait_recv()
        for cp in sends:
            cp.wait_send()
        for cp in own:
            cp.wait()

    return pl.pallas_call(
        body,
        name=name,
        out_shape=tuple(outs),
        in_specs=[_HBM] * n,
        out_specs=tuple([_HBM] * n_out),
        scratch_shapes=[pltpu.SemaphoreType.DMA((n * (N_DEV - 1),)), pltpu.SemaphoreType.DMA((n * (N_DEV - 1),)),
                        pltpu.SemaphoreType.DMA((n,))],
        compiler_params=pltpu.CompilerParams(vmem_limit_bytes=VMEM_LIMIT),
    )(*[st[0] for st in streams])


def _place_own(streams, outs, name):
    n = len(streams)

    def body(*refs):
        srcs, dsts, local_sem = refs[:n], refs[n:-1], refs[-1]
        me = _me()
        own = [pltpu.make_async_copy(st[2](srcs[e], me), st[3](dsts[st[1]], me), local_sem.at[e]) for e, st in enumerate(streams)]
        for cp in own:
            cp.start()
        for cp in own:
            cp.wait()

    return pl.pallas_call(
        body,
        name=name,
        out_shape=tuple(outs),
        in_specs=[_HBM] * n,
        out_specs=tuple([_HBM] * len(outs)),
        scratch_shapes=[pltpu.SemaphoreType.DMA((n,))],
        compiler_params=pltpu.CompilerParams(vmem_limit_bytes=VMEM_LIMIT),
    )(*[st[0] for st in streams])


def _exchange_start(streams, lands, groups, after, name):
    n, n_land, n_grp = len(streams), len(lands), len(groups)
    side_effects = pltpu.SideEffectType.DATAFLOW_SIDE_EFFECTING

    def body(*refs):
        srcs, dsts = refs[:n], refs[n:n + n_land]
        outs = refs[n + n_land + len(after):]
        send_sems, recv_sems, token = outs[:n_grp], outs[n_grp:2 * n_grp], outs[-1]
        for gi, group in enumerate(groups):
            for j, e in enumerate(group):
                outgoing, _ = _stream_copies(streams[e], srcs[e], dsts[streams[e][1]], send_sems[gi], recv_sems[gi], j * (N_DEV - 1))
                for cp in outgoing:
                    cp.start()
        token[...] = jnp.zeros_like(token)

    sems = tuple(pltpu.SemaphoreType.DMA((len(g) * (N_DEV - 1),)) for g in groups)
    passed = [st[0] for st in streams] + list(lands)
    res = pl.pallas_call(
        body,
        name=name,
        out_shape=sems + sems + tuple(pltpu.HBM(a.shape, a.dtype) for a in passed) + (jax.ShapeDtypeStruct((8, 128), F32),),
        in_specs=[_HBM] * (n + n_land) + [pl.BlockSpec(memory_space=pl.ANY)] * len(after),
        out_specs=tuple([_SEM] * (2 * n_grp) + [_HBM] * (n + n_land) + [pl.BlockSpec(memory_space=pltpu.VMEM)]),
        input_output_aliases={i: 2 * n_grp + i for i in range(n + n_land)},
        compiler_params=pltpu.CompilerParams(has_side_effects=side_effects),
    )(*passed, *after)
    handle = dict(streams=streams, groups=groups, send=res[:n_grp], recv=res[n_grp:2 * n_grp],
                  srcs=res[2 * n_grp:2 * n_grp + n], lands=res[2 * n_grp + n:2 * n_grp + n + n_land])
    return handle, res[-1]


def _exchange_wait(handle, gi, after, name):
    streams, group = handle["streams"], handle["groups"][gi]
    land_ids = sorted({streams[e][1] for e in group})
    srcs = [handle["srcs"][e] for e in group]
    lands = [handle["lands"][i] for i in land_ids]
    n, n_land = len(srcs), len(lands)
    side_effects = pltpu.SideEffectType.DATAFLOW_SIDE_EFFECTING

    def body(*refs):
        src_refs, land_refs = refs[:n], refs[n:n + n_land]
        send_sem, recv_sem = refs[n + n_land:n + n_land + 2]
        for j, e in enumerate(group):
            outgoing, incoming = _stream_copies(streams[e], src_refs[j], land_refs[land_ids.index(streams[e][1])], send_sem, recv_sem, j * (N_DEV - 1))
            for cp in outgoing:
                cp.wait_send()
            for cp in incoming:
                cp.wait_recv()

    res = pl.pallas_call(
        body,
        name=name,
        out_shape=tuple(pltpu.HBM(a.shape, a.dtype) for a in lands),
        in_specs=[_HBM] * (n + n_land) + [_SEM, _SEM] + [pl.BlockSpec(memory_space=pl.ANY)] * len(after),
        out_specs=tuple([_HBM] * n_land),
        input_output_aliases={n + i: i for i in range(n_land)},
        compiler_params=pltpu.CompilerParams(has_side_effects=side_effects),
    )(*srcs, *lands, handle["send"][gi], handle["recv"][gi], *after)
    return list(res)


def _whole(ref, _):
    return ref


def _slot(ref, q):
    return ref.at[q]


def _lane_block(width):
    def pick(ref, q):
        idx = (slice(None),) * (len(ref.shape) - 1) + (pl.ds(pl.multiple_of(q * width, 128), width),)
        return ref.at[idx]
    return pick


def _row_block(height):
    def pick(ref, q):
        idx = (slice(None),) * (len(ref.shape) - 2) + (pl.ds(pl.multiple_of(q * height, 16), height), slice(None))
        return ref.at[idx]
    return pick


def _block_of(axis_from_end, size):
    return _lane_block(size) if axis_from_end == 1 else _row_block(size)


def _gather_streams(shards):
    streams, outs = [], []
    for i, (shard, axis) in enumerate(shards):
        shape = list(shard.shape)
        if axis is None:
            shape, land = [N_DEV] + shape, _slot
        else:
            land = _block_of(axis, shape[-axis])
            shape[-axis] *= N_DEV
        streams.append((shard, i, _whole, land))
        outs.append(jax.ShapeDtypeStruct(tuple(shape), shard.dtype))
    return streams, outs


def _scatter_streams(fulls):
    streams, outs = [], []
    for i, (full, axis) in enumerate(fulls):
        shape = list(full.shape)
        shape[-axis] //= N_DEV
        streams.append((full, i, _block_of(axis, shape[-axis]), _slot))
        outs.append(jax.ShapeDtypeStruct((N_DEV,) + tuple(shape), full.dtype))
    return streams, outs


FFN_TM = 512
FFN_TF = 768


def _ffn_fwd(h, g, w13, w2, after=()):
    t = h.shape[0]
    tm, tf = min(FFN_TM, t), FFN_TF
    nf = D_FF_PAD // tf

    def body(h_ref, g_ref, w1_ref, w3_ref, w2_ref, *rest):
        ho_ref, n_ref, a_ref, b_ref, n_scr, acc = rest[len(after):]
        f = pl.program_id(1)

        @pl.when(f == 0)
        def _():
            n_scr[...] = _rms(h_ref[...], g_ref[...]).astype(MXU_DTYPE)
            n_ref[...] = n_scr[...]
            acc[...] = jnp.zeros_like(acc)

        n = n_scr[...]
        a = _dot(n, w1_ref[...])
        b = _dot(n, w3_ref[...])
        a_ref[...] = a.astype(MXU_DTYPE)
        b_ref[...] = b.astype(MXU_DTYPE)
        s = (a * _sigmoid(a)) * b
        acc[...] += _dot(s.astype(MXU_DTYPE), w2_ref[...])

        @pl.when(f == nf - 1)
        def _():
            ho_ref[...] = h_ref[...] + FFN_RES_WEIGHT * acc[...]

    return pl.pallas_call(
        body,
        name="ffn_fwd",
        grid=(t // tm, nf),
        in_specs=[
            _rows(tm, D_MODEL),
            _resident((1, D_MODEL)),
            pl.BlockSpec((None, D_MODEL, tf), lambda i, f: (0, 0, f)),
            pl.BlockSpec((None, D_MODEL, tf), lambda i, f: (1, 0, f)),
            pl.BlockSpec((tf, D_MODEL), lambda i, f: (f, 0)),
        ] + [pl.BlockSpec(memory_space=pl.ANY)] * len(after),
        out_specs=[
            _rows(tm, D_MODEL),
            _rows(tm, D_MODEL),
            pl.BlockSpec((tm, tf), lambda i, f: (i, f)),
            pl.BlockSpec((tm, tf), lambda i, f: (i, f)),
        ],
        out_shape=[
            jax.ShapeDtypeStruct((t, D_MODEL), F32),
            jax.ShapeDtypeStruct((t, D_MODEL), MXU_DTYPE),
            jax.ShapeDtypeStruct((t, D_FF_PAD), MXU_DTYPE),
            jax.ShapeDtypeStruct((t, D_FF_PAD), MXU_DTYPE),
        ],
        scratch_shapes=[pltpu.VMEM((tm, D_MODEL), MXU_DTYPE), pltpu.VMEM((tm, D_MODEL), F32)],
        compiler_params=_cparams("parallel", "arbitrary"),
    )(h, g, w13, w13, w2, *after)


def _ffn_bwd(dout, h, g, a, b, w13, w2, after=()):
    t = h.shape[0]
    tm, tf = min(FFN_TM, t), FFN_TF
    nf = D_FF_PAD // tf

    def body(do_ref, h_ref, g_ref, a_ref, b_ref, w1_ref, w3_ref, w2_ref, *rest):
        dh_ref, dob_ref, s_ref, dab_ref, dg_ref, dob_scr, acc = rest[len(after):]
        i, f = pl.program_id(0), pl.program_id(1)

        @pl.when(f == 0)
        def _():
            dob_scr[...] = (FFN_RES_WEIGHT * do_ref[...]).astype(MXU_DTYPE)
            dob_ref[...] = dob_scr[...]
            acc[...] = jnp.zeros_like(acc)

        @pl.when((f == 0) & (i == 0))
        def _():
            dg_ref[...] = jnp.zeros_like(dg_ref)

        ds = _dot_nt(dob_scr[...], w2_ref[...])
        av = a_ref[...].astype(F32)
        bv = b_ref[...].astype(F32)
        sig = _sigmoid(av)
        sil = av * sig
        s_ref[...] = (sil * bv).astype(MXU_DTYPE)
        da = (ds * bv * (sig * (1.0 + av * (1.0 - sig)))).astype(MXU_DTYPE)
        db = (ds * sil).astype(MXU_DTYPE)
        dab_ref[0] = da
        dab_ref[1] = db
        acc[...] += _dot_nt(da, w1_ref[...]) + _dot_nt(db, w3_ref[...])

        @pl.when(f == nf - 1)
        def _():
            dx, dg = _rms_bwd(h_ref[...], g_ref[...], acc[...])
            dh_ref[...] = do_ref[...] + dx
            dg_ref[...] += dg

    return pl.pallas_call(
        body,
        name="ffn_bwd",
        grid=(t // tm, nf),
        in_specs=[
            _rows(tm, D_MODEL),
            _rows(tm, D_MODEL),
            _resident((1, D_MODEL)),
            pl.BlockSpec((tm, tf), lambda i, f: (i, f)),
            pl.BlockSpec((tm, tf), lambda i, f: (i, f)),
            pl.BlockSpec((None, D_MODEL, tf), lambda i, f: (0, 0, f)),
            pl.BlockSpec((None, D_MODEL, tf), lambda i, f: (1, 0, f)),
            pl.BlockSpec((tf, D_MODEL), lambda i, f: (f, 0)),
        ] + [pl.BlockSpec(memory_space=pl.ANY)] * len(after),
        out_specs=[
            _rows(tm, D_MODEL),
            _rows(tm, D_MODEL),
            pl.BlockSpec((tm, tf), lambda i, f: (i, f)),
            pl.BlockSpec((2, tm, tf), lambda i, f: (0, i, f)),
            pl.BlockSpec((1, D_MODEL), lambda i, f: (0, 0)),
        ],
        out_shape=[
            jax.ShapeDtypeStruct((t, D_MODEL), F32),
            jax.ShapeDtypeStruct((t, D_MODEL), MXU_DTYPE),
            jax.ShapeDtypeStruct((t, D_FF_PAD), MXU_DTYPE),
            jax.ShapeDtypeStruct((2, t, D_FF_PAD), MXU_DTYPE),
            jax.ShapeDtypeStruct((1, D_MODEL), F32),
        ],
        scratch_shapes=[pltpu.VMEM((tm, D_MODEL), MXU_DTYPE), pltpu.VMEM((tm, D_MODEL), F32)],
        compiler_params=_cparams("arbitrary", "arbitrary"),
    )(dout, h, g, a, b, w13, w13, w2, *after)


def _wgrad(x, y, name, out_dtype=WIRE_DTYPE):
    t, m = x.shape
    grouped = y.ndim == 3
    groups = y.shape[0] if grouped else 1
    n = y.shape[-1]
    tk = min(1024, t)
    bm = m if m <= 1536 else m // 2
    bn = n if n <= 1536 else n // 2
    nk = t // tk

    def body(x_ref, y_ref, o_ref, acc):
        k = pl.program_id(3)

        @pl.when(k == 0)
        def _():
            acc[...] = jnp.zeros_like(acc)

        acc[...] += _dot_tn(x_ref[...].astype(MXU_DTYPE), y_ref[...].astype(MXU_DTYPE))

        @pl.when(k == nk - 1)
        def _():
            o_ref[...] = acc[...].astype(out_dtype)

    if grouped:
        y_spec = pl.BlockSpec((None, tk, bn), lambda g, i, j, k: (g, k, j))
        o_spec = pl.BlockSpec((None, bm, bn), lambda g, i, j, k: (g, i, j))
        o_shape = jax.ShapeDtypeStruct((groups, m, n), out_dtype)
    else:
        y_spec = pl.BlockSpec((tk, bn), lambda g, i, j, k: (k, j))
        o_spec = pl.BlockSpec((bm, bn), lambda g, i, j, k: (i, j))
        o_shape = jax.ShapeDtypeStruct((m, n), out_dtype)
    return pl.pallas_call(
        body,
        name=name,
        grid=(groups, m // bm, n // bn, nk),
        in_specs=[pl.BlockSpec((tk, bm), lambda g, i, j, k: (k, i)), y_spec],
        out_specs=o_spec,
        out_shape=o_shape,
        scratch_shapes=[pltpu.VMEM((bm, bn), F32)],
        compiler_params=_cparams("parallel", "parallel", "parallel", "arbitrary"),
    )(x, y)


def _conv_pre(h, g, w_pw1):
    t = h.shape[0]
    tm = min(512, t)

    def body(h_ref, g_ref, w_ref, n_ref, u_ref, glu_ref):
        n = _rms(h_ref[...], g_ref[...]).astype(MXU_DTYPE)
        n_ref[...] = n
        u = _dot(n, w_ref[...])
        u_ref[...] = u
        glu_ref[...] = u[:, :D_MODEL] * _sigmoid(u[:, D_MODEL:])

    return pl.pallas_call(
        body,
        name="conv_pre",
        grid=(t // tm,),
        in_specs=[_rows(tm, D_MODEL), _resident((1, D_MODEL)), _resident((D_MODEL, 2 * D_MODEL))],
        out_specs=[_rows(tm, D_MODEL), _rows(tm, 2 * D_MODEL), _rows(tm, D_MODEL)],
        out_shape=[
            jax.ShapeDtypeStruct((t, D_MODEL), MXU_DTYPE),
            jax.ShapeDtypeStruct((t, 2 * D_MODEL), F32),
            jax.ShapeDtypeStruct((t, D_MODEL), F32),
        ],
        compiler_params=_cparams("parallel"),
    )(h, g, w_pw1)


def _conv_main(glu, w_dw, g, w_pw2, h):
    t = h.shape[0]
    tm = min(512, t)
    per = tm // CONV_HALO

    def body(glu_ref, halo_ref, w_ref, g_ref, w2_ref, h_ref, ho_ref, c_ref, sw_ref, ext):
        i = pl.program_id(0)
        ext[pl.ds(0, CONV_HALO), :] = jnp.where(i == 0, 0.0, halo_ref[...])
        ext[pl.ds(CONV_HALO, tm), :] = glu_ref[...]
        acc = jnp.zeros((tm, D_MODEL), F32)
        for k in range(CONV_WIDTH):
            acc = acc + ext[pl.ds(CONV_HALO - (CONV_WIDTH - 1) + k, tm), :] * w_ref[pl.ds(k, 1), :]
        c_ref[...] = acc
        y = _rms(acc, g_ref[...])
        sw = (y * _sigmoid(y)).astype(MXU_DTYPE)
        sw_ref[...] = sw
        ho_ref[...] = h_ref[...] + _dot(sw, w2_ref[...])

    return pl.pallas_call(
        body,
        name="conv_main",
        grid=(t // tm,),
        in_specs=[
            _rows(tm, D_MODEL),
            pl.BlockSpec((CONV_HALO, D_MODEL), lambda i: (jnp.maximum(i * per - 1, 0), 0)),
            _resident((CONV_WIDTH, D_MODEL)),
            _resident((1, D_MODEL)),
            _resident((D_MODEL, D_MODEL)),
            _rows(tm, D_MODEL),
        ],
        out_specs=[_rows(tm, D_MODEL), _rows(tm, D_MODEL), _rows(tm, D_MODEL)],
        out_shape=[
            jax.ShapeDtypeStruct((t, D_MODEL), F32),
            jax.ShapeDtypeStruct((t, D_MODEL), F32),
            jax.ShapeDtypeStruct((t, D_MODEL), MXU_DTYPE),
        ],
        scratch_shapes=[pltpu.VMEM((tm + CONV_HALO, D_MODEL), F32)],
        compiler_params=_cparams("parallel"),
    )(glu, glu, w_dw, g, w_pw2, h)


def _conv_bwd_post(dout, c, g, w_pw2):
    t = dout.shape[0]
    tm = min(512, t)

    def body(do_ref, c_ref, g_ref, w2_ref, dc_ref, dg_ref):
        @pl.when(pl.program_id(0) == 0)
        def _():
            dg_ref[...] = jnp.zeros_like(dg_ref)

        dsw = _dot_nt(do_ref[...].astype(MXU_DTYPE), w2_ref[...])
        cv = c_ref[...]
        y = _rms(cv, g_ref[...])
        sig = _sigmoid(y)
        dy = dsw * (sig * (1.0 + y * (1.0 - sig)))
        dc, dg = _rms_bwd(cv, g_ref[...], dy)
        dc_ref[...] = dc
        dg_ref[...] += dg

    return pl.pallas_call(
        body,
        name="conv_bwd_post",
        grid=(t // tm,),
        in_specs=[_rows(tm, D_MODEL), _rows(tm, D_MODEL), _resident((1, D_MODEL)), _resident((D_MODEL, D_MODEL))],
        out_specs=[_rows(tm, D_MODEL), pl.BlockSpec((1, D_MODEL), lambda i: (0, 0))],
        out_shape=[jax.ShapeDtypeStruct((t, D_MODEL), F32), jax.ShapeDtypeStruct((1, D_MODEL), F32)],
        compiler_params=_cparams("arbitrary"),
    )(dout, c, g, w_pw2)


def _conv_bwd_pre(dc, glu, u, w_dw, w_pw1, h, g, dout):
    t = h.shape[0]
    tm = min(512, t)
    per = tm // CONV_HALO
    nt = t // tm
    last_halo = t // CONV_HALO - 1

    def body(dc_ref, dcn_ref, glu_ref, gluh_ref, u_ref, w_ref, w1_ref, h_ref, g_ref, do_ref,
             dh_ref, du_ref, dw_ref, dg_ref, dext, gext):
        i = pl.program_id(0)

        @pl.when(i == 0)
        def _():
            dw_ref[...] = jnp.zeros_like(dw_ref)
            dg_ref[...] = jnp.zeros_like(dg_ref)

        dext[pl.ds(0, tm), :] = dc_ref[...]
        dext[pl.ds(tm, CONV_HALO), :] = jnp.where(i == nt - 1, 0.0, dcn_ref[...])
        gext[pl.ds(0, CONV_HALO), :] = jnp.where(i == 0, 0.0, gluh_ref[...])
        gext[pl.ds(CONV_HALO, tm), :] = glu_ref[...]
        dcv = dc_ref[...]
        dglu = jnp.zeros((tm, D_MODEL), F32)
        for k in range(CONV_WIDTH):
            dglu = dglu + dext[pl.ds(CONV_WIDTH - 1 - k, tm), :] * w_ref[pl.ds(k, 1), :]
            prod = dcv * gext[pl.ds(CONV_HALO - (CONV_WIDTH - 1) + k, tm), :]
            dw_ref[pl.ds(k, 1), :] += jnp.sum(prod, axis=0, keepdims=True)
        uv = u_ref[...]
        av, bv = uv[:, :D_MODEL], uv[:, D_MODEL:]
        sig = _sigmoid(bv)
        du_ref[:, :D_MODEL] = (dglu * sig).astype(MXU_DTYPE)
        du_ref[:, D_MODEL:] = (dglu * av * (sig * (1.0 - sig))).astype(MXU_DTYPE)
        dn = _dot_nt(du_ref[...], w1_ref[...])
        dx, dg = _rms_bwd(h_ref[...], g_ref[...], dn)
        dh_ref[...] = do_ref[...] + dx
        dg_ref[...] += dg

    return pl.pallas_call(
        body,
        name="conv_bwd_pre",
        grid=(nt,),
        in_specs=[
            _rows(tm, D_MODEL),
            pl.BlockSpec((CONV_HALO, D_MODEL), lambda i: (jnp.minimum((i + 1) * per, last_halo), 0)),
            _rows(tm, D_MODEL),
            pl.BlockSpec((CONV_HALO, D_MODEL), lambda i: (jnp.maximum(i * per - 1, 0), 0)),
            _rows(tm, 2 * D_MODEL),
            _resident((CONV_WIDTH, D_MODEL)),
            _resident((D_MODEL, 2 * D_MODEL)),
            _rows(tm, D_MODEL),
            _resident((1, D_MODEL)),
            _rows(tm, D_MODEL),
        ],
        out_specs=[
            _rows(tm, D_MODEL),
            _rows(tm, 2 * D_MODEL),
            pl.BlockSpec((CONV_HALO, D_MODEL), lambda i: (0, 0)),
            pl.BlockSpec((1, D_MODEL), lambda i: (0, 0)),
        ],
        out_shape=[
            jax.ShapeDtypeStruct((t, D_MODEL), F32),
            jax.ShapeDtypeStruct((t, 2 * D_MODEL), MXU_DTYPE),
            jax.ShapeDtypeStruct((CONV_HALO, D_MODEL), F32),
            jax.ShapeDtypeStruct((1, D_MODEL), F32),
        ],
        scratch_shapes=[pltpu.VMEM((tm + CONV_HALO, D_MODEL), F32), pltpu.VMEM((tm + CONV_HALO, D_MODEL), F32)],
        compiler_params=_cparams("arbitrary"),
    )(dc, dc, glu, glu, u, w_dw, w_pw1, h, g, dout)


def _rope(s, cs, sa, sb):
    return s * cs - pltpu.roll(s, 96, 1) * sa + pltpu.roll(s, 32, 1) * sb


def _unrope(d, cs, sa, sb):
    return d * cs + pltpu.roll(d, 96, 1) * sa - pltpu.roll(d, 32, 1) * sb


def _mla_pre(h, g, w_a, g_q, g_kv, w_uq, w_ukv, cs, sa, sb):
    t = h.shape[0]
    tm = min(512, t)
    qw = N_HEADS * HEAD_PAD
    vw = N_HEADS * V_HEAD

    def body(h_ref, g_ref, wa_ref, gq_ref, gkv_ref, wuq_ref, wukv_ref, cs_ref, sa_ref, sb_ref,
             n_ref, a_ref, cq_ref, ckv_ref, q_ref, k_ref, v_ref):
        n = _rms(h_ref[...], g_ref[...]).astype(MXU_DTYPE)
        n_ref[...] = n
        a = _dot(n, wa_ref[...])
        a_ref[...] = a
        cq = _rms(a[:, :Q_LORA], gq_ref[...]).astype(MXU_DTYPE)
        ckv = _rms(a[:, Q_LORA:Q_LORA + KV_LORA], gkv_ref[...]).astype(MXU_DTYPE)
        cq_ref[...] = cq
        ckv_ref[...] = ckv
        q = _dot(cq, wuq_ref[...])
        kv = _dot(ckv, wukv_ref[...])
        cs_, sa_, sb_ = cs_ref[...], sa_ref[...], sb_ref[...]
        k_rot = _rope(a[:, Q_LORA + KV_LORA:], cs_, sa_, sb_).astype(MXU_DTYPE)
        for hd in range(N_HEADS):
            lo = hd * HEAD_PAD
            q_ref[:, lo:lo + QK_NOPE] = q[:, lo:lo + QK_NOPE].astype(MXU_DTYPE)
            q_ref[:, lo + QK_NOPE:lo + HEAD_PAD] = _rope(q[:, lo + QK_NOPE:lo + HEAD_PAD], cs_, sa_, sb_).astype(MXU_DTYPE)
            k_ref[:, lo:lo + QK_NOPE] = kv[:, lo:lo + QK_NOPE].astype(MXU_DTYPE)
            k_ref[:, lo + QK_NOPE:lo + HEAD_PAD] = k_rot
            v_ref[:, hd * V_HEAD:(hd + 1) * V_HEAD] = kv[:, lo + QK_NOPE:lo + HEAD_PAD].astype(MXU_DTYPE)

    return pl.pallas_call(
        body,
        name="mla_pre",
        grid=(t // tm,),
        in_specs=[
            _rows(tm, D_MODEL),
            _resident((1, D_MODEL)),
            _resident((D_MODEL, A_PAD)),
            _resident((1, Q_LORA)),
            _resident((1, KV_LORA)),
            _resident((Q_LORA, qw)),
            _resident((KV_LORA, qw)),
            _rows(tm, 128),
            _rows(tm, 128),
            _rows(tm, 128),
        ],
        out_specs=[_rows(tm, D_MODEL), _rows(tm, A_PAD), _rows(tm, Q_LORA), _rows(tm, KV_LORA), _rows(tm, qw), _rows(tm, qw), _rows(tm, vw)],
        out_shape=[
            jax.ShapeDtypeStruct((t, D_MODEL), MXU_DTYPE),
            jax.ShapeDtypeStruct((t, A_PAD), F32),
            jax.ShapeDtypeStruct((t, Q_LORA), MXU_DTYPE),
            jax.ShapeDtypeStruct((t, KV_LORA), MXU_DTYPE),
            jax.ShapeDtypeStruct((t, qw), MXU_DTYPE),
            jax.ShapeDtypeStruct((t, qw), MXU_DTYPE),
            jax.ShapeDtypeStruct((t, vw), MXU_DTYPE),
        ],
        compiler_params=_cparams("parallel"),
    )(h, g, w_a, g_q, g_kv, w_uq, w_ukv, cs, sa, sb)


def _attn_tile(t):
    return min(1024, t)


def _chunk_mask(qi, ki, tq, tk):
    rows = qi * tq + lax.broadcasted_iota(jnp.int32, (tq, tk), 0)
    cols = ki * tk + lax.broadcasted_iota(jnp.int32, (tq, tk), 1)
    return (rows >> CHUNK_SHIFT) >= (cols >> CHUNK_SHIFT)


def _flash_fwd(q, k, v):
    t = q.shape[0]
    tq = tk = _attn_tile(t)
    nk = t // tk

    def body(q_ref, k_ref, v_ref, o_ref, lse_ref, m_s, l_s, acc):
        qi, ki = pl.program_id(1), pl.program_id(2)

        @pl.when(ki == 0)
        def _():
            m_s[...] = jnp.full_like(m_s, -jnp.inf)
            l_s[...] = jnp.zeros_like(l_s)
            acc[...] = jnp.zeros_like(acc)

        def block(on_diagonal):
            s = _dot_nt(q_ref[...], k_ref[...]) * ATTN_SCALE
            if on_diagonal:
                s = jnp.where(_chunk_mask(qi, ki, tq, tk), s, -jnp.inf)
            m_prev = m_s[...]
            m_new = jnp.maximum(m_prev, jnp.max(s, axis=1, keepdims=True))
            alpha = jnp.exp(m_prev - m_new)
            p = jnp.exp(s - m_new)
            l_s[...] = alpha * l_s[...] + jnp.sum(p, axis=1, keepdims=True)
            acc[...] = alpha * acc[...] + _dot(p.astype(MXU_DTYPE), v_ref[...])
            m_s[...] = m_new

        pl.when(ki < qi)(lambda: block(False))
        pl.when(ki == qi)(lambda: block(True))

        @pl.when(ki == nk - 1)
        def _():
            o_ref[...] = acc[...] / l_s[...]
            lse_ref[...] = jnp.broadcast_to(m_s[...] + jnp.log(l_s[...]), (tq, V_HEAD))

    return pl.pallas_call(
        body,
        name="flash_fwd",
        grid=(N_HEADS, t // tq, nk),
        in_specs=[
            pl.BlockSpec((tq, HEAD_PAD), lambda h, i, j: (i, h)),
            pl.BlockSpec((tk, HEAD_PAD), lambda h, i, j: (jnp.minimum(i, j), h)),
            pl.BlockSpec((tk, V_HEAD), lambda h, i, j: (jnp.minimum(i, j), h)),
        ],
        out_specs=[pl.BlockSpec((tq, V_HEAD), lambda h, i, j: (i, h)), pl.BlockSpec((tq, V_HEAD), lambda h, i, j: (i, h))],
        out_shape=[jax.ShapeDtypeStruct((t, N_HEADS * V_HEAD), F32), jax.ShapeDtypeStruct((t, N_HEADS * V_HEAD), F32)],
        scratch_shapes=[pltpu.VMEM((tq, 1), F32), pltpu.VMEM((tq, 1), F32), pltpu.VMEM((tq, V_HEAD), F32)],
        compiler_params=_cparams("parallel", "parallel", "arbitrary"),
    )(q, k, v)


def _flash_bwd(q, k, v, do, lse, delta):
    t = q.shape[0]
    tq = tk = _attn_tile(t)
    nq = t // tq

    def body(q_ref, k_ref, v_ref, do_ref, lse_ref, dl_ref, dq_ref, dk_ref, dv_ref, dk_acc, dv_acc):
        kj, qi = pl.program_id(1), pl.program_id(2)

        @pl.when((kj == 0) & (qi == 0))
        def _():
            dq_ref[...] = jnp.zeros_like(dq_ref)

        @pl.when(qi == 0)
        def _():
            dk_acc[...] = jnp.zeros_like(dk_acc)
            dv_acc[...] = jnp.zeros_like(dv_acc)

        def block(on_diagonal):
            qv, kv, dov = q_ref[...], k_ref[...], do_ref[...]
            s = _dot_nt(qv, kv) * ATTN_SCALE
            if on_diagonal:
                s = jnp.where(_chunk_mask(qi, kj, tq, tk), s, -jnp.inf)
            p = jnp.exp(s - lse_ref[:, :1])
            dp = _dot_nt(dov, v_ref[...])
            ds = (p * (dp - dl_ref[:, :1]) * ATTN_SCALE).astype(MXU_DTYPE)
            dv_acc[...] += _dot_tn(p.astype(MXU_DTYPE), dov)
            dk_acc[...] += _dot_tn(ds, qv)
            rows = pl.ds(pl.multiple_of(qi * tq, tq), tq)
            dq_ref[rows, :] += _dot(ds, kv)

        pl.when(qi > kj)(lambda: block(False))
        pl.when(qi == kj)(lambda: block(True))

        @pl.when(qi == nq - 1)
        def _():
            dk_ref[...] = dk_acc[...]
            dv_ref[...] = dv_acc[...]

    return pl.pallas_call(
        body,
        name="flash_bwd",
        grid=(N_HEADS, t // tk, nq),
        in_specs=[
            pl.BlockSpec((tq, HEAD_PAD), lambda h, j, i: (jnp.maximum(i, j), h)),
            pl.BlockSpec((tk, HEAD_PAD), lambda h, j, i: (j, h)),
            pl.BlockSpec((tk, V_HEAD), lambda h, j, i: (j, h)),
            pl.BlockSpec((tq, V_HEAD), lambda h, j, i: (jnp.maximum(i, j), h)),
            pl.BlockSpec((tq, V_HEAD), lambda h, j, i: (jnp.maximum(i, j), h)),
            pl.BlockSpec((tq, V_HEAD), lambda h, j, i: (jnp.maximum(i, j), h)),
        ],
        out_specs=[
            pl.BlockSpec((t, HEAD_PAD), lambda h, j, i: (0, h)),
            pl.BlockSpec((tk, HEAD_PAD), lambda h, j, i: (j, h)),
            pl.BlockSpec((tk, V_HEAD), lambda h, j, i: (j, h)),
        ],
        out_shape=[
            jax.ShapeDtypeStruct((t, N_HEADS * HEAD_PAD), F32),
            jax.ShapeDtypeStruct((t, N_HEADS * HEAD_PAD), F32),
            jax.ShapeDtypeStruct((t, N_HEADS * V_HEAD), F32),
        ],
        scratch_shapes=[pltpu.VMEM((tk, HEAD_PAD), F32), pltpu.VMEM((tk, V_HEAD), F32)],
        compiler_params=_cparams("arbitrary", "arbitrary", "arbitrary"),
    )(q, k, v, do, lse, delta)


def _attn_out(o, w_o, h):
    t = h.shape[0]
    tm = min(512, t)

    def body(o_ref, w_ref, h_ref, ho_ref):
        ho_ref[...] = h_ref[...] + _dot(o_ref[...].astype(MXU_DTYPE), w_ref[...])

    return pl.pallas_call(
        body,
        name="attn_out",
        grid=(t // tm,),
        in_specs=[_rows(tm, D_MODEL), _resident((D_MODEL, D_MODEL)), _rows(tm, D_MODEL)],
        out_specs=_rows(tm, D_MODEL),
        out_shape=jax.ShapeDtypeStruct((t, D_MODEL), F32),
        compiler_params=_cparams("parallel"),
    )(o, w_o, h)


def _attn_out_bwd(dout, o, w_o):
    t = dout.shape[0]
    tm = min(512, t)

    def body(d_ref, o_ref, w_ref, do_ref, dl_ref):
        do = _dot_nt(d_ref[...].astype(MXU_DTYPE), w_ref[...])
        do_ref[...] = do.astype(MXU_DTYPE)
        prod = do * o_ref[...]
        for hd in range(N_HEADS):
            lanes = slice(hd * V_HEAD, (hd + 1) * V_HEAD)
            dl_ref[:, lanes] = jnp.broadcast_to(jnp.sum(prod[:, lanes], axis=1, keepdims=True), (tm, V_HEAD))

    return pl.pallas_call(
        body,
        name="attn_out_bwd",
        grid=(t // tm,),
        in_specs=[_rows(tm, D_MODEL), _rows(tm, D_MODEL), _resident((D_MODEL, D_MODEL))],
        out_specs=[_rows(tm, D_MODEL), _rows(tm, D_MODEL)],
        out_shape=[jax.ShapeDtypeStruct((t, D_MODEL), MXU_DTYPE), jax.ShapeDtypeStruct((t, D_MODEL), F32)],
        compiler_params=_cparams("parallel"),
    )(dout, o, w_o)


def _mla_bwd_pre(dq, dk, dv, a, h, dout, g, g_q, g_kv, w_a, w_uq, w_ukv, cs, sa, sb):
    t = h.shape[0]
    tm = min(512, t)
    qw = N_HEADS * HEAD_PAD
    vw = N_HEADS * V_HEAD

    def body(dq_ref, dk_ref, dv_ref, a_ref, h_ref, do_ref, g_ref, gq_ref, gkv_ref, wa_ref, wuq_ref, wukv_ref,
             cs_ref, sa_ref, sb_ref, dh_ref, dqp_ref, dkv_ref, da_ref, dg_ref, dgq_ref, dgkv_ref):
        @pl.when(pl.program_id(0) == 0)
        def _():
            dg_ref[...] = jnp.zeros_like(dg_ref)
            dgq_ref[...] = jnp.zeros_like(dgq_ref)
            dgkv_ref[...] = jnp.zeros_like(dgkv_ref)

        cs_, sa_, sb_ = cs_ref[...], sa_ref[...], sb_ref[...]
        slab = jnp.zeros((tm, 128), F32)
        for hd in range(N_HEADS):
            lo = hd * HEAD_PAD
            dqp_ref[:, lo:lo + QK_NOPE] = dq_ref[:, lo:lo + QK_NOPE].astype(MXU_DTYPE)
            dqp_ref[:, lo + QK_NOPE:lo + HEAD_PAD] = _unrope(dq_ref[:, lo + QK_NOPE:lo + HEAD_PAD], cs_, sa_, sb_).astype(MXU_DTYPE)
            dkv_ref[:, lo:lo + QK_NOPE] = dk_ref[:, lo:lo + QK_NOPE].astype(MXU_DTYPE)
            dkv_ref[:, lo + QK_NOPE:lo + HEAD_PAD] = dv_ref[:, hd * V_HEAD:(hd + 1) * V_HEAD].astype(MXU_DTYPE)
            slab = slab + dk_ref[:, lo + QK_NOPE:lo + HEAD_PAD]
        dcq = _dot_nt(dqp_ref[...], wuq_ref[...])
        dckv = _dot_nt(dkv_ref[...], wukv_ref[...])
        av = a_ref[...]
        daq, dgq = _rms_bwd(av[:, :Q_LORA], gq_ref[...], dcq)
        dakv, dgkv = _rms_bwd(av[:, Q_LORA:Q_LORA + KV_LORA], gkv_ref[...], dckv)
        da_ref[:, :Q_LORA] = daq.astype(MXU_DTYPE)
        da_ref[:, Q_LORA:Q_LORA + KV_LORA] = dakv.astype(MXU_DTYPE)
        da_ref[:, Q_LORA + KV_LORA:] = _unrope(slab, cs_, sa_, sb_).astype(MXU_DTYPE)
        dn = _dot_nt(da_ref[...], wa_ref[...])
        dx, dg = _rms_bwd(h_ref[...], g_ref[...], dn)
        dh_ref[...] = do_ref[...] + dx
        dg_ref[...] += dg
        dgq_ref[...] += dgq
        dgkv_ref[...] += dgkv

    def const(n):
        return pl.BlockSpec((1, n), lambda i: (0, 0))

    return pl.pallas_call(
        body,
        name="mla_bwd_pre",
        grid=(t // tm,),
        in_specs=[
            _rows(tm, qw), _rows(tm, qw), _rows(tm, vw), _rows(tm, A_PAD), _rows(tm, D_MODEL), _rows(tm, D_MODEL),
            _resident((1, D_MODEL)), _resident((1, Q_LORA)), _resident((1, KV_LORA)),
            _resident((D_MODEL, A_PAD)), _resident((Q_LORA, qw)), _resident((KV_LORA, qw)),
            _rows(tm, 128), _rows(tm, 128), _rows(tm, 128),
        ],
        out_specs=[_rows(tm, D_MODEL), _rows(tm, qw), _rows(tm, qw), _rows(tm, A_PAD),
                   const(D_MODEL), const(Q_LORA), const(KV_LORA)],
        out_shape=[
            jax.ShapeDtypeStruct((t, D_MODEL), F32),
            jax.ShapeDtypeStruct((t, qw), MXU_DTYPE),
            jax.ShapeDtypeStruct((t, qw), MXU_DTYPE),
            jax.ShapeDtypeStruct((t, A_PAD), MXU_DTYPE),
            jax.ShapeDtypeStruct((1, D_MODEL), F32),
            jax.ShapeDtypeStruct((1, Q_LORA), F32),
            jax.ShapeDtypeStruct((1, KV_LORA), F32),
        ],
        compiler_params=_cparams("arbitrary"),
    )(dq, dk, dv, a, h, dout, g, g_q, g_kv, w_a, w_uq, w_ukv, cs, sa, sb)


def _loss_head(h, g, target):
    t = h.shape[0]
    tm = min(512, t)

    def body(h_ref, g_ref, t_ref, sq_ref, dh_ref, dg_ref):
        @pl.when(pl.program_id(0) == 0)
        def _():
            sq_ref[...] = jnp.zeros_like(sq_ref)
            dg_ref[...] = jnp.zeros_like(dg_ref)

        x = h_ref[...]
        err = _rms(x, g_ref[...]) - t_ref[...]
        sq_ref[...] += jnp.sum(err * err, axis=0, keepdims=True)
        dx, dg = _rms_bwd(x, g_ref[...], err * (1.0 / D_MODEL))
        dh_ref[...] = dx
        dg_ref[...] += dg

    return pl.pallas_call(
        body,
        name="loss_head",
        grid=(t // tm,),
        in_specs=[_rows(tm, D_MODEL), _resident((1, D_MODEL)), _rows(tm, D_MODEL)],
        out_specs=[pl.BlockSpec((1, D_MODEL), lambda i: (0, 0)), _rows(tm, D_MODEL), pl.BlockSpec((1, D_MODEL), lambda i: (0, 0))],
        out_shape=[jax.ShapeDtypeStruct((1, D_MODEL), F32), jax.ShapeDtypeStruct((t, D_MODEL), F32), jax.ShapeDtypeStruct((1, D_MODEL), F32)],
        compiler_params=_cparams("arbitrary"),
    )(h, g, target)


def _adamw(parts, w, m, v, name, tr, layer=0, row_offset=0, into=None):
    layers, rows, width = w.shape
    pwidth = parts.shape[-1]
    off = row_offset // tr

    def body(p_ref, w_ref, m_ref, v_ref, *rest):
        g_ref, d_ref, mo_ref, vo_ref = rest[-4:]

        def part(q):
            return p_ref[q, :, pl.ds(0, width)].astype(F32)

        g = part(0)
        for q in range(1, N_DEV):
            g = g + part(q)
        g_ref[...] = g
        m_new = ADAM_B1 * m_ref[...] + (1.0 - ADAM_B1) * g
        v_new = ADAM_B2 * v_ref[...] + (1.0 - ADAM_B2) * (g * g)
        m_hat = m_new / (1.0 - ADAM_B1 ** ADAM_STEP)
        v_hat = v_new / (1.0 - ADAM_B2 ** ADAM_STEP)
        d_ref[...] = -ADAM_LR * (m_hat / (jnp.sqrt(v_hat) + ADAM_EPS) + ADAM_WD * w_ref[...])
        mo_ref[...] = m_new
        vo_ref[...] = v_new

    blk = pl.BlockSpec((None, tr, width), lambda i: (layer, i, 0))
    earlier = () if into is None else tuple(into)
    return pl.pallas_call(
        body,
        name=name,
        grid=(rows // tr,),
        in_specs=[pl.BlockSpec((N_DEV, tr, pwidth), lambda i: (0, off + i, 0)), blk, blk, blk] + [pl.BlockSpec(memory_space=pl.ANY)] * len(earlier),
        out_specs=[blk, blk, blk, blk],
        out_shape=[jax.ShapeDtypeStruct((layers, rows, width), F32)] * 4,
        input_output_aliases={4 + j: j for j in range(len(earlier))},
        compiler_params=_cparams("parallel"),
    )(parts, w, m, v, *earlier)


_GAINS = ("ffn_norm1", "mix_norm", "ffn_norm2", "conv_norm", "final_norm")
_TINY = ("conv_w_dw", "mla_q_norm", "mla_kv_norm")


def _small_pack(parts):
    flat = jnp.concatenate([p.reshape(-1).astype(F32) for p in parts])
    return jnp.pad(flat, (0, SMALL_ROWS * D_MODEL - flat.shape[0])).reshape(SMALL_ROWS, D_MODEL)


def _small_unpack(packed, shapes):
    flat, out, off = packed.reshape(-1), [], 0
    for s in shapes:
        size = 1
        for d in s:
            size *= d
        out.append(flat[off:off + size].reshape(s))
        off += size
    return out


def _pad_to(a, axis, size):
    pad = [(0, 0)] * a.ndim
    pad[axis] = (0, size - a.shape[axis])
    return jnp.pad(a, pad)


def kernel(x, positions, ffn_norm1, ffn1_w1, ffn1_w3, ffn1_w2, mix_norm, ffn_norm2, ffn2_w1, ffn2_w3, ffn2_w2, conv_w_pw1, conv_w_dw, conv_norm, conv_w_pw2, mla_w_a, mla_q_norm, mla_kv_norm, mla_w_uq, mla_w_ukv, mla_w_o, final_norm, loss_target, m_ffn_norm1, m_ffn1_w1, m_ffn1_w3, m_ffn1_w2, m_mix_norm, m_ffn_norm2, m_ffn2_w1, m_ffn2_w3, m_ffn2_w2, m_conv_w_pw1, m_conv_w_dw, m_conv_norm, m_conv_w_pw2, m_mla_w_a, m_mla_q_norm, m_mla_kv_norm, m_mla_w_uq, m_mla_w_ukv, m_mla_w_o, m_final_norm, v_ffn_norm1, v_ffn1_w1, v_ffn1_w3, v_ffn1_w2, v_mix_norm, v_ffn_norm2, v_ffn2_w1, v_ffn2_w3, v_ffn2_w2, v_conv_w_pw1, v_conv_w_dw, v_conv_norm, v_conv_w_pw2, v_mla_w_a, v_mla_q_norm, v_mla_kv_norm, v_mla_w_uq, v_mla_w_ukv, v_mla_w_o, v_final_norm):
    args = dict(locals())
    wire = lambda a: a.astype(WIRE_DTYPE)

    def ffn_shards(prefix, layer):
        w13_shard = _pad_to(jnp.stack([args[prefix + "_w1"][layer], args[prefix + "_w3"][layer]]), 2, FF_SHARD_PAD)
        return [(wire(w13_shard), 1), (wire(_pad_to(args[prefix + "_w2"][layer], 0, FF_SHARD_PAD)), 2)]

    def uq_pad(a):
        return _pad_to(a, 3, HEAD_PAD).reshape(1, a.shape[1], N_HEADS * HEAD_PAD)

    ukv_rows = lambda a: a.reshape(1, a.shape[1], N_HEADS * HEAD_PAD)
    tiny_shapes = [conv_w_dw.shape, mla_q_norm.shape, mla_kv_norm.shape]

    first = _exchange(*_gather_streams(ffn_shards("ffn1", 0)), "weight_gather_first")
    later_groups = [
        [(wire(conv_w_pw1[0]), 1), (wire(conv_w_pw2[0]), 2), (_small_pack([conv_w_dw, mla_q_norm, mla_kv_norm]), None)],
        ffn_shards("ffn2", 0),
        ffn_shards("ffn1", 1),
        [(wire(_pad_to(mla_w_a[0], 1, A_PAD)), 2), (wire(uq_pad(mla_w_uq)[0]), 2), (wire(ukv_rows(mla_w_ukv)[0]), 2), (wire(mla_w_o[0]), 2)],
        ffn_shards("ffn2", 1),
    ]
    later_streams, later_outs = _gather_streams([sh for grp in later_groups for sh in grp])
    group_ids, at = [], 0
    for grp in later_groups:
        group_ids.append(list(range(at, at + len(grp))))
        at += len(grp)
    gather, gather_token = _exchange_start(
        later_streams, _place_own(later_streams, later_outs, "weight_gather_own"), group_ids, (first[0],), "weight_gather_start")

    inv_freq = ROPE_THETA ** (-2.0 * jnp.arange(QK_ROPE // 2, dtype=F32) / QK_ROPE)
    ang = positions[0].astype(F32)[:, None] * inv_freq
    cos, sin, zero = jnp.cos(ang), jnp.sin(ang), jnp.zeros_like(ang)
    cs = jnp.concatenate([cos, cos, zero, zero], axis=1)
    sa = jnp.concatenate([sin, zero, zero, zero], axis=1)
    sb = jnp.concatenate([zero, sin, zero, zero], axis=1)

    ffn_w = {("ffn1", 0): first}
    h0 = x[0]
    h1, n1, a1, b1 = _ffn_fwd(h0, ffn_norm1[0:1], *ffn_w[("ffn1", 0)], after=(gather_token,))
    w_pw1, w_pw2, tiny = _exchange_wait(gather, 0, (h1,), "weight_gather_wait_conv")
    tiny = [_small_unpack(tiny[q], tiny_shapes) for q in range(N_DEV)]
    w_dw = jnp.concatenate([t_[0][0] for t_ in tiny], axis=1)
    g_q = jnp.concatenate([t_[1] for t_ in tiny], axis=1)
    g_kv = jnp.concatenate([t_[2] for t_ in tiny], axis=1)
    n_c, u, glu = _conv_pre(h1, mix_norm[0:1], w_pw1)
    h2, c, sw = _conv_main(glu, w_dw, conv_norm, w_pw2, h1)
    ffn_w[("ffn2", 0)] = _exchange_wait(gather, 1, (h2,), "weight_gather_wait_ffn2_0")
    h3, n2, a2, b2 = _ffn_fwd(h2, ffn_norm2[0:1], *ffn_w[("ffn2", 0)])
    ffn_w[("ffn1", 1)] = _exchange_wait(gather, 2, (h3,), "weight_gather_wait_ffn1_1")
    h4, n3, a3, b3 = _ffn_fwd(h3, ffn_norm1[1:2], *ffn_w[("ffn1", 1)])
    w_a, w_uq, w_ukv, w_o = _exchange_wait(gather, 3, (h4,), "weight_gather_wait_mla")
    n_a, a_lat, cq, ckv, q, k, v = _mla_pre(h4, mix_norm[1:2], w_a, g_q, g_kv, w_uq, w_ukv, cs, sa, sb)
    o, lse = _flash_fwd(q, k, v)
    h5 = _attn_out(o, w_o, h4)
    ffn_w[("ffn2", 1)] = _exchange_wait(gather, 4, (h5,), "weight_gather_wait_ffn2_1")
    h6, n4, a4, b4 = _ffn_fwd(h5, ffn_norm2[1:2], *ffn_w[("ffn2", 1)])

    sq, dh, dg_final = _loss_head(h6, final_norm[None, :], loss_target[0])
    loss = lax.psum(0.5 / D_MODEL * jnp.sum(sq), ("x", "y", "c"))

    gain = {}

    def ffn_backward(prefix, norm, layer, dh, h_in, n, a, b, after=()):
        dh, dob, s, dab, gain[(norm, layer)] = _ffn_bwd(dh, h_in, args[norm][layer:layer + 1], a, b, *ffn_w[(prefix, layer)], after=after)
        return dh, [(_wgrad(n, dab, "wgrad_ffn_in"), 1), (_wgrad(s, dob, "wgrad_ffn_out"), 2)]

    dh, g_ffn2_1 = ffn_backward("ffn2", "ffn_norm2", 1, dh, h5, n4, a4, b4)
    do, delta = _attn_out_bwd(dh, o, w_o)
    d_o = _wgrad(o, dh, "wgrad_attn_out")
    dq, dk, dv = _flash_bwd(q, k, v, do, lse, delta)
    dh, dqp, dkv, da_lat, gain[("mix_norm", 1)], d_gq, d_gkv = _mla_bwd_pre(
        dq, dk, dv, a_lat, h4, dh, mix_norm[1:2], g_q, g_kv, w_a, w_uq, w_ukv, cs, sa, sb)
    g_mla = [(_wgrad(n_a, da_lat, "wgrad_mla_a"), 2), (_wgrad(cq, dqp, "wgrad_mla_uq"), 2), (_wgrad(ckv, dkv, "wgrad_mla_ukv"), 2), (d_o, 2)]
    dh, g_ffn1_1 = ffn_backward("ffn1", "ffn_norm1", 1, dh, h3, n3, a3, b3)
    streams_a, outs_a = _scatter_streams(g_ffn2_1 + g_mla + g_ffn1_1)
    scatter_a, token_a = _exchange_start(
        streams_a, _place_own(streams_a, outs_a, "grad_scatter_own_layer1"), [list(range(len(streams_a)))], (), "grad_scatter_start_layer1")

    dh, g_ffn2_0 = ffn_backward("ffn2", "ffn_norm2", 0, dh, h2, n2, a2, b2, after=(token_a,))
    dc, gain[("conv_norm", 0)] = _conv_bwd_post(dh, c, conv_norm, w_pw2)
    d_pw2 = _wgrad(sw, dh, "wgrad_conv_pw2")
    dh, du, d_dw, gain[("mix_norm", 0)] = _conv_bwd_pre(dc, glu, u, w_dw, w_pw1, h1, mix_norm[0:1], dh)
    d_pw1 = _wgrad(n_c, du, "wgrad_conv_pw1")
    streams_b, outs_b = _scatter_streams(g_ffn2_0 + [(d_pw1, 1), (d_pw2, 2)])
    scatter_b, token_b = _exchange_start(
        streams_b, _place_own(streams_b, outs_b, "grad_scatter_own_conv"), [list(range(len(streams_b)))], (), "grad_scatter_start_conv")

    dh, g_ffn1_0 = ffn_backward("ffn1", "ffn_norm1", 0, dh, h0, n1, a1, b1, after=(token_b,))
    grad_x = dh[None]

    gain_rows = jnp.concatenate([
        gain[("ffn_norm1", 0)], gain[("ffn_norm1", 1)], gain[("mix_norm", 0)], gain[("mix_norm", 1)],
        gain[("ffn_norm2", 0)], gain[("ffn_norm2", 1)], gain[("conv_norm", 0)], dg_final])
    dw_by_dest = jnp.moveaxis(d_dw[:CONV_WIDTH].reshape(CONV_WIDTH, N_DEV, -1), 1, 0)
    small_by_dest = jnp.stack([
        _small_pack([gain_rows, dw_by_dest[p], d_gq.reshape(N_DEV, -1)[p], d_gkv.reshape(N_DEV, -1)[p]]) for p in range(N_DEV)])
    streams_c, outs_c = _scatter_streams(g_ffn1_0)
    streams_c.append((small_by_dest, len(outs_c), _slot, _slot))
    outs_c.append(jax.ShapeDtypeStruct((N_DEV, SMALL_ROWS, D_MODEL), F32))
    p13_ffn1_0, p2_ffn1_0, p_small = _exchange(streams_c, outs_c, "grad_scatter_last")
    (p13_ffn2_1, p2_ffn2_1, p_a, p_uq, p_ukv, p_o, p13_ffn1_1, p2_ffn1_1) = _exchange_wait(scatter_a, 0, (p_small,), "grad_scatter_wait_layer1")
    p13_ffn2_0, p2_ffn2_0, p_pw1, p_pw2 = _exchange_wait(scatter_b, 0, (p_small,), "grad_scatter_wait_conv")

    results = {}

    def adam(name, parts, tr, view=lambda a: a, **kw):
        results[name] = _adamw(parts, view(args[name]), view(args["m_" + name]), view(args["v_" + name]), "adamw_" + name, tr,
                               into=results.get(name), **kw)

    for prefix, p13s, p2s in (("ffn1", (p13_ffn1_0, p13_ffn1_1), (p2_ffn1_0, p2_ffn1_1)), ("ffn2", (p13_ffn2_0, p13_ffn2_1), (p2_ffn2_0, p2_ffn2_1))):
        for layer in range(2):
            p13 = p13s[layer].reshape(N_DEV, 2 * D_MODEL, FF_SHARD_PAD)
            adam(prefix + "_w1", p13, 256, layer=layer)
            adam(prefix + "_w3", p13, 256, layer=layer, row_offset=D_MODEL)
            adam(prefix + "_w2", p2s[layer], FF_SHARD, layer=layer)
    adam("conv_w_pw1", p_pw1, 256)
    adam("conv_w_pw2", p_pw2, 128)
    adam("mla_w_a", p_a, 128)
    adam("mla_w_uq", p_uq, 64, view=uq_pad)
    adam("mla_w_ukv", p_ukv, 32, view=ukv_rows)
    adam("mla_w_o", p_o, 128)
    results["mla_w_uq"] = [r.reshape(1, -1, N_HEADS, HEAD_PAD)[..., :QK_NOPE + QK_ROPE] for r in results["mla_w_uq"]]
    results["mla_w_ukv"] = [r.reshape(mla_w_ukv.shape) for r in results["mla_w_ukv"]]

    small_names = _GAINS + _TINY
    small_shapes = [args[n].shape for n in small_names]
    pack_small = lambda prefix: _small_pack([args[prefix + n] for n in small_names])[None]
    small = _adamw(p_small, pack_small(""), pack_small("m_"), pack_small("v_"), "adamw_small", SMALL_ROWS)
    for kind in range(4):
        for n, leaf in zip(small_names, _small_unpack(small[kind][0], small_shapes)):
            results.setdefault(n, [None] * 4)[kind] = leaf

    order = ("ffn_norm1", "ffn1_w1", "ffn1_w3", "ffn1_w2", "mix_norm", "ffn_norm2", "ffn2_w1", "ffn2_w3", "ffn2_w2",
             "conv_w_pw1", "conv_w_dw", "conv_norm", "conv_w_pw2", "mla_w_a", "mla_q_norm", "mla_kv_norm",
             "mla_w_uq", "mla_w_ukv", "mla_w_o", "final_norm")
    outputs = [loss, grad_x]
    for kind in range(4):
        outputs.extend(results[n][kind] for n in order)
    return tuple(outputs)
```

```python
import jax
import jax.numpy as jnp
from jax import lax
from jax.experimental import pallas as pl
from jax.experimental.pallas import tpu as pltpu

F32 = jnp.float32
MXU_DTYPE = jnp.bfloat16
WIRE_DTYPE = jnp.bfloat16

N_DEV = 8
D_MODEL = 1024
FF_SHARD = 352
FF_SHARD_PAD = 384
D_FF_PAD = N_DEV * FF_SHARD_PAD
N_HEADS = 8
QK_NOPE = 128
QK_ROPE = 64
V_HEAD = 128
Q_LORA = 512
KV_LORA = 256
HEAD_PAD = 256
A_WIDTH = Q_LORA + KV_LORA + QK_ROPE
A_PAD = Q_LORA + KV_LORA + 128
CONV_WIDTH = 31
CONV_HALO = 32
CHUNK_SHIFT = 6
ROPE_THETA = 10000.0
RMS_EPS = 1e-6
ATTN_SCALE = (QK_NOPE + QK_ROPE) ** -0.5
FFN_RES_WEIGHT = 0.5
ADAM_LR = 0.001
ADAM_B1 = 0.9
ADAM_B2 = 0.999
ADAM_EPS = 1e-08
ADAM_WD = 0.01
ADAM_STEP = 10

SMALL_ROWS = 16
VMEM_LIMIT = 56 << 20


def _cparams(*sem):
    return pltpu.CompilerParams(dimension_semantics=sem, vmem_limit_bytes=VMEM_LIMIT)


def _dot(a, b):
    return lax.dot_general(a, b, (((1,), (0,)), ((), ())), preferred_element_type=F32)


def _dot_nt(a, b):
    return lax.dot_general(a, b, (((1,), (1,)), ((), ())), preferred_element_type=F32)


def _dot_tn(a, b):
    return lax.dot_general(a, b, (((0,), (0,)), ((), ())), preferred_element_type=F32)


def _resident(shape, index=None):
    fixed = index if index is not None else (0,) * len(shape)
    return pl.BlockSpec(shape, lambda *_: fixed, pipeline_mode=pl.Buffered(1))


def _rows(tm, n):
    return pl.BlockSpec((tm, n), lambda t, *_: (t, 0))


def _rms(x, g):
    r = lax.rsqrt(jnp.mean(x * x, axis=-1, keepdims=True) + RMS_EPS)
    return x * r * g


def _rms_bwd(x, g, dy):
    r = lax.rsqrt(jnp.mean(x * x, axis=-1, keepdims=True) + RMS_EPS)
    xr = x * r
    dg = jnp.sum(dy * xr, axis=0, keepdims=True)
    u = dy * g
    dx = r * (u - xr * jnp.mean(u * xr, axis=-1, keepdims=True))
    return dx, dg


def _sigmoid(x):
    return 1.0 / (1.0 + jnp.exp(-x))


def _me():
    return 4 * lax.axis_index("x") + 2 * lax.axis_index("y") + lax.axis_index("c")


def _peer(k):
    px, py, pc = lax.axis_index("x") ^ ((k >> 2) & 1), lax.axis_index("y") ^ ((k >> 1) & 1), lax.axis_index("c") ^ (k & 1)
    return (px, py, pc), 4 * px + 2 * py + pc


def _stream_copies(stream, src_ref, dst_ref, send_sem, recv_sem, base):
    send, land = stream[2], stream[3]
    me = _me()
    outgoing, incoming = [], []
    for k in range(1, N_DEV):
        pos, p = _peer(k)
        common = dict(send_sem=send_sem.at[base + k - 1], recv_sem=recv_sem.at[base + k - 1], device_id=pos, device_id_type=pl.DeviceIdType.MESH)
        outgoing.append(pltpu.make_async_remote_copy(src_ref=send(src_ref, p), dst_ref=land(dst_ref, me), **common))
        incoming.append(pltpu.make_async_remote_copy(src_ref=send(src_ref, p), dst_ref=land(dst_ref, p), **common))
    return outgoing, incoming


_HBM = pl.BlockSpec(memory_space=pltpu.HBM)
_SEM = pl.BlockSpec(memory_space=pltpu.SEMAPHORE)


def _exchange(streams, outs, name):
    n, n_out = len(streams), len(outs)

    def body(*refs):
        srcs, dsts = refs[:n], refs[n:n + n_out]
        send_sem, recv_sem, local_sem = refs[n + n_out:]
        me = _me()
        own = [pltpu.make_async_copy(st[2](srcs[e], me), st[3](dsts[st[1]], me), local_sem.at[e]) for e, st in enumerate(streams)]
        for cp in own:
            cp.start()
        sends, recvs = [], []
        for e, st in enumerate(streams):
            outgoing, incoming = _stream_copies(st, srcs[e], dsts[st[1]], send_sem, recv_sem, e * (N_DEV - 1))
            sends += outgoing
            recvs += incoming
        for cp in sends:
            cp.start()
        for cp in recvs:
            cp.wait_recv()
        for cp in sends:
            cp.wait_send()
        for cp in own:
            cp.wait()

    return pl.pallas_call(
        body,
        name=name,
        out_shape=tuple(outs),
        in_specs=[_HBM] * n,
        out_specs=tuple([_HBM] * n_out),
        scratch_shapes=[pltpu.SemaphoreType.DMA((n * (N_DEV - 1),)), pltpu.SemaphoreType.DMA((n * (N_DEV - 1),)),
                        pltpu.SemaphoreType.DMA((n,))],
        compiler_params=pltpu.CompilerParams(vmem_limit_bytes=VMEM_LIMIT),
    )(*[st[0] for st in streams])


def _landing(streams, outs):
    me = _me()
    lands = [lax.empty(o.shape, o.dtype) for o in outs]
    for st in streams:
        lands[st[1]] = st[4](lands[st[1]], st[0], me)
    return lands


def _exchange_start(streams, lands, groups, after, name):
    n, n_land, n_grp = len(streams), len(lands), len(groups)
    side_effects = pltpu.SideEffectType.DATAFLOW_SIDE_EFFECTING

    def body(*refs):
        srcs, dsts = refs[:n], refs[n:n + n_land]
        outs = refs[n + n_land + len(after):]
        send_sems, recv_sems, token = outs[:n_grp], outs[n_grp:2 * n_grp], outs[-1]
        for gi, group in enumerate(groups):
            for j, e in enumerate(group):
                outgoing, _ = _stream_copies(streams[e], srcs[e], dsts[streams[e][1]], send_sems[gi], recv_sems[gi], j * (N_DEV - 1))
                for cp in outgoing:
                    cp.start()
        token[...] = jnp.zeros_like(token)

    sems = tuple(pltpu.SemaphoreType.DMA((len(g) * (N_DEV - 1),)) for g in groups)
    passed = [st[0] for st in streams] + list(lands)
    res = pl.pallas_call(
        body,
        name=name,
        out_shape=sems + sems + tuple(pltpu.HBM(a.shape, a.dtype) for a in passed) + (jax.ShapeDtypeStruct((8, 128), F32),),
        in_specs=[_HBM] * (n + n_land) + [pl.BlockSpec(memory_space=pl.ANY)] * len(after),
        out_specs=tuple([_SEM] * (2 * n_grp) + [_HBM] * (n + n_land) + [pl.BlockSpec(memory_space=pltpu.VMEM)]),
        input_output_aliases={i: 2 * n_grp + i for i in range(n + n_land)},
        compiler_params=pltpu.CompilerParams(has_side_effects=side_effects),
    )(*passed, *after)
    handle = dict(streams=streams, groups=groups, send=res[:n_grp], recv=res[n_grp:2 * n_grp],
                  srcs=res[2 * n_grp:2 * n_grp + n], lands=res[2 * n_grp + n:2 * n_grp + n + n_land])
    return handle, res[-1]


def _exchange_wait(handle, gi, after, name):
    streams, group = handle["streams"], handle["groups"][gi]
    land_ids = sorted({streams[e][1] for e in group})
    srcs = [handle["srcs"][e] for e in group]
    lands = [handle["lands"][i] for i in land_ids]
    n, n_land = len(srcs), len(lands)
    side_effects = pltpu.SideEffectType.DATAFLOW_SIDE_EFFECTING

    def body(*refs):
        src_refs, land_refs = refs[:n], refs[n:n + n_land]
        send_sem, recv_sem = refs[n + n_land:n + n_land + 2]
        for j, e in enumerate(group):
            outgoing, incoming = _stream_copies(streams[e], src_refs[j], land_refs[land_ids.index(streams[e][1])], send_sem, recv_sem, j * (N_DEV - 1))
            for cp in outgoing:
                cp.wait_send()
            for cp in incoming:
                cp.wait_recv()

    res = pl.pallas_call(
        body,
        name=name,
        out_shape=tuple(pltpu.HBM(a.shape, a.dtype) for a in lands),
        in_specs=[_HBM] * (n + n_land) + [_SEM, _SEM] + [pl.BlockSpec(memory_space=pl.ANY)] * len(after),
        out_specs=tuple([_HBM] * n_land),
        input_output_aliases={n + i: i for i in range(n_land)},
        compiler_params=pltpu.CompilerParams(has_side_effects=side_effects),
    )(*srcs, *lands, handle["send"][gi], handle["recv"][gi], *after)
    return list(res)


def _whole(ref, _):
    return ref


def _slot(ref, q):
    return ref.at[q]


def _lane_block(width):
    def pick(ref, q):
        idx = (slice(None),) * (len(ref.shape) - 1) + (pl.ds(pl.multiple_of(q * width, 128), width),)
        return ref.at[idx]
    return pick


def _row_block(height):
    def pick(ref, q):
        idx = (slice(None),) * (len(ref.shape) - 2) + (pl.ds(pl.multiple_of(q * height, 16), height), slice(None))
        return ref.at[idx]
    return pick


def _block_of(axis_from_end, size):
    return _lane_block(size) if axis_from_end == 1 else _row_block(size)


def _gather_streams(shards):
    streams, outs = [], []
    for i, (shard, axis) in enumerate(shards):
        shape = list(shard.shape)
        if axis is None:
            shape, land = [N_DEV] + shape, _slot
            place = _place_slot
        else:
            size = shape[-axis]
            land = _block_of(axis, size)
            shape[-axis] *= N_DEV
            place = lambda dst, src, me, size=size, axis=axis: lax.dynamic_update_slice_in_dim(dst, src, me * size, dst.ndim - axis)
        streams.append((shard, i, _whole, land, place))
        outs.append(jax.ShapeDtypeStruct(tuple(shape), shard.dtype))
    return streams, outs


def _place_slot(dst, src, me):
    return lax.dynamic_update_index_in_dim(dst, src, me, 0)


def _scatter_streams(fulls):
    streams, outs = [], []
    for i, (full, axis) in enumerate(fulls):
        shape = list(full.shape)
        size = shape[-axis] // N_DEV
        shape[-axis] = size
        place = lambda dst, src, me, size=size, axis=axis: _place_slot(
            dst, lax.dynamic_slice_in_dim(src, me * size, size, src.ndim - axis), me)
        streams.append((full, i, _block_of(axis, size), _slot, place))
        outs.append(jax.ShapeDtypeStruct((N_DEV,) + tuple(shape), full.dtype))
    return streams, outs


FFN_TM = 512
FFN_TF = 768


def _ffn_fwd(h, g, w13, w2, after=()):
    t = h.shape[0]
    tm, tf = min(FFN_TM, t), FFN_TF
    nf = D_FF_PAD // tf

    def body(h_ref, g_ref, w1_ref, w3_ref, w2_ref, *rest):
        ho_ref, n_ref, a_ref, b_ref, n_scr, acc = rest[len(after):]
        f = pl.program_id(1)

        @pl.when(f == 0)
        def _():
            n_scr[...] = _rms(h_ref[...], g_ref[...]).astype(MXU_DTYPE)
            n_ref[...] = n_scr[...]
            acc[...] = jnp.zeros_like(acc)

        n = n_scr[...]
        a = _dot(n, w1_ref[...])
        b = _dot(n, w3_ref[...])
        a_ref[...] = a.astype(MXU_DTYPE)
        b_ref[...] = b.astype(MXU_DTYPE)
        s = (a * _sigmoid(a)) * b
        acc[...] += _dot(s.astype(MXU_DTYPE), w2_ref[...])

        @pl.when(f == nf - 1)
        def _():
            ho_ref[...] = h_ref[...] + FFN_RES_WEIGHT * acc[...]

    return pl.pallas_call(
        body,
        name="ffn_fwd",
        grid=(t // tm, nf),
        in_specs=[
            _rows(tm, D_MODEL),
            _resident((1, D_MODEL)),
            pl.BlockSpec((None, D_MODEL, tf), lambda i, f: (0, 0, f)),
            pl.BlockSpec((None, D_MODEL, tf), lambda i, f: (1, 0, f)),
            pl.BlockSpec((tf, D_MODEL), lambda i, f: (f, 0)),
        ] + [pl.BlockSpec(memory_space=pl.ANY)] * len(after),
        out_specs=[
            _rows(tm, D_MODEL),
            _rows(tm, D_MODEL),
            pl.BlockSpec((tm, tf), lambda i, f: (i, f)),
            pl.BlockSpec((tm, tf), lambda i, f: (i, f)),
        ],
        out_shape=[
            jax.ShapeDtypeStruct((t, D_MODEL), F32),
            jax.ShapeDtypeStruct((t, D_MODEL), MXU_DTYPE),
            jax.ShapeDtypeStruct((t, D_FF_PAD), MXU_DTYPE),
            jax.ShapeDtypeStruct((t, D_FF_PAD), MXU_DTYPE),
        ],
        scratch_shapes=[pltpu.VMEM((tm, D_MODEL), MXU_DTYPE), pltpu.VMEM((tm, D_MODEL), F32)],
        compiler_params=_cparams("parallel", "arbitrary"),
    )(h, g, w13, w13, w2, *after)


def _ffn_bwd(dout, h, g, a, b, w13, w2, after=()):
    t = h.shape[0]
    tm, tf = min(FFN_TM, t), FFN_TF
    nf = D_FF_PAD // tf

    def body(do_ref, h_ref, g_ref, a_ref, b_ref, w1_ref, w3_ref, w2_ref, *rest):
        dh_ref, dob_ref, s_ref, dab_ref, dg_ref, dob_scr, acc = rest[len(after):]
        i, f = pl.program_id(0), pl.program_id(1)

        @pl.when(f == 0)
        def _():
            dob_scr[...] = (FFN_RES_WEIGHT * do_ref[...]).astype(MXU_DTYPE)
            dob_ref[...] = dob_scr[...]
            acc[...] = jnp.zeros_like(acc)

        @pl.when((f == 0) & (i == 0))
        def _():
            dg_ref[...] = jnp.zeros_like(dg_ref)

        ds = _dot_nt(dob_scr[...], w2_ref[...])
        av = a_ref[...].astype(F32)
        bv = b_ref[...].astype(F32)
        sig = _sigmoid(av)
        sil = av * sig
        s_ref[...] = (sil * bv).astype(MXU_DTYPE)
        da = (ds * bv * (sig * (1.0 + av * (1.0 - sig)))).astype(MXU_DTYPE)
        db = (ds * sil).astype(MXU_DTYPE)
        dab_ref[0] = da
        dab_ref[1] = db
        acc[...] += _dot_nt(da, w1_ref[...]) + _dot_nt(db, w3_ref[...])

        @pl.when(f == nf - 1)
        def _():
            dx, dg = _rms_bwd(h_ref[...], g_ref[...], acc[...])
            dh_ref[...] = do_ref[...] + dx
            dg_ref[...] += dg

    return pl.pallas_call(
        body,
        name="ffn_bwd",
        grid=(t // tm, nf),
        in_specs=[
            _rows(tm, D_MODEL),
            _rows(tm, D_MODEL),
            _resident((1, D_MODEL)),
            pl.BlockSpec((tm, tf), lambda i, f: (i, f)),
            pl.BlockSpec((tm, tf), lambda i, f: (i, f)),
            pl.BlockSpec((None, D_MODEL, tf), lambda i, f: (0, 0, f)),
            pl.BlockSpec((None, D_MODEL, tf), lambda i, f: (1, 0, f)),
            pl.BlockSpec((tf, D_MODEL), lambda i, f: (f, 0)),
        ] + [pl.BlockSpec(memory_space=pl.ANY)] * len(after),
        out_specs=[
            _rows(tm, D_MODEL),
            _rows(tm, D_MODEL),
            pl.BlockSpec((tm, tf), lambda i, f: (i, f)),
            pl.BlockSpec((2, tm, tf), lambda i, f: (0, i, f)),
            pl.BlockSpec((1, D_MODEL), lambda i, f: (0, 0)),
        ],
        out_shape=[
            jax.ShapeDtypeStruct((t, D_MODEL), F32),
            jax.ShapeDtypeStruct((t, D_MODEL), MXU_DTYPE),
            jax.ShapeDtypeStruct((t, D_FF_PAD), MXU_DTYPE),
            jax.ShapeDtypeStruct((2, t, D_FF_PAD), MXU_DTYPE),
            jax.ShapeDtypeStruct((1, D_MODEL), F32),
        ],
        scratch_shapes=[pltpu.VMEM((tm, D_MODEL), MXU_DTYPE), pltpu.VMEM((tm, D_MODEL), F32)],
        compiler_params=_cparams("arbitrary", "arbitrary"),
    )(dout, h, g, a, b, w13, w13, w2, *after)


def _wgrad(x, y, name, out_dtype=WIRE_DTYPE):
    t, m = x.shape
    grouped = y.ndim == 3
    groups = y.shape[0] if grouped else 1
    n = y.shape[-1]
    tk = min(1024, t)
    bm = m if m <= 1536 else m // 2
    bn = n if n <= 1536 else n // 2
    nk = t // tk

    def body(x_ref, y_ref, o_ref, acc):
        k = pl.program_id(3)

        @pl.when(k == 0)
        def _():
            acc[...] = jnp.zeros_like(acc)

        acc[...] += _dot_tn(x_ref[...].astype(MXU_DTYPE), y_ref[...].astype(MXU_DTYPE))

        @pl.when(k == nk - 1)
        def _():
            o_ref[...] = acc[...].astype(out_dtype)

    if grouped:
        y_spec = pl.BlockSpec((None, tk, bn), lambda g, i, j, k: (g, k, j))
        o_spec = pl.BlockSpec((None, bm, bn), lambda g, i, j, k: (g, i, j))
        o_shape = jax.ShapeDtypeStruct((groups, m, n), out_dtype)
    else:
        y_spec = pl.BlockSpec((tk, bn), lambda g, i, j, k: (k, j))
        o_spec = pl.BlockSpec((bm, bn), lambda g, i, j, k: (i, j))
        o_shape = jax.ShapeDtypeStruct((m, n), out_dtype)
    return pl.pallas_call(
        body,
        name=name,
        grid=(groups, m // bm, n // bn, nk),
        in_specs=[pl.BlockSpec((tk, bm), lambda g, i, j, k: (k, i)), y_spec],
        out_specs=o_spec,
        out_shape=o_shape,
        scratch_shapes=[pltpu.VMEM((bm, bn), F32)],
        compiler_params=_cparams("parallel", "parallel", "parallel", "arbitrary"),
    )(x, y)


def _conv_pre(h, g, w_pw1):
    t = h.shape[0]
    tm = min(512, t)

    def body(h_ref, g_ref, w_ref, n_ref, u_ref, glu_ref):
        n = _rms(h_ref[...], g_ref[...]).astype(MXU_DTYPE)
        n_ref[...] = n
        u = _dot(n, w_ref[...])
        u_ref[...] = u
        glu_ref[...] = u[:, :D_MODEL] * _sigmoid(u[:, D_MODEL:])

    return pl.pallas_call(
        body,
        name="conv_pre",
        grid=(t // tm,),
        in_specs=[_rows(tm, D_MODEL), _resident((1, D_MODEL)), _resident((D_MODEL, 2 * D_MODEL))],
        out_specs=[_rows(tm, D_MODEL), _rows(tm, 2 * D_MODEL), _rows(tm, D_MODEL)],
        out_shape=[
            jax.ShapeDtypeStruct((t, D_MODEL), MXU_DTYPE),
            jax.ShapeDtypeStruct((t, 2 * D_MODEL), F32),
            jax.ShapeDtypeStruct((t, D_MODEL), F32),
        ],
        compiler_params=_cparams("parallel"),
    )(h, g, w_pw1)


def _conv_main(glu, w_dw, g, w_pw2, h):
    t = h.shape[0]
    tm = min(512, t)
    per = tm // CONV_HALO

    def body(glu_ref, halo_ref, w_ref, g_ref, w2_ref, h_ref, ho_ref, c_ref, sw_ref, ext):
        i = pl.program_id(0)
        ext[pl.ds(0, CONV_HALO), :] = jnp.where(i == 0, 0.0, halo_ref[...])
        ext[pl.ds(CONV_HALO, tm), :] = glu_ref[...]
        acc = jnp.zeros((tm, D_MODEL), F32)
        for k in range(CONV_WIDTH):
            acc = acc + ext[pl.ds(CONV_HALO - (CONV_WIDTH - 1) + k, tm), :] * w_ref[pl.ds(k, 1), :]
        c_ref[...] = acc
        y = _rms(acc, g_ref[...])
        sw = (y * _sigmoid(y)).astype(MXU_DTYPE)
        sw_ref[...] = sw
        ho_ref[...] = h_ref[...] + _dot(sw, w2_ref[...])

    return pl.pallas_call(
        body,
        name="conv_main",
        grid=(t // tm,),
        in_specs=[
            _rows(tm, D_MODEL),
            pl.BlockSpec((CONV_HALO, D_MODEL), lambda i: (jnp.maximum(i * per - 1, 0), 0)),
            _resident((CONV_WIDTH, D_MODEL)),
            _resident((1, D_MODEL)),
            _resident((D_MODEL, D_MODEL)),
            _rows(tm, D_MODEL),
        ],
        out_specs=[_rows(tm, D_MODEL), _rows(tm, D_MODEL), _rows(tm, D_MODEL)],
        out_shape=[
            jax.ShapeDtypeStruct((t, D_MODEL), F32),
            jax.ShapeDtypeStruct((t, D_MODEL), F32),
            jax.ShapeDtypeStruct((t, D_MODEL), MXU_DTYPE),
        ],
        scratch_shapes=[pltpu.VMEM((tm + CONV_HALO, D_MODEL), F32)],
        compiler_params=_cparams("parallel"),
    )(glu, glu, w_dw, g, w_pw2, h)


def _conv_bwd_post(dout, c, g, w_pw2):
    t = dout.shape[0]
    tm = min(512, t)

    def body(do_ref, c_ref, g_ref, w2_ref, dc_ref, dg_ref):
        @pl.when(pl.program_id(0) == 0)
        def _():
            dg_ref[...] = jnp.zeros_like(dg_ref)

        dsw = _dot_nt(do_ref[...].astype(MXU_DTYPE), w2_ref[...])
        cv = c_ref[...]
        y = _rms(cv, g_ref[...])
        sig = _sigmoid(y)
        dy = dsw * (sig * (1.0 + y * (1.0 - sig)))
        dc, dg = _rms_bwd(cv, g_ref[...], dy)
        dc_ref[...] = dc
        dg_ref[...] += dg

    return pl.pallas_call(
        body,
        name="conv_bwd_post",
        grid=(t // tm,),
        in_specs=[_rows(tm, D_MODEL), _rows(tm, D_MODEL), _resident((1, D_MODEL)), _resident((D_MODEL, D_MODEL))],
        out_specs=[_rows(tm, D_MODEL), pl.BlockSpec((1, D_MODEL), lambda i: (0, 0))],
        out_shape=[jax.ShapeDtypeStruct((t, D_MODEL), F32), jax.ShapeDtypeStruct((1, D_MODEL), F32)],
        compiler_params=_cparams("arbitrary"),
    )(dout, c, g, w_pw2)


def _conv_bwd_pre(dc, glu, u, w_dw, w_pw1, h, g, dout):
    t = h.shape[0]
    tm = min(512, t)
    per = tm // CONV_HALO
    nt = t // tm
    last_halo = t // CONV_HALO - 1

    def body(dc_ref, dcn_ref, glu_ref, gluh_ref, u_ref, w_ref, w1_ref, h_ref, g_ref, do_ref,
             dh_ref, du_ref, dw_ref, dg_ref, dext, gext):
        i = pl.program_id(0)

        @pl.when(i == 0)
        def _():
            dw_ref[...] = jnp.zeros_like(dw_ref)
            dg_ref[...] = jnp.zeros_like(dg_ref)

        dext[pl.ds(0, tm), :] = dc_ref[...]
        dext[pl.ds(tm, CONV_HALO), :] = jnp.where(i == nt - 1, 0.0, dcn_ref[...])
        gext[pl.ds(0, CONV_HALO), :] = jnp.where(i == 0, 0.0, gluh_ref[...])
        gext[pl.ds(CONV_HALO, tm), :] = glu_ref[...]
        dcv = dc_ref[...]
        dglu = jnp.zeros((tm, D_MODEL), F32)
        for k in range(CONV_WIDTH):
            dglu = dglu + dext[pl.ds(CONV_WIDTH - 1 - k, tm), :] * w_ref[pl.ds(k, 1), :]
            prod = dcv * gext[pl.ds(CONV_HALO - (CONV_WIDTH - 1) + k, tm), :]
            dw_ref[pl.ds(k, 1), :] += jnp.sum(prod, axis=0, keepdims=True)
        uv = u_ref[...]
        av, bv = uv[:, :D_MODEL], uv[:, D_MODEL:]
        sig = _sigmoid(bv)
        du_ref[:, :D_MODEL] = (dglu * sig).astype(MXU_DTYPE)
        du_ref[:, D_MODEL:] = (dglu * av * (sig * (1.0 - sig))).astype(MXU_DTYPE)
        dn = _dot_nt(du_ref[...], w1_ref[...])
        dx, dg = _rms_bwd(h_ref[...], g_ref[...], dn)
        dh_ref[...] = do_ref[...] + dx
        dg_ref[...] += dg

    return pl.pallas_call(
        body,
        name="conv_bwd_pre",
        grid=(nt,),
        in_specs=[
            _rows(tm, D_MODEL),
            pl.BlockSpec((CONV_HALO, D_MODEL), lambda i: (jnp.minimum((i + 1) * per, last_halo), 0)),
            _rows(tm, D_MODEL),
            pl.BlockSpec((CONV_HALO, D_MODEL), lambda i: (jnp.maximum(i * per - 1, 0), 0)),
            _rows(tm, 2 * D_MODEL),
            _resident((CONV_WIDTH, D_MODEL)),
            _resident((D_MODEL, 2 * D_MODEL)),
            _rows(tm, D_MODEL),
            _resident((1, D_MODEL)),
            _rows(tm, D_MODEL),
        ],
        out_specs=[
            _rows(tm, D_MODEL),
            _rows(tm, 2 * D_MODEL),
            pl.BlockSpec((CONV_HALO, D_MODEL), lambda i: (0, 0)),
            pl.BlockSpec((1, D_MODEL), lambda i: (0, 0)),
        ],
        out_shape=[
            jax.ShapeDtypeStruct((t, D_MODEL), F32),
            jax.ShapeDtypeStruct((t, 2 * D_MODEL), MXU_DTYPE),
            jax.ShapeDtypeStruct((CONV_HALO, D_MODEL), F32),
            jax.ShapeDtypeStruct((1, D_MODEL), F32),
        ],
        scratch_shapes=[pltpu.VMEM((tm + CONV_HALO, D_MODEL), F32), pltpu.VMEM((tm + CONV_HALO, D_MODEL), F32)],
        compiler_params=_cparams("arbitrary"),
    )(dc, dc, glu, glu, u, w_dw, w_pw1, h, g, dout)


def _rope(s, cs, sa, sb):
    return s * cs - pltpu.roll(s, 96, 1) * sa + pltpu.roll(s, 32, 1) * sb


def _unrope(d, cs, sa, sb):
    return d * cs + pltpu.roll(d, 96, 1) * sa - pltpu.roll(d, 32, 1) * sb


def _mla_pre(h, g, w_a, g_q, g_kv, w_uq, w_ukv, cs, sa, sb):
    t = h.shape[0]
    tm = min(512, t)
    qw = N_HEADS * HEAD_PAD
    vw = N_HEADS * V_HEAD

    def body(h_ref, g_ref, wa_ref, gq_ref, gkv_ref, wuq_ref, wukv_ref, cs_ref, sa_ref, sb_ref,
             n_ref, a_ref, cq_ref, ckv_ref, q_ref, k_ref, v_ref):
        n = _rms(h_ref[...], g_ref[...]).astype(MXU_DTYPE)
        n_ref[...] = n
        a = _dot(n, wa_ref[...])
        a_ref[...] = a
        cq = _rms(a[:, :Q_LORA], gq_ref[...]).astype(MXU_DTYPE)
        ckv = _rms(a[:, Q_LORA:Q_LORA + KV_LORA], gkv_ref[...]).astype(MXU_DTYPE)
        cq_ref[...] = cq
        ckv_ref[...] = ckv
        q = _dot(cq, wuq_ref[...]) * ATTN_SCALE
        kv = _dot(ckv, wukv_ref[...])
        cs_, sa_, sb_ = cs_ref[...], sa_ref[...], sb_ref[...]
        k_rot = _rope(a[:, Q_LORA + KV_LORA:], cs_, sa_, sb_).astype(MXU_DTYPE)
        for hd in range(N_HEADS):
            lo = hd * HEAD_PAD
            q_ref[:, lo:lo + QK_NOPE] = q[:, lo:lo + QK_NOPE].astype(MXU_DTYPE)
            q_ref[:, lo + QK_NOPE:lo + HEAD_PAD] = _rope(q[:, lo + QK_NOPE:lo + HEAD_PAD], cs_, sa_, sb_).astype(MXU_DTYPE)
            k_ref[:, lo:lo + QK_NOPE] = kv[:, lo:lo + QK_NOPE].astype(MXU_DTYPE)
            k_ref[:, lo + QK_NOPE:lo + HEAD_PAD] = k_rot
            v_ref[:, hd * V_HEAD:(hd + 1) * V_HEAD] = kv[:, lo + QK_NOPE:lo + HEAD_PAD].astype(MXU_DTYPE)

    return pl.pallas_call(
        body,
        name="mla_pre",
        grid=(t // tm,),
        in_specs=[
            _rows(tm, D_MODEL),
            _resident((1, D_MODEL)),
            _resident((D_MODEL, A_PAD)),
            _resident((1, Q_LORA)),
            _resident((1, KV_LORA)),
            _resident((Q_LORA, qw)),
            _resident((KV_LORA, qw)),
            _rows(tm, 128),
            _rows(tm, 128),
            _rows(tm, 128),
        ],
        out_specs=[_rows(tm, D_MODEL), _rows(tm, A_PAD), _rows(tm, Q_LORA), _rows(tm, KV_LORA), _rows(tm, qw), _rows(tm, qw), _rows(tm, vw)],
        out_shape=[
            jax.ShapeDtypeStruct((t, D_MODEL), MXU_DTYPE),
            jax.ShapeDtypeStruct((t, A_PAD), F32),
            jax.ShapeDtypeStruct((t, Q_LORA), MXU_DTYPE),
            jax.ShapeDtypeStruct((t, KV_LORA), MXU_DTYPE),
            jax.ShapeDtypeStruct((t, qw), MXU_DTYPE),
            jax.ShapeDtypeStruct((t, qw), MXU_DTYPE),
            jax.ShapeDtypeStruct((t, vw), MXU_DTYPE),
        ],
        compiler_params=_cparams("parallel"),
    )(h, g, w_a, g_q, g_kv, w_uq, w_ukv, cs, sa, sb)


def _attn_tile(t):
    return min(1024, t)


def _chunk_mask(qi, ki, tq, tk):
    rows = qi * tq + lax.broadcasted_iota(jnp.int32, (tq, tk), 0)
    cols = ki * tk + lax.broadcasted_iota(jnp.int32, (tq, tk), 1)
    return (rows >> CHUNK_SHIFT) >= (cols >> CHUNK_SHIFT)


def _flash_fwd(q, k, v):
    t = q.shape[0]
    tq = tk = _attn_tile(t)
    nk = t // tk

    def body(q_ref, k_ref, v_ref, o_ref, lse_ref, m_s, l_s, acc):
        qi, ki = pl.program_id(1), pl.program_id(2)

        @pl.when(ki == 0)
        def _():
            m_s[...] = jnp.full_like(m_s, -jnp.inf)
            l_s[...] = jnp.zeros_like(l_s)
            acc[...] = jnp.zeros_like(acc)

        def block(on_diagonal):
            s = _dot_nt(q_ref[...], k_ref[...])
            if on_diagonal:
                s = jnp.where(_chunk_mask(qi, ki, tq, tk), s, -jnp.inf)
            m_prev = m_s[...]
            m_new = jnp.maximum(m_prev, jnp.max(s, axis=1, keepdims=True))
            alpha = jnp.exp(m_prev - m_new)
            p = jnp.exp(s - m_new)
            l_s[...] = alpha * l_s[...] + jnp.sum(p, axis=1, keepdims=True)
            acc[...] = alpha * acc[...] + _dot(p.astype(MXU_DTYPE), v_ref[...])
            m_s[...] = m_new

        pl.when(ki < qi)(lambda: block(False))
        pl.when(ki == qi)(lambda: block(True))

        @pl.when(ki == nk - 1)
        def _():
            o_ref[...] = acc[...] / l_s[...]
            lse_ref[...] = jnp.broadcast_to(m_s[...] + jnp.log(l_s[...]), (tq, V_HEAD))

    return pl.pallas_call(
        body,
        name="flash_fwd",
        grid=(N_HEADS, t // tq, nk),
        in_specs=[
            pl.BlockSpec((tq, HEAD_PAD), lambda h, i, j: (i, h)),
            pl.BlockSpec((tk, HEAD_PAD), lambda h, i, j: (jnp.minimum(i, j), h)),
            pl.BlockSpec((tk, V_HEAD), lambda h, i, j: (jnp.minimum(i, j), h)),
        ],
        out_specs=[pl.BlockSpec((tq, V_HEAD), lambda h, i, j: (i, h)), pl.BlockSpec((tq, V_HEAD), lambda h, i, j: (i, h))],
        out_shape=[jax.ShapeDtypeStruct((t, N_HEADS * V_HEAD), F32), jax.ShapeDtypeStruct((t, N_HEADS * V_HEAD), F32)],
        scratch_shapes=[pltpu.VMEM((tq, 1), F32), pltpu.VMEM((tq, 1), F32), pltpu.VMEM((tq, V_HEAD), F32)],
        compiler_params=_cparams("parallel", "parallel", "arbitrary"),
    )(q, k, v)


def _flash_bwd(q, k, v, do, lse, delta):
    t = q.shape[0]
    tq = tk = _attn_tile(t)
    nq = t // tq

    def body(q_ref, k_ref, v_ref, do_ref, lse_ref, dl_ref, dq_ref, dk_ref, dv_ref, dk_acc, dv_acc):
        kj, qi = pl.program_id(1), pl.program_id(2)

        @pl.when((kj == 0) & (qi == 0))
        def _():
            dq_ref[...] = jnp.zeros_like(dq_ref)

        @pl.when(qi == 0)
        def _():
            dk_acc[...] = jnp.zeros_like(dk_acc)
            dv_acc[...] = jnp.zeros_like(dv_acc)

        def block(on_diagonal):
            qv, kv, dov = q_ref[...], k_ref[...], do_ref[...]
            s = _dot_nt(qv, kv)
            if on_diagonal:
                s = jnp.where(_chunk_mask(qi, kj, tq, tk), s, -jnp.inf)
            p = jnp.exp(s - lse_ref[:, :1])
            dp = _dot_nt(dov, v_ref[...])
            ds = (p * (dp - dl_ref[:, :1])).astype(MXU_DTYPE)
            dv_acc[...] += _dot_tn(p.astype(MXU_DTYPE), dov)
            dk_acc[...] += _dot_tn(ds, qv)
            rows = pl.ds(pl.multiple_of(qi * tq, tq), tq)
            dq_ref[rows, :] += _dot(ds, kv)

        pl.when(qi > kj)(lambda: block(False))
        pl.when(qi == kj)(lambda: block(True))

        @pl.when(qi == nq - 1)
        def _():
            dk_ref[...] = dk_acc[...]
            dv_ref[...] = dv_acc[...]

    return pl.pallas_call(
        body,
        name="flash_bwd",
        grid=(N_HEADS, t // tk, nq),
        in_specs=[
            pl.BlockSpec((tq, HEAD_PAD), lambda h, j, i: (jnp.maximum(i, j), h)),
            pl.BlockSpec((tk, HEAD_PAD), lambda h, j, i: (j, h)),
            pl.BlockSpec((tk, V_HEAD), lambda h, j, i: (j, h)),
            pl.BlockSpec((tq, V_HEAD), lambda h, j, i: (jnp.maximum(i, j), h)),
            pl.BlockSpec((tq, V_HEAD), lambda h, j, i: (jnp.maximum(i, j), h)),
            pl.BlockSpec((tq, V_HEAD), lambda h, j, i: (jnp.maximum(i, j), h)),
        ],
        out_specs=[
            pl.BlockSpec((t, HEAD_PAD), lambda h, j, i: (0, h)),
            pl.BlockSpec((tk, HEAD_PAD), lambda h, j, i: (j, h)),
            pl.BlockSpec((tk, V_HEAD), lambda h, j, i: (j, h)),
        ],
        out_shape=[
            jax.ShapeDtypeStruct((t, N_HEADS * HEAD_PAD), F32),
            jax.ShapeDtypeStruct((t, N_HEADS * HEAD_PAD), F32),
            jax.ShapeDtypeStruct((t, N_HEADS * V_HEAD), F32),
        ],
        scratch_shapes=[pltpu.VMEM((tk, HEAD_PAD), F32), pltpu.VMEM((tk, V_HEAD), F32)],
        compiler_params=_cparams("arbitrary", "arbitrary", "arbitrary"),
    )(q, k, v, do, lse, delta)


def _attn_out(o, w_o, h):
    t = h.shape[0]
    tm = min(512, t)

    def body(o_ref, w_ref, h_ref, ho_ref):
        ho_ref[...] = h_ref[...] + _dot(o_ref[...].astype(MXU_DTYPE), w_ref[...])

    return pl.pallas_call(
        body,
        name="attn_out",
        grid=(t // tm,),
        in_specs=[_rows(tm, D_MODEL), _resident((D_MODEL, D_MODEL)), _rows(tm, D_MODEL)],
        out_specs=_rows(tm, D_MODEL),
        out_shape=jax.ShapeDtypeStruct((t, D_MODEL), F32),
        compiler_params=_cparams("parallel"),
    )(o, w_o, h)


def _attn_out_bwd(dout, o, w_o):
    t = dout.shape[0]
    tm = min(512, t)

    def body(d_ref, o_ref, w_ref, do_ref, dl_ref):
        do = _dot_nt(d_ref[...].astype(MXU_DTYPE), w_ref[...])
        do_ref[...] = do.astype(MXU_DTYPE)
        prod = do * o_ref[...]
        for hd in range(N_HEADS):
            lanes = slice(hd * V_HEAD, (hd + 1) * V_HEAD)
            dl_ref[:, lanes] = jnp.broadcast_to(jnp.sum(prod[:, lanes], axis=1, keepdims=True), (tm, V_HEAD))

    return pl.pallas_call(
        body,
        name="attn_out_bwd",
        grid=(t // tm,),
        in_specs=[_rows(tm, D_MODEL), _rows(tm, D_MODEL), _resident((D_MODEL, D_MODEL))],
        out_specs=[_rows(tm, D_MODEL), _rows(tm, D_MODEL)],
        out_shape=[jax.ShapeDtypeStruct((t, D_MODEL), MXU_DTYPE), jax.ShapeDtypeStruct((t, D_MODEL), F32)],
        compiler_params=_cparams("parallel"),
    )(dout, o, w_o)


def _mla_bwd_pre(dq, dk, dv, a, h, dout, g, g_q, g_kv, w_a, w_uq, w_ukv, cs, sa, sb):
    t = h.shape[0]
    tm = min(512, t)
    qw = N_HEADS * HEAD_PAD
    vw = N_HEADS * V_HEAD

    def body(dq_ref, dk_ref, dv_ref, a_ref, h_ref, do_ref, g_ref, gq_ref, gkv_ref, wa_ref, wuq_ref, wukv_ref,
             cs_ref, sa_ref, sb_ref, dh_ref, dqp_ref, dkv_ref, da_ref, dg_ref, dgq_ref, dgkv_ref):
        @pl.when(pl.program_id(0) == 0)
        def _():
            dg_ref[...] = jnp.zeros_like(dg_ref)
            dgq_ref[...] = jnp.zeros_like(dgq_ref)
            dgkv_ref[...] = jnp.zeros_like(dgkv_ref)

        cs_, sa_, sb_ = cs_ref[...], sa_ref[...], sb_ref[...]
        slab = jnp.zeros((tm, 128), F32)
        for hd in range(N_HEADS):
            lo = hd * HEAD_PAD
            dqp_ref[:, lo:lo + QK_NOPE] = (dq_ref[:, lo:lo + QK_NOPE] * ATTN_SCALE).astype(MXU_DTYPE)
            dqp_ref[:, lo + QK_NOPE:lo + HEAD_PAD] = _unrope(dq_ref[:, lo + QK_NOPE:lo + HEAD_PAD] * ATTN_SCALE, cs_, sa_, sb_).astype(MXU_DTYPE)
            dkv_ref[:, lo:lo + QK_NOPE] = dk_ref[:, lo:lo + QK_NOPE].astype(MXU_DTYPE)
            dkv_ref[:, lo + QK_NOPE:lo + HEAD_PAD] = dv_ref[:, hd * V_HEAD:(hd + 1) * V_HEAD].astype(MXU_DTYPE)
            slab = slab + dk_ref[:, lo + QK_NOPE:lo + HEAD_PAD]
        dcq = _dot_nt(dqp_ref[...], wuq_ref[...])
        dckv = _dot_nt(dkv_ref[...], wukv_ref[...])
        av = a_ref[...]
        daq, dgq = _rms_bwd(av[:, :Q_LORA], gq_ref[...], dcq)
        dakv, dgkv = _rms_bwd(av[:, Q_LORA:Q_LORA + KV_LORA], gkv_ref[...], dckv)
        da_ref[:, :Q_LORA] = daq.astype(MXU_DTYPE)
        da_ref[:, Q_LORA:Q_LORA + KV_LORA] = dakv.astype(MXU_DTYPE)
        da_ref[:, Q_LORA + KV_LORA:] = _unrope(slab, cs_, sa_, sb_).astype(MXU_DTYPE)
        dn = _dot_nt(da_ref[...], wa_ref[...])
        dx, dg = _rms_bwd(h_ref[...], g_ref[...], dn)
        dh_ref[...] = do_ref[...] + dx
        dg_ref[...] += dg
        dgq_ref[...] += dgq
        dgkv_ref[...] += dgkv

    def const(n):
        return pl.BlockSpec((1, n), lambda i: (0, 0))

    return pl.pallas_call(
        body,
        name="mla_bwd_pre",
        grid=(t // tm,),
        in_specs=[
            _rows(tm, qw), _rows(tm, qw), _rows(tm, vw), _rows(tm, A_PAD), _rows(tm, D_MODEL), _rows(tm, D_MODEL),
            _resident((1, D_MODEL)), _resident((1, Q_LORA)), _resident((1, KV_LORA)),
            _resident((D_MODEL, A_PAD)), _resident((Q_LORA, qw)), _resident((KV_LORA, qw)),
            _rows(tm, 128), _rows(tm, 128), _rows(tm, 128),
        ],
        out_specs=[_rows(tm, D_MODEL), _rows(tm, qw), _rows(tm, qw), _rows(tm, A_PAD),
                   const(D_MODEL), const(Q_LORA), const(KV_LORA)],
        out_shape=[
            jax.ShapeDtypeStruct((t, D_MODEL), F32),
            jax.ShapeDtypeStruct((t, qw), MXU_DTYPE),
            jax.ShapeDtypeStruct((t, qw), MXU_DTYPE),
            jax.ShapeDtypeStruct((t, A_PAD), MXU_DTYPE),
            jax.ShapeDtypeStruct((1, D_MODEL), F32),
            jax.ShapeDtypeStruct((1, Q_LORA), F32),
            jax.ShapeDtypeStruct((1, KV_LORA), F32),
        ],
        compiler_params=_cparams("arbitrary"),
    )(dq, dk, dv, a, h, dout, g, g_q, g_kv, w_a, w_uq, w_ukv, cs, sa, sb)


def _loss_head(h, g, target):
    t = h.shape[0]
    tm = min(512, t)

    def body(h_ref, g_ref, t_ref, sq_ref, dh_ref, dg_ref):
        @pl.when(pl.program_id(0) == 0)
        def _():
            sq_ref[...] = jnp.zeros_like(sq_ref)
            dg_ref[...] = jnp.zeros_like(dg_ref)

        x = h_ref[...]
        err = _rms(x, g_ref[...]) - t_ref[...]
        sq_ref[...] += jnp.sum(err * err, axis=0, keepdims=True)
        dx, dg = _rms_bwd(x, g_ref[...], err * (1.0 / D_MODEL))
        dh_ref[...] = dx
        dg_ref[...] += dg

    return pl.pallas_call(
        body,
        name="loss_head",
        grid=(t // tm,),
        in_specs=[_rows(tm, D_MODEL), _resident((1, D_MODEL)), _rows(tm, D_MODEL)],
        out_specs=[pl.BlockSpec((1, D_MODEL), lambda i: (0, 0)), _rows(tm, D_MODEL), pl.BlockSpec((1, D_MODEL), lambda i: (0, 0))],
        out_shape=[jax.ShapeDtypeStruct((1, D_MODEL), F32), jax.ShapeDtypeStruct((t, D_MODEL), F32), jax.ShapeDtypeStruct((1, D_MODEL), F32)],
        compiler_params=_cparams("arbitrary"),
    )(h, g, target)


def _adamw(parts, w, m, v, name, tr, layer=0, row_offset=0, into=None):
    layers, rows, width = w.shape
    pwidth = parts.shape[-1]
    off = row_offset // tr

    def body(p_ref, w_ref, m_ref, v_ref, *rest):
        g_ref, d_ref, mo_ref, vo_ref = rest[-4:]

        def part(q):
            return p_ref[q, :, pl.ds(0, width)].astype(F32)

        g = part(0)
        for q in range(1, N_DEV):
            g = g + part(q)
        g_ref[...] = g
        m_new = ADAM_B1 * m_ref[...] + (1.0 - ADAM_B1) * g
        v_new = ADAM_B2 * v_ref[...] + (1.0 - ADAM_B2) * (g * g)
        m_hat = m_new / (1.0 - ADAM_B1 ** ADAM_STEP)
        v_hat = v_new / (1.0 - ADAM_B2 ** ADAM_STEP)
        d_ref[...] = -ADAM_LR * (m_hat / (jnp.sqrt(v_hat) + ADAM_EPS) + ADAM_WD * w_ref[...])
        mo_ref[...] = m_new
        vo_ref[...] = v_new

    blk = pl.BlockSpec((None, tr, width), lambda i: (layer, i, 0))
    earlier = () if into is None else tuple(into)
    return pl.pallas_call(
        body,
        name=name,
        grid=(rows // tr,),
        in_specs=[pl.BlockSpec((N_DEV, tr, pwidth), lambda i: (0, off + i, 0)), blk, blk, blk] + [pl.BlockSpec(memory_space=pl.ANY)] * len(earlier),
        out_specs=[blk, blk, blk, blk],
        out_shape=[jax.ShapeDtypeStruct((layers, rows, width), F32)] * 4,
        input_output_aliases={4 + j: j for j in range(len(earlier))},
        compiler_params=_cparams("parallel"),
    )(parts, w, m, v, *earlier)


_GAINS = ("ffn_norm1", "mix_norm", "ffn_norm2", "conv_norm", "final_norm")
_TINY = ("conv_w_dw", "mla_q_norm", "mla_kv_norm")


def _small_pack(parts):
    flat = jnp.concatenate([p.reshape(-1).astype(F32) for p in parts])
    return jnp.pad(flat, (0, SMALL_ROWS * D_MODEL - flat.shape[0])).reshape(SMALL_ROWS, D_MODEL)


def _small_unpack(packed, shapes):
    flat, out, off = packed.reshape(-1), [], 0
    for s in shapes:
        size = 1
        for d in s:
            size *= d
        out.append(flat[off:off + size].reshape(s))
        off += size
    return out


def _pad_to(a, axis, size):
    pad = [(0, 0)] * a.ndim
    pad[axis] = (0, size - a.shape[axis])
    return jnp.pad(a, pad)


def kernel(x, positions, ffn_norm1, ffn1_w1, ffn1_w3, ffn1_w2, mix_norm, ffn_norm2, ffn2_w1, ffn2_w3, ffn2_w2, conv_w_pw1, conv_w_dw, conv_norm, conv_w_pw2, mla_w_a, mla_q_norm, mla_kv_norm, mla_w_uq, mla_w_ukv, mla_w_o, final_norm, loss_target, m_ffn_norm1, m_ffn1_w1, m_ffn1_w3, m_ffn1_w2, m_mix_norm, m_ffn_norm2, m_ffn2_w1, m_ffn2_w3, m_ffn2_w2, m_conv_w_pw1, m_conv_w_dw, m_conv_norm, m_conv_w_pw2, m_mla_w_a, m_mla_q_norm, m_mla_kv_norm, m_mla_w_uq, m_mla_w_ukv, m_mla_w_o, m_final_norm, v_ffn_norm1, v_ffn1_w1, v_ffn1_w3, v_ffn1_w2, v_mix_norm, v_ffn_norm2, v_ffn2_w1, v_ffn2_w3, v_ffn2_w2, v_conv_w_pw1, v_conv_w_dw, v_conv_norm, v_conv_w_pw2, v_mla_w_a, v_mla_q_norm, v_mla_kv_norm, v_mla_w_uq, v_mla_w_ukv, v_mla_w_o, v_final_norm):
    args = dict(locals())
    wire = lambda a: a.astype(WIRE_DTYPE)

    def ffn_shards(prefix, layer):
        w13_shard = _pad_to(jnp.stack([args[prefix + "_w1"][layer], args[prefix + "_w3"][layer]]), 2, FF_SHARD_PAD)
        return [(wire(w13_shard), 1), (wire(_pad_to(args[prefix + "_w2"][layer], 0, FF_SHARD_PAD)), 2)]

    def uq_pad(a):
        return _pad_to(a, 3, HEAD_PAD).reshape(1, a.shape[1], N_HEADS * HEAD_PAD)

    ukv_rows = lambda a: a.reshape(1, a.shape[1], N_HEADS * HEAD_PAD)
    tiny_shapes = [conv_w_dw.shape, mla_q_norm.shape, mla_kv_norm.shape]

    first = _exchange(*_gather_streams(ffn_shards("ffn1", 0)), "weight_gather_first")
    later_groups = [
        [(wire(conv_w_pw1[0]), 1), (wire(conv_w_pw2[0]), 2), (_small_pack([conv_w_dw, mla_q_norm, mla_kv_norm]), None)],
        ffn_shards("ffn2", 0),
        ffn_shards("ffn1", 1),
        [(wire(_pad_to(mla_w_a[0], 1, A_PAD)), 2), (wire(uq_pad(mla_w_uq)[0]), 2), (wire(ukv_rows(mla_w_ukv)[0]), 2), (wire(mla_w_o[0]), 2)],
        ffn_shards("ffn2", 1),
    ]
    later_streams, later_outs = _gather_streams([sh for grp in later_groups for sh in grp])
    group_ids, at = [], 0
    for grp in later_groups:
        group_ids.append(list(range(at, at + len(grp))))
        at += len(grp)
    gather, gather_token = _exchange_start(
        later_streams, _landing(later_streams, later_outs), group_ids, (first[0],), "weight_gather_start")

    inv_freq = ROPE_THETA ** (-2.0 * jnp.arange(QK_ROPE // 2, dtype=F32) / QK_ROPE)
    ang = positions[0].astype(F32)[:, None] * inv_freq
    cos, sin, zero = jnp.cos(ang), jnp.sin(ang), jnp.zeros_like(ang)
    cs = jnp.concatenate([cos, cos, zero, zero], axis=1)
    sa = jnp.concatenate([sin, zero, zero, zero], axis=1)
    sb = jnp.concatenate([zero, sin, zero, zero], axis=1)

    ffn_w = {("ffn1", 0): first}
    h0 = x[0]
    h1, n1, a1, b1 = _ffn_fwd(h0, ffn_norm1[0:1], *ffn_w[("ffn1", 0)], after=(gather_token,))
    w_pw1, w_pw2, tiny = _exchange_wait(gather, 0, (h1,), "weight_gather_wait_conv")
    tiny = [_small_unpack(tiny[q], tiny_shapes) for q in range(N_DEV)]
    w_dw = jnp.concatenate([t_[0][0] for t_ in tiny], axis=1)
    g_q = jnp.concatenate([t_[1] for t_ in tiny], axis=1)
    g_kv = jnp.concatenate([t_[2] for t_ in tiny], axis=1)
    n_c, u, glu = _conv_pre(h1, mix_norm[0:1], w_pw1)
    h2, c, sw = _conv_main(glu, w_dw, conv_norm, w_pw2, h1)
    ffn_w[("ffn2", 0)] = _exchange_wait(gather, 1, (h2,), "weight_gather_wait_ffn2_0")
    h3, n2, a2, b2 = _ffn_fwd(h2, ffn_norm2[0:1], *ffn_w[("ffn2", 0)])
    ffn_w[("ffn1", 1)] = _exchange_wait(gather, 2, (h3,), "weight_gather_wait_ffn1_1")
    h4, n3, a3, b3 = _ffn_fwd(h3, ffn_norm1[1:2], *ffn_w[("ffn1", 1)])
    w_a, w_uq, w_ukv, w_o = _exchange_wait(gather, 3, (h4,), "weight_gather_wait_mla")
    n_a, a_lat, cq, ckv, q, k, v = _mla_pre(h4, mix_norm[1:2], w_a, g_q, g_kv, w_uq, w_ukv, cs, sa, sb)
    o, lse = _flash_fwd(q, k, v)
    h5 = _attn_out(o, w_o, h4)
    ffn_w[("ffn2", 1)] = _exchange_wait(gather, 4, (h5,), "weight_gather_wait_ffn2_1")
    h6, n4, a4, b4 = _ffn_fwd(h5, ffn_norm2[1:2], *ffn_w[("ffn2", 1)])

    sq, dh, dg_final = _loss_head(h6, final_norm[None, :], loss_target[0])
    loss = lax.psum(0.5 / D_MODEL * jnp.sum(sq), ("x", "y", "c"))

    gain = {}

    def ffn_backward(prefix, norm, layer, dh, h_in, n, a, b, after=()):
        dh, dob, s, dab, gain[(norm, layer)] = _ffn_bwd(dh, h_in, args[norm][layer:layer + 1], a, b, *ffn_w[(prefix, layer)], after=after)
        return dh, [(_wgrad(n, dab, "wgrad_ffn_in"), 1), (_wgrad(s, dob, "wgrad_ffn_out"), 2)]

    dh, g_ffn2_1 = ffn_backward("ffn2", "ffn_norm2", 1, dh, h5, n4, a4, b4)
    do, delta = _attn_out_bwd(dh, o, w_o)
    d_o = _wgrad(o, dh, "wgrad_attn_out")
    dq, dk, dv = _flash_bwd(q, k, v, do, lse, delta)
    dh, dqp, dkv, da_lat, gain[("mix_norm", 1)], d_gq, d_gkv = _mla_bwd_pre(
        dq, dk, dv, a_lat, h4, dh, mix_norm[1:2], g_q, g_kv, w_a, w_uq, w_ukv, cs, sa, sb)
    g_mla = [(_wgrad(n_a, da_lat, "wgrad_mla_a"), 2), (_wgrad(cq, dqp, "wgrad_mla_uq"), 2), (_wgrad(ckv, dkv, "wgrad_mla_ukv"), 2), (d_o, 2)]
    dh, g_ffn1_1 = ffn_backward("ffn1", "ffn_norm1", 1, dh, h3, n3, a3, b3)
    streams_a, outs_a = _scatter_streams(g_ffn2_1 + g_mla + g_ffn1_1)
    scatter_a, token_a = _exchange_start(
        streams_a, _landing(streams_a, outs_a), [list(range(len(streams_a)))], (), "grad_scatter_start_layer1")

    dh, g_ffn2_0 = ffn_backward("ffn2", "ffn_norm2", 0, dh, h2, n2, a2, b2, after=(token_a,))
    dc, gain[("conv_norm", 0)] = _conv_bwd_post(dh, c, conv_norm, w_pw2)
    d_pw2 = _wgrad(sw, dh, "wgrad_conv_pw2")
    dh, du, d_dw, gain[("mix_norm", 0)] = _conv_bwd_pre(dc, glu, u, w_dw, w_pw1, h1, mix_norm[0:1], dh)
    d_pw1 = _wgrad(n_c, du, "wgrad_conv_pw1")
    streams_b, outs_b = _scatter_streams(g_ffn2_0 + [(d_pw1, 1), (d_pw2, 2)])
    scatter_b, token_b = _exchange_start(
        streams_b, _landing(streams_b, outs_b), [list(range(len(streams_b)))], (), "grad_scatter_start_conv")

    dh, g_ffn1_0 = ffn_backward("ffn1", "ffn_norm1", 0, dh, h0, n1, a1, b1, after=(token_b,))
    grad_x = dh[None]

    gain_rows = jnp.concatenate([
        gain[("ffn_norm1", 0)], gain[("ffn_norm1", 1)], gain[("mix_norm", 0)], gain[("mix_norm", 1)],
        gain[("ffn_norm2", 0)], gain[("ffn_norm2", 1)], gain[("conv_norm", 0)], dg_final])
    dw_by_dest = jnp.moveaxis(d_dw[:CONV_WIDTH].reshape(CONV_WIDTH, N_DEV, -1), 1, 0)
    small_by_dest = jnp.stack([
        _small_pack([gain_rows, dw_by_dest[p], d_gq.reshape(N_DEV, -1)[p], d_gkv.reshape(N_DEV, -1)[p]]) for p in range(N_DEV)])
    streams_c, outs_c = _scatter_streams(g_ffn1_0)
    streams_c.append((small_by_dest, len(outs_c), _slot, _slot, lambda dst, src, me: _place_slot(dst, lax.dynamic_index_in_dim(src, me, 0, False), me)))
    outs_c.append(jax.ShapeDtypeStruct((N_DEV, SMALL_ROWS, D_MODEL), F32))
    scatter_c, token_c = _exchange_start(streams_c, _landing(streams_c, outs_c), [list(range(len(streams_c)))], (), "grad_scatter_start_last")

    (p13_ffn2_1, p2_ffn2_1, p_a, p_uq, p_ukv, p_o, p13_ffn1_1, p2_ffn1_1) = _exchange_wait(scatter_a, 0, (token_c,), "grad_scatter_wait_layer1")
    p13_ffn2_0, p2_ffn2_0, p_pw1, p_pw2 = _exchange_wait(scatter_b, 0, (token_c,), "grad_scatter_wait_conv")
    results = {}

    def adam(name, parts, tr, view=lambda a: a, **kw):
        results[name] = _adamw(parts, view(args[name]), view(args["m_" + name]), view(args["v_" + name]), "adamw_" + name, tr,
                               into=results.get(name), **kw)

    def adam_ffn(prefix, layer, p13, p2):
        p13 = p13.reshape(N_DEV, 2 * D_MODEL, FF_SHARD_PAD)
        adam(prefix + "_w1", p13, 256, layer=layer)
        adam(prefix + "_w3", p13, 256, layer=layer, row_offset=D_MODEL)
        adam(prefix + "_w2", p2, FF_SHARD, layer=layer)

    adam_ffn("ffn2", 0, p13_ffn2_0, p2_ffn2_0)
    adam_ffn("ffn2", 1, p13_ffn2_1, p2_ffn2_1)
    adam_ffn("ffn1", 1, p13_ffn1_1, p2_ffn1_1)
    adam("conv_w_pw1", p_pw1, 256)
    adam("conv_w_pw2", p_pw2, 128)
    adam("mla_w_a", p_a, 128)
    adam("mla_w_uq", p_uq, 64, view=uq_pad)
    adam("mla_w_ukv", p_ukv, 32, view=ukv_rows)
    adam("mla_w_o", p_o, 128)
    results["mla_w_uq"] = [r.reshape(1, -1, N_HEADS, HEAD_PAD)[..., :QK_NOPE + QK_ROPE] for r in results["mla_w_uq"]]
    results["mla_w_ukv"] = [r.reshape(mla_w_ukv.shape) for r in results["mla_w_ukv"]]

    p13_ffn1_0, p2_ffn1_0, p_small = _exchange_wait(scatter_c, 0, (results["mla_w_o"][0],), "grad_scatter_wait_last")
    adam_ffn("ffn1", 0, p13_ffn1_0, p2_ffn1_0)
    small_names = _GAINS + _TINY
    small_shapes = [args[n].shape for n in small_names]
    pack_small = lambda prefix: _small_pack([args[prefix + n] for n in small_names])[None]
    small = _adamw(p_small, pack_small(""), pack_small("m_"), pack_small("v_"), "adamw_small", SMALL_ROWS)
    for kind in range(4):
        for n, leaf in zip(small_names, _small_unpack(small[kind][0], small_shapes)):
            results.setdefault(n, [None] * 4)[kind] = leaf

    order = ("ffn_norm1", "ffn1_w1", "ffn1_w3", "ffn1_w2", "mix_norm", "ffn_norm2", "ffn2_w1", "ffn2_w3", "ffn2_w2",
             "conv_w_pw1", "conv_w_dw", "conv_norm", "conv_w_pw2", "mla_w_a", "mla_q_norm", "mla_kv_norm",
             "mla_w_uq", "mla_w_ukv", "mla_w_o", "final_norm")
    outputs = [loss, grad_x]
    for kind in range(4):
        outputs.extend(results[n][kind] for n in order)
    return tuple(outputs)
```

```python
import jax
import jax.numpy as jnp
from jax import lax
from jax.experimental import pallas as pl
from jax.experimental.pallas import tpu as pltpu

F32 = jnp.float32
MXU_DTYPE = jnp.bfloat16
WIRE_DTYPE = jnp.bfloat16

N_DEV = 8
D_MODEL = 1024
FF_SHARD = 352
FF_SHARD_PAD = 384
D_FF_PAD = N_DEV * FF_SHARD_PAD
N_HEADS = 8
QK_NOPE = 128
QK_ROPE = 64
V_HEAD = 128
Q_LORA = 512
KV_LORA = 256
HEAD_PAD = 256
A_WIDTH = Q_LORA + KV_LORA + QK_ROPE
A_PAD = Q_LORA + KV_LORA + 128
CONV_WIDTH = 31
CONV_HALO = 32
CONV_ROWS = 16
CHUNK_SHIFT = 6
ROPE_THETA = 10000.0
RMS_EPS = 1e-6
ATTN_SCALE = (QK_NOPE + QK_ROPE) ** -0.5
FFN_RES_WEIGHT = 0.5
ADAM_LR = 0.001
ADAM_B1 = 0.9
ADAM_B2 = 0.999
ADAM_EPS = 1e-08
ADAM_WD = 0.01
ADAM_STEP = 10

SMALL_ROWS = 16
VMEM_LIMIT = 56 << 20


def _cparams(*sem):
    return pltpu.CompilerParams(dimension_semantics=sem, vmem_limit_bytes=VMEM_LIMIT)


def _dot(a, b):
    return lax.dot_general(a, b, (((1,), (0,)), ((), ())), preferred_element_type=F32)


def _dot_nt(a, b):
    return lax.dot_general(a, b, (((1,), (1,)), ((), ())), preferred_element_type=F32)


def _dot_tn(a, b):
    return lax.dot_general(a, b, (((0,), (0,)), ((), ())), preferred_element_type=F32)


def _resident(shape, index=None):
    fixed = index if index is not None else (0,) * len(shape)
    return pl.BlockSpec(shape, lambda *_: fixed, pipeline_mode=pl.Buffered(1))


def _rows(tm, n):
    return pl.BlockSpec((tm, n), lambda t, *_: (t, 0))


def _rms(x, g):
    r = lax.rsqrt(jnp.mean(x * x, axis=-1, keepdims=True) + RMS_EPS)
    return x * r * g


def _rms_bwd(x, g, dy):
    r = lax.rsqrt(jnp.mean(x * x, axis=-1, keepdims=True) + RMS_EPS)
    xr = x * r
    dg = jnp.sum(dy * xr, axis=0, keepdims=True)
    u = dy * g
    dx = r * (u - xr * jnp.mean(u * xr, axis=-1, keepdims=True))
    return dx, dg


def _sigmoid(x):
    return 1.0 / (1.0 + jnp.exp(-x))


def _me():
    return 4 * lax.axis_index("x") + 2 * lax.axis_index("y") + lax.axis_index("c")


def _peer(k):
    px, py, pc = lax.axis_index("x") ^ ((k >> 2) & 1), lax.axis_index("y") ^ ((k >> 1) & 1), lax.axis_index("c") ^ (k & 1)
    return (px, py, pc), 4 * px + 2 * py + pc


def _stream_copies(stream, src_ref, dst_ref, send_sem, recv_sem, base):
    send, land = stream[2], stream[3]
    me = _me()
    outgoing, incoming = [], []
    for k in range(1, N_DEV):
        pos, p = _peer(k)
        common = dict(send_sem=send_sem.at[base + k - 1], recv_sem=recv_sem.at[base + k - 1], device_id=pos, device_id_type=pl.DeviceIdType.MESH)
        outgoing.append(pltpu.make_async_remote_copy(src_ref=send(src_ref, p), dst_ref=land(dst_ref, me), **common))
        incoming.append(pltpu.make_async_remote_copy(src_ref=send(src_ref, p), dst_ref=land(dst_ref, p), **common))
    return outgoing, incoming


_HBM = pl.BlockSpec(memory_space=pltpu.HBM)
_SEM = pl.BlockSpec(memory_space=pltpu.SEMAPHORE)


def _exchange(streams, outs, name):
    n, n_out = len(streams), len(outs)

    def body(*refs):
        srcs, dsts = refs[:n], refs[n:n + n_out]
        send_sem, recv_sem, local_sem = refs[n + n_out:]
        me = _me()
        own = [pltpu.make_async_copy(st[2](srcs[e], me), st[3](dsts[st[1]], me), local_sem.at[e]) for e, st in enumerate(streams)]
        for cp in own:
            cp.start()
        sends, recvs = [], []
        for e, st in enumerate(streams):
            outgoing, incoming = _stream_copies(st, srcs[e], dsts[st[1]], send_sem, recv_sem, e * (N_DEV - 1))
            sends += outgoing
            recvs += incoming
        for cp in sends:
            cp.start()
        for cp in recvs:
            cp.wait_recv()
        for cp in sends:
            cp.wait_send()
        for cp in own:
            cp.wait()

    return pl.pallas_call(
        body,
        name=name,
        out_shape=tuple(outs),
        in_specs=[_HBM] * n,
        out_specs=tuple([_HBM] * n_out),
        scratch_shapes=[pltpu.SemaphoreType.DMA((n * (N_DEV - 1),)), pltpu.SemaphoreType.DMA((n * (N_DEV - 1),)),
                        pltpu.SemaphoreType.DMA((n,))],
        compiler_params=pltpu.CompilerParams(vmem_limit_bytes=VMEM_LIMIT),
    )(*[st[0] for st in streams])


def _landing(streams, outs):
    me = _me()
    lands = [lax.empty(o.shape, o.dtype) for o in outs]
    for st in streams:
        lands[st[1]] = st[4](lands[st[1]], st[0], me)
    return lands


def _exchange_start(streams, lands, groups, after, name):
    n, n_land, n_grp = len(streams), len(lands), len(groups)
    side_effects = pltpu.SideEffectType.DATAFLOW_SIDE_EFFECTING

    def body(*refs):
        srcs, dsts = refs[:n], refs[n:n + n_land]
        outs = refs[n + n_land + len(after):]
        send_sems, recv_sems, token = outs[:n_grp], outs[n_grp:2 * n_grp], outs[-1]
        for gi, group in enumerate(groups):
            for j, e in enumerate(group):
                outgoing, _ = _stream_copies(streams[e], srcs[e], dsts[streams[e][1]], send_sems[gi], recv_sems[gi], j * (N_DEV - 1))
                for cp in outgoing:
                    cp.start()
        token[...] = jnp.zeros_like(token)

    sems = tuple(pltpu.SemaphoreType.DMA((len(g) * (N_DEV - 1),)) for g in groups)
    passed = [st[0] for st in streams] + list(lands)
    res = pl.pallas_call(
        body,
        name=name,
        out_shape=sems + sems + tuple(pltpu.HBM(a.shape, a.dtype) for a in passed) + (jax.ShapeDtypeStruct((8, 128), F32),),
        in_specs=[_HBM] * (n + n_land) + [pl.BlockSpec(memory_space=pl.ANY)] * len(after),
        out_specs=tuple([_SEM] * (2 * n_grp) + [_HBM] * (n + n_land) + [pl.BlockSpec(memory_space=pltpu.VMEM)]),
        input_output_aliases={i: 2 * n_grp + i for i in range(n + n_land)},
        compiler_params=pltpu.CompilerParams(has_side_effects=side_effects),
    )(*passed, *after)
    handle = dict(streams=streams, groups=groups, send=res[:n_grp], recv=res[n_grp:2 * n_grp],
                  srcs=res[2 * n_grp:2 * n_grp + n], lands=res[2 * n_grp + n:2 * n_grp + n + n_land])
    return handle, res[-1]


def _exchange_wait(handle, gi, after, name):
    streams, group = handle["streams"], handle["groups"][gi]
    land_ids = sorted({streams[e][1] for e in group})
    srcs = [handle["srcs"][e] for e in group]
    lands = [handle["lands"][i] for i in land_ids]
    n, n_land = len(srcs), len(lands)
    side_effects = pltpu.SideEffectType.DATAFLOW_SIDE_EFFECTING

    def body(*refs):
        src_refs, land_refs = refs[:n], refs[n:n + n_land]
        send_sem, recv_sem = refs[n + n_land:n + n_land + 2]
        for j, e in enumerate(group):
            outgoing, incoming = _stream_copies(streams[e], src_refs[j], land_refs[land_ids.index(streams[e][1])], send_sem, recv_sem, j * (N_DEV - 1))
            for cp in outgoing:
                cp.wait_send()
            for cp in incoming:
                cp.wait_recv()

    res = pl.pallas_call(
        body,
        name=name,
        out_shape=tuple(pltpu.HBM(a.shape, a.dtype) for a in lands),
        in_specs=[_HBM] * (n + n_land) + [_SEM, _SEM] + [pl.BlockSpec(memory_space=pl.ANY)] * len(after),
        out_specs=tuple([_HBM] * n_land),
        input_output_aliases={n + i: i for i in range(n_land)},
        compiler_params=pltpu.CompilerParams(has_side_effects=side_effects),
    )(*srcs, *lands, handle["send"][gi], handle["recv"][gi], *after)
    return list(res)


def _whole(ref, _):
    return ref


def _slot(ref, q):
    return ref.at[q]


def _lane_block(width):
    def pick(ref, q):
        idx = (slice(None),) * (len(ref.shape) - 1) + (pl.ds(pl.multiple_of(q * width, 128), width),)
        return ref.at[idx]
    return pick


def _row_block(height):
    def pick(ref, q):
        idx = (slice(None),) * (len(ref.shape) - 2) + (pl.ds(pl.multiple_of(q * height, 16), height), slice(None))
        return ref.at[idx]
    return pick


def _block_of(axis_from_end, size):
    return _lane_block(size) if axis_from_end == 1 else _row_block(size)


def _gather_streams(shards):
    streams, outs = [], []
    for i, (shard, axis) in enumerate(shards):
        shape = list(shard.shape)
        if axis is None:
            shape, land = [N_DEV] + shape, _slot
            place = _place_slot
        else:
            size = shape[-axis]
            land = _block_of(axis, size)
            shape[-axis] *= N_DEV
            place = lambda dst, src, me, size=size, axis=axis: lax.dynamic_update_slice_in_dim(dst, src, me * size, dst.ndim - axis)
        streams.append((shard, i, _whole, land, place))
        outs.append(jax.ShapeDtypeStruct(tuple(shape), shard.dtype))
    return streams, outs


def _place_slot(dst, src, me):
    return lax.dynamic_update_index_in_dim(dst, src, me, 0)


def _scatter_streams(fulls):
    streams, outs = [], []
    for i, (full, axis) in enumerate(fulls):
        shape = list(full.shape)
        size = shape[-axis] // N_DEV
        shape[-axis] = size
        place = lambda dst, src, me, size=size, axis=axis: _place_slot(
            dst, lax.dynamic_slice_in_dim(src, me * size, size, src.ndim - axis), me)
        streams.append((full, i, _block_of(axis, size), _slot, place))
        outs.append(jax.ShapeDtypeStruct((N_DEV,) + tuple(shape), full.dtype))
    return streams, outs


FFN_TM = 512
FFN_TF = 768


def _ffn_fwd(h, g, w13, w2, after=()):
    t = h.shape[0]
    tm, tf = min(FFN_TM, t), FFN_TF
    nf = D_FF_PAD // tf

    def body(h_ref, g_ref, w1_ref, w3_ref, w2_ref, *rest):
        ho_ref, n_ref, a_ref, b_ref, n_scr, acc = rest[len(after):]
        f = pl.program_id(1)

        @pl.when(f == 0)
        def _():
            n_scr[...] = _rms(h_ref[...], g_ref[...]).astype(MXU_DTYPE)
            n_ref[...] = n_scr[...]
            acc[...] = jnp.zeros_like(acc)

        n = n_scr[...]
        a = _dot(n, w1_ref[...])
        b = _dot(n, w3_ref[...])
        a_ref[...] = a.astype(MXU_DTYPE)
        b_ref[...] = b.astype(MXU_DTYPE)
        s = (a * _sigmoid(a)) * b
        acc[...] += _dot(s.astype(MXU_DTYPE), w2_ref[...])

        @pl.when(f == nf - 1)
        def _():
            ho_ref[...] = h_ref[...] + FFN_RES_WEIGHT * acc[...]

    return pl.pallas_call(
        body,
        name="ffn_fwd",
        grid=(t // tm, nf),
        in_specs=[
            _rows(tm, D_MODEL),
            _resident((1, D_MODEL)),
            pl.BlockSpec((None, D_MODEL, tf), lambda i, f: (0, 0, f)),
            pl.BlockSpec((None, D_MODEL, tf), lambda i, f: (1, 0, f)),
            pl.BlockSpec((tf, D_MODEL), lambda i, f: (f, 0)),
        ] + [pl.BlockSpec(memory_space=pl.ANY)] * len(after),
        out_specs=[
            _rows(tm, D_MODEL),
            _rows(tm, D_MODEL),
            pl.BlockSpec((tm, tf), lambda i, f: (i, f)),
            pl.BlockSpec((tm, tf), lambda i, f: (i, f)),
        ],
        out_shape=[
            jax.ShapeDtypeStruct((t, D_MODEL), F32),
            jax.ShapeDtypeStruct((t, D_MODEL), MXU_DTYPE),
            jax.ShapeDtypeStruct((t, D_FF_PAD), MXU_DTYPE),
            jax.ShapeDtypeStruct((t, D_FF_PAD), MXU_DTYPE),
        ],
        scratch_shapes=[pltpu.VMEM((tm, D_MODEL), MXU_DTYPE), pltpu.VMEM((tm, D_MODEL), F32)],
        compiler_params=_cparams("parallel", "arbitrary"),
    )(h, g, w13, w13, w2, *after)


def _ffn_bwd(dout, h, g, a, b, w13, w2, after=()):
    t = h.shape[0]
    tm, tf = min(FFN_TM, t), FFN_TF
    nf = D_FF_PAD // tf

    def body(do_ref, h_ref, g_ref, a_ref, b_ref, w1_ref, w3_ref, w2_ref, *rest):
        dh_ref, dob_ref, s_ref, dab_ref, dg_ref, dob_scr, acc = rest[len(after):]
        i, f = pl.program_id(0), pl.program_id(1)

        @pl.when(f == 0)
        def _():
            dob_scr[...] = (FFN_RES_WEIGHT * do_ref[...]).astype(MXU_DTYPE)
            dob_ref[...] = dob_scr[...]
            acc[...] = jnp.zeros_like(acc)

        @pl.when((f == 0) & (i == 0))
        def _():
            dg_ref[...] = jnp.zeros_like(dg_ref)

        ds = _dot_nt(dob_scr[...], w2_ref[...])
        av = a_ref[...].astype(F32)
        bv = b_ref[...].astype(F32)
        sig = _sigmoid(av)
        sil = av * sig
        s_ref[...] = (sil * bv).astype(MXU_DTYPE)
        da = (ds * bv * (sig * (1.0 + av * (1.0 - sig)))).astype(MXU_DTYPE)
        db = (ds * sil).astype(MXU_DTYPE)
        dab_ref[0] = da
        dab_ref[1] = db
        acc[...] += _dot_nt(da, w1_ref[...]) + _dot_nt(db, w3_ref[...])

        @pl.when(f == nf - 1)
        def _():
            dx, dg = _rms_bwd(h_ref[...], g_ref[...], acc[...])
            dh_ref[...] = do_ref[...] + dx
            dg_ref[...] += dg

    return pl.pallas_call(
        body,
        name="ffn_bwd",
        grid=(t // tm, nf),
        in_specs=[
            _rows(tm, D_MODEL),
            _rows(tm, D_MODEL),
            _resident((1, D_MODEL)),
            pl.BlockSpec((tm, tf), lambda i, f: (i, f)),
            pl.BlockSpec((tm, tf), lambda i, f: (i, f)),
            pl.BlockSpec((None, D_MODEL, tf), lambda i, f: (0, 0, f)),
            pl.BlockSpec((None, D_MODEL, tf), lambda i, f: (1, 0, f)),
            pl.BlockSpec((tf, D_MODEL), lambda i, f: (f, 0)),
        ] + [pl.BlockSpec(memory_space=pl.ANY)] * len(after),
        out_specs=[
            _rows(tm, D_MODEL),
            _rows(tm, D_MODEL),
            pl.BlockSpec((tm, tf), lambda i, f: (i, f)),
            pl.BlockSpec((2, tm, tf), lambda i, f: (0, i, f)),
            pl.BlockSpec((1, D_MODEL), lambda i, f: (0, 0)),
        ],
        out_shape=[
            jax.ShapeDtypeStruct((t, D_MODEL), F32),
            jax.ShapeDtypeStruct((t, D_MODEL), MXU_DTYPE),
            jax.ShapeDtypeStruct((t, D_FF_PAD), MXU_DTYPE),
            jax.ShapeDtypeStruct((2, t, D_FF_PAD), MXU_DTYPE),
            jax.ShapeDtypeStruct((1, D_MODEL), F32),
        ],
        scratch_shapes=[pltpu.VMEM((tm, D_MODEL), MXU_DTYPE), pltpu.VMEM((tm, D_MODEL), F32)],
        compiler_params=_cparams("arbitrary", "arbitrary"),
    )(dout, h, g, a, b, w13, w13, w2, *after)


def _wgrad(x, y, name, out_dtype=WIRE_DTYPE):
    t, m = x.shape
    grouped = y.ndim == 3
    groups = y.shape[0] if grouped else 1
    n = y.shape[-1]
    tk = min(1024, t)
    bm = m if m <= 1536 else m // 2
    bn = n if n <= 1536 else n // 2
    nk = t // tk

    def body(x_ref, y_ref, o_ref, acc):
        k = pl.program_id(3)

        @pl.when(k == 0)
        def _():
            acc[...] = jnp.zeros_like(acc)

        acc[...] += _dot_tn(x_ref[...].astype(MXU_DTYPE), y_ref[...].astype(MXU_DTYPE))

        @pl.when(k == nk - 1)
        def _():
            o_ref[...] = acc[...].astype(out_dtype)

    if grouped:
        y_spec = pl.BlockSpec((None, tk, bn), lambda g, i, j, k: (g, k, j))
        o_spec = pl.BlockSpec((None, bm, bn), lambda g, i, j, k: (g, i, j))
        o_shape = jax.ShapeDtypeStruct((groups, m, n), out_dtype)
    else:
        y_spec = pl.BlockSpec((tk, bn), lambda g, i, j, k: (k, j))
        o_spec = pl.BlockSpec((bm, bn), lambda g, i, j, k: (i, j))
        o_shape = jax.ShapeDtypeStruct((m, n), out_dtype)
    return pl.pallas_call(
        body,
        name=name,
        grid=(groups, m // bm, n // bn, nk),
        in_specs=[pl.BlockSpec((tk, bm), lambda g, i, j, k: (k, i)), y_spec],
        out_specs=o_spec,
        out_shape=o_shape,
        scratch_shapes=[pltpu.VMEM((bm, bn), F32)],
        compiler_params=_cparams("parallel", "parallel", "parallel", "arbitrary"),
    )(x, y)


def _conv_pre(h, g, w_pw1):
    t = h.shape[0]
    tm = min(512, t)

    def body(h_ref, g_ref, w_ref, n_ref, u_ref, glu_ref):
        n = _rms(h_ref[...], g_ref[...]).astype(MXU_DTYPE)
        n_ref[...] = n
        u = _dot(n, w_ref[...])
        u_ref[...] = u
        glu_ref[...] = u[:, :D_MODEL] * _sigmoid(u[:, D_MODEL:])

    return pl.pallas_call(
        body,
        name="conv_pre",
        grid=(t // tm,),
        in_specs=[_rows(tm, D_MODEL), _resident((1, D_MODEL)), _resident((D_MODEL, 2 * D_MODEL))],
        out_specs=[_rows(tm, D_MODEL), _rows(tm, 2 * D_MODEL), _rows(tm, D_MODEL)],
        out_shape=[
            jax.ShapeDtypeStruct((t, D_MODEL), MXU_DTYPE),
            jax.ShapeDtypeStruct((t, 2 * D_MODEL), F32),
            jax.ShapeDtypeStruct((t, D_MODEL), F32),
        ],
        compiler_params=_cparams("parallel"),
    )(h, g, w_pw1)


def _conv_main(glu, w_dw, g, w_pw2, h):
    t = h.shape[0]
    tm = min(512, t)
    per = tm // CONV_HALO

    def body(glu_ref, halo_ref, w_ref, g_ref, w2_ref, h_ref, ho_ref, c_ref, sw_ref, ext):
        i = pl.program_id(0)
        ext[pl.ds(0, CONV_HALO), :] = jnp.where(i == 0, 0.0, halo_ref[...])
        ext[pl.ds(CONV_HALO, tm), :] = glu_ref[...]
        acc = jnp.zeros((tm, D_MODEL), F32)
        for k in range(CONV_WIDTH):
            acc = acc + ext[pl.ds(CONV_HALO - (CONV_WIDTH - 1) + k, tm), :] * w_ref[pl.ds(k, 1), :]
        c_ref[...] = acc
        y = _rms(acc, g_ref[...])
        sw = (y * _sigmoid(y)).astype(MXU_DTYPE)
        sw_ref[...] = sw
        ho_ref[...] = h_ref[...] + _dot(sw, w2_ref[...])

    return pl.pallas_call(
        body,
        name="conv_main",
        grid=(t // tm,),
        in_specs=[
            _rows(tm, D_MODEL),
            pl.BlockSpec((CONV_HALO, D_MODEL), lambda i: (jnp.maximum(i * per - 1, 0), 0)),
            _resident((CONV_WIDTH, D_MODEL)),
            _resident((1, D_MODEL)),
            _resident((D_MODEL, D_MODEL)),
            _rows(tm, D_MODEL),
        ],
        out_specs=[_rows(tm, D_MODEL), _rows(tm, D_MODEL), _rows(tm, D_MODEL)],
        out_shape=[
            jax.ShapeDtypeStruct((t, D_MODEL), F32),
            jax.ShapeDtypeStruct((t, D_MODEL), F32),
            jax.ShapeDtypeStruct((t, D_MODEL), MXU_DTYPE),
        ],
        scratch_shapes=[pltpu.VMEM((tm + CONV_HALO, D_MODEL), F32)],
        compiler_params=_cparams("parallel"),
    )(glu, glu, w_dw, g, w_pw2, h)


def _conv_bwd_post(dout, c, g, w_pw2):
    t = dout.shape[0]
    tm = min(512, t)

    def body(do_ref, c_ref, g_ref, w2_ref, dc_ref, dg_ref):
        @pl.when(pl.program_id(0) == 0)
        def _():
            dg_ref[...] = jnp.zeros_like(dg_ref)

        dsw = _dot_nt(do_ref[...].astype(MXU_DTYPE), w2_ref[...])
        cv = c_ref[...]
        y = _rms(cv, g_ref[...])
        sig = _sigmoid(y)
        dy = dsw * (sig * (1.0 + y * (1.0 - sig)))
        dc, dg = _rms_bwd(cv, g_ref[...], dy)
        dc_ref[...] = dc
        dg_ref[...] += dg

    return pl.pallas_call(
        body,
        name="conv_bwd_post",
        grid=(t // tm,),
        in_specs=[_rows(tm, D_MODEL), _rows(tm, D_MODEL), _resident((1, D_MODEL)), _resident((D_MODEL, D_MODEL))],
        out_specs=[_rows(tm, D_MODEL), pl.BlockSpec((1, D_MODEL), lambda i: (0, 0))],
        out_shape=[jax.ShapeDtypeStruct((t, D_MODEL), F32), jax.ShapeDtypeStruct((1, D_MODEL), F32)],
        compiler_params=_cparams("arbitrary"),
    )(dout, c, g, w_pw2)


def _conv_bwd_pre(dc, glu, u, w_dw, w_pw1, h, g, dout):
    t = h.shape[0]
    tm = min(512, t)
    per = tm // CONV_HALO
    nt = t // tm
    last_halo = t // CONV_HALO - 1

    def body(dc_ref, dcn_ref, glu_ref, gluh_ref, u_ref, w_ref, w1_ref, h_ref, g_ref, do_ref,
             dh_ref, du_ref, dw_ref, dg_ref, dext, gext):
        i = pl.program_id(0)

        @pl.when(i == 0)
        def _():
            dw_ref[...] = jnp.zeros_like(dw_ref)
            dg_ref[...] = jnp.zeros_like(dg_ref)

        dext[pl.ds(0, tm), :] = dc_ref[...]
        dext[pl.ds(tm, CONV_HALO), :] = jnp.where(i == nt - 1, 0.0, dcn_ref[...])
        gext[pl.ds(0, CONV_HALO), :] = jnp.where(i == 0, 0.0, gluh_ref[...])
        gext[pl.ds(CONV_HALO, tm), :] = glu_ref[...]
        for base in range(0, tm, CONV_ROWS):
            rows = pl.ds(base, CONV_ROWS)
            dglu = jnp.zeros((CONV_ROWS, D_MODEL), F32)
            for k in range(CONV_WIDTH):
                dglu = dglu + dext[pl.ds(base + CONV_WIDTH - 1 - k, CONV_ROWS), :] * w_ref[pl.ds(k, 1), :]
            av, bv = u_ref[rows, :D_MODEL], u_ref[rows, D_MODEL:]
            sig = _sigmoid(bv)
            du_ref[rows, :D_MODEL] = (dglu * sig).astype(MXU_DTYPE)
            du_ref[rows, D_MODEL:] = (dglu * av * (sig * (1.0 - sig))).astype(MXU_DTYPE)
        for k in range(CONV_WIDTH):
            part = jnp.zeros((CONV_ROWS, D_MODEL), F32)
            for base in range(0, tm, CONV_ROWS):
                part = part + dc_ref[pl.ds(base, CONV_ROWS), :] * gext[pl.ds(base + CONV_HALO - (CONV_WIDTH - 1) + k, CONV_ROWS), :]
            dw_ref[pl.ds(k, 1), :] += jnp.sum(part, axis=0, keepdims=True)
        dn = _dot_nt(du_ref[...], w1_ref[...])
        dx, dg = _rms_bwd(h_ref[...], g_ref[...], dn)
        dh_ref[...] = do_ref[...] + dx
        dg_ref[...] += dg

    return pl.pallas_call(
        body,
        name="conv_bwd_pre",
        grid=(nt,),
        in_specs=[
            _rows(tm, D_MODEL),
            pl.BlockSpec((CONV_HALO, D_MODEL), lambda i: (jnp.minimum((i + 1) * per, last_halo), 0)),
            _rows(tm, D_MODEL),
            pl.BlockSpec((CONV_HALO, D_MODEL), lambda i: (jnp.maximum(i * per - 1, 0), 0)),
            _rows(tm, 2 * D_MODEL),
            _resident((CONV_WIDTH, D_MODEL)),
            _resident((D_MODEL, 2 * D_MODEL)),
            _rows(tm, D_MODEL),
            _resident((1, D_MODEL)),
            _rows(tm, D_MODEL),
        ],
        out_specs=[
            _rows(tm, D_MODEL),
            _rows(tm, 2 * D_MODEL),
            pl.BlockSpec((CONV_HALO, D_MODEL), lambda i: (0, 0)),
            pl.BlockSpec((1, D_MODEL), lambda i: (0, 0)),
        ],
        out_shape=[
            jax.ShapeDtypeStruct((t, D_MODEL), F32),
            jax.ShapeDtypeStruct((t, 2 * D_MODEL), MXU_DTYPE),
            jax.ShapeDtypeStruct((CONV_HALO, D_MODEL), F32),
            jax.ShapeDtypeStruct((1, D_MODEL), F32),
        ],
        scratch_shapes=[pltpu.VMEM((tm + CONV_HALO, D_MODEL), F32), pltpu.VMEM((tm + CONV_HALO, D_MODEL), F32)],
        compiler_params=_cparams("arbitrary"),
    )(dc, dc, glu, glu, u, w_dw, w_pw1, h, g, dout)


def _rope(s, cs, sa, sb):
    return s * cs - pltpu.roll(s, 96, 1) * sa + pltpu.roll(s, 32, 1) * sb


def _unrope(d, cs, sa, sb):
    return d * cs + pltpu.roll(d, 96, 1) * sa - pltpu.roll(d, 32, 1) * sb


def _mla_pre(h, g, w_a, g_q, g_kv, w_uq, w_ukv, cs, sa, sb):
    t = h.shape[0]
    tm = min(512, t)
    qw = N_HEADS * HEAD_PAD
    vw = N_HEADS * V_HEAD

    def body(h_ref, g_ref, wa_ref, gq_ref, gkv_ref, wuq_ref, wukv_ref, cs_ref, sa_ref, sb_ref,
             n_ref, a_ref, cq_ref, ckv_ref, q_ref, k_ref, v_ref):
        n = _rms(h_ref[...], g_ref[...]).astype(MXU_DTYPE)
        n_ref[...] = n
        a = _dot(n, wa_ref[...])
        a_ref[...] = a
        cq = _rms(a[:, :Q_LORA], gq_ref[...]).astype(MXU_DTYPE)
        ckv = _rms(a[:, Q_LORA:Q_LORA + KV_LORA], gkv_ref[...]).astype(MXU_DTYPE)
        cq_ref[...] = cq
        ckv_ref[...] = ckv
        q = _dot(cq, wuq_ref[...]) * ATTN_SCALE
        kv = _dot(ckv, wukv_ref[...])
        cs_, sa_, sb_ = cs_ref[...], sa_ref[...], sb_ref[...]
        k_rot = _rope(a[:, Q_LORA + KV_LORA:], cs_, sa_, sb_).astype(MXU_DTYPE)
        for hd in range(N_HEADS):
            lo = hd * HEAD_PAD
            q_ref[:, lo:lo + QK_NOPE] = q[:, lo:lo + QK_NOPE].astype(MXU_DTYPE)
            q_ref[:, lo + QK_NOPE:lo + HEAD_PAD] = _rope(q[:, lo + QK_NOPE:lo + HEAD_PAD], cs_, sa_, sb_).astype(MXU_DTYPE)
            k_ref[:, lo:lo + QK_NOPE] = kv[:, lo:lo + QK_NOPE].astype(MXU_DTYPE)
            k_ref[:, lo + QK_NOPE:lo + HEAD_PAD] = k_rot
            v_ref[:, hd * V_HEAD:(hd + 1) * V_HEAD] = kv[:, lo + QK_NOPE:lo + HEAD_PAD].astype(MXU_DTYPE)

    return pl.pallas_call(
        body,
        name="mla_pre",
        grid=(t // tm,),
        in_specs=[
            _rows(tm, D_MODEL),
            _resident((1, D_MODEL)),
            _resident((D_MODEL, A_PAD)),
            _resident((1, Q_LORA)),
            _resident((1, KV_LORA)),
            _resident((Q_LORA, qw)),
            _resident((KV_LORA, qw)),
            _rows(tm, 128),
            _rows(tm, 128),
            _rows(tm, 128),
        ],
        out_specs=[_rows(tm, D_MODEL), _rows(tm, A_PAD), _rows(tm, Q_LORA), _rows(tm, KV_LORA), _rows(tm, qw), _rows(tm, qw), _rows(tm, vw)],
        out_shape=[
            jax.ShapeDtypeStruct((t, D_MODEL), MXU_DTYPE),
            jax.ShapeDtypeStruct((t, A_PAD), F32),
            jax.ShapeDtypeStruct((t, Q_LORA), MXU_DTYPE),
            jax.ShapeDtypeStruct((t, KV_LORA), MXU_DTYPE),
            jax.ShapeDtypeStruct((t, qw), MXU_DTYPE),
            jax.ShapeDtypeStruct((t, qw), MXU_DTYPE),
            jax.ShapeDtypeStruct((t, vw), MXU_DTYPE),
        ],
        compiler_params=_cparams("parallel"),
    )(h, g, w_a, g_q, g_kv, w_uq, w_ukv, cs, sa, sb)


def _attn_tile(t):
    return min(1024, t)


SOFTMAX_ROWS = 16


def _chunk_mask(row0, col0, rows, cols):
    r = row0 + lax.broadcasted_iota(jnp.int32, (rows, cols), 0)
    c = col0 + lax.broadcasted_iota(jnp.int32, (rows, cols), 1)
    return (r >> CHUNK_SHIFT) >= (c >> CHUNK_SHIFT)


def _causal_pairs(n, by_column=False):
    if by_column:
        pairs = [(i, j) for j in range(n) for i in range(j, n)]
    else:
        pairs = [(i, j) for i in range(n) for j in range(i + 1)]
    return jnp.asarray([p[0] for p in pairs], jnp.int32), jnp.asarray([p[1] for p in pairs], jnp.int32)


def _flash_fwd(q, k, v):
    t = q.shape[0]
    tq = tk = _attn_tile(t)
    rows_of, cols_of = _causal_pairs(t // tq)

    def body(qi_ref, ki_ref, q_ref, k_ref, v_ref, o_ref, lse_ref, m_s, l_s, acc, s_scr, p_scr):
        step = pl.program_id(1)
        qi, ki = qi_ref[step], ki_ref[step]

        @pl.when(ki == 0)
        def _():
            m_s[...] = jnp.full_like(m_s, -jnp.inf)
            l_s[...] = jnp.zeros_like(l_s)
            acc[...] = jnp.zeros_like(acc)

        def block(on_diagonal):
            s_scr[...] = _dot_nt(q_ref[...], k_ref[...])

            def softmax_rows(c, carry):
                r = pl.ds(pl.multiple_of(c * SOFTMAX_ROWS, SOFTMAX_ROWS), SOFTMAX_ROWS)
                s = s_scr[r, :]
                if on_diagonal:
                    s = jnp.where(_chunk_mask(c * SOFTMAX_ROWS, 0, SOFTMAX_ROWS, tk), s, -jnp.inf)
                m_prev = m_s[r, :]
                m_new = jnp.maximum(m_prev, jnp.max(s, axis=1, keepdims=True))
                alpha = jnp.exp(m_prev - m_new)
                p = jnp.exp(s - m_new)
                l_s[r, :] = alpha * l_s[r, :] + jnp.sum(p, axis=1, keepdims=True)
                m_s[r, :] = m_new
                acc[r, :] = alpha * acc[r, :]
                p_scr[r, :] = p.astype(MXU_DTYPE)
                return carry

            lax.fori_loop(0, tq // SOFTMAX_ROWS, softmax_rows, 0)
            acc[...] += _dot(p_scr[...], v_ref[...])

        pl.when(ki < qi)(lambda: block(False))

        @pl.when(ki == qi)
        def _():
            block(True)
            o_ref[...] = acc[...] / l_s[...]
            lse_ref[...] = jnp.broadcast_to(m_s[...] + jnp.log(l_s[...]), (tq, V_HEAD))

    grid_spec = pltpu.PrefetchScalarGridSpec(
        num_scalar_prefetch=2,
        grid=(N_HEADS, rows_of.shape[0]),
        in_specs=[
            pl.BlockSpec((tq, HEAD_PAD), lambda h, s, qi, ki: (qi[s], h)),
            pl.BlockSpec((tk, HEAD_PAD), lambda h, s, qi, ki: (ki[s], h)),
            pl.BlockSpec((tk, V_HEAD), lambda h, s, qi, ki: (ki[s], h)),
        ],
        out_specs=[pl.BlockSpec((tq, V_HEAD), lambda h, s, qi, ki: (qi[s], h)), pl.BlockSpec((tq, V_HEAD), lambda h, s, qi, ki: (qi[s], h))],
        scratch_shapes=[pltpu.VMEM((tq, 1), F32), pltpu.VMEM((tq, 1), F32), pltpu.VMEM((tq, V_HEAD), F32),
                        pltpu.VMEM((tq, tk), F32), pltpu.VMEM((tq, tk), MXU_DTYPE)],
    )
    return pl.pallas_call(
        body,
        name="flash_fwd",
        grid_spec=grid_spec,
        out_shape=[jax.ShapeDtypeStruct((t, N_HEADS * V_HEAD), F32), jax.ShapeDtypeStruct((t, N_HEADS * V_HEAD), F32)],
        compiler_params=_cparams("parallel", "arbitrary"),
    )(rows_of, cols_of, q, k, v)


def _flash_bwd(q, k, v, do, lse, delta):
    t = q.shape[0]
    tq = tk = _attn_tile(t)
    nq = t // tq
    rows_of, cols_of = _causal_pairs(nq, by_column=True)

    def body(qi_ref, kj_ref, q_ref, k_ref, v_ref, do_ref, lse_ref, dl_ref, dq_ref, dk_ref, dv_ref, dk_acc, dv_acc, s_scr, dp_scr, p_scr, ds_scr):
        step = pl.program_id(1)
        qi, kj = qi_ref[step], kj_ref[step]

        @pl.when(step == 0)
        def _():
            dq_ref[...] = jnp.zeros_like(dq_ref)

        def block(on_diagonal):
            qv, kv, dov = q_ref[...], k_ref[...], do_ref[...]
            s_scr[...] = _dot_nt(qv, kv)
            dp_scr[...] = _dot_nt(dov, v_ref[...])

            def softmax_rows(c, carry):
                r = pl.ds(pl.multiple_of(c * SOFTMAX_ROWS, SOFTMAX_ROWS), SOFTMAX_ROWS)
                s = s_scr[r, :]
                if on_diagonal:
                    s = jnp.where(_chunk_mask(c * SOFTMAX_ROWS, 0, SOFTMAX_ROWS, tk), s, -jnp.inf)
                p = jnp.exp(s - lse_ref[r, :1])
                p_scr[r, :] = p.astype(MXU_DTYPE)
                ds_scr[r, :] = (p * (dp_scr[r, :] - dl_ref[r, :1])).astype(MXU_DTYPE)
                return carry

            lax.fori_loop(0, tq // SOFTMAX_ROWS, softmax_rows, 0)
            dv_new = _dot_tn(p_scr[...], dov)
            dk_new = _dot_tn(ds_scr[...], qv)
            if on_diagonal:
                dv_acc[...] = dv_new
                dk_acc[...] = dk_new
            else:
                dv_acc[...] += dv_new
                dk_acc[...] += dk_new
            rows = pl.ds(pl.multiple_of(qi * tq, tq), tq)
            dq_ref[rows, :] += _dot(ds_scr[...], kv)

        pl.when(qi > kj)(lambda: block(False))
        pl.when(qi == kj)(lambda: block(True))

        @pl.when(qi == nq - 1)
        def _():
            dk_ref[...] = dk_acc[...]
            dv_ref[...] = dv_acc[...]

    grid_spec = pltpu.PrefetchScalarGridSpec(
        num_scalar_prefetch=2,
        grid=(N_HEADS, rows_of.shape[0]),
        in_specs=[
            pl.BlockSpec((tq, HEAD_PAD), lambda h, s, qi, kj: (qi[s], h)),
            pl.BlockSpec((tk, HEAD_PAD), lambda h, s, qi, kj: (kj[s], h)),
            pl.BlockSpec((tk, V_HEAD), lambda h, s, qi, kj: (kj[s], h)),
            pl.BlockSpec((tq, V_HEAD), lambda h, s, qi, kj: (qi[s], h)),
            pl.BlockSpec((tq, V_HEAD), lambda h, s, qi, kj: (qi[s], h)),
            pl.BlockSpec((tq, V_HEAD), lambda h, s, qi, kj: (qi[s], h)),
        ],
        out_specs=[
            pl.BlockSpec((t, HEAD_PAD), lambda h, s, qi, kj: (0, h)),
            pl.BlockSpec((tk, HEAD_PAD), lambda h, s, qi, kj: (kj[s], h)),
            pl.BlockSpec((tk, V_HEAD), lambda h, s, qi, kj: (kj[s], h)),
        ],
        scratch_shapes=[pltpu.VMEM((tk, HEAD_PAD), F32), pltpu.VMEM((tk, V_HEAD), F32),
                        pltpu.VMEM((tq, tk), F32), pltpu.VMEM((tq, tk), F32), pltpu.VMEM((tq, tk), MXU_DTYPE), pltpu.VMEM((tq, tk), MXU_DTYPE)],
    )
    return pl.pallas_call(
        body,
        name="flash_bwd",
        grid_spec=grid_spec,
        out_shape=[
            jax.ShapeDtypeStruct((t, N_HEADS * HEAD_PAD), F32),
            jax.ShapeDtypeStruct((t, N_HEADS * HEAD_PAD), F32),
            jax.ShapeDtypeStruct((t, N_HEADS * V_HEAD), F32),
        ],
        compiler_params=_cparams("arbitrary", "arbitrary"),
    )(rows_of, cols_of, q, k, v, do, lse, delta)


def _attn_out(o, w_o, h):
    t = h.shape[0]
    tm = min(512, t)

    def body(o_ref, w_ref, h_ref, ho_ref):
        ho_ref[...] = h_ref[...] + _dot(o_ref[...].astype(MXU_DTYPE), w_ref[...])

    return pl.pallas_call(
        body,
        name="attn_out",
        grid=(t // tm,),
        in_specs=[_rows(tm, D_MODEL), _resident((D_MODEL, D_MODEL)), _rows(tm, D_MODEL)],
        out_specs=_rows(tm, D_MODEL),
        out_shape=jax.ShapeDtypeStruct((t, D_MODEL), F32),
        compiler_params=_cparams("parallel"),
    )(o, w_o, h)


def _attn_out_bwd(dout, o, w_o):
    t = dout.shape[0]
    tm = min(512, t)

    def body(d_ref, o_ref, w_ref, do_ref, dl_ref):
        do = _dot_nt(d_ref[...].astype(MXU_DTYPE), w_ref[...])
        do_ref[...] = do.astype(MXU_DTYPE)
        prod = do * o_ref[...]
        for hd in range(N_HEADS):
            lanes = slice(hd * V_HEAD, (hd + 1) * V_HEAD)
            dl_ref[:, lanes] = jnp.broadcast_to(jnp.sum(prod[:, lanes], axis=1, keepdims=True), (tm, V_HEAD))

    return pl.pallas_call(
        body,
        name="attn_out_bwd",
        grid=(t // tm,),
        in_specs=[_rows(tm, D_MODEL), _rows(tm, D_MODEL), _resident((D_MODEL, D_MODEL))],
        out_specs=[_rows(tm, D_MODEL), _rows(tm, D_MODEL)],
        out_shape=[jax.ShapeDtypeStruct((t, D_MODEL), MXU_DTYPE), jax.ShapeDtypeStruct((t, D_MODEL), F32)],
        compiler_params=_cparams("parallel"),
    )(dout, o, w_o)


def _mla_bwd_pre(dq, dk, dv, a, h, dout, g, g_q, g_kv, w_a, w_uq, w_ukv, cs, sa, sb):
    t = h.shape[0]
    tm = min(512, t)
    qw = N_HEADS * HEAD_PAD
    vw = N_HEADS * V_HEAD

    def body(dq_ref, dk_ref, dv_ref, a_ref, h_ref, do_ref, g_ref, gq_ref, gkv_ref, wa_ref, wuq_ref, wukv_ref,
             cs_ref, sa_ref, sb_ref, dh_ref, dqp_ref, dkv_ref, da_ref, dg_ref, dgq_ref, dgkv_ref):
        @pl.when(pl.program_id(0) == 0)
        def _():
            dg_ref[...] = jnp.zeros_like(dg_ref)
            dgq_ref[...] = jnp.zeros_like(dgq_ref)
            dgkv_ref[...] = jnp.zeros_like(dgkv_ref)

        cs_, sa_, sb_ = cs_ref[...], sa_ref[...], sb_ref[...]
        slab = jnp.zeros((tm, 128), F32)
        for hd in range(N_HEADS):
            lo = hd * HEAD_PAD
            dqp_ref[:, lo:lo + QK_NOPE] = (dq_ref[:, lo:lo + QK_NOPE] * ATTN_SCALE).astype(MXU_DTYPE)
            dqp_ref[:, lo + QK_NOPE:lo + HEAD_PAD] = _unrope(dq_ref[:, lo + QK_NOPE:lo + HEAD_PAD] * ATTN_SCALE, cs_, sa_, sb_).astype(MXU_DTYPE)
            dkv_ref[:, lo:lo + QK_NOPE] = dk_ref[:, lo:lo + QK_NOPE].astype(MXU_DTYPE)
            dkv_ref[:, lo + QK_NOPE:lo + HEAD_PAD] = dv_ref[:, hd * V_HEAD:(hd + 1) * V_HEAD].astype(MXU_DTYPE)
            slab = slab + dk_ref[:, lo + QK_NOPE:lo + HEAD_PAD]
        dcq = _dot_nt(dqp_ref[...], wuq_ref[...])
        dckv = _dot_nt(dkv_ref[...], wukv_ref[...])
        av = a_ref[...]
        daq, dgq = _rms_bwd(av[:, :Q_LORA], gq_ref[...], dcq)
        dakv, dgkv = _rms_bwd(av[:, Q_LORA:Q_LORA + KV_LORA], gkv_ref[...], dckv)
        da_ref[:, :Q_LORA] = daq.astype(MXU_DTYPE)
        da_ref[:, Q_LORA:Q_LORA + KV_LORA] = dakv.astype(MXU_DTYPE)
        da_ref[:, Q_LORA + KV_LORA:] = _unrope(slab, cs_, sa_, sb_).astype(MXU_DTYPE)
        dn = _dot_nt(da_ref[...], wa_ref[...])
        dx, dg = _rms_bwd(h_ref[...], g_ref[...], dn)
        dh_ref[...] = do_ref[...] + dx
        dg_ref[...] += dg
        dgq_ref[...] += dgq
        dgkv_ref[...] += dgkv

    def const(n):
        return pl.BlockSpec((1, n), lambda i: (0, 0))

    return pl.pallas_call(
        body,
        name="mla_bwd_pre",
        grid=(t // tm,),
        in_specs=[
            _rows(tm, qw), _rows(tm, qw), _rows(tm, vw), _rows(tm, A_PAD), _rows(tm, D_MODEL), _rows(tm, D_MODEL),
            _resident((1, D_MODEL)), _resident((1, Q_LORA)), _resident((1, KV_LORA)),
            _resident((D_MODEL, A_PAD)), _resident((Q_LORA, qw)), _resident((KV_LORA, qw)),
            _rows(tm, 128), _rows(tm, 128), _rows(tm, 128),
        ],
        out_specs=[_rows(tm, D_MODEL), _rows(tm, qw), _rows(tm, qw), _rows(tm, A_PAD),
                   const(D_MODEL), const(Q_LORA), const(KV_LORA)],
        out_shape=[
            jax.ShapeDtypeStruct((t, D_MODEL), F32),
            jax.ShapeDtypeStruct((t, qw), MXU_DTYPE),
            jax.ShapeDtypeStruct((t, qw), MXU_DTYPE),
            jax.ShapeDtypeStruct((t, A_PAD), MXU_DTYPE),
            jax.ShapeDtypeStruct((1, D_MODEL), F32),
            jax.ShapeDtypeStruct((1, Q_LORA), F32),
            jax.ShapeDtypeStruct((1, KV_LORA), F32),
        ],
        compiler_params=_cparams("arbitrary"),
    )(dq, dk, dv, a, h, dout, g, g_q, g_kv, w_a, w_uq, w_ukv, cs, sa, sb)


def _loss_head(h, g, target):
    t = h.shape[0]
    tm = min(512, t)

    def body(h_ref, g_ref, t_ref, sq_ref, dh_ref, dg_ref):
        @pl.when(pl.program_id(0) == 0)
        def _():
            sq_ref[...] = jnp.zeros_like(sq_ref)
            dg_ref[...] = jnp.zeros_like(dg_ref)

        x = h_ref[...]
        err = _rms(x, g_ref[...]) - t_ref[...]
        sq_ref[...] += jnp.sum(err * err, axis=0, keepdims=True)
        dx, dg = _rms_bwd(x, g_ref[...], err * (1.0 / D_MODEL))
        dh_ref[...] = dx
        dg_ref[...] += dg

    return pl.pallas_call(
        body,
        name="loss_head",
        grid=(t // tm,),
        in_specs=[_rows(tm, D_MODEL), _resident((1, D_MODEL)), _rows(tm, D_MODEL)],
        out_specs=[pl.BlockSpec((1, D_MODEL), lambda i: (0, 0)), _rows(tm, D_MODEL), pl.BlockSpec((1, D_MODEL), lambda i: (0, 0))],
        out_shape=[jax.ShapeDtypeStruct((1, D_MODEL), F32), jax.ShapeDtypeStruct((t, D_MODEL), F32), jax.ShapeDtypeStruct((1, D_MODEL), F32)],
        compiler_params=_cparams("arbitrary"),
    )(h, g, target)


def _adamw(parts, w, m, v, name, tr, layer=0, row_offset=0, into=None):
    layers, rows, width = w.shape
    pwidth = parts.shape[-1]
    off = row_offset // tr

    def body(p_ref, w_ref, m_ref, v_ref, *rest):
        g_ref, d_ref, mo_ref, vo_ref = rest[-4:]

        def part(q):
            return p_ref[q, :, pl.ds(0, width)].astype(F32)

        g = part(0)
        for q in range(1, N_DEV):
            g = g + part(q)
        g_ref[...] = g
        m_new = ADAM_B1 * m_ref[...] + (1.0 - ADAM_B1) * g
        v_new = ADAM_B2 * v_ref[...] + (1.0 - ADAM_B2) * (g * g)
        m_hat = m_new / (1.0 - ADAM_B1 ** ADAM_STEP)
        v_hat = v_new / (1.0 - ADAM_B2 ** ADAM_STEP)
        d_ref[...] = -ADAM_LR * (m_hat / (jnp.sqrt(v_hat) + ADAM_EPS) + ADAM_WD * w_ref[...])
        mo_ref[...] = m_new
        vo_ref[...] = v_new

    blk = pl.BlockSpec((None, tr, width), lambda i: (layer, i, 0))
    earlier = () if into is None else tuple(into)
    return pl.pallas_call(
        body,
        name=name,
        grid=(rows // tr,),
        in_specs=[pl.BlockSpec((N_DEV, tr, pwidth), lambda i: (0, off + i, 0)), blk, blk, blk] + [pl.BlockSpec(memory_space=pl.ANY)] * len(earlier),
        out_specs=[blk, blk, blk, blk],
        out_shape=[jax.ShapeDtypeStruct((layers, rows, width), F32)] * 4,
        input_output_aliases={4 + j: j for j in range(len(earlier))},
        compiler_params=_cparams("parallel"),
    )(parts, w, m, v, *earlier)


_GAINS = ("ffn_norm1", "mix_norm", "ffn_norm2", "conv_norm", "final_norm")
_TINY = ("conv_w_dw", "mla_q_norm", "mla_kv_norm")


def _small_pack(parts):
    flat = jnp.concatenate([p.reshape(-1).astype(F32) for p in parts])
    return jnp.pad(flat, (0, SMALL_ROWS * D_MODEL - flat.shape[0])).reshape(SMALL_ROWS, D_MODEL)


def _small_unpack(packed, shapes):
    flat, out, off = packed.reshape(-1), [], 0
    for s in shapes:
        size = 1
        for d in s:
            size *= d
        out.append(flat[off:off + size].reshape(s))
        off += size
    return out


def _pad_to(a, axis, size):
    pad = [(0, 0)] * a.ndim
    pad[axis] = (0, size - a.shape[axis])
    return jnp.pad(a, pad)


def kernel(x, positions, ffn_norm1, ffn1_w1, ffn1_w3, ffn1_w2, mix_norm, ffn_norm2, ffn2_w1, ffn2_w3, ffn2_w2, conv_w_pw1, conv_w_dw, conv_norm, conv_w_pw2, mla_w_a, mla_q_norm, mla_kv_norm, mla_w_uq, mla_w_ukv, mla_w_o, final_norm, loss_target, m_ffn_norm1, m_ffn1_w1, m_ffn1_w3, m_ffn1_w2, m_mix_norm, m_ffn_norm2, m_ffn2_w1, m_ffn2_w3, m_ffn2_w2, m_conv_w_pw1, m_conv_w_dw, m_conv_norm, m_conv_w_pw2, m_mla_w_a, m_mla_q_norm, m_mla_kv_norm, m_mla_w_uq, m_mla_w_ukv, m_mla_w_o, m_final_norm, v_ffn_norm1, v_ffn1_w1, v_ffn1_w3, v_ffn1_w2, v_mix_norm, v_ffn_norm2, v_ffn2_w1, v_ffn2_w3, v_ffn2_w2, v_conv_w_pw1, v_conv_w_dw, v_conv_norm, v_conv_w_pw2, v_mla_w_a, v_mla_q_norm, v_mla_kv_norm, v_mla_w_uq, v_mla_w_ukv, v_mla_w_o, v_final_norm):
    args = dict(locals())
    wire = lambda a: a.astype(WIRE_DTYPE)

    def ffn_shards(prefix, layer):
        w13_shard = _pad_to(jnp.stack([args[prefix + "_w1"][layer], args[prefix + "_w3"][layer]]), 2, FF_SHARD_PAD)
        return [(wire(w13_shard), 1), (wire(_pad_to(args[prefix + "_w2"][layer], 0, FF_SHARD_PAD)), 2)]

    def uq_pad(a):
        return _pad_to(a, 3, HEAD_PAD).reshape(1, a.shape[1], N_HEADS * HEAD_PAD)

    ukv_rows = lambda a: a.reshape(1, a.shape[1], N_HEADS * HEAD_PAD)
    tiny_shapes = [conv_w_dw.shape, mla_q_norm.shape, mla_kv_norm.shape]

    first = _exchange(*_gather_streams(ffn_shards("ffn1", 0)), "weight_gather_first")
    later_groups = [
        [(wire(conv_w_pw1[0]), 1), (wire(conv_w_pw2[0]), 2), (_small_pack([conv_w_dw, mla_q_norm, mla_kv_norm]), None)],
        ffn_shards("ffn2", 0),
        ffn_shards("ffn1", 1),
        [(wire(_pad_to(mla_w_a[0], 1, A_PAD)), 2), (wire(uq_pad(mla_w_uq)[0]), 2), (wire(ukv_rows(mla_w_ukv)[0]), 2), (wire(mla_w_o[0]), 2)],
        ffn_shards("ffn2", 1),
    ]
    later_streams, later_outs = _gather_streams([sh for grp in later_groups for sh in grp])
    group_ids, at = [], 0
    for grp in later_groups:
        group_ids.append(list(range(at, at + len(grp))))
        at += len(grp)
    gather, gather_token = _exchange_start(
        later_streams, _landing(later_streams, later_outs), group_ids, (first[0],), "weight_gather_start")

    inv_freq = ROPE_THETA ** (-2.0 * jnp.arange(QK_ROPE // 2, dtype=F32) / QK_ROPE)
    ang = positions[0].astype(F32)[:, None] * inv_freq
    cos, sin, zero = jnp.cos(ang), jnp.sin(ang), jnp.zeros_like(ang)
    cs = jnp.concatenate([cos, cos, zero, zero], axis=1)
    sa = jnp.concatenate([sin, zero, zero, zero], axis=1)
    sb = jnp.concatenate([zero, sin, zero, zero], axis=1)

    ffn_w = {("ffn1", 0): first}
    h0 = x[0]
    h1, n1, a1, b1 = _ffn_fwd(h0, ffn_norm1[0:1], *ffn_w[("ffn1", 0)], after=(gather_token,))
    w_pw1, w_pw2, tiny = _exchange_wait(gather, 0, (h1,), "weight_gather_wait_conv")
    tiny = [_small_unpack(tiny[q], tiny_shapes) for q in range(N_DEV)]
    w_dw = jnp.concatenate([t_[0][0] for t_ in tiny], axis=1)
    g_q = jnp.concatenate([t_[1] for t_ in tiny], axis=1)
    g_kv = jnp.concatenate([t_[2] for t_ in tiny], axis=1)
    n_c, u, glu = _conv_pre(h1, mix_norm[0:1], w_pw1)
    h2, c, sw = _conv_main(glu, w_dw, conv_norm, w_pw2, h1)
    ffn_w[("ffn2", 0)] = _exchange_wait(gather, 1, (h2,), "weight_gather_wait_ffn2_0")
    h3, n2, a2, b2 = _ffn_fwd(h2, ffn_norm2[0:1], *ffn_w[("ffn2", 0)])
    ffn_w[("ffn1", 1)] = _exchange_wait(gather, 2, (h3,), "weight_gather_wait_ffn1_1")
    h4, n3, a3, b3 = _ffn_fwd(h3, ffn_norm1[1:2], *ffn_w[("ffn1", 1)])
    w_a, w_uq, w_ukv, w_o = _exchange_wait(gather, 3, (h4,), "weight_gather_wait_mla")
    n_a, a_lat, cq, ckv, q, k, v = _mla_pre(h4, mix_norm[1:2], w_a, g_q, g_kv, w_uq, w_ukv, cs, sa, sb)
    o, lse = _flash_fwd(q, k, v)
    h5 = _attn_out(o, w_o, h4)
    ffn_w[("ffn2", 1)] = _exchange_wait(gather, 4, (h5,), "weight_gather_wait_ffn2_1")
    h6, n4, a4, b4 = _ffn_fwd(h5, ffn_norm2[1:2], *ffn_w[("ffn2", 1)])

    sq, dh, dg_final = _loss_head(h6, final_norm[None, :], loss_target[0])
    loss = lax.psum(0.5 / D_MODEL * jnp.sum(sq), ("x", "y", "c"))

    gain = {}

    def ffn_backward(prefix, norm, layer, dh, h_in, n, a, b, after=()):
        dh, dob, s, dab, gain[(norm, layer)] = _ffn_bwd(dh, h_in, args[norm][layer:layer + 1], a, b, *ffn_w[(prefix, layer)], after=after)
        return dh, [(_wgrad(n, dab, "wgrad_ffn_in"), 1), (_wgrad(s, dob, "wgrad_ffn_out"), 2)]

    dh, g_ffn2_1 = ffn_backward("ffn2", "ffn_norm2", 1, dh, h5, n4, a4, b4)
    do, delta = _attn_out_bwd(dh, o, w_o)
    d_o = _wgrad(o, dh, "wgrad_attn_out")
    dq, dk, dv = _flash_bwd(q, k, v, do, lse, delta)
    dh, dqp, dkv, da_lat, gain[("mix_norm", 1)], d_gq, d_gkv = _mla_bwd_pre(
        dq, dk, dv, a_lat, h4, dh, mix_norm[1:2], g_q, g_kv, w_a, w_uq, w_ukv, cs, sa, sb)
    g_mla = [(_wgrad(n_a, da_lat, "wgrad_mla_a"), 2), (_wgrad(cq, dqp, "wgrad_mla_uq"), 2), (_wgrad(ckv, dkv, "wgrad_mla_ukv"), 2), (d_o, 2)]
    dh, g_ffn1_1 = ffn_backward("ffn1", "ffn_norm1", 1, dh, h3, n3, a3, b3)
    streams_a, outs_a = _scatter_streams(g_ffn2_1 + g_mla + g_ffn1_1)
    scatter_a, token_a = _exchange_start(
        streams_a, _landing(streams_a, outs_a), [list(range(len(streams_a)))], (), "grad_scatter_start_layer1")

    dh, g_ffn2_0 = ffn_backward("ffn2", "ffn_norm2", 0, dh, h2, n2, a2, b2, after=(token_a,))
    dc, gain[("conv_norm", 0)] = _conv_bwd_post(dh, c, conv_norm, w_pw2)
    d_pw2 = _wgrad(sw, dh, "wgrad_conv_pw2")
    dh, du, d_dw, gain[("mix_norm", 0)] = _conv_bwd_pre(dc, glu, u, w_dw, w_pw1, h1, mix_norm[0:1], dh)
    d_pw1 = _wgrad(n_c, du, "wgrad_conv_pw1")
    streams_b, outs_b = _scatter_streams(g_ffn2_0 + [(d_pw1, 1), (d_pw2, 2)])
    scatter_b, token_b = _exchange_start(
        streams_b, _landing(streams_b, outs_b), [list(range(len(streams_b)))], (), "grad_scatter_start_conv")

    dh, g_ffn1_0 = ffn_backward("ffn1", "ffn_norm1", 0, dh, h0, n1, a1, b1, after=(token_b,))
    grad_x = dh[None]

    gain_rows = jnp.concatenate([
        gain[("ffn_norm1", 0)], gain[("ffn_norm1", 1)], gain[("mix_norm", 0)], gain[("mix_norm", 1)],
        gain[("ffn_norm2", 0)], gain[("ffn_norm2", 1)], gain[("conv_norm", 0)], dg_final])
    dw_by_dest = jnp.moveaxis(d_dw[:CONV_WIDTH].reshape(CONV_WIDTH, N_DEV, -1), 1, 0)
    small_by_dest = jnp.stack([
        _small_pack([gain_rows, dw_by_dest[p], d_gq.reshape(N_DEV, -1)[p], d_gkv.reshape(N_DEV, -1)[p]]) for p in range(N_DEV)])
    streams_c, outs_c = _scatter_streams(g_ffn1_0)
    streams_c.append((small_by_dest, len(outs_c), _slot, _slot, lambda dst, src, me: _place_slot(dst, lax.dynamic_index_in_dim(src, me, 0, False), me)))
    outs_c.append(jax.ShapeDtypeStruct((N_DEV, SMALL_ROWS, D_MODEL), F32))
    scatter_c, token_c = _exchange_start(streams_c, _landing(streams_c, outs_c), [list(range(len(streams_c)))], (), "grad_scatter_start_last")

    (p13_ffn2_1, p2_ffn2_1, p_a, p_uq, p_ukv, p_o, p13_ffn1_1, p2_ffn1_1) = _exchange_wait(scatter_a, 0, (token_c,), "grad_scatter_wait_layer1")
    p13_ffn2_0, p2_ffn2_0, p_pw1, p_pw2 = _exchange_wait(scatter_b, 0, (token_c,), "grad_scatter_wait_conv")
    results = {}

    def adam(name, parts, tr, view=lambda a: a, **kw):
        results[name] = _adamw(parts, view(args[name]), view(args["m_" + name]), view(args["v_" + name]), "adamw_" + name, tr,
                               into=results.get(name), **kw)

    def adam_ffn(prefix, layer, p13, p2):
        p13 = p13.reshape(N_DEV, 2 * D_MODEL, FF_SHARD_PAD)
        adam(prefix + "_w1", p13, 256, layer=layer)
        adam(prefix + "_w3", p13, 256, layer=layer, row_offset=D_MODEL)
        adam(prefix + "_w2", p2, FF_SHARD, layer=layer)

    adam_ffn("ffn2", 0, p13_ffn2_0, p2_ffn2_0)
    adam_ffn("ffn2", 1, p13_ffn2_1, p2_ffn2_1)
    adam_ffn("ffn1", 1, p13_ffn1_1, p2_ffn1_1)
    adam("conv_w_pw1", p_pw1, 256)
    adam("conv_w_pw2", p_pw2, 128)
    adam("mla_w_a", p_a, 128)
    adam("mla_w_uq", p_uq, 64, view=uq_pad)
    adam("mla_w_ukv", p_ukv, 32, view=ukv_rows)
    adam("mla_w_o", p_o, 128)
    results["mla_w_uq"] = [r.reshape(1, -1, N_HEADS, HEAD_PAD)[..., :QK_NOPE + QK_ROPE] for r in results["mla_w_uq"]]
    results["mla_w_ukv"] = [r.reshape(mla_w_ukv.shape) for r in results["mla_w_ukv"]]

    p13_ffn1_0, p2_ffn1_0, p_small = _exchange_wait(scatter_c, 0, (results["mla_w_o"][0],), "grad_scatter_wait_last")
    adam_ffn("ffn1", 0, p13_ffn1_0, p2_ffn1_0)
    small_names = _GAINS + _TINY
    small_shapes = [args[n].shape for n in small_names]
    pack_small = lambda prefix: _small_pack([args[prefix + n] for n in small_names])[None]
    small = _adamw(p_small, pack_small(""), pack_small("m_"), pack_small("v_"), "adamw_small", SMALL_ROWS)
    for kind in range(4):
        for n, leaf in zip(small_names, _small_unpack(small[kind][0], small_shapes)):
            results.setdefault(n, [None] * 4)[kind] = leaf

    order = ("ffn_norm1", "ffn1_w1", "ffn1_w3", "ffn1_w2", "mix_norm", "ffn_norm2", "ffn2_w1", "ffn2_w3", "ffn2_w2",
             "conv_w_pw1", "conv_w_dw", "conv_norm", "conv_w_pw2", "mla_w_a", "mla_q_norm", "mla_kv_norm",
             "mla_w_uq", "mla_w_ukv", "mla_w_o", "final_norm")
    outputs = [loss, grad_x]
    for kind in range(4):
        outputs.extend(results[n][kind] for n in order)
    return tuple(outputs)
```

```python
import jax
import jax.numpy as jnp
from jax import lax
from jax.experimental import pallas as pl
from jax.experimental.pallas import tpu as pltpu

F32 = jnp.float32
MXU_DTYPE = jnp.bfloat16
WIRE_DTYPE = jnp.bfloat16

N_DEV = 8
D_MODEL = 1024
FF_SHARD = 352
FF_SHARD_PAD = 384
D_FF_PAD = N_DEV * FF_SHARD_PAD
N_HEADS = 8
QK_NOPE = 128
QK_ROPE = 64
V_HEAD = 128
Q_LORA = 512
KV_LORA = 256
HEAD_PAD = 256
A_WIDTH = Q_LORA + KV_LORA + QK_ROPE
A_PAD = Q_LORA + KV_LORA + 128
CONV_WIDTH = 31
CONV_HALO = 32
CONV_ROWS = 16
CHUNK_SHIFT = 6
ROPE_THETA = 10000.0
RMS_EPS = 1e-6
ATTN_SCALE = (QK_NOPE + QK_ROPE) ** -0.5
FFN_RES_WEIGHT = 0.5
ADAM_LR = 0.001
ADAM_B1 = 0.9
ADAM_B2 = 0.999
ADAM_EPS = 1e-08
ADAM_WD = 0.01
ADAM_STEP = 10

SMALL_ROWS = 16
VMEM_LIMIT = 56 << 20


def _cparams(*sem):
    return pltpu.CompilerParams(dimension_semantics=sem, vmem_limit_bytes=VMEM_LIMIT)


def _dot(a, b):
    return lax.dot_general(a, b, (((1,), (0,)), ((), ())), preferred_element_type=F32)


def _dot_nt(a, b):
    return lax.dot_general(a, b, (((1,), (1,)), ((), ())), preferred_element_type=F32)


def _dot_tn(a, b):
    return lax.dot_general(a, b, (((0,), (0,)), ((), ())), preferred_element_type=F32)


def _resident(shape, index=None):
    fixed = index if index is not None else (0,) * len(shape)
    return pl.BlockSpec(shape, lambda *_: fixed, pipeline_mode=pl.Buffered(1))


def _rows(tm, n):
    return pl.BlockSpec((tm, n), lambda t, *_: (t, 0))


def _rms(x, g):
    r = lax.rsqrt(jnp.mean(x * x, axis=-1, keepdims=True) + RMS_EPS)
    return x * r * g


def _rms_bwd(x, g, dy):
    r = lax.rsqrt(jnp.mean(x * x, axis=-1, keepdims=True) + RMS_EPS)
    xr = x * r
    dg = jnp.sum(dy * xr, axis=0, keepdims=True)
    u = dy * g
    dx = r * (u - xr * jnp.mean(u * xr, axis=-1, keepdims=True))
    return dx, dg


def _sigmoid(x):
    return 1.0 / (1.0 + jnp.exp(-x))


def _me():
    return 4 * lax.axis_index("x") + 2 * lax.axis_index("y") + lax.axis_index("c")


def _peer(k):
    px, py, pc = lax.axis_index("x") ^ ((k >> 2) & 1), lax.axis_index("y") ^ ((k >> 1) & 1), lax.axis_index("c") ^ (k & 1)
    return (px, py, pc), 4 * px + 2 * py + pc


def _stream_copies(stream, src_ref, dst_ref, send_sem, recv_sem, base):
    send, land = stream[2], stream[3]
    me = _me()
    outgoing, incoming = [], []
    for k in range(1, N_DEV):
        pos, p = _peer(k)
        common = dict(send_sem=send_sem.at[base + k - 1], recv_sem=recv_sem.at[base + k - 1], device_id=pos, device_id_type=pl.DeviceIdType.MESH)
        outgoing.append(pltpu.make_async_remote_copy(src_ref=send(src_ref, p), dst_ref=land(dst_ref, me), **common))
        incoming.append(pltpu.make_async_remote_copy(src_ref=send(src_ref, p), dst_ref=land(dst_ref, p), **common))
    return outgoing, incoming


_HBM = pl.BlockSpec(memory_space=pltpu.HBM)
_SEM = pl.BlockSpec(memory_space=pltpu.SEMAPHORE)


def _exchange(streams, outs, name):
    n, n_out = len(streams), len(outs)

    def body(*refs):
        srcs, dsts = refs[:n], refs[n:n + n_out]
        send_sem, recv_sem, local_sem = refs[n + n_out:]
        me = _me()
        own = [pltpu.make_async_copy(st[2](srcs[e], me), st[3](dsts[st[1]], me), local_sem.at[e]) for e, st in enumerate(streams)]
        for cp in own:
            cp.start()
        sends, recvs = [], []
        for e, st in enumerate(streams):
            outgoing, incoming = _stream_copies(st, srcs[e], dsts[st[1]], send_sem, recv_sem, e * (N_DEV - 1))
            sends += outgoing
            recvs += incoming
        for cp in sends:
            cp.start()
        for cp in recvs:
            cp.wait_recv()
        for cp in sends:
            cp.wait_send()
        for cp in own:
            cp.wait()

    return pl.pallas_call(
        body,
        name=name,
        out_shape=tuple(outs),
        in_specs=[_HBM] * n,
        out_specs=tuple([_HBM] * n_out),
        scratch_shapes=[pltpu.SemaphoreType.DMA((n * (N_DEV - 1),)), pltpu.SemaphoreType.DMA((n * (N_DEV - 1),)),
                        pltpu.SemaphoreType.DMA((n,))],
        compiler_params=pltpu.CompilerParams(vmem_limit_bytes=VMEM_LIMIT),
    )(*[st[0] for st in streams])


def _landing(streams, outs):
    me = _me()
    lands = [lax.empty(o.shape, o.dtype) for o in outs]
    for st in streams:
        lands[st[1]] = st[4](lands[st[1]], st[0], me)
    return lands


def _exchange_start(streams, lands, groups, after, name):
    n, n_land, n_grp = len(streams), len(lands), len(groups)
    side_effects = pltpu.SideEffectType.DATAFLOW_SIDE_EFFECTING

    def body(*refs):
        srcs, dsts = refs[:n], refs[n:n + n_land]
        outs = refs[n + n_land + len(after):]
        send_sems, recv_sems, token = outs[:n_grp], outs[n_grp:2 * n_grp], outs[-1]
        for gi, group in enumerate(groups):
            for j, e in enumerate(group):
                outgoing, _ = _stream_copies(streams[e], srcs[e], dsts[streams[e][1]], send_sems[gi], recv_sems[gi], j * (N_DEV - 1))
                for cp in outgoing:
                    cp.start()
        token[...] = jnp.zeros_like(token)

    sems = tuple(pltpu.SemaphoreType.DMA((len(g) * (N_DEV - 1),)) for g in groups)
    passed = [st[0] for st in streams] + list(lands)
    res = pl.pallas_call(
        body,
        name=name,
        out_shape=sems + sems + tuple(pltpu.HBM(a.shape, a.dtype) for a in passed) + (jax.ShapeDtypeStruct((8, 128), F32),),
        in_specs=[_HBM] * (n + n_land) + [pl.BlockSpec(memory_space=pl.ANY)] * len(after),
        out_specs=tuple([_SEM] * (2 * n_grp) + [_HBM] * (n + n_land) + [pl.BlockSpec(memory_space=pltpu.VMEM)]),
        input_output_aliases={i: 2 * n_grp + i for i in range(n + n_land)},
        compiler_params=pltpu.CompilerParams(has_side_effects=side_effects),
    )(*passed, *after)
    handle = dict(streams=streams, groups=groups, send=res[:n_grp], recv=res[n_grp:2 * n_grp],
                  srcs=res[2 * n_grp:2 * n_grp + n], lands=res[2 * n_grp + n:2 * n_grp + n + n_land])
    return handle, res[-1]


def _exchange_wait(handle, gi, after, name):
    streams, group = handle["streams"], handle["groups"][gi]
    land_ids = sorted({streams[e][1] for e in group})
    srcs = [handle["srcs"][e] for e in group]
    lands = [handle["lands"][i] for i in land_ids]
    n, n_land = len(srcs), len(lands)
    side_effects = pltpu.SideEffectType.DATAFLOW_SIDE_EFFECTING

    def body(*refs):
        src_refs, land_refs = refs[:n], refs[n:n + n_land]
        send_sem, recv_sem = refs[n + n_land:n + n_land + 2]
        for j, e in enumerate(group):
            outgoing, incoming = _stream_copies(streams[e], src_refs[j], land_refs[land_ids.index(streams[e][1])], send_sem, recv_sem, j * (N_DEV - 1))
            for cp in outgoing:
                cp.wait_send()
            for cp in incoming:
                cp.wait_recv()

    res = pl.pallas_call(
        body,
        name=name,
        out_shape=tuple(pltpu.HBM(a.shape, a.dtype) for a in lands),
        in_specs=[_HBM] * (n + n_land) + [_SEM, _SEM] + [pl.BlockSpec(memory_space=pl.ANY)] * len(after),
        out_specs=tuple([_HBM] * n_land),
        input_output_aliases={n + i: i for i in range(n_land)},
        compiler_params=pltpu.CompilerParams(has_side_effects=side_effects),
    )(*srcs, *lands, handle["send"][gi], handle["recv"][gi], *after)
    return list(res)


def _whole(ref, _):
    return ref


def _slot(ref, q):
    return ref.at[q]


def _lane_block(width):
    def pick(ref, q):
        idx = (slice(None),) * (len(ref.shape) - 1) + (pl.ds(pl.multiple_of(q * width, 128), width),)
        return ref.at[idx]
    return pick


def _row_block(height):
    def pick(ref, q):
        idx = (slice(None),) * (len(ref.shape) - 2) + (pl.ds(pl.multiple_of(q * height, 16), height), slice(None))
        return ref.at[idx]
    return pick


def _block_of(axis_from_end, size):
    return _lane_block(size) if axis_from_end == 1 else _row_block(size)


def _gather_streams(shards):
    streams, outs = [], []
    for i, (shard, axis) in enumerate(shards):
        shape = list(shard.shape)
        if axis is None:
            shape, land = [N_DEV] + shape, _slot
            place = _place_slot
        else:
            size = shape[-axis]
            land = _block_of(axis, size)
            shape[-axis] *= N_DEV
            place = lambda dst, src, me, size=size, axis=axis: lax.dynamic_update_slice_in_dim(dst, src, me * size, dst.ndim - axis)
        streams.append((shard, i, _whole, land, place))
        outs.append(jax.ShapeDtypeStruct(tuple(shape), shard.dtype))
    return streams, outs


def _place_slot(dst, src, me):
    return lax.dynamic_update_index_in_dim(dst, src, me, 0)


def _scatter_streams(fulls):
    streams, outs = [], []
    for i, (full, axis) in enumerate(fulls):
        shape = list(full.shape)
        size = shape[-axis] // N_DEV
        shape[-axis] = size
        place = lambda dst, src, me, size=size, axis=axis: _place_slot(
            dst, lax.dynamic_slice_in_dim(src, me * size, size, src.ndim - axis), me)
        streams.append((full, i, _block_of(axis, size), _slot, place))
        outs.append(jax.ShapeDtypeStruct((N_DEV,) + tuple(shape), full.dtype))
    return streams, outs


FFN_TM = 512
FFN_TF = 1024
FFN_FWD_TM = 1024
FFN_FWD_TF = 768


def _ffn_fwd(h, g, w13, w2, after=()):
    t = h.shape[0]
    tm, tf = min(FFN_FWD_TM, t), FFN_FWD_TF
    nf = D_FF_PAD // tf

    def body(h_ref, g_ref, w1_ref, w3_ref, w2_ref, *rest):
        ho_ref, n_ref, a_ref, b_ref, n_scr, acc = rest[len(after):]
        f = pl.program_id(1)

        @pl.when(f == 0)
        def _():
            n_scr[...] = _rms(h_ref[...], g_ref[...]).astype(MXU_DTYPE)
            n_ref[...] = n_scr[...]
            acc[...] = jnp.zeros_like(acc)

        n = n_scr[...]
        a = _dot(n, w1_ref[...])
        b = _dot(n, w3_ref[...])
        a_ref[...] = a.astype(MXU_DTYPE)
        b_ref[...] = b.astype(MXU_DTYPE)
        s = (a * _sigmoid(a)) * b
        acc[...] += _dot(s.astype(MXU_DTYPE), w2_ref[...])

        @pl.when(f == nf - 1)
        def _():
            ho_ref[...] = h_ref[...] + FFN_RES_WEIGHT * acc[...]

    return pl.pallas_call(
        body,
        name="ffn_fwd",
        grid=(t // tm, nf),
        in_specs=[
            _rows(tm, D_MODEL),
            _resident((1, D_MODEL)),
            pl.BlockSpec((None, D_MODEL, tf), lambda i, f: (0, 0, f)),
            pl.BlockSpec((None, D_MODEL, tf), lambda i, f: (1, 0, f)),
            pl.BlockSpec((tf, D_MODEL), lambda i, f: (f, 0)),
        ] + [pl.BlockSpec(memory_space=pl.ANY)] * len(after),
        out_specs=[
            _rows(tm, D_MODEL),
            _rows(tm, D_MODEL),
            pl.BlockSpec((tm, tf), lambda i, f: (i, f)),
            pl.BlockSpec((tm, tf), lambda i, f: (i, f)),
        ],
        out_shape=[
            jax.ShapeDtypeStruct((t, D_MODEL), F32),
            jax.ShapeDtypeStruct((t, D_MODEL), MXU_DTYPE),
            jax.ShapeDtypeStruct((t, D_FF_PAD), MXU_DTYPE),
            jax.ShapeDtypeStruct((t, D_FF_PAD), MXU_DTYPE),
        ],
        scratch_shapes=[pltpu.VMEM((tm, D_MODEL), MXU_DTYPE), pltpu.VMEM((tm, D_MODEL), F32)],
        compiler_params=_cparams("parallel", "arbitrary"),
    )(h, g, w13, w13, w2, *after)


def _ffn_bwd(dout, h, g, a, b, w13, w2, after=()):
    t = h.shape[0]
    tm, tf = min(FFN_TM, t), FFN_TF
    nf = D_FF_PAD // tf

    def body(do_ref, h_ref, g_ref, a_ref, b_ref, w1_ref, w3_ref, w2_ref, *rest):
        dh_ref, dob_ref, s_ref, dab_ref, dg_ref, dob_scr, acc = rest[len(after):]
        i, f = pl.program_id(0), pl.program_id(1)

        @pl.when(f == 0)
        def _():
            dob_scr[...] = (FFN_RES_WEIGHT * do_ref[...]).astype(MXU_DTYPE)
            dob_ref[...] = dob_scr[...]
            acc[...] = jnp.zeros_like(acc)

        @pl.when((f == 0) & (i == 0))
        def _():
            dg_ref[...] = jnp.zeros_like(dg_ref)

        ds = _dot_nt(dob_scr[...], w2_ref[...])
        av = a_ref[...].astype(F32)
        bv = b_ref[...].astype(F32)
        sig = _sigmoid(av)
        sil = av * sig
        s_ref[...] = (sil * bv).astype(MXU_DTYPE)
        da = (ds * bv * (sig * (1.0 + av * (1.0 - sig)))).astype(MXU_DTYPE)
        db = (ds * sil).astype(MXU_DTYPE)
        dab_ref[0] = da
        dab_ref[1] = db
        acc[...] += _dot_nt(da, w1_ref[...]) + _dot_nt(db, w3_ref[...])

        @pl.when(f == nf - 1)
        def _():
            dx, dg = _rms_bwd(h_ref[...], g_ref[...], acc[...])
            dh_ref[...] = do_ref[...] + dx
            dg_ref[...] += dg

    return pl.pallas_call(
        body,
        name="ffn_bwd",
        grid=(t // tm, nf),
        in_specs=[
            _rows(tm, D_MODEL),
            _rows(tm, D_MODEL),
            _resident((1, D_MODEL)),
            pl.BlockSpec((tm, tf), lambda i, f: (i, f)),
            pl.BlockSpec((tm, tf), lambda i, f: (i, f)),
            pl.BlockSpec((None, D_MODEL, tf), lambda i, f: (0, 0, f)),
            pl.BlockSpec((None, D_MODEL, tf), lambda i, f: (1, 0, f)),
            pl.BlockSpec((tf, D_MODEL), lambda i, f: (f, 0)),
        ] + [pl.BlockSpec(memory_space=pl.ANY)] * len(after),
        out_specs=[
            _rows(tm, D_MODEL),
            _rows(tm, D_MODEL),
            pl.BlockSpec((tm, tf), lambda i, f: (i, f)),
            pl.BlockSpec((2, tm, tf), lambda i, f: (0, i, f)),
            pl.BlockSpec((1, D_MODEL), lambda i, f: (0, 0)),
        ],
        out_shape=[
            jax.ShapeDtypeStruct((t, D_MODEL), F32),
            jax.ShapeDtypeStruct((t, D_MODEL), MXU_DTYPE),
            jax.ShapeDtypeStruct((t, D_FF_PAD), MXU_DTYPE),
            jax.ShapeDtypeStruct((2, t, D_FF_PAD), MXU_DTYPE),
            jax.ShapeDtypeStruct((1, D_MODEL), F32),
        ],
        scratch_shapes=[pltpu.VMEM((tm, D_MODEL), MXU_DTYPE), pltpu.VMEM((tm, D_MODEL), F32)],
        compiler_params=_cparams("arbitrary", "arbitrary"),
    )(dout, h, g, a, b, w13, w13, w2, *after)


def _wgrad(x, y, name, out_dtype=WIRE_DTYPE):
    t, m = x.shape
    grouped = y.ndim == 3
    groups = y.shape[0] if grouped else 1
    n = y.shape[-1]
    tk = min(1024, t)
    bm = m if m <= 1536 else m // 2
    bn = n if n <= 1536 else n // 2
    nk = t // tk

    def body(x_ref, y_ref, o_ref, acc):
        k = pl.program_id(3)

        @pl.when(k == 0)
        def _():
            acc[...] = jnp.zeros_like(acc)

        acc[...] += _dot_tn(x_ref[...].astype(MXU_DTYPE), y_ref[...].astype(MXU_DTYPE))

        @pl.when(k == nk - 1)
        def _():
            o_ref[...] = acc[...].astype(out_dtype)

    if grouped:
        y_spec = pl.BlockSpec((None, tk, bn), lambda g, i, j, k: (g, k, j))
        o_spec = pl.BlockSpec((None, bm, bn), lambda g, i, j, k: (g, i, j))
        o_shape = jax.ShapeDtypeStruct((groups, m, n), out_dtype)
    else:
        y_spec = pl.BlockSpec((tk, bn), lambda g, i, j, k: (k, j))
        o_spec = pl.BlockSpec((bm, bn), lambda g, i, j, k: (i, j))
        o_shape = jax.ShapeDtypeStruct((m, n), out_dtype)
    return pl.pallas_call(
        body,
        name=name,
        grid=(groups, m // bm, n // bn, nk),
        in_specs=[pl.BlockSpec((tk, bm), lambda g, i, j, k: (k, i)), y_spec],
        out_specs=o_spec,
        out_shape=o_shape,
        scratch_shapes=[pltpu.VMEM((bm, bn), F32)],
        compiler_params=_cparams("parallel", "parallel", "parallel", "arbitrary"),
    )(x, y)


def _conv_pre(h, g, w_pw1):
    t = h.shape[0]
    tm = min(512, t)

    def body(h_ref, g_ref, w_ref, n_ref, u_ref, glu_ref):
        n = _rms(h_ref[...], g_ref[...]).astype(MXU_DTYPE)
        n_ref[...] = n
        u = _dot(n, w_ref[...])
        u_ref[...] = u
        glu_ref[...] = u[:, :D_MODEL] * _sigmoid(u[:, D_MODEL:])

    return pl.pallas_call(
        body,
        name="conv_pre",
        grid=(t // tm,),
        in_specs=[_rows(tm, D_MODEL), _resident((1, D_MODEL)), _resident((D_MODEL, 2 * D_MODEL))],
        out_specs=[_rows(tm, D_MODEL), _rows(tm, 2 * D_MODEL), _rows(tm, D_MODEL)],
        out_shape=[
            jax.ShapeDtypeStruct((t, D_MODEL), MXU_DTYPE),
            jax.ShapeDtypeStruct((t, 2 * D_MODEL), F32),
            jax.ShapeDtypeStruct((t, D_MODEL), F32),
        ],
        compiler_params=_cparams("parallel"),
    )(h, g, w_pw1)


def _shifted_copies(dst, src, rows):
    for r in range(1, 8):
        dst[r - 1] = src[pl.ds(r, rows), :]


def _window(src, shifted, offset, base, rows):
    r = offset % 8
    start = base + offset - r
    return src[pl.ds(start, rows), :] if r == 0 else shifted[r - 1, pl.ds(start, rows), :]


def _conv_main(glu, w_dw, g, w_pw2, h):
    t = h.shape[0]
    tm = min(512, t)
    per = tm // CONV_HALO
    shifted_rows = tm + CONV_HALO - 8

    def body(glu_ref, halo_ref, w_ref, g_ref, w2_ref, h_ref, ho_ref, c_ref, sw_ref, ext, ext_sh):
        i = pl.program_id(0)
        ext[pl.ds(0, CONV_HALO), :] = jnp.where(i == 0, 0.0, halo_ref[...])
        ext[pl.ds(CONV_HALO, tm), :] = glu_ref[...]
        _shifted_copies(ext_sh, ext, shifted_rows)
        for base in range(0, tm, CONV_ROWS):
            acc = jnp.zeros((CONV_ROWS, D_MODEL), F32)
            for k in range(CONV_WIDTH):
                acc = acc + _window(ext, ext_sh, CONV_HALO - (CONV_WIDTH - 1) + k, base, CONV_ROWS) * w_ref[pl.ds(k, 1), :]
            c_ref[pl.ds(base, CONV_ROWS), :] = acc
        y = _rms(c_ref[...], g_ref[...])
        sw = (y * _sigmoid(y)).astype(MXU_DTYPE)
        sw_ref[...] = sw
        ho_ref[...] = h_ref[...] + _dot(sw, w2_ref[...])

    return pl.pallas_call(
        body,
        name="conv_main",
        grid=(t // tm,),
        in_specs=[
            _rows(tm, D_MODEL),
            pl.BlockSpec((CONV_HALO, D_MODEL), lambda i: (jnp.maximum(i * per - 1, 0), 0)),
            _resident((CONV_WIDTH, D_MODEL)),
            _resident((1, D_MODEL)),
            _resident((D_MODEL, D_MODEL)),
            _rows(tm, D_MODEL),
        ],
        out_specs=[_rows(tm, D_MODEL), _rows(tm, D_MODEL), _rows(tm, D_MODEL)],
        out_shape=[
            jax.ShapeDtypeStruct((t, D_MODEL), F32),
            jax.ShapeDtypeStruct((t, D_MODEL), F32),
            jax.ShapeDtypeStruct((t, D_MODEL), MXU_DTYPE),
        ],
        scratch_shapes=[pltpu.VMEM((tm + CONV_HALO, D_MODEL), F32), pltpu.VMEM((7, shifted_rows, D_MODEL), F32)],
        compiler_params=_cparams("parallel"),
    )(glu, glu, w_dw, g, w_pw2, h)


def _conv_bwd_post(dout, c, g, w_pw2):
    t = dout.shape[0]
    tm = min(512, t)

    def body(do_ref, c_ref, g_ref, w2_ref, dc_ref, dg_ref):
        @pl.when(pl.program_id(0) == 0)
        def _():
            dg_ref[...] = jnp.zeros_like(dg_ref)

        dsw = _dot_nt(do_ref[...].astype(MXU_DTYPE), w2_ref[...])
        cv = c_ref[...]
        y = _rms(cv, g_ref[...])
        sig = _sigmoid(y)
        dy = dsw * (sig * (1.0 + y * (1.0 - sig)))
        dc, dg = _rms_bwd(cv, g_ref[...], dy)
        dc_ref[...] = dc
        dg_ref[...] += dg

    return pl.pallas_call(
        body,
        name="conv_bwd_post",
        grid=(t // tm,),
        in_specs=[_rows(tm, D_MODEL), _rows(tm, D_MODEL), _resident((1, D_MODEL)), _resident((D_MODEL, D_MODEL))],
        out_specs=[_rows(tm, D_MODEL), pl.BlockSpec((1, D_MODEL), lambda i: (0, 0))],
        out_shape=[jax.ShapeDtypeStruct((t, D_MODEL), F32), jax.ShapeDtypeStruct((1, D_MODEL), F32)],
        compiler_params=_cparams("arbitrary"),
    )(dout, c, g, w_pw2)


def _conv_bwd_pre(dc, glu, u, w_dw, w_pw1, h, g, dout):
    t = h.shape[0]
    tm = min(256, t)
    per = tm // CONV_HALO
    nt = t // tm
    last_halo = t // CONV_HALO - 1
    shifted_rows = tm + CONV_HALO - 8

    def body(dc_ref, dcn_ref, glu_ref, gluh_ref, u_ref, w_ref, w1_ref, h_ref, g_ref, do_ref,
             dh_ref, du_ref, dw_ref, dg_ref, dext, gext, dext_sh, gext_sh):
        i = pl.program_id(0)

        @pl.when(i == 0)
        def _():
            dw_ref[...] = jnp.zeros_like(dw_ref)
            dg_ref[...] = jnp.zeros_like(dg_ref)

        dext[pl.ds(0, tm), :] = dc_ref[...]
        dext[pl.ds(tm, CONV_HALO), :] = jnp.where(i == nt - 1, 0.0, dcn_ref[...])
        gext[pl.ds(0, CONV_HALO), :] = jnp.where(i == 0, 0.0, gluh_ref[...])
        gext[pl.ds(CONV_HALO, tm), :] = glu_ref[...]
        _shifted_copies(dext_sh, dext, shifted_rows)
        _shifted_copies(gext_sh, gext, shifted_rows)
        for base in range(0, tm, CONV_ROWS):
            rows = pl.ds(base, CONV_ROWS)
            dglu = jnp.zeros((CONV_ROWS, D_MODEL), F32)
            for k in range(CONV_WIDTH):
                dglu = dglu + _window(dext, dext_sh, CONV_WIDTH - 1 - k, base, CONV_ROWS) * w_ref[pl.ds(k, 1), :]
            av, bv = u_ref[rows, :D_MODEL], u_ref[rows, D_MODEL:]
            sig = _sigmoid(bv)
            du_ref[rows, :D_MODEL] = (dglu * sig).astype(MXU_DTYPE)
            du_ref[rows, D_MODEL:] = (dglu * av * (sig * (1.0 - sig))).astype(MXU_DTYPE)
        for k in range(CONV_WIDTH):
            part = jnp.zeros((CONV_ROWS, D_MODEL), F32)
            for base in range(0, tm, CONV_ROWS):
                part = part + dc_ref[pl.ds(base, CONV_ROWS), :] * _window(gext, gext_sh, CONV_HALO - (CONV_WIDTH - 1) + k, base, CONV_ROWS)
            dw_ref[pl.ds(k, 1), :] += jnp.sum(part, axis=0, keepdims=True)
        dn = _dot_nt(du_ref[...], w1_ref[...])
        dx, dg = _rms_bwd(h_ref[...], g_ref[...], dn)
        dh_ref[...] = do_ref[...] + dx
        dg_ref[...] += dg

    return pl.pallas_call(
        body,
        name="conv_bwd_pre",
        grid=(nt,),
        in_specs=[
            _rows(tm, D_MODEL),
            pl.BlockSpec((CONV_HALO, D_MODEL), lambda i: (jnp.minimum((i + 1) * per, last_halo), 0)),
            _rows(tm, D_MODEL),
            pl.BlockSpec((CONV_HALO, D_MODEL), lambda i: (jnp.maximum(i * per - 1, 0), 0)),
            _rows(tm, 2 * D_MODEL),
            _resident((CONV_WIDTH, D_MODEL)),
            _resident((D_MODEL, 2 * D_MODEL)),
            _rows(tm, D_MODEL),
            _resident((1, D_MODEL)),
            _rows(tm, D_MODEL),
        ],
        out_specs=[
            _rows(tm, D_MODEL),
            _rows(tm, 2 * D_MODEL),
            pl.BlockSpec((CONV_HALO, D_MODEL), lambda i: (0, 0)),
            pl.BlockSpec((1, D_MODEL), lambda i: (0, 0)),
        ],
        out_shape=[
            jax.ShapeDtypeStruct((t, D_MODEL), F32),
            jax.ShapeDtypeStruct((t, 2 * D_MODEL), MXU_DTYPE),
            jax.ShapeDtypeStruct((CONV_HALO, D_MODEL), F32),
            jax.ShapeDtypeStruct((1, D_MODEL), F32),
        ],
        scratch_shapes=[pltpu.VMEM((tm + CONV_HALO, D_MODEL), F32), pltpu.VMEM((tm + CONV_HALO, D_MODEL), F32),
                        pltpu.VMEM((7, shifted_rows, D_MODEL), F32), pltpu.VMEM((7, shifted_rows, D_MODEL), F32)],
        compiler_params=_cparams("arbitrary"),
    )(dc, dc, glu, glu, u, w_dw, w_pw1, h, g, dout)


def _rope(s, cs, sa, sb):
    return s * cs - pltpu.roll(s, 96, 1) * sa + pltpu.roll(s, 32, 1) * sb


def _unrope(d, cs, sa, sb):
    return d * cs + pltpu.roll(d, 96, 1) * sa - pltpu.roll(d, 32, 1) * sb


def _mla_pre(h, g, w_a, g_q, g_kv, w_uq, w_ukv, cs, sa, sb):
    t = h.shape[0]
    tm = min(512, t)
    qw = N_HEADS * HEAD_PAD
    vw = N_HEADS * V_HEAD

    def body(h_ref, g_ref, wa_ref, gq_ref, gkv_ref, wuq_ref, wukv_ref, cs_ref, sa_ref, sb_ref,
             n_ref, a_ref, cq_ref, ckv_ref, q_ref, k_ref, v_ref):
        n = _rms(h_ref[...], g_ref[...]).astype(MXU_DTYPE)
        n_ref[...] = n
        a = _dot(n, wa_ref[...])
        a_ref[...] = a
        cq = _rms(a[:, :Q_LORA], gq_ref[...]).astype(MXU_DTYPE)
        ckv = _rms(a[:, Q_LORA:Q_LORA + KV_LORA], gkv_ref[...]).astype(MXU_DTYPE)
        cq_ref[...] = cq
        ckv_ref[...] = ckv
        q = _dot(cq, wuq_ref[...]) * ATTN_SCALE
        kv = _dot(ckv, wukv_ref[...])
        cs_, sa_, sb_ = cs_ref[...], sa_ref[...], sb_ref[...]
        k_rot = _rope(a[:, Q_LORA + KV_LORA:], cs_, sa_, sb_).astype(MXU_DTYPE)
        for hd in range(N_HEADS):
            lo = hd * HEAD_PAD
            q_ref[:, lo:lo + QK_NOPE] = q[:, lo:lo + QK_NOPE].astype(MXU_DTYPE)
            q_ref[:, lo + QK_NOPE:lo + HEAD_PAD] = _rope(q[:, lo + QK_NOPE:lo + HEAD_PAD], cs_, sa_, sb_).astype(MXU_DTYPE)
            k_ref[:, lo:lo + QK_NOPE] = kv[:, lo:lo + QK_NOPE].astype(MXU_DTYPE)
            k_ref[:, lo + QK_NOPE:lo + HEAD_PAD] = k_rot
            v_ref[:, hd * V_HEAD:(hd + 1) * V_HEAD] = kv[:, lo + QK_NOPE:lo + HEAD_PAD].astype(MXU_DTYPE)

    return pl.pallas_call(
        body,
        name="mla_pre",
        grid=(t // tm,),
        in_specs=[
            _rows(tm, D_MODEL),
            _resident((1, D_MODEL)),
            _resident((D_MODEL, A_PAD)),
            _resident((1, Q_LORA)),
            _resident((1, KV_LORA)),
            _resident((Q_LORA, qw)),
            _resident((KV_LORA, qw)),
            _rows(tm, 128),
            _rows(tm, 128),
            _rows(tm, 128),
        ],
        out_specs=[_rows(tm, D_MODEL), _rows(tm, A_PAD), _rows(tm, Q_LORA), _rows(tm, KV_LORA), _rows(tm, qw), _rows(tm, qw), _rows(tm, vw)],
        out_shape=[
            jax.ShapeDtypeStruct((t, D_MODEL), MXU_DTYPE),
            jax.ShapeDtypeStruct((t, A_PAD), F32),
            jax.ShapeDtypeStruct((t, Q_LORA), MXU_DTYPE),
            jax.ShapeDtypeStruct((t, KV_LORA), MXU_DTYPE),
            jax.ShapeDtypeStruct((t, qw), MXU_DTYPE),
            jax.ShapeDtypeStruct((t, qw), MXU_DTYPE),
            jax.ShapeDtypeStruct((t, vw), MXU_DTYPE),
        ],
        compiler_params=_cparams("parallel"),
    )(h, g, w_a, g_q, g_kv, w_uq, w_ukv, cs, sa, sb)


def _attn_tile(t):
    return min(1024, t)


def _chunk_mask(rows, cols):
    r = lax.broadcasted_iota(jnp.int32, (rows, cols), 0)
    c = lax.broadcasted_iota(jnp.int32, (rows, cols), 1)
    return (r >> CHUNK_SHIFT) >= (c >> CHUNK_SHIFT)


def _causal_pairs(n, by_column=False):
    if by_column:
        pairs = [(i, j) for j in range(n) for i in range(j, n)]
    else:
        pairs = [(i, j) for i in range(n) for j in range(i + 1)]
    return jnp.asarray([p[0] for p in pairs], jnp.int32), jnp.asarray([p[1] for p in pairs], jnp.int32)


def _flash_fwd(q, k, v):
    t = q.shape[0]
    tq = tk = _attn_tile(t)
    rows_of, cols_of = _causal_pairs(t // tq)

    def body(qi_ref, ki_ref, q_ref, k_ref, v_ref, o_ref, lse_ref, m_s, l_s, acc):
        step = pl.program_id(1)
        qi, ki = qi_ref[step], ki_ref[step]

        @pl.when(ki == 0)
        def _():
            m_s[...] = jnp.full_like(m_s, -jnp.inf)
            l_s[...] = jnp.zeros_like(l_s)
            acc[...] = jnp.zeros_like(acc)

        def block(on_diagonal):
            s = _dot_nt(q_ref[...], k_ref[...])
            if on_diagonal:
                s = jnp.where(_chunk_mask(tq, tk), s, -jnp.inf)
            m_prev = m_s[...]
            m_new = jnp.maximum(m_prev, jnp.max(s, axis=1, keepdims=True))
            alpha = jnp.exp(m_prev - m_new)
            p = jnp.exp(s - m_new)
            l_s[...] = alpha * l_s[...] + jnp.sum(p, axis=1, keepdims=True)
            acc[...] = alpha * acc[...] + _dot(p.astype(MXU_DTYPE), v_ref[...])
            m_s[...] = m_new

        pl.when(ki < qi)(lambda: block(False))

        @pl.when(ki == qi)
        def _():
            block(True)
            o_ref[...] = acc[...] / l_s[...]
            lse_ref[...] = jnp.broadcast_to(m_s[...] + jnp.log(l_s[...]), (tq, V_HEAD))

    grid_spec = pltpu.PrefetchScalarGridSpec(
        num_scalar_prefetch=2,
        grid=(N_HEADS, rows_of.shape[0]),
        in_specs=[
            pl.BlockSpec((tq, HEAD_PAD), lambda h, s, qi, ki: (qi[s], h)),
            pl.BlockSpec((tk, HEAD_PAD), lambda h, s, qi, ki: (ki[s], h)),
            pl.BlockSpec((tk, V_HEAD), lambda h, s, qi, ki: (ki[s], h)),
        ],
        out_specs=[pl.BlockSpec((tq, V_HEAD), lambda h, s, qi, ki: (qi[s], h)), pl.BlockSpec((tq, V_HEAD), lambda h, s, qi, ki: (qi[s], h))],
        scratch_shapes=[pltpu.VMEM((tq, 1), F32), pltpu.VMEM((tq, 1), F32), pltpu.VMEM((tq, V_HEAD), F32)],
    )
    return pl.pallas_call(
        body,
        name="flash_fwd",
        grid_spec=grid_spec,
        out_shape=[jax.ShapeDtypeStruct((t, N_HEADS * V_HEAD), F32), jax.ShapeDtypeStruct((t, N_HEADS * V_HEAD), F32)],
        compiler_params=_cparams("parallel", "arbitrary"),
    )(rows_of, cols_of, q, k, v)


def _flash_bwd(q, k, v, do, lse, delta):
    t = q.shape[0]
    tq = tk = _attn_tile(t)
    nq = t // tq
    rows_of, cols_of = _causal_pairs(nq, by_column=True)

    def body(qi_ref, kj_ref, q_ref, k_ref, v_ref, do_ref, lse_ref, dl_ref, dq_ref, dk_ref, dv_ref, dk_acc, dv_acc):
        step = pl.program_id(1)
        qi, kj = qi_ref[step], kj_ref[step]

        @pl.when(step == 0)
        def _():
            dq_ref[...] = jnp.zeros_like(dq_ref)

        def block(on_diagonal):
            qv, kv, dov = q_ref[...], k_ref[...], do_ref[...]
            s = _dot_nt(qv, kv)
            if on_diagonal:
                s = jnp.where(_chunk_mask(tq, tk), s, -jnp.inf)
            p = jnp.exp(s - lse_ref[:, :1])
            dp = _dot_nt(dov, v_ref[...])
            ds = (p * (dp - dl_ref[:, :1])).astype(MXU_DTYPE)
            dv_new = _dot_tn(p.astype(MXU_DTYPE), dov)
            dk_new = _dot_tn(ds, qv)
            if on_diagonal:
                dv_acc[...] = dv_new
                dk_acc[...] = dk_new
            else:
                dv_acc[...] += dv_new
                dk_acc[...] += dk_new
            rows = pl.ds(pl.multiple_of(qi * tq, tq), tq)
            dq_ref[rows, :] += _dot(ds, kv)

        pl.when(qi > kj)(lambda: block(False))
        pl.when(qi == kj)(lambda: block(True))

        @pl.when(qi == nq - 1)
        def _():
            dk_ref[...] = dk_acc[...]
            dv_ref[...] = dv_acc[...]

    grid_spec = pltpu.PrefetchScalarGridSpec(
        num_scalar_prefetch=2,
        grid=(N_HEADS, rows_of.shape[0]),
        in_specs=[
            pl.BlockSpec((tq, HEAD_PAD), lambda h, s, qi, kj: (qi[s], h)),
            pl.BlockSpec((tk, HEAD_PAD), lambda h, s, qi, kj: (kj[s], h)),
            pl.BlockSpec((tk, V_HEAD), lambda h, s, qi, kj: (kj[s], h)),
            pl.BlockSpec((tq, V_HEAD), lambda h, s, qi, kj: (qi[s], h)),
            pl.BlockSpec((tq, V_HEAD), lambda h, s, qi, kj: (qi[s], h)),
            pl.BlockSpec((tq, V_HEAD), lambda h, s, qi, kj: (qi[s], h)),
        ],
        out_specs=[
            pl.BlockSpec((t, HEAD_PAD), lambda h, s, qi, kj: (0, h)),
            pl.BlockSpec((tk, HEAD_PAD), lambda h, s, qi, kj: (kj[s], h)),
            pl.BlockSpec((tk, V_HEAD), lambda h, s, qi, kj: (kj[s], h)),
        ],
        scratch_shapes=[pltpu.VMEM((tk, HEAD_PAD), F32), pltpu.VMEM((tk, V_HEAD), F32)],
    )
    return pl.pallas_call(
        body,
        name="flash_bwd",
        grid_spec=grid_spec,
        out_shape=[
            jax.ShapeDtypeStruct((t, N_HEADS * HEAD_PAD), F32),
            jax.ShapeDtypeStruct((t, N_HEADS * HEAD_PAD), F32),
            jax.ShapeDtypeStruct((t, N_HEADS * V_HEAD), F32),
        ],
        compiler_params=_cparams("arbitrary", "arbitrary"),
    )(rows_of, cols_of, q, k, v, do, lse, delta)


def _attn_out(o, w_o, h):
    t = h.shape[0]
    tm = min(512, t)

    def body(o_ref, w_ref, h_ref, ho_ref):
        ho_ref[...] = h_ref[...] + _dot(o_ref[...].astype(MXU_DTYPE), w_ref[...])

    return pl.pallas_call(
        body,
        name="attn_out",
        grid=(t // tm,),
        in_specs=[_rows(tm, D_MODEL), _resident((D_MODEL, D_MODEL)), _rows(tm, D_MODEL)],
        out_specs=_rows(tm, D_MODEL),
        out_shape=jax.ShapeDtypeStruct((t, D_MODEL), F32),
        compiler_params=_cparams("parallel"),
    )(o, w_o, h)


def _attn_out_bwd(dout, o, w_o):
    t = dout.shape[0]
    tm = min(512, t)

    def body(d_ref, o_ref, w_ref, do_ref, dl_ref):
        do = _dot_nt(d_ref[...].astype(MXU_DTYPE), w_ref[...])
        do_ref[...] = do.astype(MXU_DTYPE)
        prod = do * o_ref[...]
        for hd in range(N_HEADS):
            lanes = slice(hd * V_HEAD, (hd + 1) * V_HEAD)
            dl_ref[:, lanes] = jnp.broadcast_to(jnp.sum(prod[:, lanes], axis=1, keepdims=True), (tm, V_HEAD))

    return pl.pallas_call(
        body,
        name="attn_out_bwd",
        grid=(t // tm,),
        in_specs=[_rows(tm, D_MODEL), _rows(tm, D_MODEL), _resident((D_MODEL, D_MODEL))],
        out_specs=[_rows(tm, D_MODEL), _rows(tm, D_MODEL)],
        out_shape=[jax.ShapeDtypeStruct((t, D_MODEL), MXU_DTYPE), jax.ShapeDtypeStruct((t, D_MODEL), F32)],
        compiler_params=_cparams("parallel"),
    )(dout, o, w_o)


def _mla_bwd_pre(dq, dk, dv, a, h, dout, g, g_q, g_kv, w_a, w_uq, w_ukv, cs, sa, sb):
    t = h.shape[0]
    tm = min(512, t)
    qw = N_HEADS * HEAD_PAD
    vw = N_HEADS * V_HEAD

    def body(dq_ref, dk_ref, dv_ref, a_ref, h_ref, do_ref, g_ref, gq_ref, gkv_ref, wa_ref, wuq_ref, wukv_ref,
             cs_ref, sa_ref, sb_ref, dh_ref, dqp_ref, dkv_ref, da_ref, dg_ref, dgq_ref, dgkv_ref):
        @pl.when(pl.program_id(0) == 0)
        def _():
            dg_ref[...] = jnp.zeros_like(dg_ref)
            dgq_ref[...] = jnp.zeros_like(dgq_ref)
            dgkv_ref[...] = jnp.zeros_like(dgkv_ref)

        cs_, sa_, sb_ = cs_ref[...], sa_ref[...], sb_ref[...]
        slab = jnp.zeros((tm, 128), F32)
        for hd in range(N_HEADS):
            lo = hd * HEAD_PAD
            dqp_ref[:, lo:lo + QK_NOPE] = (dq_ref[:, lo:lo + QK_NOPE] * ATTN_SCALE).astype(MXU_DTYPE)
            dqp_ref[:, lo + QK_NOPE:lo + HEAD_PAD] = _unrope(dq_ref[:, lo + QK_NOPE:lo + HEAD_PAD] * ATTN_SCALE, cs_, sa_, sb_).astype(MXU_DTYPE)
            dkv_ref[:, lo:lo + QK_NOPE] = dk_ref[:, lo:lo + QK_NOPE].astype(MXU_DTYPE)
            dkv_ref[:, lo + QK_NOPE:lo + HEAD_PAD] = dv_ref[:, hd * V_HEAD:(hd + 1) * V_HEAD].astype(MXU_DTYPE)
            slab = slab + dk_ref[:, lo + QK_NOPE:lo + HEAD_PAD]
        dcq = _dot_nt(dqp_ref[...], wuq_ref[...])
        dckv = _dot_nt(dkv_ref[...], wukv_ref[...])
        av = a_ref[...]
        daq, dgq = _rms_bwd(av[:, :Q_LORA], gq_ref[...], dcq)
        dakv, dgkv = _rms_bwd(av[:, Q_LORA:Q_LORA + KV_LORA], gkv_ref[...], dckv)
        da_ref[:, :Q_LORA] = daq.astype(MXU_DTYPE)
        da_ref[:, Q_LORA:Q_LORA + KV_LORA] = dakv.astype(MXU_DTYPE)
        da_ref[:, Q_LORA + KV_LORA:] = _unrope(slab, cs_, sa_, sb_).astype(MXU_DTYPE)
        dn = _dot_nt(da_ref[...], wa_ref[...])
        dx, dg = _rms_bwd(h_ref[...], g_ref[...], dn)
        dh_ref[...] = do_ref[...] + dx
        dg_ref[...] += dg
        dgq_ref[...] += dgq
        dgkv_ref[...] += dgkv

    def const(n):
        return pl.BlockSpec((1, n), lambda i: (0, 0))

    return pl.pallas_call(
        body,
        name="mla_bwd_pre",
        grid=(t // tm,),
        in_specs=[
            _rows(tm, qw), _rows(tm, qw), _rows(tm, vw), _rows(tm, A_PAD), _rows(tm, D_MODEL), _rows(tm, D_MODEL),
            _resident((1, D_MODEL)), _resident((1, Q_LORA)), _resident((1, KV_LORA)),
            _resident((D_MODEL, A_PAD)), _resident((Q_LORA, qw)), _resident((KV_LORA, qw)),
            _rows(tm, 128), _rows(tm, 128), _rows(tm, 128),
        ],
        out_specs=[_rows(tm, D_MODEL), _rows(tm, qw), _rows(tm, qw), _rows(tm, A_PAD),
                   const(D_MODEL), const(Q_LORA), const(KV_LORA)],
        out_shape=[
            jax.ShapeDtypeStruct((t, D_MODEL), F32),
            jax.ShapeDtypeStruct((t, qw), MXU_DTYPE),
            jax.ShapeDtypeStruct((t, qw), MXU_DTYPE),
            jax.ShapeDtypeStruct((t, A_PAD), MXU_DTYPE),
            jax.ShapeDtypeStruct((1, D_MODEL), F32),
            jax.ShapeDtypeStruct((1, Q_LORA), F32),
            jax.ShapeDtypeStruct((1, KV_LORA), F32),
        ],
        compiler_params=_cparams("arbitrary"),
    )(dq, dk, dv, a, h, dout, g, g_q, g_kv, w_a, w_uq, w_ukv, cs, sa, sb)


def _loss_head(h, g, target):
    t = h.shape[0]
    tm = min(512, t)

    def body(h_ref, g_ref, t_ref, sq_ref, dh_ref, dg_ref):
        @pl.when(pl.program_id(0) == 0)
        def _():
            sq_ref[...] = jnp.zeros_like(sq_ref)
            dg_ref[...] = jnp.zeros_like(dg_ref)

        x = h_ref[...]
        err = _rms(x, g_ref[...]) - t_ref[...]
        sq_ref[...] += jnp.sum(err * err, axis=0, keepdims=True)
        dx, dg = _rms_bwd(x, g_ref[...], err * (1.0 / D_MODEL))
        dh_ref[...] = dx
        dg_ref[...] += dg

    return pl.pallas_call(
        body,
        name="loss_head",
        grid=(t // tm,),
        in_specs=[_rows(tm, D_MODEL), _resident((1, D_MODEL)), _rows(tm, D_MODEL)],
        out_specs=[pl.BlockSpec((1, D_MODEL), lambda i: (0, 0)), _rows(tm, D_MODEL), pl.BlockSpec((1, D_MODEL), lambda i: (0, 0))],
        out_shape=[jax.ShapeDtypeStruct((1, D_MODEL), F32), jax.ShapeDtypeStruct((t, D_MODEL), F32), jax.ShapeDtypeStruct((1, D_MODEL), F32)],
        compiler_params=_cparams("arbitrary"),
    )(h, g, target)


def _adamw(parts, w, m, v, name, tr, layer=0, row_offset=0, into=None):
    layers, rows, width = w.shape
    pwidth = parts.shape[-1]
    off = row_offset // tr

    def body(p_ref, w_ref, m_ref, v_ref, *rest):
        g_ref, d_ref, mo_ref, vo_ref = rest[-4:]

        def part(q):
            return p_ref[q, :, pl.ds(0, width)].astype(F32)

        g = part(0)
        for q in range(1, N_DEV):
            g = g + part(q)
        g_ref[...] = g
        m_new = ADAM_B1 * m_ref[...] + (1.0 - ADAM_B1) * g
        v_new = ADAM_B2 * v_ref[...] + (1.0 - ADAM_B2) * (g * g)
        m_hat = m_new / (1.0 - ADAM_B1 ** ADAM_STEP)
        v_hat = v_new / (1.0 - ADAM_B2 ** ADAM_STEP)
        d_ref[...] = -ADAM_LR * (m_hat / (jnp.sqrt(v_hat) + ADAM_EPS) + ADAM_WD * w_ref[...])
        mo_ref[...] = m_new
        vo_ref[...] = v_new

    blk = pl.BlockSpec((None, tr, width), lambda i: (layer, i, 0))
    earlier = () if into is None else tuple(into)
    return pl.pallas_call(
        body,
        name=name,
        grid=(rows // tr,),
        in_specs=[pl.BlockSpec((N_DEV, tr, pwidth), lambda i: (0, off + i, 0)), blk, blk, blk] + [pl.BlockSpec(memory_space=pl.ANY)] * len(earlier),
        out_specs=[blk, blk, blk, blk],
        out_shape=[jax.ShapeDtypeStruct((layers, rows, width), F32)] * 4,
        input_output_aliases={4 + j: j for j in range(len(earlier))},
        compiler_params=_cparams("parallel"),
    )(parts, w, m, v, *earlier)


_GAINS = ("ffn_norm1", "mix_norm", "ffn_norm2", "conv_norm", "final_norm")
_TINY = ("conv_w_dw", "mla_q_norm", "mla_kv_norm")


def _small_pack(parts):
    flat = jnp.concatenate([p.reshape(-1).astype(F32) for p in parts])
    return jnp.pad(flat, (0, SMALL_ROWS * D_MODEL - flat.shape[0])).reshape(SMALL_ROWS, D_MODEL)


def _small_unpack(packed, shapes):
    flat, out, off = packed.reshape(-1), [], 0
    for s in shapes:
        size = 1
        for d in s:
            size *= d
        out.append(flat[off:off + size].reshape(s))
        off += size
    return out


def _pad_to(a, axis, size):
    pad = [(0, 0)] * a.ndim
    pad[axis] = (0, size - a.shape[axis])
    return jnp.pad(a, pad)


def kernel(x, positions, ffn_norm1, ffn1_w1, ffn1_w3, ffn1_w2, mix_norm, ffn_norm2, ffn2_w1, ffn2_w3, ffn2_w2, conv_w_pw1, conv_w_dw, conv_norm, conv_w_pw2, mla_w_a, mla_q_norm, mla_kv_norm, mla_w_uq, mla_w_ukv, mla_w_o, final_norm, loss_target, m_ffn_norm1, m_ffn1_w1, m_ffn1_w3, m_ffn1_w2, m_mix_norm, m_ffn_norm2, m_ffn2_w1, m_ffn2_w3, m_ffn2_w2, m_conv_w_pw1, m_conv_w_dw, m_conv_norm, m_conv_w_pw2, m_mla_w_a, m_mla_q_norm, m_mla_kv_norm, m_mla_w_uq, m_mla_w_ukv, m_mla_w_o, m_final_norm, v_ffn_norm1, v_ffn1_w1, v_ffn1_w3, v_ffn1_w2, v_mix_norm, v_ffn_norm2, v_ffn2_w1, v_ffn2_w3, v_ffn2_w2, v_conv_w_pw1, v_conv_w_dw, v_conv_norm, v_conv_w_pw2, v_mla_w_a, v_mla_q_norm, v_mla_kv_norm, v_mla_w_uq, v_mla_w_ukv, v_mla_w_o, v_final_norm):
    args = dict(locals())
    wire = lambda a: a.astype(WIRE_DTYPE)

    def ffn_shards(prefix, layer):
        w13_shard = _pad_to(jnp.stack([args[prefix + "_w1"][layer], args[prefix + "_w3"][layer]]), 2, FF_SHARD_PAD)
        return [(wire(w13_shard), 1), (wire(_pad_to(args[prefix + "_w2"][layer], 0, FF_SHARD_PAD)), 2)]

    def uq_pad(a):
        return _pad_to(a, 3, HEAD_PAD).reshape(1, a.shape[1], N_HEADS * HEAD_PAD)

    ukv_rows = lambda a: a.reshape(1, a.shape[1], N_HEADS * HEAD_PAD)
    tiny_shapes = [conv_w_dw.shape, mla_q_norm.shape, mla_kv_norm.shape]

    first = _exchange(*_gather_streams(ffn_shards("ffn1", 0)), "weight_gather_first")
    later_groups = [
        [(wire(conv_w_pw1[0]), 1), (wire(conv_w_pw2[0]), 2), (_small_pack([conv_w_dw, mla_q_norm, mla_kv_norm]), None)],
        ffn_shards("ffn2", 0),
        ffn_shards("ffn1", 1),
        [(wire(_pad_to(mla_w_a[0], 1, A_PAD)), 2), (wire(uq_pad(mla_w_uq)[0]), 2), (wire(ukv_rows(mla_w_ukv)[0]), 2), (wire(mla_w_o[0]), 2)],
        ffn_shards("ffn2", 1),
    ]
    later_streams, later_outs = _gather_streams([sh for grp in later_groups for sh in grp])
    group_ids, at = [], 0
    for grp in later_groups:
        group_ids.append(list(range(at, at + len(grp))))
        at += len(grp)
    gather, gather_token = _exchange_start(
        later_streams, _landing(later_streams, later_outs), group_ids, (first[0],), "weight_gather_start")

    inv_freq = ROPE_THETA ** (-2.0 * jnp.arange(QK_ROPE // 2, dtype=F32) / QK_ROPE)
    ang = positions[0].astype(F32)[:, None] * inv_freq
    cos, sin, zero = jnp.cos(ang), jnp.sin(ang), jnp.zeros_like(ang)
    cs = jnp.concatenate([cos, cos, zero, zero], axis=1)
    sa = jnp.concatenate([sin, zero, zero, zero], axis=1)
    sb = jnp.concatenate([zero, sin, zero, zero], axis=1)

    ffn_w = {("ffn1", 0): first}
    h0 = x[0]
    h1, n1, a1, b1 = _ffn_fwd(h0, ffn_norm1[0:1], *ffn_w[("ffn1", 0)], after=(gather_token,))
    w_pw1, w_pw2, tiny = _exchange_wait(gather, 0, (h1,), "weight_gather_wait_conv")
    tiny = [_small_unpack(tiny[q], tiny_shapes) for q in range(N_DEV)]
    w_dw = jnp.concatenate([t_[0][0] for t_ in tiny], axis=1)
    g_q = jnp.concatenate([t_[1] for t_ in tiny], axis=1)
    g_kv = jnp.concatenate([t_[2] for t_ in tiny], axis=1)
    n_c, u, glu = _conv_pre(h1, mix_norm[0:1], w_pw1)
    h2, c, sw = _conv_main(glu, w_dw, conv_norm, w_pw2, h1)
    ffn_w[("ffn2", 0)] = _exchange_wait(gather, 1, (h2,), "weight_gather_wait_ffn2_0")
    h3, n2, a2, b2 = _ffn_fwd(h2, ffn_norm2[0:1], *ffn_w[("ffn2", 0)])
    ffn_w[("ffn1", 1)] = _exchange_wait(gather, 2, (h3,), "weight_gather_wait_ffn1_1")
    h4, n3, a3, b3 = _ffn_fwd(h3, ffn_norm1[1:2], *ffn_w[("ffn1", 1)])
    w_a, w_uq, w_ukv, w_o = _exchange_wait(gather, 3, (h4,), "weight_gather_wait_mla")
    n_a, a_lat, cq, ckv, q, k, v = _mla_pre(h4, mix_norm[1:2], w_a, g_q, g_kv, w_uq, w_ukv, cs, sa, sb)
    o, lse = _flash_fwd(q, k, v)
    h5 = _attn_out(o, w_o, h4)
    ffn_w[("ffn2", 1)] = _exchange_wait(gather, 4, (h5,), "weight_gather_wait_ffn2_1")
    h6, n4, a4, b4 = _ffn_fwd(h5, ffn_norm2[1:2], *ffn_w[("ffn2", 1)])

    sq, dh, dg_final = _loss_head(h6, final_norm[None, :], loss_target[0])
    loss = lax.psum(0.5 / D_MODEL * jnp.sum(sq), ("x", "y", "c"))

    gain = {}

    def ffn_backward(prefix, norm, layer, dh, h_in, n, a, b, after=()):
        dh, dob, s, dab, gain[(norm, layer)] = _ffn_bwd(dh, h_in, args[norm][layer:layer + 1], a, b, *ffn_w[(prefix, layer)], after=after)
        return dh, [(_wgrad(n, dab, "wgrad_ffn_in"), 1), (_wgrad(s, dob, "wgrad_ffn_out"), 2)]

    dh, g_ffn2_1 = ffn_backward("ffn2", "ffn_norm2", 1, dh, h5, n4, a4, b4)
    do, delta = _attn_out_bwd(dh, o, w_o)
    d_o = _wgrad(o, dh, "wgrad_attn_out")
    dq, dk, dv = _flash_bwd(q, k, v, do, lse, delta)
    dh, dqp, dkv, da_lat, gain[("mix_norm", 1)], d_gq, d_gkv = _mla_bwd_pre(
        dq, dk, dv, a_lat, h4, dh, mix_norm[1:2], g_q, g_kv, w_a, w_uq, w_ukv, cs, sa, sb)
    g_mla = [(_wgrad(n_a, da_lat, "wgrad_mla_a"), 2), (_wgrad(cq, dqp, "wgrad_mla_uq"), 2), (_wgrad(ckv, dkv, "wgrad_mla_ukv"), 2), (d_o, 2)]
    dh, g_ffn1_1 = ffn_backward("ffn1", "ffn_norm1", 1, dh, h3, n3, a3, b3)
    streams_a, outs_a = _scatter_streams(g_ffn2_1 + g_mla + g_ffn1_1)
    scatter_a, token_a = _exchange_start(
        streams_a, _landing(streams_a, outs_a), [list(range(len(streams_a)))], (), "grad_scatter_start_layer1")

    dh, g_ffn2_0 = ffn_backward("ffn2", "ffn_norm2", 0, dh, h2, n2, a2, b2, after=(token_a,))
    dc, gain[("conv_norm", 0)] = _conv_bwd_post(dh, c, conv_norm, w_pw2)
    d_pw2 = _wgrad(sw, dh, "wgrad_conv_pw2")
    dh, du, d_dw, gain[("mix_norm", 0)] = _conv_bwd_pre(dc, glu, u, w_dw, w_pw1, h1, mix_norm[0:1], dh)
    d_pw1 = _wgrad(n_c, du, "wgrad_conv_pw1")
    streams_b, outs_b = _scatter_streams(g_ffn2_0 + [(d_pw1, 1), (d_pw2, 2)])
    scatter_b, token_b = _exchange_start(
        streams_b, _landing(streams_b, outs_b), [list(range(len(streams_b)))], (), "grad_scatter_start_conv")

    dh, g_ffn1_0 = ffn_backward("ffn1", "ffn_norm1", 0, dh, h0, n1, a1, b1, after=(token_b,))
    grad_x = dh[None]

    gain_rows = jnp.concatenate([
        gain[("ffn_norm1", 0)], gain[("ffn_norm1", 1)], gain[("mix_norm", 0)], gain[("mix_norm", 1)],
        gain[("ffn_norm2", 0)], gain[("ffn_norm2", 1)], gain[("conv_norm", 0)], dg_final])
    dw_by_dest = jnp.moveaxis(d_dw[:CONV_WIDTH].reshape(CONV_WIDTH, N_DEV, -1), 1, 0)
    small_by_dest = jnp.stack([
        _small_pack([gain_rows, dw_by_dest[p], d_gq.reshape(N_DEV, -1)[p], d_gkv.reshape(N_DEV, -1)[p]]) for p in range(N_DEV)])
    streams_c, outs_c = _scatter_streams(g_ffn1_0)
    streams_c.append((small_by_dest, len(outs_c), _slot, _slot, lambda dst, src, me: _place_slot(dst, lax.dynamic_index_in_dim(src, me, 0, False), me)))
    outs_c.append(jax.ShapeDtypeStruct((N_DEV, SMALL_ROWS, D_MODEL), F32))
    scatter_c, token_c = _exchange_start(streams_c, _landing(streams_c, outs_c), [list(range(len(streams_c)))], (), "grad_scatter_start_last")

    (p13_ffn2_1, p2_ffn2_1, p_a, p_uq, p_ukv, p_o, p13_ffn1_1, p2_ffn1_1) = _exchange_wait(scatter_a, 0, (token_c,), "grad_scatter_wait_layer1")
    p13_ffn2_0, p2_ffn2_0, p_pw1, p_pw2 = _exchange_wait(scatter_b, 0, (token_c,), "grad_scatter_wait_conv")
    results = {}

    def adam(name, parts, tr, view=lambda a: a, **kw):
        results[name] = _adamw(parts, view(args[name]), view(args["m_" + name]), view(args["v_" + name]), "adamw_" + name, tr,
                               into=results.get(name), **kw)

    def adam_ffn(prefix, layer, p13, p2):
        p13 = p13.reshape(N_DEV, 2 * D_MODEL, FF_SHARD_PAD)
        adam(prefix + "_w1", p13, 256, layer=layer)
        adam(prefix + "_w3", p13, 256, layer=layer, row_offset=D_MODEL)
        adam(prefix + "_w2", p2, FF_SHARD, layer=layer)

    adam_ffn("ffn2", 0, p13_ffn2_0, p2_ffn2_0)
    adam_ffn("ffn2", 1, p13_ffn2_1, p2_ffn2_1)
    adam_ffn("ffn1", 1, p13_ffn1_1, p2_ffn1_1)
    adam("conv_w_pw1", p_pw1, 256)
    adam("conv_w_pw2", p_pw2, 128)
    adam("mla_w_a", p_a, 128)
    adam("mla_w_uq", p_uq, 64, view=uq_pad)
    adam("mla_w_ukv", p_ukv, 32, view=ukv_rows)
    adam("mla_w_o", p_o, 128)
    results["mla_w_uq"] = [r.reshape(1, -1, N_HEADS, HEAD_PAD)[..., :QK_NOPE + QK_ROPE] for r in results["mla_w_uq"]]
    results["mla_w_ukv"] = [r.reshape(mla_w_ukv.shape) for r in results["mla_w_ukv"]]

    p13_ffn1_0, p2_ffn1_0, p_small = _exchange_wait(scatter_c, 0, (results["mla_w_o"][0],), "grad_scatter_wait_last")
    adam_ffn("ffn1", 0, p13_ffn1_0, p2_ffn1_0)
    small_names = _GAINS + _TINY
    small_shapes = [args[n].shape for n in small_names]
    pack_small = lambda prefix: _small_pack([args[prefix + n] for n in small_names])[None]
    small = _adamw(p_small, pack_small(""), pack_small("m_"), pack_small("v_"), "adamw_small", SMALL_ROWS)
    for kind in range(4):
        for n, leaf in zip(small_names, _small_unpack(small[kind][0], small_shapes)):
            results.setdefault(n, [None] * 4)[kind] = leaf

    order = ("ffn_norm1", "ffn1_w1", "ffn1_w3", "ffn1_w2", "mix_norm", "ffn_norm2", "ffn2_w1", "ffn2_w3", "ffn2_w2",
             "conv_w_pw1", "conv_w_dw", "conv_norm", "conv_w_pw2", "mla_w_a", "mla_q_norm", "mla_kv_norm",
             "mla_w_uq", "mla_w_ukv", "mla_w_o", "final_norm")
    outputs = [loss, grad_x]
    for kind in range(4):
        outputs.extend(results[n][kind] for n in order)
    return tuple(outputs)
```

```python
import jax
import jax.numpy as jnp
from jax import lax
from jax.experimental import pallas as pl
from jax.experimental.pallas import tpu as pltpu

F32 = jnp.float32
MXU_DTYPE = jnp.bfloat16
WIRE_DTYPE = jnp.bfloat16

N_DEV = 8
D_MODEL = 1024
FF_SHARD = 352
FF_SHARD_PAD = 384
D_FF_PAD = N_DEV * FF_SHARD_PAD
N_HEADS = 8
QK_NOPE = 128
QK_ROPE = 64
V_HEAD = 128
Q_LORA = 512
KV_LORA = 256
HEAD_PAD = 256
A_WIDTH = Q_LORA + KV_LORA + QK_ROPE
A_PAD = Q_LORA + KV_LORA + 128
CONV_WIDTH = 31
CONV_HALO = 32
CONV_ROWS = 16
CHUNK_SHIFT = 6
ROPE_THETA = 10000.0
RMS_EPS = 1e-6
ATTN_SCALE = (QK_NOPE + QK_ROPE) ** -0.5
FFN_RES_WEIGHT = 0.5
ADAM_LR = 0.001
ADAM_B1 = 0.9
ADAM_B2 = 0.999
ADAM_EPS = 1e-08
ADAM_WD = 0.01
ADAM_STEP = 10

SMALL_ROWS = 16
VMEM_LIMIT = 56 << 20


def _cparams(*sem):
    return pltpu.CompilerParams(dimension_semantics=sem, vmem_limit_bytes=VMEM_LIMIT)


def _dot(a, b):
    return lax.dot_general(a, b, (((1,), (0,)), ((), ())), preferred_element_type=F32)


def _dot_nt(a, b):
    return lax.dot_general(a, b, (((1,), (1,)), ((), ())), preferred_element_type=F32)


def _dot_tn(a, b):
    return lax.dot_general(a, b, (((0,), (0,)), ((), ())), preferred_element_type=F32)


def _resident(shape, index=None):
    fixed = index if index is not None else (0,) * len(shape)
    return pl.BlockSpec(shape, lambda *_: fixed, pipeline_mode=pl.Buffered(1))


def _rows(tm, n):
    return pl.BlockSpec((tm, n), lambda t, *_: (t, 0))


def _rms(x, g):
    r = lax.rsqrt(jnp.mean(x * x, axis=-1, keepdims=True) + RMS_EPS)
    return x * r * g


def _rms_bwd(x, g, dy):
    r = lax.rsqrt(jnp.mean(x * x, axis=-1, keepdims=True) + RMS_EPS)
    xr = x * r
    dg = jnp.sum(dy * xr, axis=0, keepdims=True)
    u = dy * g
    dx = r * (u - xr * jnp.mean(u * xr, axis=-1, keepdims=True))
    return dx, dg


def _sigmoid(x):
    return 1.0 / (1.0 + jnp.exp(-x))


def _me():
    return 4 * lax.axis_index("x") + 2 * lax.axis_index("y") + lax.axis_index("c")


def _peer(k):
    px, py, pc = lax.axis_index("x") ^ ((k >> 2) & 1), lax.axis_index("y") ^ ((k >> 1) & 1), lax.axis_index("c") ^ (k & 1)
    return (px, py, pc), 4 * px + 2 * py + pc


def _stream_copies(stream, src_ref, dst_ref, send_sem, recv_sem, base):
    send, land = stream[2], stream[3]
    me = _me()
    outgoing, incoming = [], []
    for k in range(1, N_DEV):
        pos, p = _peer(k)
        common = dict(send_sem=send_sem.at[base + k - 1], recv_sem=recv_sem.at[base + k - 1], device_id=pos, device_id_type=pl.DeviceIdType.MESH)
        outgoing.append(pltpu.make_async_remote_copy(src_ref=send(src_ref, p), dst_ref=land(dst_ref, me), **common))
        incoming.append(pltpu.make_async_remote_copy(src_ref=send(src_ref, p), dst_ref=land(dst_ref, p), **common))
    return outgoing, incoming


_HBM = pl.BlockSpec(memory_space=pltpu.HBM)
_SEM = pl.BlockSpec(memory_space=pltpu.SEMAPHORE)


def _exchange(streams, outs, name):
    n, n_out = len(streams), len(outs)

    def body(*refs):
        srcs, dsts = refs[:n], refs[n:n + n_out]
        send_sem, recv_sem, local_sem = refs[n + n_out:]
        me = _me()
        own = [pltpu.make_async_copy(st[2](srcs[e], me), st[3](dsts[st[1]], me), local_sem.at[e]) for e, st in enumerate(streams)]
        for cp in own:
            cp.start()
        sends, recvs = [], []
        for e, st in enumerate(streams):
            outgoing, incoming = _stream_copies(st, srcs[e], dsts[st[1]], send_sem, recv_sem, e * (N_DEV - 1))
            sends += outgoing
            recvs += incoming
        for cp in sends:
            cp.start()
        for cp in recvs:
            cp.wait_recv()
        for cp in sends:
            cp.wait_send()
        for cp in own:
            cp.wait()

    return pl.pallas_call(
        body,
        name=name,
        out_shape=tuple(outs),
        in_specs=[_HBM] * n,
        out_specs=tuple([_HBM] * n_out),
        scratch_shapes=[pltpu.SemaphoreType.DMA((n * (N_DEV - 1),)), pltpu.SemaphoreType.DMA((n * (N_DEV - 1),)),
                        pltpu.SemaphoreType.DMA((n,))],
        compiler_params=pltpu.CompilerParams(vmem_limit_bytes=VMEM_LIMIT),
    )(*[st[0] for st in streams])


def _landing(streams, outs):
    me = _me()
    lands = [lax.empty(o.shape, o.dtype) for o in outs]
    for st in streams:
        lands[st[1]] = st[4](lands[st[1]], st[0], me)
    return lands


def _exchange_start(streams, lands, groups, after, name):
    n, n_land, n_grp = len(streams), len(lands), len(groups)
    side_effects = pltpu.SideEffectType.DATAFLOW_SIDE_EFFECTING

    def body(*refs):
        srcs, dsts = refs[:n], refs[n:n + n_land]
        outs = refs[n + n_land + len(after):]
        send_sems, recv_sems, token = outs[:n_grp], outs[n_grp:2 * n_grp], outs[-1]
        for gi, group in enumerate(groups):
            for j, e in enumerate(group):
                outgoing, _ = _stream_copies(streams[e], srcs[e], dsts[streams[e][1]], send_sems[gi], recv_sems[gi], j * (N_DEV - 1))
                for cp in outgoing:
                    cp.start()
        token[...] = jnp.zeros_like(token)

    sems = tuple(pltpu.SemaphoreType.DMA((len(g) * (N_DEV - 1),)) for g in groups)
    passed = [st[0] for st in streams] + list(lands)
    res = pl.pallas_call(
        body,
        name=name,
        out_shape=sems + sems + tuple(pltpu.HBM(a.shape, a.dtype) for a in passed) + (jax.ShapeDtypeStruct((8, 128), F32),),
        in_specs=[_HBM] * (n + n_land) + [pl.BlockSpec(memory_space=pl.ANY)] * len(after),
        out_specs=tuple([_SEM] * (2 * n_grp) + [_HBM] * (n + n_land) + [pl.BlockSpec(memory_space=pltpu.VMEM)]),
        input_output_aliases={i: 2 * n_grp + i for i in range(n + n_land)},
        compiler_params=pltpu.CompilerParams(has_side_effects=side_effects),
    )(*passed, *after)
    handle = dict(streams=streams, groups=groups, send=res[:n_grp], recv=res[n_grp:2 * n_grp],
                  srcs=res[2 * n_grp:2 * n_grp + n], lands=res[2 * n_grp + n:2 * n_grp + n + n_land])
    return handle, res[-1]


def _exchange_wait(handle, gi, after, name):
    streams, group = handle["streams"], handle["groups"][gi]
    land_ids = sorted({streams[e][1] for e in group})
    srcs = [handle["srcs"][e] for e in group]
    lands = [handle["lands"][i] for i in land_ids]
    n, n_land = len(srcs), len(lands)
    side_effects = pltpu.SideEffectType.DATAFLOW_SIDE_EFFECTING

    def body(*refs):
        src_refs, land_refs = refs[:n], refs[n:n + n_land]
        send_sem, recv_sem = refs[n + n_land:n + n_land + 2]
        for j, e in enumerate(group):
            outgoing, incoming = _stream_copies(streams[e], src_refs[j], land_refs[land_ids.index(streams[e][1])], send_sem, recv_sem, j * (N_DEV - 1))
            for cp in outgoing:
                cp.wait_send()
            for cp in incoming:
                cp.wait_recv()

    res = pl.pallas_call(
        body,
        name=name,
        out_shape=tuple(pltpu.HBM(a.shape, a.dtype) for a in lands),
        in_specs=[_HBM] * (n + n_land) + [_SEM, _SEM] + [pl.BlockSpec(memory_space=pl.ANY)] * len(after),
        out_specs=tuple([_HBM] * n_land),
        input_output_aliases={n + i: i for i in range(n_land)},
        compiler_params=pltpu.CompilerParams(has_side_effects=side_effects),
    )(*srcs, *lands, handle["send"][gi], handle["recv"][gi], *after)
    return list(res)


def _whole(ref, _):
    return ref


def _slot(ref, q):
    return ref.at[q]


def _lane_block(width):
    def pick(ref, q):
        idx = (slice(None),) * (len(ref.shape) - 1) + (pl.ds(pl.multiple_of(q * width, 128), width),)
        return ref.at[idx]
    return pick


def _row_block(height):
    def pick(ref, q):
        idx = (slice(None),) * (len(ref.shape) - 2) + (pl.ds(pl.multiple_of(q * height, 16), height), slice(None))
        return ref.at[idx]
    return pick


def _block_of(axis_from_end, size):
    return _lane_block(size) if axis_from_end == 1 else _row_block(size)


def _gather_streams(shards):
    streams, outs = [], []
    for i, (shard, axis) in enumerate(shards):
        shape = list(shard.shape)
        if axis is None:
            shape, land = [N_DEV] + shape, _slot
            place = _place_slot
        else:
            size = shape[-axis]
            land = _block_of(axis, size)
            shape[-axis] *= N_DEV
            place = lambda dst, src, me, size=size, axis=axis: lax.dynamic_update_slice_in_dim(dst, src, me * size, dst.ndim - axis)
        streams.append((shard, i, _whole, land, place))
        outs.append(jax.ShapeDtypeStruct(tuple(shape), shard.dtype))
    return streams, outs


def _place_slot(dst, src, me):
    return lax.dynamic_update_index_in_dim(dst, src, me, 0)


def _scatter_streams(fulls):
    streams, outs = [], []
    for i, (full, axis) in enumerate(fulls):
        shape = list(full.shape)
        size = shape[-axis] // N_DEV
        shape[-axis] = size
        place = lambda dst, src, me, size=size, axis=axis: _place_slot(
            dst, lax.dynamic_slice_in_dim(src, me * size, size, src.ndim - axis), me)
        streams.append((full, i, _block_of(axis, size), _slot, place))
        outs.append(jax.ShapeDtypeStruct((N_DEV,) + tuple(shape), full.dtype))
    return streams, outs


FFN_TM = 512
FFN_TF = 1024
FFN_FWD_TM = 1024
FFN_FWD_TF = 1024


def _ffn_fwd(h, g, w13, w2, after=()):
    t = h.shape[0]
    tm, tf = min(FFN_FWD_TM, t), FFN_FWD_TF
    nf = D_FF_PAD // tf

    def body(h_ref, g_ref, w1_ref, w3_ref, w2_ref, *rest):
        ho_ref, n_ref, a_ref, b_ref, n_scr, acc = rest[len(after):]
        f = pl.program_id(1)

        @pl.when(f == 0)
        def _():
            n_scr[...] = _rms(h_ref[...], g_ref[...]).astype(MXU_DTYPE)
            n_ref[...] = n_scr[...]
            acc[...] = jnp.zeros_like(acc)

        n = n_scr[...]
        a = _dot(n, w1_ref[...])
        b = _dot(n, w3_ref[...])
        a_ref[...] = a.astype(MXU_DTYPE)
        b_ref[...] = b.astype(MXU_DTYPE)
        s = (a * _sigmoid(a)) * b
        acc[...] += _dot(s.astype(MXU_DTYPE), w2_ref[...])

        @pl.when(f == nf - 1)
        def _():
            ho_ref[...] = h_ref[...] + FFN_RES_WEIGHT * acc[...]

    return pl.pallas_call(
        body,
        name="ffn_fwd",
        grid=(t // tm, nf),
        in_specs=[
            _rows(tm, D_MODEL),
            _resident((1, D_MODEL)),
            pl.BlockSpec((None, D_MODEL, tf), lambda i, f: (0, 0, f)),
            pl.BlockSpec((None, D_MODEL, tf), lambda i, f: (1, 0, f)),
            pl.BlockSpec((tf, D_MODEL), lambda i, f: (f, 0)),
        ] + [pl.BlockSpec(memory_space=pl.ANY)] * len(after),
        out_specs=[
            _rows(tm, D_MODEL),
            _rows(tm, D_MODEL),
            pl.BlockSpec((tm, tf), lambda i, f: (i, f)),
            pl.BlockSpec((tm, tf), lambda i, f: (i, f)),
        ],
        out_shape=[
            jax.ShapeDtypeStruct((t, D_MODEL), F32),
            jax.ShapeDtypeStruct((t, D_MODEL), MXU_DTYPE),
            jax.ShapeDtypeStruct((t, D_FF_PAD), MXU_DTYPE),
            jax.ShapeDtypeStruct((t, D_FF_PAD), MXU_DTYPE),
        ],
        scratch_shapes=[pltpu.VMEM((tm, D_MODEL), MXU_DTYPE), pltpu.VMEM((tm, D_MODEL), F32)],
        compiler_params=_cparams("parallel", "arbitrary"),
    )(h, g, w13, w13, w2, *after)


def _ffn_bwd(dout, h, g, a, b, w13, w2, after=()):
    t = h.shape[0]
    tm, tf = min(FFN_TM, t), FFN_TF
    nf = D_FF_PAD // tf

    def body(do_ref, h_ref, g_ref, a_ref, b_ref, w1_ref, w3_ref, w2_ref, *rest):
        dh_ref, dob_ref, s_ref, dab_ref, dg_ref, dob_scr, acc = rest[len(after):]
        i, f = pl.program_id(0), pl.program_id(1)

        @pl.when(f == 0)
        def _():
            dob_scr[...] = (FFN_RES_WEIGHT * do_ref[...]).astype(MXU_DTYPE)
            dob_ref[...] = dob_scr[...]
            acc[...] = jnp.zeros_like(acc)

        @pl.when((f == 0) & (i == 0))
        def _():
            dg_ref[...] = jnp.zeros_like(dg_ref)

        ds = _dot_nt(dob_scr[...], w2_ref[...])
        av = a_ref[...].astype(F32)
        bv = b_ref[...].astype(F32)
        sig = _sigmoid(av)
        sil = av * sig
        s_ref[...] = (sil * bv).astype(MXU_DTYPE)
        da = (ds * bv * (sig * (1.0 + av * (1.0 - sig)))).astype(MXU_DTYPE)
        db = (ds * sil).astype(MXU_DTYPE)
        dab_ref[0] = da
        dab_ref[1] = db
        acc[...] += _dot_nt(da, w1_ref[...]) + _dot_nt(db, w3_ref[...])

        @pl.when(f == nf - 1)
        def _():
            dx, dg = _rms_bwd(h_ref[...], g_ref[...], acc[...])
            dh_ref[...] = do_ref[...] + dx
            dg_ref[...] += dg

    return pl.pallas_call(
        body,
        name="ffn_bwd",
        grid=(t // tm, nf),
        in_specs=[
            _rows(tm, D_MODEL),
            _rows(tm, D_MODEL),
            _resident((1, D_MODEL)),
            pl.BlockSpec((tm, tf), lambda i, f: (i, f)),
            pl.BlockSpec((tm, tf), lambda i, f: (i, f)),
            pl.BlockSpec((None, D_MODEL, tf), lambda i, f: (0, 0, f)),
            pl.BlockSpec((None, D_MODEL, tf), lambda i, f: (1, 0, f)),
            pl.BlockSpec((tf, D_MODEL), lambda i, f: (f, 0)),
        ] + [pl.BlockSpec(memory_space=pl.ANY)] * len(after),
        out_specs=[
            _rows(tm, D_MODEL),
            _rows(tm, D_MODEL),
            pl.BlockSpec((tm, tf), lambda i, f: (i, f)),
            pl.BlockSpec((2, tm, tf), lambda i, f: (0, i, f)),
            pl.BlockSpec((1, D_MODEL), lambda i, f: (0, 0)),
        ],
        out_shape=[
            jax.ShapeDtypeStruct((t, D_MODEL), F32),
            jax.ShapeDtypeStruct((t, D_MODEL), MXU_DTYPE),
            jax.ShapeDtypeStruct((t, D_FF_PAD), MXU_DTYPE),
            jax.ShapeDtypeStruct((2, t, D_FF_PAD), MXU_DTYPE),
            jax.ShapeDtypeStruct((1, D_MODEL), F32),
        ],
        scratch_shapes=[pltpu.VMEM((tm, D_MODEL), MXU_DTYPE), pltpu.VMEM((tm, D_MODEL), F32)],
        compiler_params=_cparams("arbitrary", "arbitrary"),
    )(dout, h, g, a, b, w13, w13, w2, *after)


def _wgrad(x, y, name, out_dtype=WIRE_DTYPE, after=(), only=None):
    t, m = x.shape
    grouped = y.ndim == 3 and only is None
    groups = y.shape[0] if grouped else 1
    n = y.shape[-1]
    tk = min(2048, t)
    bm = m if m <= 1536 else m // 2
    bn = n if n <= 1536 else n // 2
    nk = t // tk

    def body(x_ref, y_ref, *rest):
        o_ref, acc = rest[len(after):]
        k = pl.program_id(3)

        @pl.when(k == 0)
        def _():
            acc[...] = jnp.zeros_like(acc)

        acc[...] += _dot_tn(x_ref[...].astype(MXU_DTYPE), y_ref[...].astype(MXU_DTYPE))

        @pl.when(k == nk - 1)
        def _():
            o_ref[...] = acc[...].astype(out_dtype)

    if grouped:
        y_spec = pl.BlockSpec((None, tk, bn), lambda g, i, j, k: (g, k, j))
        o_spec = pl.BlockSpec((None, bm, bn), lambda g, i, j, k: (g, i, j))
        o_shape = jax.ShapeDtypeStruct((groups, m, n), out_dtype)
    else:
        if only is None:
            y_spec = pl.BlockSpec((tk, bn), lambda g, i, j, k: (k, j))
        else:
            y_spec = pl.BlockSpec((None, tk, bn), lambda g, i, j, k: (only, k, j))
        o_spec = pl.BlockSpec((bm, bn), lambda g, i, j, k: (i, j))
        o_shape = jax.ShapeDtypeStruct((m, n), out_dtype)
    return pl.pallas_call(
        body,
        name=name,
        grid=(groups, m // bm, n // bn, nk),
        in_specs=[pl.BlockSpec((tk, bm), lambda g, i, j, k: (k, i)), y_spec] + [pl.BlockSpec(memory_space=pl.ANY)] * len(after),
        out_specs=o_spec,
        out_shape=o_shape,
        scratch_shapes=[pltpu.VMEM((bm, bn), F32)],
        compiler_params=_cparams("parallel", "parallel", "parallel", "arbitrary"),
    )(x, y, *after)


def _conv_pre(h, g, w_pw1):
    t = h.shape[0]
    tm = min(512, t)

    def body(h_ref, g_ref, w_ref, n_ref, u_ref, glu_ref):
        n = _rms(h_ref[...], g_ref[...]).astype(MXU_DTYPE)
        n_ref[...] = n
        u = _dot(n, w_ref[...])
        u_ref[...] = u
        glu_ref[...] = u[:, :D_MODEL] * _sigmoid(u[:, D_MODEL:])

    return pl.pallas_call(
        body,
        name="conv_pre",
        grid=(t // tm,),
        in_specs=[_rows(tm, D_MODEL), _resident((1, D_MODEL)), _resident((D_MODEL, 2 * D_MODEL))],
        out_specs=[_rows(tm, D_MODEL), _rows(tm, 2 * D_MODEL), _rows(tm, D_MODEL)],
        out_shape=[
            jax.ShapeDtypeStruct((t, D_MODEL), MXU_DTYPE),
            jax.ShapeDtypeStruct((t, 2 * D_MODEL), F32),
            jax.ShapeDtypeStruct((t, D_MODEL), F32),
        ],
        compiler_params=_cparams("parallel"),
    )(h, g, w_pw1)


def _shifted_copies(dst, src, rows):
    for r in range(1, 8):
        dst[r - 1] = src[pl.ds(r, rows), :]


def _window(src, shifted, offset, base, rows):
    r = offset % 8
    start = base + offset - r
    return src[pl.ds(start, rows), :] if r == 0 else shifted[r - 1, pl.ds(start, rows), :]


def _conv_main(glu, w_dw, g, w_pw2, h):
    t = h.shape[0]
    tm = min(512, t)
    per = tm // CONV_HALO
    shifted_rows = tm + CONV_HALO - 8

    def body(glu_ref, halo_ref, w_ref, g_ref, w2_ref, h_ref, ho_ref, c_ref, sw_ref, ext, ext_sh):
        i = pl.program_id(0)
        ext[pl.ds(0, CONV_HALO), :] = jnp.where(i == 0, 0.0, halo_ref[...])
        ext[pl.ds(CONV_HALO, tm), :] = glu_ref[...]
        _shifted_copies(ext_sh, ext, shifted_rows)
        for base in range(0, tm, CONV_ROWS):
            acc = jnp.zeros((CONV_ROWS, D_MODEL), F32)
            for k in range(CONV_WIDTH):
                acc = acc + _window(ext, ext_sh, CONV_HALO - (CONV_WIDTH - 1) + k, base, CONV_ROWS) * w_ref[pl.ds(k, 1), :]
            c_ref[pl.ds(base, CONV_ROWS), :] = acc
        y = _rms(c_ref[...], g_ref[...])
        sw = (y * _sigmoid(y)).astype(MXU_DTYPE)
        sw_ref[...] = sw
        ho_ref[...] = h_ref[...] + _dot(sw, w2_ref[...])

    return pl.pallas_call(
        body,
        name="conv_main",
        grid=(t // tm,),
        in_specs=[
            _rows(tm, D_MODEL),
            pl.BlockSpec((CONV_HALO, D_MODEL), lambda i: (jnp.maximum(i * per - 1, 0), 0)),
            _resident((CONV_WIDTH, D_MODEL)),
            _resident((1, D_MODEL)),
            _resident((D_MODEL, D_MODEL)),
            _rows(tm, D_MODEL),
        ],
        out_specs=[_rows(tm, D_MODEL), _rows(tm, D_MODEL), _rows(tm, D_MODEL)],
        out_shape=[
            jax.ShapeDtypeStruct((t, D_MODEL), F32),
            jax.ShapeDtypeStruct((t, D_MODEL), F32),
            jax.ShapeDtypeStruct((t, D_MODEL), MXU_DTYPE),
        ],
        scratch_shapes=[pltpu.VMEM((tm + CONV_HALO, D_MODEL), F32), pltpu.VMEM((7, shifted_rows, D_MODEL), F32)],
        compiler_params=_cparams("parallel"),
    )(glu, glu, w_dw, g, w_pw2, h)


def _conv_bwd_post(dout, c, g, w_pw2):
    t = dout.shape[0]
    tm = min(1024, t)

    def body(do_ref, c_ref, g_ref, w2_ref, dc_ref, dg_ref):
        @pl.when(pl.program_id(0) == 0)
        def _():
            dg_ref[...] = jnp.zeros_like(dg_ref)

        dsw = _dot_nt(do_ref[...].astype(MXU_DTYPE), w2_ref[...])
        cv = c_ref[...]
        y = _rms(cv, g_ref[...])
        sig = _sigmoid(y)
        dy = dsw * (sig * (1.0 + y * (1.0 - sig)))
        dc, dg = _rms_bwd(cv, g_ref[...], dy)
        dc_ref[...] = dc
        dg_ref[...] += dg

    return pl.pallas_call(
        body,
        name="conv_bwd_post",
        grid=(t // tm,),
        in_specs=[_rows(tm, D_MODEL), _rows(tm, D_MODEL), _resident((1, D_MODEL)), _resident((D_MODEL, D_MODEL))],
        out_specs=[_rows(tm, D_MODEL), pl.BlockSpec((1, D_MODEL), lambda i: (0, 0))],
        out_shape=[jax.ShapeDtypeStruct((t, D_MODEL), F32), jax.ShapeDtypeStruct((1, D_MODEL), F32)],
        compiler_params=_cparams("arbitrary"),
    )(dout, c, g, w_pw2)


def _conv_bwd_pre(dc, glu, u, w_dw, w_pw1, h, g, dout):
    t = h.shape[0]
    tm = min(256, t)
    per = tm // CONV_HALO
    nt = t // tm
    last_halo = t // CONV_HALO - 1
    shifted_rows = tm + CONV_HALO - 8

    def body(dc_ref, dcn_ref, glu_ref, gluh_ref, u_ref, w_ref, w1_ref, h_ref, g_ref, do_ref,
             dh_ref, du_ref, dw_ref, dg_ref, dext, gext, dext_sh, gext_sh):
        i = pl.program_id(0)

        @pl.when(i == 0)
        def _():
            dw_ref[...] = jnp.zeros_like(dw_ref)
            dg_ref[...] = jnp.zeros_like(dg_ref)

        dext[pl.ds(0, tm), :] = dc_ref[...]
        dext[pl.ds(tm, CONV_HALO), :] = jnp.where(i == nt - 1, 0.0, dcn_ref[...])
        gext[pl.ds(0, CONV_HALO), :] = jnp.where(i == 0, 0.0, gluh_ref[...])
        gext[pl.ds(CONV_HALO, tm), :] = glu_ref[...]
        _shifted_copies(dext_sh, dext, shifted_rows)
        _shifted_copies(gext_sh, gext, shifted_rows)
        for base in range(0, tm, CONV_ROWS):
            rows = pl.ds(base, CONV_ROWS)
            dglu = jnp.zeros((CONV_ROWS, D_MODEL), F32)
            for k in range(CONV_WIDTH):
                dglu = dglu + _window(dext, dext_sh, CONV_WIDTH - 1 - k, base, CONV_ROWS) * w_ref[pl.ds(k, 1), :]
            av, bv = u_ref[rows, :D_MODEL], u_ref[rows, D_MODEL:]
            sig = _sigmoid(bv)
            du_ref[rows, :D_MODEL] = (dglu * sig).astype(MXU_DTYPE)
            du_ref[rows, D_MODEL:] = (dglu * av * (sig * (1.0 - sig))).astype(MXU_DTYPE)
        for k in range(CONV_WIDTH):
            part = jnp.zeros((CONV_ROWS, D_MODEL), F32)
            for base in range(0, tm, CONV_ROWS):
                part = part + dc_ref[pl.ds(base, CONV_ROWS), :] * _window(gext, gext_sh, CONV_HALO - (CONV_WIDTH - 1) + k, base, CONV_ROWS)
            dw_ref[pl.ds(k, 1), :] += jnp.sum(part, axis=0, keepdims=True)
        dn = _dot_nt(du_ref[...], w1_ref[...])
        dx, dg = _rms_bwd(h_ref[...], g_ref[...], dn)
        dh_ref[...] = do_ref[...] + dx
        dg_ref[...] += dg

    return pl.pallas_call(
        body,
        name="conv_bwd_pre",
        grid=(nt,),
        in_specs=[
            _rows(tm, D_MODEL),
            pl.BlockSpec((CONV_HALO, D_MODEL), lambda i: (jnp.minimum((i + 1) * per, last_halo), 0)),
            _rows(tm, D_MODEL),
            pl.BlockSpec((CONV_HALO, D_MODEL), lambda i: (jnp.maximum(i * per - 1, 0), 0)),
            _rows(tm, 2 * D_MODEL),
            _resident((CONV_WIDTH, D_MODEL)),
            _resident((D_MODEL, 2 * D_MODEL)),
            _rows(tm, D_MODEL),
            _resident((1, D_MODEL)),
            _rows(tm, D_MODEL),
        ],
        out_specs=[
            _rows(tm, D_MODEL),
            _rows(tm, 2 * D_MODEL),
            pl.BlockSpec((CONV_HALO, D_MODEL), lambda i: (0, 0)),
            pl.BlockSpec((1, D_MODEL), lambda i: (0, 0)),
        ],
        out_shape=[
            jax.ShapeDtypeStruct((t, D_MODEL), F32),
            jax.ShapeDtypeStruct((t, 2 * D_MODEL), MXU_DTYPE),
            jax.ShapeDtypeStruct((CONV_HALO, D_MODEL), F32),
            jax.ShapeDtypeStruct((1, D_MODEL), F32),
        ],
        scratch_shapes=[pltpu.VMEM((tm + CONV_HALO, D_MODEL), F32), pltpu.VMEM((tm + CONV_HALO, D_MODEL), F32),
                        pltpu.VMEM((7, shifted_rows, D_MODEL), F32), pltpu.VMEM((7, shifted_rows, D_MODEL), F32)],
        compiler_params=_cparams("arbitrary"),
    )(dc, dc, glu, glu, u, w_dw, w_pw1, h, g, dout)


def _rope(s, cs, sa, sb):
    return s * cs - pltpu.roll(s, 96, 1) * sa + pltpu.roll(s, 32, 1) * sb


def _unrope(d, cs, sa, sb):
    return d * cs + pltpu.roll(d, 96, 1) * sa - pltpu.roll(d, 32, 1) * sb


def _mla_pre(h, g, w_a, g_q, g_kv, w_uq, w_ukv, cs, sa, sb):
    t = h.shape[0]
    tm = min(512, t)
    qw = N_HEADS * HEAD_PAD
    vw = N_HEADS * V_HEAD

    def body(h_ref, g_ref, wa_ref, gq_ref, gkv_ref, wuq_ref, wukv_ref, cs_ref, sa_ref, sb_ref,
             n_ref, a_ref, cq_ref, ckv_ref, q_ref, k_ref, v_ref):
        n = _rms(h_ref[...], g_ref[...]).astype(MXU_DTYPE)
        n_ref[...] = n
        a = _dot(n, wa_ref[...])
        a_ref[...] = a
        cq = _rms(a[:, :Q_LORA], gq_ref[...]).astype(MXU_DTYPE)
        ckv = _rms(a[:, Q_LORA:Q_LORA + KV_LORA], gkv_ref[...]).astype(MXU_DTYPE)
        cq_ref[...] = cq
        ckv_ref[...] = ckv
        q = _dot(cq, wuq_ref[...]) * ATTN_SCALE
        kv = _dot(ckv, wukv_ref[...])
        cs_, sa_, sb_ = cs_ref[...], sa_ref[...], sb_ref[...]
        k_rot = _rope(a[:, Q_LORA + KV_LORA:], cs_, sa_, sb_).astype(MXU_DTYPE)
        for hd in range(N_HEADS):
            lo = hd * HEAD_PAD
            q_ref[:, lo:lo + QK_NOPE] = q[:, lo:lo + QK_NOPE].astype(MXU_DTYPE)
            q_ref[:, lo + QK_NOPE:lo + HEAD_PAD] = _rope(q[:, lo + QK_NOPE:lo + HEAD_PAD], cs_, sa_, sb_).astype(MXU_DTYPE)
            k_ref[:, lo:lo + QK_NOPE] = kv[:, lo:lo + QK_NOPE].astype(MXU_DTYPE)
            k_ref[:, lo + QK_NOPE:lo + HEAD_PAD] = k_rot
            v_ref[:, hd * V_HEAD:(hd + 1) * V_HEAD] = kv[:, lo + QK_NOPE:lo + HEAD_PAD].astype(MXU_DTYPE)

    return pl.pallas_call(
        body,
        name="mla_pre",
        grid=(t // tm,),
        in_specs=[
            _rows(tm, D_MODEL),
            _resident((1, D_MODEL)),
            _resident((D_MODEL, A_PAD)),
            _resident((1, Q_LORA)),
            _resident((1, KV_LORA)),
            _resident((Q_LORA, qw)),
            _resident((KV_LORA, qw)),
            _rows(tm, 128),
            _rows(tm, 128),
            _rows(tm, 128),
        ],
        out_specs=[_rows(tm, D_MODEL), _rows(tm, A_PAD), _rows(tm, Q_LORA), _rows(tm, KV_LORA), _rows(tm, qw), _rows(tm, qw), _rows(tm, vw)],
        out_shape=[
            jax.ShapeDtypeStruct((t, D_MODEL), MXU_DTYPE),
            jax.ShapeDtypeStruct((t, A_PAD), F32),
            jax.ShapeDtypeStruct((t, Q_LORA), MXU_DTYPE),
            jax.ShapeDtypeStruct((t, KV_LORA), MXU_DTYPE),
            jax.ShapeDtypeStruct((t, qw), MXU_DTYPE),
            jax.ShapeDtypeStruct((t, qw), MXU_DTYPE),
            jax.ShapeDtypeStruct((t, vw), MXU_DTYPE),
        ],
        compiler_params=_cparams("parallel"),
    )(h, g, w_a, g_q, g_kv, w_uq, w_ukv, cs, sa, sb)


def _attn_tile(t):
    return min(1024, t)


def _chunk_mask(rows, cols):
    r = lax.broadcasted_iota(jnp.int32, (rows, cols), 0)
    c = lax.broadcasted_iota(jnp.int32, (rows, cols), 1)
    return (r >> CHUNK_SHIFT) >= (c >> CHUNK_SHIFT)


def _causal_pairs(n, by_column=False):
    if by_column:
        pairs = [(i, j) for j in range(n) for i in range(j, n)]
    else:
        pairs = [(i, j) for i in range(n) for j in range(i + 1)]
    return jnp.asarray([p[0] for p in pairs], jnp.int32), jnp.asarray([p[1] for p in pairs], jnp.int32)


def _flash_fwd(q, k, v):
    t = q.shape[0]
    tq = tk = _attn_tile(t)
    rows_of, cols_of = _causal_pairs(t // tq)

    def body(qi_ref, ki_ref, q_ref, k_ref, v_ref, o_ref, lse_ref, m_s, l_s, acc):
        step = pl.program_id(1)
        qi, ki = qi_ref[step], ki_ref[step]

        @pl.when(ki == 0)
        def _():
            m_s[...] = jnp.full_like(m_s, -jnp.inf)
            l_s[...] = jnp.zeros_like(l_s)
            acc[...] = jnp.zeros_like(acc)

        def block(on_diagonal):
            s = _dot_nt(q_ref[...], k_ref[...])
            if on_diagonal:
                s = jnp.where(_chunk_mask(tq, tk), s, -jnp.inf)
            m_prev = m_s[...]
            m_new = jnp.maximum(m_prev, jnp.max(s, axis=1, keepdims=True))
            alpha = jnp.exp(m_prev - m_new)
            p = jnp.exp(s - m_new)
            l_s[...] = alpha * l_s[...] + jnp.sum(p, axis=1, keepdims=True)
            acc[...] = alpha * acc[...] + _dot(p.astype(MXU_DTYPE), v_ref[...])
            m_s[...] = m_new

        pl.when(ki < qi)(lambda: block(False))

        @pl.when(ki == qi)
        def _():
            block(True)
            o_ref[...] = acc[...] / l_s[...]
            lse_ref[...] = jnp.broadcast_to(m_s[...] + jnp.log(l_s[...]), (tq, V_HEAD))

    grid_spec = pltpu.PrefetchScalarGridSpec(
        num_scalar_prefetch=2,
        grid=(N_HEADS, rows_of.shape[0]),
        in_specs=[
            pl.BlockSpec((tq, HEAD_PAD), lambda h, s, qi, ki: (qi[s], h)),
            pl.BlockSpec((tk, HEAD_PAD), lambda h, s, qi, ki: (ki[s], h)),
            pl.BlockSpec((tk, V_HEAD), lambda h, s, qi, ki: (ki[s], h)),
        ],
        out_specs=[pl.BlockSpec((tq, V_HEAD), lambda h, s, qi, ki: (qi[s], h)), pl.BlockSpec((tq, V_HEAD), lambda h, s, qi, ki: (qi[s], h))],
        scratch_shapes=[pltpu.VMEM((tq, 1), F32), pltpu.VMEM((tq, 1), F32), pltpu.VMEM((tq, V_HEAD), F32)],
    )
    return pl.pallas_call(
        body,
        name="flash_fwd",
        grid_spec=grid_spec,
        out_shape=[jax.ShapeDtypeStruct((t, N_HEADS * V_HEAD), F32), jax.ShapeDtypeStruct((t, N_HEADS * V_HEAD), F32)],
        compiler_params=_cparams("parallel", "arbitrary"),
    )(rows_of, cols_of, q, k, v)


def _flash_bwd(q, k, v, do, lse, delta):
    t = q.shape[0]
    tq = tk = _attn_tile(t)
    nq = t // tq
    rows_of, cols_of = _causal_pairs(nq, by_column=True)

    def body(qi_ref, kj_ref, q_ref, k_ref, v_ref, do_ref, lse_ref, dl_ref, dq_ref, dk_ref, dv_ref, dk_acc, dv_acc):
        step = pl.program_id(1)
        qi, kj = qi_ref[step], kj_ref[step]

        @pl.when(step == 0)
        def _():
            dq_ref[...] = jnp.zeros_like(dq_ref)

        def block(on_diagonal):
            qv, kv, dov = q_ref[...], k_ref[...], do_ref[...]
            s = _dot_nt(qv, kv)
            if on_diagonal:
                s = jnp.where(_chunk_mask(tq, tk), s, -jnp.inf)
            p = jnp.exp(s - lse_ref[:, :1])
            dp = _dot_nt(dov, v_ref[...])
            ds = (p * (dp - dl_ref[:, :1])).astype(MXU_DTYPE)
            dv_new = _dot_tn(p.astype(MXU_DTYPE), dov)
            dk_new = _dot_tn(ds, qv)
            if on_diagonal:
                dv_acc[...] = dv_new
                dk_acc[...] = dk_new
            else:
                dv_acc[...] += dv_new
                dk_acc[...] += dk_new
            rows = pl.ds(pl.multiple_of(qi * tq, tq), tq)
            dq_ref[rows, :] += _dot(ds, kv)

        pl.when(qi > kj)(lambda: block(False))
        pl.when(qi == kj)(lambda: block(True))

        @pl.when(qi == nq - 1)
        def _():
            dk_ref[...] = dk_acc[...]
            dv_ref[...] = dv_acc[...]

    grid_spec = pltpu.PrefetchScalarGridSpec(
        num_scalar_prefetch=2,
        grid=(N_HEADS, rows_of.shape[0]),
        in_specs=[
            pl.BlockSpec((tq, HEAD_PAD), lambda h, s, qi, kj: (qi[s], h)),
            pl.BlockSpec((tk, HEAD_PAD), lambda h, s, qi, kj: (kj[s], h)),
            pl.BlockSpec((tk, V_HEAD), lambda h, s, qi, kj: (kj[s], h)),
            pl.BlockSpec((tq, V_HEAD), lambda h, s, qi, kj: (qi[s], h)),
            pl.BlockSpec((tq, V_HEAD), lambda h, s, qi, kj: (qi[s], h)),
            pl.BlockSpec((tq, V_HEAD), lambda h, s, qi, kj: (qi[s], h)),
        ],
        out_specs=[
            pl.BlockSpec((t, HEAD_PAD), lambda h, s, qi, kj: (0, h)),
            pl.BlockSpec((tk, HEAD_PAD), lambda h, s, qi, kj: (kj[s], h)),
            pl.BlockSpec((tk, V_HEAD), lambda h, s, qi, kj: (kj[s], h)),
        ],
        scratch_shapes=[pltpu.VMEM((tk, HEAD_PAD), F32), pltpu.VMEM((tk, V_HEAD), F32)],
    )
    return pl.pallas_call(
        body,
        name="flash_bwd",
        grid_spec=grid_spec,
        out_shape=[
            jax.ShapeDtypeStruct((t, N_HEADS * HEAD_PAD), F32),
            jax.ShapeDtypeStruct((t, N_HEADS * HEAD_PAD), F32),
            jax.ShapeDtypeStruct((t, N_HEADS * V_HEAD), F32),
        ],
        compiler_params=_cparams("arbitrary", "arbitrary"),
    )(rows_of, cols_of, q, k, v, do, lse, delta)


def _attn_out(o, w_o, h):
    t = h.shape[0]
    tm = min(1024, t)

    def body(o_ref, w_ref, h_ref, ho_ref):
        ho_ref[...] = h_ref[...] + _dot(o_ref[...].astype(MXU_DTYPE), w_ref[...])

    return pl.pallas_call(
        body,
        name="attn_out",
        grid=(t // tm,),
        in_specs=[_rows(tm, D_MODEL), _resident((D_MODEL, D_MODEL)), _rows(tm, D_MODEL)],
        out_specs=_rows(tm, D_MODEL),
        out_shape=jax.ShapeDtypeStruct((t, D_MODEL), F32),
        compiler_params=_cparams("parallel"),
    )(o, w_o, h)


def _attn_out_bwd(dout, o, w_o):
    t = dout.shape[0]
    tm = min(1024, t)

    def body(d_ref, o_ref, w_ref, do_ref, dl_ref):
        do = _dot_nt(d_ref[...].astype(MXU_DTYPE), w_ref[...])
        do_ref[...] = do.astype(MXU_DTYPE)
        prod = do * o_ref[...]
        for hd in range(N_HEADS):
            lanes = slice(hd * V_HEAD, (hd + 1) * V_HEAD)
            dl_ref[:, lanes] = jnp.broadcast_to(jnp.sum(prod[:, lanes], axis=1, keepdims=True), (tm, V_HEAD))

    return pl.pallas_call(
        body,
        name="attn_out_bwd",
        grid=(t // tm,),
        in_specs=[_rows(tm, D_MODEL), _rows(tm, D_MODEL), _resident((D_MODEL, D_MODEL))],
        out_specs=[_rows(tm, D_MODEL), _rows(tm, D_MODEL)],
        out_shape=[jax.ShapeDtypeStruct((t, D_MODEL), MXU_DTYPE), jax.ShapeDtypeStruct((t, D_MODEL), F32)],
        compiler_params=_cparams("parallel"),
    )(dout, o, w_o)


def _mla_bwd_pre(dq, dk, dv, a, h, dout, g, g_q, g_kv, w_a, w_uq, w_ukv, cs, sa, sb):
    t = h.shape[0]
    tm = min(512, t)
    qw = N_HEADS * HEAD_PAD
    vw = N_HEADS * V_HEAD

    def body(dq_ref, dk_ref, dv_ref, a_ref, h_ref, do_ref, g_ref, gq_ref, gkv_ref, wa_ref, wuq_ref, wukv_ref,
             cs_ref, sa_ref, sb_ref, dh_ref, dqp_ref, dkv_ref, da_ref, dg_ref, dgq_ref, dgkv_ref):
        @pl.when(pl.program_id(0) == 0)
        def _():
            dg_ref[...] = jnp.zeros_like(dg_ref)
            dgq_ref[...] = jnp.zeros_like(dgq_ref)
            dgkv_ref[...] = jnp.zeros_like(dgkv_ref)

        cs_, sa_, sb_ = cs_ref[...], sa_ref[...], sb_ref[...]
        slab = jnp.zeros((tm, 128), F32)
        for hd in range(N_HEADS):
            lo = hd * HEAD_PAD
            dqp_ref[:, lo:lo + QK_NOPE] = (dq_ref[:, lo:lo + QK_NOPE] * ATTN_SCALE).astype(MXU_DTYPE)
            dqp_ref[:, lo + QK_NOPE:lo + HEAD_PAD] = _unrope(dq_ref[:, lo + QK_NOPE:lo + HEAD_PAD] * ATTN_SCALE, cs_, sa_, sb_).astype(MXU_DTYPE)
            dkv_ref[:, lo:lo + QK_NOPE] = dk_ref[:, lo:lo + QK_NOPE].astype(MXU_DTYPE)
            dkv_ref[:, lo + QK_NOPE:lo + HEAD_PAD] = dv_ref[:, hd * V_HEAD:(hd + 1) * V_HEAD].astype(MXU_DTYPE)
            slab = slab + dk_ref[:, lo + QK_NOPE:lo + HEAD_PAD]
        dcq = _dot_nt(dqp_ref[...], wuq_ref[...])
        dckv = _dot_nt(dkv_ref[...], wukv_ref[...])
        av = a_ref[...]
        daq, dgq = _rms_bwd(av[:, :Q_LORA], gq_ref[...], dcq)
        dakv, dgkv = _rms_bwd(av[:, Q_LORA:Q_LORA + KV_LORA], gkv_ref[...], dckv)
        da_ref[:, :Q_LORA] = daq.astype(MXU_DTYPE)
        da_ref[:, Q_LORA:Q_LORA + KV_LORA] = dakv.astype(MXU_DTYPE)
        da_ref[:, Q_LORA + KV_LORA:] = _unrope(slab, cs_, sa_, sb_).astype(MXU_DTYPE)
        dn = _dot_nt(da_ref[...], wa_ref[...])
        dx, dg = _rms_bwd(h_ref[...], g_ref[...], dn)
        dh_ref[...] = do_ref[...] + dx
        dg_ref[...] += dg
        dgq_ref[...] += dgq
        dgkv_ref[...] += dgkv

    def const(n):
        return pl.BlockSpec((1, n), lambda i: (0, 0))

    return pl.pallas_call(
        body,
        name="mla_bwd_pre",
        grid=(t // tm,),
        in_specs=[
            _rows(tm, qw), _rows(tm, qw), _rows(tm, vw), _rows(tm, A_PAD), _rows(tm, D_MODEL), _rows(tm, D_MODEL),
            _resident((1, D_MODEL)), _resident((1, Q_LORA)), _resident((1, KV_LORA)),
            _resident((D_MODEL, A_PAD)), _resident((Q_LORA, qw)), _resident((KV_LORA, qw)),
            _rows(tm, 128), _rows(tm, 128), _rows(tm, 128),
        ],
        out_specs=[_rows(tm, D_MODEL), _rows(tm, qw), _rows(tm, qw), _rows(tm, A_PAD),
                   const(D_MODEL), const(Q_LORA), const(KV_LORA)],
        out_shape=[
            jax.ShapeDtypeStruct((t, D_MODEL), F32),
            jax.ShapeDtypeStruct((t, qw), MXU_DTYPE),
            jax.ShapeDtypeStruct((t, qw), MXU_DTYPE),
            jax.ShapeDtypeStruct((t, A_PAD), MXU_DTYPE),
            jax.ShapeDtypeStruct((1, D_MODEL), F32),
            jax.ShapeDtypeStruct((1, Q_LORA), F32),
            jax.ShapeDtypeStruct((1, KV_LORA), F32),
        ],
        compiler_params=_cparams("arbitrary"),
    )(dq, dk, dv, a, h, dout, g, g_q, g_kv, w_a, w_uq, w_ukv, cs, sa, sb)


def _loss_head(h, g, target):
    t = h.shape[0]
    tm = min(1024, t)

    def body(h_ref, g_ref, t_ref, sq_ref, dh_ref, dg_ref):
        @pl.when(pl.program_id(0) == 0)
        def _():
            sq_ref[...] = jnp.zeros_like(sq_ref)
            dg_ref[...] = jnp.zeros_like(dg_ref)

        x = h_ref[...]
        err = _rms(x, g_ref[...]) - t_ref[...]
        sq_ref[...] += jnp.sum(err * err, axis=0, keepdims=True)
        dx, dg = _rms_bwd(x, g_ref[...], err * (1.0 / D_MODEL))
        dh_ref[...] = dx
        dg_ref[...] += dg

    return pl.pallas_call(
        body,
        name="loss_head",
        grid=(t // tm,),
        in_specs=[_rows(tm, D_MODEL), _resident((1, D_MODEL)), _rows(tm, D_MODEL)],
        out_specs=[pl.BlockSpec((1, D_MODEL), lambda i: (0, 0)), _rows(tm, D_MODEL), pl.BlockSpec((1, D_MODEL), lambda i: (0, 0))],
        out_shape=[jax.ShapeDtypeStruct((1, D_MODEL), F32), jax.ShapeDtypeStruct((t, D_MODEL), F32), jax.ShapeDtypeStruct((1, D_MODEL), F32)],
        compiler_params=_cparams("arbitrary"),
    )(h, g, target)


def _adamw(parts, w, m, v, name, tr, layer=0, row_offset=0, into=None):
    layers, rows, width = w.shape
    pwidth = parts.shape[-1]
    off = row_offset // tr

    def body(p_ref, w_ref, m_ref, v_ref, *rest):
        g_ref, d_ref, mo_ref, vo_ref = rest[-4:]

        def part(q):
            return p_ref[q, :, pl.ds(0, width)].astype(F32)

        g = part(0)
        for q in range(1, N_DEV):
            g = g + part(q)
        g_ref[...] = g
        m_new = ADAM_B1 * m_ref[...] + (1.0 - ADAM_B1) * g
        v_new = ADAM_B2 * v_ref[...] + (1.0 - ADAM_B2) * (g * g)
        m_hat = m_new / (1.0 - ADAM_B1 ** ADAM_STEP)
        v_hat = v_new / (1.0 - ADAM_B2 ** ADAM_STEP)
        d_ref[...] = -ADAM_LR * (m_hat / (jnp.sqrt(v_hat) + ADAM_EPS) + ADAM_WD * w_ref[...])
        mo_ref[...] = m_new
        vo_ref[...] = v_new

    blk = pl.BlockSpec((None, tr, width), lambda i: (layer, i, 0))
    earlier = () if into is None else tuple(into)
    return pl.pallas_call(
        body,
        name=name,
        grid=(rows // tr,),
        in_specs=[pl.BlockSpec((N_DEV, tr, pwidth), lambda i: (0, off + i, 0)), blk, blk, blk] + [pl.BlockSpec(memory_space=pl.ANY)] * len(earlier),
        out_specs=[blk, blk, blk, blk],
        out_shape=[jax.ShapeDtypeStruct((layers, rows, width), F32)] * 4,
        input_output_aliases={4 + j: j for j in range(len(earlier))},
        compiler_params=_cparams("parallel"),
    )(parts, w, m, v, *earlier)


_GAINS = ("ffn_norm1", "mix_norm", "ffn_norm2", "conv_norm", "final_norm")
_TINY = ("conv_w_dw", "mla_q_norm", "mla_kv_norm")


def _small_pack(parts):
    flat = jnp.concatenate([p.reshape(-1).astype(F32) for p in parts])
    return jnp.pad(flat, (0, SMALL_ROWS * D_MODEL - flat.shape[0])).reshape(SMALL_ROWS, D_MODEL)


def _small_unpack(packed, shapes):
    flat, out, off = packed.reshape(-1), [], 0
    for s in shapes:
        size = 1
        for d in s:
            size *= d
        out.append(flat[off:off + size].reshape(s))
        off += size
    return out


def _pad_to(a, axis, size):
    pad = [(0, 0)] * a.ndim
    pad[axis] = (0, size - a.shape[axis])
    return jnp.pad(a, pad)


def kernel(x, positions, ffn_norm1, ffn1_w1, ffn1_w3, ffn1_w2, mix_norm, ffn_norm2, ffn2_w1, ffn2_w3, ffn2_w2, conv_w_pw1, conv_w_dw, conv_norm, conv_w_pw2, mla_w_a, mla_q_norm, mla_kv_norm, mla_w_uq, mla_w_ukv, mla_w_o, final_norm, loss_target, m_ffn_norm1, m_ffn1_w1, m_ffn1_w3, m_ffn1_w2, m_mix_norm, m_ffn_norm2, m_ffn2_w1, m_ffn2_w3, m_ffn2_w2, m_conv_w_pw1, m_conv_w_dw, m_conv_norm, m_conv_w_pw2, m_mla_w_a, m_mla_q_norm, m_mla_kv_norm, m_mla_w_uq, m_mla_w_ukv, m_mla_w_o, m_final_norm, v_ffn_norm1, v_ffn1_w1, v_ffn1_w3, v_ffn1_w2, v_mix_norm, v_ffn_norm2, v_ffn2_w1, v_ffn2_w3, v_ffn2_w2, v_conv_w_pw1, v_conv_w_dw, v_conv_norm, v_conv_w_pw2, v_mla_w_a, v_mla_q_norm, v_mla_kv_norm, v_mla_w_uq, v_mla_w_ukv, v_mla_w_o, v_final_norm):
    args = dict(locals())
    wire = lambda a: a.astype(WIRE_DTYPE)

    def ffn_shards(prefix, layer):
        w13_shard = _pad_to(jnp.stack([args[prefix + "_w1"][layer], args[prefix + "_w3"][layer]]), 2, FF_SHARD_PAD)
        return [(wire(w13_shard), 1), (wire(_pad_to(args[prefix + "_w2"][layer], 0, FF_SHARD_PAD)), 2)]

    def uq_pad(a):
        return _pad_to(a, 3, HEAD_PAD).reshape(1, a.shape[1], N_HEADS * HEAD_PAD)

    ukv_rows = lambda a: a.reshape(1, a.shape[1], N_HEADS * HEAD_PAD)
    tiny_shapes = [conv_w_dw.shape, mla_q_norm.shape, mla_kv_norm.shape]

    first = _exchange(*_gather_streams(ffn_shards("ffn1", 0)), "weight_gather_first")
    later_groups = [
        [(wire(conv_w_pw1[0]), 1), (wire(conv_w_pw2[0]), 2), (_small_pack([conv_w_dw, mla_q_norm, mla_kv_norm]), None)],
        ffn_shards("ffn2", 0),
        ffn_shards("ffn1", 1),
        [(wire(_pad_to(mla_w_a[0], 1, A_PAD)), 2), (wire(uq_pad(mla_w_uq)[0]), 2), (wire(ukv_rows(mla_w_ukv)[0]), 2), (wire(mla_w_o[0]), 2)],
        ffn_shards("ffn2", 1),
    ]
    later_streams, later_outs = _gather_streams([sh for grp in later_groups for sh in grp])
    group_ids, at = [], 0
    for grp in later_groups:
        group_ids.append(list(range(at, at + len(grp))))
        at += len(grp)
    gather, gather_token = _exchange_start(
        later_streams, _landing(later_streams, later_outs), group_ids, (first[0],), "weight_gather_start")

    inv_freq = ROPE_THETA ** (-2.0 * jnp.arange(QK_ROPE // 2, dtype=F32) / QK_ROPE)
    ang = positions[0].astype(F32)[:, None] * inv_freq
    cos, sin, zero = jnp.cos(ang), jnp.sin(ang), jnp.zeros_like(ang)
    cs = jnp.concatenate([cos, cos, zero, zero], axis=1)
    sa = jnp.concatenate([sin, zero, zero, zero], axis=1)
    sb = jnp.concatenate([zero, sin, zero, zero], axis=1)

    ffn_w = {("ffn1", 0): first}
    h0 = x[0]
    h1, n1, a1, b1 = _ffn_fwd(h0, ffn_norm1[0:1], *ffn_w[("ffn1", 0)], after=(gather_token,))
    w_pw1, w_pw2, tiny = _exchange_wait(gather, 0, (h1,), "weight_gather_wait_conv")
    tiny = [_small_unpack(tiny[q], tiny_shapes) for q in range(N_DEV)]
    w_dw = jnp.concatenate([t_[0][0] for t_ in tiny], axis=1)
    g_q = jnp.concatenate([t_[1] for t_ in tiny], axis=1)
    g_kv = jnp.concatenate([t_[2] for t_ in tiny], axis=1)
    n_c, u, glu = _conv_pre(h1, mix_norm[0:1], w_pw1)
    h2, c, sw = _conv_main(glu, w_dw, conv_norm, w_pw2, h1)
    ffn_w[("ffn2", 0)] = _exchange_wait(gather, 1, (h2,), "weight_gather_wait_ffn2_0")
    h3, n2, a2, b2 = _ffn_fwd(h2, ffn_norm2[0:1], *ffn_w[("ffn2", 0)])
    ffn_w[("ffn1", 1)] = _exchange_wait(gather, 2, (h3,), "weight_gather_wait_ffn1_1")
    h4, n3, a3, b3 = _ffn_fwd(h3, ffn_norm1[1:2], *ffn_w[("ffn1", 1)])
    w_a, w_uq, w_ukv, w_o = _exchange_wait(gather, 3, (h4,), "weight_gather_wait_mla")
    n_a, a_lat, cq, ckv, q, k, v = _mla_pre(h4, mix_norm[1:2], w_a, g_q, g_kv, w_uq, w_ukv, cs, sa, sb)
    o, lse = _flash_fwd(q, k, v)
    h5 = _attn_out(o, w_o, h4)
    ffn_w[("ffn2", 1)] = _exchange_wait(gather, 4, (h5,), "weight_gather_wait_ffn2_1")
    h6, n4, a4, b4 = _ffn_fwd(h5, ffn_norm2[1:2], *ffn_w[("ffn2", 1)])

    sq, dh, dg_final = _loss_head(h6, final_norm[None, :], loss_target[0])
    loss = lax.psum(0.5 / D_MODEL * jnp.sum(sq), ("x", "y", "c"))

    gain = {}

    def ffn_backward(prefix, norm, layer, dh, h_in, n, a, b, after=()):
        dh, dob, s, dab, gain[(norm, layer)] = _ffn_bwd(dh, h_in, args[norm][layer:layer + 1], a, b, *ffn_w[(prefix, layer)], after=after)
        return dh, [(_wgrad(n, dab, "wgrad_ffn_in"), 1), (_wgrad(s, dob, "wgrad_ffn_out"), 2)]

    dh, g_ffn2_1 = ffn_backward("ffn2", "ffn_norm2", 1, dh, h5, n4, a4, b4)
    do, delta = _attn_out_bwd(dh, o, w_o)
    d_o = _wgrad(o, dh, "wgrad_attn_out")
    dq, dk, dv = _flash_bwd(q, k, v, do, lse, delta)
    dh, dqp, dkv, da_lat, gain[("mix_norm", 1)], d_gq, d_gkv = _mla_bwd_pre(
        dq, dk, dv, a_lat, h4, dh, mix_norm[1:2], g_q, g_kv, w_a, w_uq, w_ukv, cs, sa, sb)
    g_mla = [(_wgrad(n_a, da_lat, "wgrad_mla_a"), 2), (_wgrad(cq, dqp, "wgrad_mla_uq"), 2), (_wgrad(ckv, dkv, "wgrad_mla_ukv"), 2), (d_o, 2)]
    dh, g_ffn1_1 = ffn_backward("ffn1", "ffn_norm1", 1, dh, h3, n3, a3, b3)
    streams_a, outs_a = _scatter_streams(g_ffn2_1 + g_mla + g_ffn1_1)
    scatter_a, token_a = _exchange_start(
        streams_a, _landing(streams_a, outs_a), [list(range(len(streams_a)))], (), "grad_scatter_start_layer1")

    dh, g_ffn2_0 = ffn_backward("ffn2", "ffn_norm2", 0, dh, h2, n2, a2, b2, after=(token_a,))
    dc, gain[("conv_norm", 0)] = _conv_bwd_post(dh, c, conv_norm, w_pw2)
    d_pw2 = _wgrad(sw, dh, "wgrad_conv_pw2")
    dh, du, d_dw, gain[("mix_norm", 0)] = _conv_bwd_pre(dc, glu, u, w_dw, w_pw1, h1, mix_norm[0:1], dh)
    d_pw1 = _wgrad(n_c, du, "wgrad_conv_pw1")
    streams_b, outs_b = _scatter_streams(g_ffn2_0 + [(d_pw1, 1), (d_pw2, 2)])
    scatter_b, token_b = _exchange_start(
        streams_b, _landing(streams_b, outs_b), [list(range(len(streams_b)))], (), "grad_scatter_start_conv")

    dh, dob, s_act, dab, gain[("ffn_norm1", 0)] = _ffn_bwd(dh, h0, ffn_norm1[0:1], a1, b1, *ffn_w[("ffn1", 0)], after=(token_b,))
    grad_x = dh[None]
    streams_d, outs_d = _scatter_streams([(_wgrad(s_act, dob, "wgrad_ffn_out"), 2)])
    scatter_d, token_d = _exchange_start(streams_d, _landing(streams_d, outs_d), [[0]], (), "grad_scatter_start_last_out")
    streams_e, outs_e = _scatter_streams([(_wgrad(n1, dab, "wgrad_ffn_in_half", after=(token_d,), only=0), 1)])
    scatter_e, token_e = _exchange_start(streams_e, _landing(streams_e, outs_e), [[0]], (), "grad_scatter_start_last_w1")
    g_ffn1_0 = [(_wgrad(n1, dab, "wgrad_ffn_in_half", after=(token_e,), only=1), 1)]

    gain_rows = jnp.concatenate([
        gain[("ffn_norm1", 0)], gain[("ffn_norm1", 1)], gain[("mix_norm", 0)], gain[("mix_norm", 1)],
        gain[("ffn_norm2", 0)], gain[("ffn_norm2", 1)], gain[("conv_norm", 0)], dg_final])
    dw_by_dest = jnp.moveaxis(d_dw[:CONV_WIDTH].reshape(CONV_WIDTH, N_DEV, -1), 1, 0)
    small_by_dest = jnp.stack([
        _small_pack([gain_rows, dw_by_dest[p], d_gq.reshape(N_DEV, -1)[p], d_gkv.reshape(N_DEV, -1)[p]]) for p in range(N_DEV)])
    streams_c, outs_c = _scatter_streams(g_ffn1_0)
    streams_c.append((small_by_dest, len(outs_c), _slot, _slot, lambda dst, src, me: _place_slot(dst, lax.dynamic_index_in_dim(src, me, 0, False), me)))
    outs_c.append(jax.ShapeDtypeStruct((N_DEV, SMALL_ROWS, D_MODEL), F32))
    scatter_c, token_c = _exchange_start(streams_c, _landing(streams_c, outs_c), [list(range(len(streams_c)))], (), "grad_scatter_start_last")

    (p13_ffn2_1, p2_ffn2_1, p_a, p_uq, p_ukv, p_o, p13_ffn1_1, p2_ffn1_1) = _exchange_wait(scatter_a, 0, (token_c,), "grad_scatter_wait_layer1")
    p13_ffn2_0, p2_ffn2_0, p_pw1, p_pw2 = _exchange_wait(scatter_b, 0, (token_c,), "grad_scatter_wait_conv")
    results = {}

    def adam(name, parts, tr, view=lambda a: a, **kw):
        results[name] = _adamw(parts, view(args[name]), view(args["m_" + name]), view(args["v_" + name]), "adamw_" + name, tr,
                               into=results.get(name), **kw)

    def adam_ffn(prefix, layer, p13, p2):
        p13 = p13.reshape(N_DEV, 2 * D_MODEL, FF_SHARD_PAD)
        adam(prefix + "_w1", p13, 256, layer=layer)
        adam(prefix + "_w3", p13, 256, layer=layer, row_offset=D_MODEL)
        adam(prefix + "_w2", p2, FF_SHARD, layer=layer)

    adam_ffn("ffn2", 0, p13_ffn2_0, p2_ffn2_0)
    adam_ffn("ffn2", 1, p13_ffn2_1, p2_ffn2_1)
    adam_ffn("ffn1", 1, p13_ffn1_1, p2_ffn1_1)
    adam("conv_w_pw1", p_pw1, 256)
    adam("conv_w_pw2", p_pw2, 128)
    adam("mla_w_a", p_a, 128)
    adam("mla_w_uq", p_uq, 64, view=uq_pad)
    adam("mla_w_ukv", p_ukv, 32, view=ukv_rows)
    adam("mla_w_o", p_o, 128)
    results["mla_w_uq"] = [r.reshape(1, -1, N_HEADS, HEAD_PAD)[..., :QK_NOPE + QK_ROPE] for r in results["mla_w_uq"]]
    results["mla_w_ukv"] = [r.reshape(mla_w_ukv.shape) for r in results["mla_w_ukv"]]

    (p2_ffn1_0,) = _exchange_wait(scatter_d, 0, (results["mla_w_o"][0],), "grad_scatter_wait_last_out")
    (p1_ffn1_0,) = _exchange_wait(scatter_e, 0, (p2_ffn1_0,), "grad_scatter_wait_last_w1")
    p3_ffn1_0, p_small = _exchange_wait(scatter_c, 0, (p1_ffn1_0,), "grad_scatter_wait_last")
    adam("ffn1_w1", p1_ffn1_0, 256, layer=0)
    adam("ffn1_w3", p3_ffn1_0, 256, layer=0)
    adam("ffn1_w2", p2_ffn1_0, FF_SHARD, layer=0)
    small_names = _GAINS + _TINY
    small_shapes = [args[n].shape for n in small_names]
    pack_small = lambda prefix: _small_pack([args[prefix + n] for n in small_names])[None]
    small = _adamw(p_small, pack_small(""), pack_small("m_"), pack_small("v_"), "adamw_small", SMALL_ROWS)
    for kind in range(4):
        for n, leaf in zip(small_names, _small_unpack(small[kind][0], small_shapes)):
            results.setdefault(n, [None] * 4)[kind] = leaf

    order = ("ffn_norm1", "ffn1_w1", "ffn1_w3", "ffn1_w2", "mix_norm", "ffn_norm2", "ffn2_w1", "ffn2_w3", "ffn2_w2",
             "conv_w_pw1", "conv_w_dw", "conv_norm", "conv_w_pw2", "mla_w_a", "mla_q_norm", "mla_kv_norm",
             "mla_w_uq", "mla_w_ukv", "mla_w_o", "final_norm")
    outputs = [loss, grad_x]
    for kind in range(4):
        outputs.extend(results[n][kind] for n in order)
    return tuple(outputs)
```

```python
import jax
import jax.numpy as jnp
from jax import lax
from jax.experimental import pallas as pl
from jax.experimental.pallas import tpu as pltpu

F32 = jnp.float32
MXU_DTYPE = jnp.bfloat16
WIRE_DTYPE = jnp.bfloat16

N_DEV = 8
D_MODEL = 1024
FF_SHARD = 352
FF_SHARD_PAD = 384
D_FF_PAD = N_DEV * FF_SHARD_PAD
N_HEADS = 8
QK_NOPE = 128
QK_ROPE = 64
V_HEAD = 128
Q_LORA = 512
KV_LORA = 256
HEAD_PAD = 256
A_WIDTH = Q_LORA + KV_LORA + QK_ROPE
A_PAD = Q_LORA + KV_LORA + 128
CONV_WIDTH = 31
CONV_HALO = 32
CONV_ROWS = 16
CHUNK_SHIFT = 6
ROPE_THETA = 10000.0
RMS_EPS = 1e-6
ATTN_SCALE = (QK_NOPE + QK_ROPE) ** -0.5
FFN_RES_WEIGHT = 0.5
ADAM_LR = 0.001
ADAM_B1 = 0.9
ADAM_B2 = 0.999
ADAM_EPS = 1e-08
ADAM_WD = 0.01
ADAM_STEP = 10

SMALL_ROWS = 16
VMEM_LIMIT = 56 << 20


def _cparams(*sem):
    return pltpu.CompilerParams(dimension_semantics=sem, vmem_limit_bytes=VMEM_LIMIT)


def _dot(a, b):
    return lax.dot_general(a, b, (((1,), (0,)), ((), ())), preferred_element_type=F32)


def _dot_nt(a, b):
    return lax.dot_general(a, b, (((1,), (1,)), ((), ())), preferred_element_type=F32)


def _dot_tn(a, b):
    return lax.dot_general(a, b, (((0,), (0,)), ((), ())), preferred_element_type=F32)


def _resident(shape, index=None):
    fixed = index if index is not None else (0,) * len(shape)
    return pl.BlockSpec(shape, lambda *_: fixed, pipeline_mode=pl.Buffered(1))


def _rows(tm, n):
    return pl.BlockSpec((tm, n), lambda t, *_: (t, 0))


def _rms(x, g):
    r = lax.rsqrt(jnp.mean(x * x, axis=-1, keepdims=True) + RMS_EPS)
    return x * r * g


def _rms_bwd(x, g, dy):
    r = lax.rsqrt(jnp.mean(x * x, axis=-1, keepdims=True) + RMS_EPS)
    xr = x * r
    dg = jnp.sum(dy * xr, axis=0, keepdims=True)
    u = dy * g
    dx = r * (u - xr * jnp.mean(u * xr, axis=-1, keepdims=True))
    return dx, dg


def _sigmoid(x):
    return 1.0 / (1.0 + jnp.exp(-x))


def _me():
    return 4 * lax.axis_index("x") + 2 * lax.axis_index("y") + lax.axis_index("c")


def _peer(k):
    px, py, pc = lax.axis_index("x") ^ ((k >> 2) & 1), lax.axis_index("y") ^ ((k >> 1) & 1), lax.axis_index("c") ^ (k & 1)
    return (px, py, pc), 4 * px + 2 * py + pc


def _stream_copies(stream, src_ref, dst_ref, send_sem, recv_sem, base):
    send, land = stream[2], stream[3]
    me = _me()
    outgoing, incoming = [], []
    for k in range(1, N_DEV):
        pos, p = _peer(k)
        common = dict(send_sem=send_sem.at[base + k - 1], recv_sem=recv_sem.at[base + k - 1], device_id=pos, device_id_type=pl.DeviceIdType.MESH)
        outgoing.append(pltpu.make_async_remote_copy(src_ref=send(src_ref, p), dst_ref=land(dst_ref, me), **common))
        incoming.append(pltpu.make_async_remote_copy(src_ref=send(src_ref, p), dst_ref=land(dst_ref, p), **common))
    return outgoing, incoming


_HBM = pl.BlockSpec(memory_space=pltpu.HBM)
_SEM = pl.BlockSpec(memory_space=pltpu.SEMAPHORE)


_OTHER_CHIPS = (2, 4, 6)


def _gather_two_level(streams, outs, name):
    n = len(streams)
    slots = N_DEV - 1

    def body(*refs):
        srcs, dsts = refs[:n], refs[n:2 * n]
        send_sem, recv_sem, local_sem = refs[2 * n:]
        me = _me()
        sibling_pos, sibling = _peer(1)

        def block(e, q):
            return streams[e][3](dsts[e], q)

        def copy(e, slot, src_ref, q, to):
            return pltpu.make_async_remote_copy(src_ref=src_ref, dst_ref=block(e, q), send_sem=send_sem.at[e * slots + slot],
                                                recv_sem=recv_sem.at[e * slots + slot], device_id=to, device_id_type=pl.DeviceIdType.MESH)

        own = [pltpu.make_async_copy(srcs[e], block(e, me), local_sem.at[e]) for e in range(n)]
        for cp in own:
            cp.start()
        first = []
        for e in range(n):
            first.append(copy(e, 0, srcs[e], me, sibling_pos))
            for j, k in enumerate(_OTHER_CHIPS):
                first.append(copy(e, 1 + j, srcs[e], me, _peer(k)[0]))
        for cp in first:
            cp.start()
        passed = []
        for e in range(n):
            for j, k in enumerate(_OTHER_CHIPS):
                pos, p = _peer(k)
                copy(e, 1 + j, srcs[e], p, pos).wait_recv()
                passed.append(copy(e, 4 + j, block(e, p), p, sibling_pos))
                passed[-1].start()
        for e in range(n):
            copy(e, 0, srcs[e], sibling, sibling_pos).wait_recv()
            for j, k in enumerate(_OTHER_CHIPS):
                p = _peer(k ^ 1)[1]
                copy(e, 4 + j, block(e, p), p, sibling_pos).wait_recv()
        for cp in first + passed:
            cp.wait_send()
        for cp in own:
            cp.wait()

    return pl.pallas_call(
        body,
        name=name,
        out_shape=tuple(outs),
        in_specs=[_HBM] * n,
        out_specs=tuple([_HBM] * n),
        scratch_shapes=[pltpu.SemaphoreType.DMA((n * slots,)), pltpu.SemaphoreType.DMA((n * slots,)), pltpu.SemaphoreType.DMA((n,))],
        compiler_params=pltpu.CompilerParams(vmem_limit_bytes=VMEM_LIMIT),
    )(*[st[0] for st in streams])


def _landing(streams, outs):
    me = _me()
    lands = [lax.empty(o.shape, o.dtype) for o in outs]
    for st in streams:
        lands[st[1]] = st[4](lands[st[1]], st[0], me)
    return lands


def _exchange_start(streams, lands, groups, after, name):
    n, n_land, n_grp = len(streams), len(lands), len(groups)
    side_effects = pltpu.SideEffectType.DATAFLOW_SIDE_EFFECTING

    def body(*refs):
        srcs, dsts = refs[:n], refs[n:n + n_land]
        outs = refs[n + n_land + len(after):]
        send_sems, recv_sems, token = outs[:n_grp], outs[n_grp:2 * n_grp], outs[-1]
        for gi, group in enumerate(groups):
            for j, e in enumerate(group):
                outgoing, _ = _stream_copies(streams[e], srcs[e], dsts[streams[e][1]], send_sems[gi], recv_sems[gi], j * (N_DEV - 1))
                for cp in outgoing:
                    cp.start()
        token[...] = jnp.zeros_like(token)

    sems = tuple(pltpu.SemaphoreType.DMA((len(g) * (N_DEV - 1),)) for g in groups)
    passed = [st[0] for st in streams] + list(lands)
    res = pl.pallas_call(
        body,
        name=name,
        out_shape=sems + sems + tuple(pltpu.HBM(a.shape, a.dtype) for a in passed) + (jax.ShapeDtypeStruct((8, 128), F32),),
        in_specs=[_HBM] * (n + n_land) + [pl.BlockSpec(memory_space=pl.ANY)] * len(after),
        out_specs=tuple([_SEM] * (2 * n_grp) + [_HBM] * (n + n_land) + [pl.BlockSpec(memory_space=pltpu.VMEM)]),
        input_output_aliases={i: 2 * n_grp + i for i in range(n + n_land)},
        compiler_params=pltpu.CompilerParams(has_side_effects=side_effects),
    )(*passed, *after)
    handle = dict(streams=streams, groups=groups, send=res[:n_grp], recv=res[n_grp:2 * n_grp],
                  srcs=res[2 * n_grp:2 * n_grp + n], lands=res[2 * n_grp + n:2 * n_grp + n + n_land])
    return handle, res[-1]


def _exchange_wait(handle, gi, after, name):
    streams, group = handle["streams"], handle["groups"][gi]
    land_ids = sorted({streams[e][1] for e in group})
    srcs = [handle["srcs"][e] for e in group]
    lands = [handle["lands"][i] for i in land_ids]
    n, n_land = len(srcs), len(lands)
    side_effects = pltpu.SideEffectType.DATAFLOW_SIDE_EFFECTING

    def body(*refs):
        src_refs, land_refs = refs[:n], refs[n:n + n_land]
        send_sem, recv_sem = refs[n + n_land:n + n_land + 2]
        for j, e in enumerate(group):
            outgoing, incoming = _stream_copies(streams[e], src_refs[j], land_refs[land_ids.index(streams[e][1])], send_sem, recv_sem, j * (N_DEV - 1))
            for cp in outgoing:
                cp.wait_send()
            for cp in incoming:
                cp.wait_recv()

    res = pl.pallas_call(
        body,
        name=name,
        out_shape=tuple(pltpu.HBM(a.shape, a.dtype) for a in lands),
        in_specs=[_HBM] * (n + n_land) + [_SEM, _SEM] + [pl.BlockSpec(memory_space=pl.ANY)] * len(after),
        out_specs=tuple([_HBM] * n_land),
        input_output_aliases={n + i: i for i in range(n_land)},
        compiler_params=pltpu.CompilerParams(has_side_effects=side_effects),
    )(*srcs, *lands, handle["send"][gi], handle["recv"][gi], *after)
    return list(res)


def _whole(ref, _):
    return ref


def _slot(ref, q):
    return ref.at[q]


def _lane_block(width):
    def pick(ref, q):
        idx = (slice(None),) * (len(ref.shape) - 1) + (pl.ds(pl.multiple_of(q * width, 128), width),)
        return ref.at[idx]
    return pick


def _row_block(height):
    def pick(ref, q):
        idx = (slice(None),) * (len(ref.shape) - 2) + (pl.ds(pl.multiple_of(q * height, 16), height), slice(None))
        return ref.at[idx]
    return pick


def _block_of(axis_from_end, size):
    return _lane_block(size) if axis_from_end == 1 else _row_block(size)


def _gather_streams(shards):
    streams, outs = [], []
    for i, (shard, axis) in enumerate(shards):
        shape = list(shard.shape)
        if axis is None:
            shape, land = [N_DEV] + shape, _slot
            place = _place_slot
        else:
            size = shape[-axis]
            land = _block_of(axis, size)
            shape[-axis] *= N_DEV
            place = lambda dst, src, me, size=size, axis=axis: lax.dynamic_update_slice_in_dim(dst, src, me * size, dst.ndim - axis)
        streams.append((shard, i, _whole, land, place))
        outs.append(jax.ShapeDtypeStruct(tuple(shape), shard.dtype))
    return streams, outs


def _place_slot(dst, src, me):
    return lax.dynamic_update_index_in_dim(dst, src, me, 0)


def _scatter_streams(fulls):
    streams, outs = [], []
    for i, (full, axis) in enumerate(fulls):
        shape = list(full.shape)
        size = shape[-axis] // N_DEV
        shape[-axis] = size
        place = lambda dst, src, me, size=size, axis=axis: _place_slot(
            dst, lax.dynamic_slice_in_dim(src, me * size, size, src.ndim - axis), me)
        streams.append((full, i, _block_of(axis, size), _slot, place))
        outs.append(jax.ShapeDtypeStruct((N_DEV,) + tuple(shape), full.dtype))
    return streams, outs


FFN_TM = 512
FFN_TF = 1024
FFN_FWD_TM = 1024
FFN_FWD_TF = 1024


def _ffn_fwd(h, g, w13, w2, after=()):
    t = h.shape[0]
    tm, tf = min(FFN_FWD_TM, t), FFN_FWD_TF
    nf = D_FF_PAD // tf

    def body(h_ref, g_ref, w1_ref, w3_ref, w2_ref, *rest):
        ho_ref, n_ref, a_ref, b_ref, n_scr, acc = rest[len(after):]
        f = pl.program_id(1)

        @pl.when(f == 0)
        def _():
            n_scr[...] = _rms(h_ref[...], g_ref[...]).astype(MXU_DTYPE)
            n_ref[...] = n_scr[...]
            acc[...] = jnp.zeros_like(acc)

        n = n_scr[...]
        a = _dot(n, w1_ref[...])
        b = _dot(n, w3_ref[...])
        a_ref[...] = a.astype(MXU_DTYPE)
        b_ref[...] = b.astype(MXU_DTYPE)
        s = (a * _sigmoid(a)) * b
        acc[...] += _dot(s.astype(MXU_DTYPE), w2_ref[...])

        @pl.when(f == nf - 1)
        def _():
            ho_ref[...] = h_ref[...] + FFN_RES_WEIGHT * acc[...]

    return pl.pallas_call(
        body,
        name="ffn_fwd",
        grid=(t // tm, nf),
        in_specs=[
            _rows(tm, D_MODEL),
            _resident((1, D_MODEL)),
            pl.BlockSpec((None, D_MODEL, tf), lambda i, f: (0, 0, f)),
            pl.BlockSpec((None, D_MODEL, tf), lambda i, f: (1, 0, f)),
            pl.BlockSpec((tf, D_MODEL), lambda i, f: (f, 0)),
        ] + [pl.BlockSpec(memory_space=pl.ANY)] * len(after),
        out_specs=[
            _rows(tm, D_MODEL),
            _rows(tm, D_MODEL),
            pl.BlockSpec((tm, tf), lambda i, f: (i, f)),
            pl.BlockSpec((tm, tf), lambda i, f: (i, f)),
        ],
        out_shape=[
            jax.ShapeDtypeStruct((t, D_MODEL), F32),
            jax.ShapeDtypeStruct((t, D_MODEL), MXU_DTYPE),
            jax.ShapeDtypeStruct((t, D_FF_PAD), MXU_DTYPE),
            jax.ShapeDtypeStruct((t, D_FF_PAD), MXU_DTYPE),
        ],
        scratch_shapes=[pltpu.VMEM((tm, D_MODEL), MXU_DTYPE), pltpu.VMEM((tm, D_MODEL), F32)],
        compiler_params=_cparams("parallel", "arbitrary"),
    )(h, g, w13, w13, w2, *after)


def _ffn_bwd(dout, h, g, a, b, w13, w2, after=()):
    t = h.shape[0]
    tm, tf = min(FFN_TM, t), FFN_TF
    nf = D_FF_PAD // tf

    def body(do_ref, h_ref, g_ref, a_ref, b_ref, w1_ref, w3_ref, w2_ref, *rest):
        dh_ref, dob_ref, s_ref, dab_ref, dg_ref, dob_scr, acc = rest[len(after):]
        i, f = pl.program_id(0), pl.program_id(1)

        @pl.when(f == 0)
        def _():
            dob_scr[...] = (FFN_RES_WEIGHT * do_ref[...]).astype(MXU_DTYPE)
            dob_ref[...] = dob_scr[...]
            acc[...] = jnp.zeros_like(acc)

        @pl.when((f == 0) & (i == 0))
        def _():
            dg_ref[...] = jnp.zeros_like(dg_ref)

        ds = _dot_nt(dob_scr[...], w2_ref[...])
        av = a_ref[...].astype(F32)
        bv = b_ref[...].astype(F32)
        sig = _sigmoid(av)
        sil = av * sig
        s_ref[...] = (sil * bv).astype(MXU_DTYPE)
        da = (ds * bv * (sig * (1.0 + av * (1.0 - sig)))).astype(MXU_DTYPE)
        db = (ds * sil).astype(MXU_DTYPE)
        dab_ref[0] = da
        dab_ref[1] = db
        acc[...] += _dot_nt(da, w1_ref[...]) + _dot_nt(db, w3_ref[...])

        @pl.when(f == nf - 1)
        def _():
            dx, dg = _rms_bwd(h_ref[...], g_ref[...], acc[...])
            dh_ref[...] = do_ref[...] + dx
            dg_ref[...] += dg

    return pl.pallas_call(
        body,
        name="ffn_bwd",
        grid=(t // tm, nf),
        in_specs=[
            _rows(tm, D_MODEL),
            _rows(tm, D_MODEL),
            _resident((1, D_MODEL)),
            pl.BlockSpec((tm, tf), lambda i, f: (i, f)),
            pl.BlockSpec((tm, tf), lambda i, f: (i, f)),
            pl.BlockSpec((None, D_MODEL, tf), lambda i, f: (0, 0, f)),
            pl.BlockSpec((None, D_MODEL, tf), lambda i, f: (1, 0, f)),
            pl.BlockSpec((tf, D_MODEL), lambda i, f: (f, 0)),
        ] + [pl.BlockSpec(memory_space=pl.ANY)] * len(after),
        out_specs=[
            _rows(tm, D_MODEL),
            _rows(tm, D_MODEL),
            pl.BlockSpec((tm, tf), lambda i, f: (i, f)),
            pl.BlockSpec((2, tm, tf), lambda i, f: (0, i, f)),
            pl.BlockSpec((1, D_MODEL), lambda i, f: (0, 0)),
        ],
        out_shape=[
            jax.ShapeDtypeStruct((t, D_MODEL), F32),
            jax.ShapeDtypeStruct((t, D_MODEL), MXU_DTYPE),
            jax.ShapeDtypeStruct((t, D_FF_PAD), MXU_DTYPE),
            jax.ShapeDtypeStruct((2, t, D_FF_PAD), MXU_DTYPE),
            jax.ShapeDtypeStruct((1, D_MODEL), F32),
        ],
        scratch_shapes=[pltpu.VMEM((tm, D_MODEL), MXU_DTYPE), pltpu.VMEM((tm, D_MODEL), F32)],
        compiler_params=_cparams("arbitrary", "arbitrary"),
    )(dout, h, g, a, b, w13, w13, w2, *after)


def _wgrad(x, y, name, out_dtype=WIRE_DTYPE, after=(), only=None):
    t, m = x.shape
    grouped = y.ndim == 3 and only is None
    groups = y.shape[0] if grouped else 1
    n = y.shape[-1]
    tk = min(2048, t)
    bm = m if m <= 1536 else m // 2
    bn = n if n <= 1536 else n // 2
    nk = t // tk

    def body(x_ref, y_ref, *rest):
        o_ref, acc = rest[len(after):]
        k = pl.program_id(3)

        @pl.when(k == 0)
        def _():
            acc[...] = jnp.zeros_like(acc)

        acc[...] += _dot_tn(x_ref[...].astype(MXU_DTYPE), y_ref[...].astype(MXU_DTYPE))

        @pl.when(k == nk - 1)
        def _():
            o_ref[...] = acc[...].astype(out_dtype)

    if grouped:
        y_spec = pl.BlockSpec((None, tk, bn), lambda g, i, j, k: (g, k, j))
        o_spec = pl.BlockSpec((None, bm, bn), lambda g, i, j, k: (g, i, j))
        o_shape = jax.ShapeDtypeStruct((groups, m, n), out_dtype)
    else:
        if only is None:
            y_spec = pl.BlockSpec((tk, bn), lambda g, i, j, k: (k, j))
        else:
            y_spec = pl.BlockSpec((None, tk, bn), lambda g, i, j, k: (only, k, j))
        o_spec = pl.BlockSpec((bm, bn), lambda g, i, j, k: (i, j))
        o_shape = jax.ShapeDtypeStruct((m, n), out_dtype)
    return pl.pallas_call(
        body,
        name=name,
        grid=(groups, m // bm, n // bn, nk),
        in_specs=[pl.BlockSpec((tk, bm), lambda g, i, j, k: (k, i)), y_spec] + [pl.BlockSpec(memory_space=pl.ANY)] * len(after),
        out_specs=o_spec,
        out_shape=o_shape,
        scratch_shapes=[pltpu.VMEM((bm, bn), F32)],
        compiler_params=_cparams("parallel", "parallel", "parallel", "arbitrary"),
    )(x, y, *after)


def _conv_pre(h, g, w_pw1):
    t = h.shape[0]
    tm = min(512, t)

    def body(h_ref, g_ref, w_ref, n_ref, u_ref, glu_ref):
        n = _rms(h_ref[...], g_ref[...]).astype(MXU_DTYPE)
        n_ref[...] = n
        u = _dot(n, w_ref[...])
        u_ref[...] = u
        glu_ref[...] = u[:, :D_MODEL] * _sigmoid(u[:, D_MODEL:])

    return pl.pallas_call(
        body,
        name="conv_pre",
        grid=(t // tm,),
        in_specs=[_rows(tm, D_MODEL), _resident((1, D_MODEL)), _resident((D_MODEL, 2 * D_MODEL))],
        out_specs=[_rows(tm, D_MODEL), _rows(tm, 2 * D_MODEL), _rows(tm, D_MODEL)],
        out_shape=[
            jax.ShapeDtypeStruct((t, D_MODEL), MXU_DTYPE),
            jax.ShapeDtypeStruct((t, 2 * D_MODEL), F32),
            jax.ShapeDtypeStruct((t, D_MODEL), F32),
        ],
        compiler_params=_cparams("parallel"),
    )(h, g, w_pw1)


def _shifted_copies(dst, src, rows):
    for r in range(1, 8):
        dst[r - 1] = src[pl.ds(r, rows), :]


def _window(src, shifted, offset, base, rows):
    r = offset % 8
    start = base + offset - r
    return src[pl.ds(start, rows), :] if r == 0 else shifted[r - 1, pl.ds(start, rows), :]


def _conv_main(glu, w_dw, g, w_pw2, h):
    t = h.shape[0]
    tm = min(512, t)
    per = tm // CONV_HALO
    shifted_rows = tm + CONV_HALO - 8

    def body(glu_ref, halo_ref, w_ref, g_ref, w2_ref, h_ref, ho_ref, c_ref, sw_ref, ext, ext_sh):
        i = pl.program_id(0)
        ext[pl.ds(0, CONV_HALO), :] = jnp.where(i == 0, 0.0, halo_ref[...])
        ext[pl.ds(CONV_HALO, tm), :] = glu_ref[...]
        _shifted_copies(ext_sh, ext, shifted_rows)
        for base in range(0, tm, CONV_ROWS):
            acc = jnp.zeros((CONV_ROWS, D_MODEL), F32)
            for k in range(CONV_WIDTH):
                acc = acc + _window(ext, ext_sh, CONV_HALO - (CONV_WIDTH - 1) + k, base, CONV_ROWS) * w_ref[pl.ds(k, 1), :]
            c_ref[pl.ds(base, CONV_ROWS), :] = acc
        y = _rms(c_ref[...], g_ref[...])
        sw = (y * _sigmoid(y)).astype(MXU_DTYPE)
        sw_ref[...] = sw
        ho_ref[...] = h_ref[...] + _dot(sw, w2_ref[...])

    return pl.pallas_call(
        body,
        name="conv_main",
        grid=(t // tm,),
        in_specs=[
            _rows(tm, D_MODEL),
            pl.BlockSpec((CONV_HALO, D_MODEL), lambda i: (jnp.maximum(i * per - 1, 0), 0)),
            _resident((CONV_WIDTH, D_MODEL)),
            _resident((1, D_MODEL)),
            _resident((D_MODEL, D_MODEL)),
            _rows(tm, D_MODEL),
        ],
        out_specs=[_rows(tm, D_MODEL), _rows(tm, D_MODEL), _rows(tm, D_MODEL)],
        out_shape=[
            jax.ShapeDtypeStruct((t, D_MODEL), F32),
            jax.ShapeDtypeStruct((t, D_MODEL), F32),
            jax.ShapeDtypeStruct((t, D_MODEL), MXU_DTYPE),
        ],
        scratch_shapes=[pltpu.VMEM((tm + CONV_HALO, D_MODEL), F32), pltpu.VMEM((7, shifted_rows, D_MODEL), F32)],
        compiler_params=_cparams("parallel"),
    )(glu, glu, w_dw, g, w_pw2, h)


def _conv_bwd_post(dout, c, g, w_pw2):
    t = dout.shape[0]
    tm = min(1024, t)

    def body(do_ref, c_ref, g_ref, w2_ref, dc_ref, dg_ref):
        @pl.when(pl.program_id(0) == 0)
        def _():
            dg_ref[...] = jnp.zeros_like(dg_ref)

        dsw = _dot_nt(do_ref[...].astype(MXU_DTYPE), w2_ref[...])
        cv = c_ref[...]
        y = _rms(cv, g_ref[...])
        sig = _sigmoid(y)
        dy = dsw * (sig * (1.0 + y * (1.0 - sig)))
        dc, dg = _rms_bwd(cv, g_ref[...], dy)
        dc_ref[...] = dc
        dg_ref[...] += dg

    return pl.pallas_call(
        body,
        name="conv_bwd_post",
        grid=(t // tm,),
        in_specs=[_rows(tm, D_MODEL), _rows(tm, D_MODEL), _resident((1, D_MODEL)), _resident((D_MODEL, D_MODEL))],
        out_specs=[_rows(tm, D_MODEL), pl.BlockSpec((1, D_MODEL), lambda i: (0, 0))],
        out_shape=[jax.ShapeDtypeStruct((t, D_MODEL), F32), jax.ShapeDtypeStruct((1, D_MODEL), F32)],
        compiler_params=_cparams("arbitrary"),
    )(dout, c, g, w_pw2)


def _conv_bwd_pre(dc, glu, u, w_dw, w_pw1, h, g, dout):
    t = h.shape[0]
    tm = min(256, t)
    per = tm // CONV_HALO
    nt = t // tm
    last_halo = t // CONV_HALO - 1
    shifted_rows = tm + CONV_HALO - 8

    def body(dc_ref, dcn_ref, glu_ref, gluh_ref, u_ref, w_ref, w1_ref, h_ref, g_ref, do_ref,
             dh_ref, du_ref, dw_ref, dg_ref, dext, gext, dext_sh, gext_sh):
        i = pl.program_id(0)

        @pl.when(i == 0)
        def _():
            dw_ref[...] = jnp.zeros_like(dw_ref)
            dg_ref[...] = jnp.zeros_like(dg_ref)

        dext[pl.ds(0, tm), :] = dc_ref[...]
        dext[pl.ds(tm, CONV_HALO), :] = jnp.where(i == nt - 1, 0.0, dcn_ref[...])
        gext[pl.ds(0, CONV_HALO), :] = jnp.where(i == 0, 0.0, gluh_ref[...])
        gext[pl.ds(CONV_HALO, tm), :] = glu_ref[...]
        _shifted_copies(dext_sh, dext, shifted_rows)
        _shifted_copies(gext_sh, gext, shifted_rows)
        for base in range(0, tm, CONV_ROWS):
            rows = pl.ds(base, CONV_ROWS)
            dglu = jnp.zeros((CONV_ROWS, D_MODEL), F32)
            for k in range(CONV_WIDTH):
                dglu = dglu + _window(dext, dext_sh, CONV_WIDTH - 1 - k, base, CONV_ROWS) * w_ref[pl.ds(k, 1), :]
            av, bv = u_ref[rows, :D_MODEL], u_ref[rows, D_MODEL:]
            sig = _sigmoid(bv)
            du_ref[rows, :D_MODEL] = (dglu * sig).astype(MXU_DTYPE)
            du_ref[rows, D_MODEL:] = (dglu * av * (sig * (1.0 - sig))).astype(MXU_DTYPE)
        for k in range(CONV_WIDTH):
            part = jnp.zeros((CONV_ROWS, D_MODEL), F32)
            for base in range(0, tm, CONV_ROWS):
                part = part + dc_ref[pl.ds(base, CONV_ROWS), :] * _window(gext, gext_sh, CONV_HALO - (CONV_WIDTH - 1) + k, base, CONV_ROWS)
            dw_ref[pl.ds(k, 1), :] += jnp.sum(part, axis=0, keepdims=True)
        dn = _dot_nt(du_ref[...], w1_ref[...])
        dx, dg = _rms_bwd(h_ref[...], g_ref[...], dn)
        dh_ref[...] = do_ref[...] + dx
        dg_ref[...] += dg

    return pl.pallas_call(
        body,
        name="conv_bwd_pre",
        grid=(nt,),
        in_specs=[
            _rows(tm, D_MODEL),
            pl.BlockSpec((CONV_HALO, D_MODEL), lambda i: (jnp.minimum((i + 1) * per, last_halo), 0)),
            _rows(tm, D_MODEL),
            pl.BlockSpec((CONV_HALO, D_MODEL), lambda i: (jnp.maximum(i * per - 1, 0), 0)),
            _rows(tm, 2 * D_MODEL),
            _resident((CONV_WIDTH, D_MODEL)),
            _resident((D_MODEL, 2 * D_MODEL)),
            _rows(tm, D_MODEL),
            _resident((1, D_MODEL)),
            _rows(tm, D_MODEL),
        ],
        out_specs=[
            _rows(tm, D_MODEL),
            _rows(tm, 2 * D_MODEL),
            pl.BlockSpec((CONV_HALO, D_MODEL), lambda i: (0, 0)),
            pl.BlockSpec((1, D_MODEL), lambda i: (0, 0)),
        ],
        out_shape=[
            jax.ShapeDtypeStruct((t, D_MODEL), F32),
            jax.ShapeDtypeStruct((t, 2 * D_MODEL), MXU_DTYPE),
            jax.ShapeDtypeStruct((CONV_HALO, D_MODEL), F32),
            jax.ShapeDtypeStruct((1, D_MODEL), F32),
        ],
        scratch_shapes=[pltpu.VMEM((tm + CONV_HALO, D_MODEL), F32), pltpu.VMEM((tm + CONV_HALO, D_MODEL), F32),
                        pltpu.VMEM((7, shifted_rows, D_MODEL), F32), pltpu.VMEM((7, shifted_rows, D_MODEL), F32)],
        compiler_params=_cparams("arbitrary"),
    )(dc, dc, glu, glu, u, w_dw, w_pw1, h, g, dout)


def _rope(s, cs, sa, sb):
    return s * cs - pltpu.roll(s, 96, 1) * sa + pltpu.roll(s, 32, 1) * sb


def _unrope(d, cs, sa, sb):
    return d * cs + pltpu.roll(d, 96, 1) * sa - pltpu.roll(d, 32, 1) * sb


def _mla_pre(h, g, w_a, g_q, g_kv, w_uq, w_ukv, cs, sa, sb):
    t = h.shape[0]
    tm = min(512, t)
    qw = N_HEADS * HEAD_PAD
    vw = N_HEADS * V_HEAD

    def body(h_ref, g_ref, wa_ref, gq_ref, gkv_ref, wuq_ref, wukv_ref, cs_ref, sa_ref, sb_ref,
             n_ref, a_ref, cq_ref, ckv_ref, q_ref, k_ref, v_ref):
        n = _rms(h_ref[...], g_ref[...]).astype(MXU_DTYPE)
        n_ref[...] = n
        a = _dot(n, wa_ref[...])
        a_ref[...] = a
        cq = _rms(a[:, :Q_LORA], gq_ref[...]).astype(MXU_DTYPE)
        ckv = _rms(a[:, Q_LORA:Q_LORA + KV_LORA], gkv_ref[...]).astype(MXU_DTYPE)
        cq_ref[...] = cq
        ckv_ref[...] = ckv
        q = _dot(cq, wuq_ref[...]) * ATTN_SCALE
        kv = _dot(ckv, wukv_ref[...])
        cs_, sa_, sb_ = cs_ref[...], sa_ref[...], sb_ref[...]
        k_rot = _rope(a[:, Q_LORA + KV_LORA:], cs_, sa_, sb_).astype(MXU_DTYPE)
        for hd in range(N_HEADS):
            lo = hd * HEAD_PAD
            q_ref[:, lo:lo + QK_NOPE] = q[:, lo:lo + QK_NOPE].astype(MXU_DTYPE)
            q_ref[:, lo + QK_NOPE:lo + HEAD_PAD] = _rope(q[:, lo + QK_NOPE:lo + HEAD_PAD], cs_, sa_, sb_).astype(MXU_DTYPE)
            k_ref[:, lo:lo + QK_NOPE] = kv[:, lo:lo + QK_NOPE].astype(MXU_DTYPE)
            k_ref[:, lo + QK_NOPE:lo + HEAD_PAD] = k_rot
            v_ref[:, hd * V_HEAD:(hd + 1) * V_HEAD] = kv[:, lo + QK_NOPE:lo + HEAD_PAD].astype(MXU_DTYPE)

    return pl.pallas_call(
        body,
        name="mla_pre",
        grid=(t // tm,),
        in_specs=[
            _rows(tm, D_MODEL),
            _resident((1, D_MODEL)),
            _resident((D_MODEL, A_PAD)),
            _resident((1, Q_LORA)),
            _resident((1, KV_LORA)),
            _resident((Q_LORA, qw)),
            _resident((KV_LORA, qw)),
            _rows(tm, 128),
            _rows(tm, 128),
            _rows(tm, 128),
        ],
        out_specs=[_rows(tm, D_MODEL), _rows(tm, A_PAD), _rows(tm, Q_LORA), _rows(tm, KV_LORA), _rows(tm, qw), _rows(tm, qw), _rows(tm, vw)],
        out_shape=[
            jax.ShapeDtypeStruct((t, D_MODEL), MXU_DTYPE),
            jax.ShapeDtypeStruct((t, A_PAD), F32),
            jax.ShapeDtypeStruct((t, Q_LORA), MXU_DTYPE),
            jax.ShapeDtypeStruct((t, KV_LORA), MXU_DTYPE),
            jax.ShapeDtypeStruct((t, qw), MXU_DTYPE),
            jax.ShapeDtypeStruct((t, qw), MXU_DTYPE),
            jax.ShapeDtypeStruct((t, vw), MXU_DTYPE),
        ],
        compiler_params=_cparams("parallel"),
    )(h, g, w_a, g_q, g_kv, w_uq, w_ukv, cs, sa, sb)


def _attn_tile(t):
    return min(1024, t)


def _chunk_mask(rows, cols):
    r = lax.broadcasted_iota(jnp.int32, (rows, cols), 0)
    c = lax.broadcasted_iota(jnp.int32, (rows, cols), 1)
    return (r >> CHUNK_SHIFT) >= (c >> CHUNK_SHIFT)


def _causal_pairs(n, by_column=False):
    if by_column:
        pairs = [(i, j) for j in range(n) for i in range(j, n)]
    else:
        pairs = [(i, j) for i in range(n) for j in range(i + 1)]
    return jnp.asarray([p[0] for p in pairs], jnp.int32), jnp.asarray([p[1] for p in pairs], jnp.int32)


def _flash_fwd(q, k, v):
    t = q.shape[0]
    tq = tk = _attn_tile(t)
    rows_of, cols_of = _causal_pairs(t // tq)

    def body(qi_ref, ki_ref, q_ref, k_ref, v_ref, o_ref, lse_ref, m_s, l_s, acc):
        step = pl.program_id(1)
        qi, ki = qi_ref[step], ki_ref[step]

        @pl.when(ki == 0)
        def _():
            m_s[...] = jnp.full_like(m_s, -jnp.inf)
            l_s[...] = jnp.zeros_like(l_s)
            acc[...] = jnp.zeros_like(acc)

        def block(on_diagonal):
            s = _dot_nt(q_ref[...], k_ref[...])
            if on_diagonal:
                s = jnp.where(_chunk_mask(tq, tk), s, -jnp.inf)
            m_prev = m_s[...]
            m_new = jnp.maximum(m_prev, jnp.max(s, axis=1, keepdims=True))
            alpha = jnp.exp(m_prev - m_new)
            p = jnp.exp(s - m_new)
            l_s[...] = alpha * l_s[...] + jnp.sum(p, axis=1, keepdims=True)
            acc[...] = alpha * acc[...] + _dot(p.astype(MXU_DTYPE), v_ref[...])
            m_s[...] = m_new

        pl.when(ki < qi)(lambda: block(False))

        @pl.when(ki == qi)
        def _():
            block(True)
            o_ref[...] = acc[...] / l_s[...]
            lse_ref[...] = jnp.broadcast_to(m_s[...] + jnp.log(l_s[...]), (tq, V_HEAD))

    grid_spec = pltpu.PrefetchScalarGridSpec(
        num_scalar_prefetch=2,
        grid=(N_HEADS, rows_of.shape[0]),
        in_specs=[
            pl.BlockSpec((tq, HEAD_PAD), lambda h, s, qi, ki: (qi[s], h)),
            pl.BlockSpec((tk, HEAD_PAD), lambda h, s, qi, ki: (ki[s], h)),
            pl.BlockSpec((tk, V_HEAD), lambda h, s, qi, ki: (ki[s], h)),
        ],
        out_specs=[pl.BlockSpec((tq, V_HEAD), lambda h, s, qi, ki: (qi[s], h)), pl.BlockSpec((tq, V_HEAD), lambda h, s, qi, ki: (qi[s], h))],
        scratch_shapes=[pltpu.VMEM((tq, 1), F32), pltpu.VMEM((tq, 1), F32), pltpu.VMEM((tq, V_HEAD), F32)],
    )
    return pl.pallas_call(
        body,
        name="flash_fwd",
        grid_spec=grid_spec,
        out_shape=[jax.ShapeDtypeStruct((t, N_HEADS * V_HEAD), F32), jax.ShapeDtypeStruct((t, N_HEADS * V_HEAD), F32)],
        compiler_params=_cparams("parallel", "arbitrary"),
    )(rows_of, cols_of, q, k, v)


def _flash_bwd(q, k, v, do, lse, delta):
    t = q.shape[0]
    tq = tk = _attn_tile(t)
    nq = t // tq
    rows_of, cols_of = _causal_pairs(nq, by_column=True)

    def body(qi_ref, kj_ref, q_ref, k_ref, v_ref, do_ref, lse_ref, dl_ref, dq_ref, dk_ref, dv_ref, dk_acc, dv_acc):
        step = pl.program_id(1)
        qi, kj = qi_ref[step], kj_ref[step]

        @pl.when(step == 0)
        def _():
            dq_ref[...] = jnp.zeros_like(dq_ref)

        def block(on_diagonal):
            qv, kv, dov = q_ref[...], k_ref[...], do_ref[...]
            s = _dot_nt(qv, kv)
            if on_diagonal:
                s = jnp.where(_chunk_mask(tq, tk), s, -jnp.inf)
            p = jnp.exp(s - lse_ref[:, :1])
            dp = _dot_nt(dov, v_ref[...])
            ds = (p * (dp - dl_ref[:, :1])).astype(MXU_DTYPE)
            dv_new = _dot_tn(p.astype(MXU_DTYPE), dov)
            dk_new = _dot_tn(ds, qv)
            if on_diagonal:
                dv_acc[...] = dv_new
                dk_acc[...] = dk_new
            else:
                dv_acc[...] += dv_new
                dk_acc[...] += dk_new
            rows = pl.ds(pl.multiple_of(qi * tq, tq), tq)
            dq_ref[rows, :] += _dot(ds, kv)

        pl.when(qi > kj)(lambda: block(False))
        pl.when(qi == kj)(lambda: block(True))

        @pl.when(qi == nq - 1)
        def _():
            dk_ref[...] = dk_acc[...]
            dv_ref[...] = dv_acc[...]

    grid_spec = pltpu.PrefetchScalarGridSpec(
        num_scalar_prefetch=2,
        grid=(N_HEADS, rows_of.shape[0]),
        in_specs=[
            pl.BlockSpec((tq, HEAD_PAD), lambda h, s, qi, kj: (qi[s], h)),
            pl.BlockSpec((tk, HEAD_PAD), lambda h, s, qi, kj: (kj[s], h)),
            pl.BlockSpec((tk, V_HEAD), lambda h, s, qi, kj: (kj[s], h)),
            pl.BlockSpec((tq, V_HEAD), lambda h, s, qi, kj: (qi[s], h)),
            pl.BlockSpec((tq, V_HEAD), lambda h, s, qi, kj: (qi[s], h)),
            pl.BlockSpec((tq, V_HEAD), lambda h, s, qi, kj: (qi[s], h)),
        ],
        out_specs=[
            pl.BlockSpec((t, HEAD_PAD), lambda h, s, qi, kj: (0, h)),
            pl.BlockSpec((tk, HEAD_PAD), lambda h, s, qi, kj: (kj[s], h)),
            pl.BlockSpec((tk, V_HEAD), lambda h, s, qi, kj: (kj[s], h)),
        ],
        scratch_shapes=[pltpu.VMEM((tk, HEAD_PAD), F32), pltpu.VMEM((tk, V_HEAD), F32)],
    )
    return pl.pallas_call(
        body,
        name="flash_bwd",
        grid_spec=grid_spec,
        out_shape=[
            jax.ShapeDtypeStruct((t, N_HEADS * HEAD_PAD), F32),
            jax.ShapeDtypeStruct((t, N_HEADS * HEAD_PAD), F32),
            jax.ShapeDtypeStruct((t, N_HEADS * V_HEAD), F32),
        ],
        compiler_params=_cparams("arbitrary", "arbitrary"),
    )(rows_of, cols_of, q, k, v, do, lse, delta)


def _attn_out(o, w_o, h):
    t = h.shape[0]
    tm = min(1024, t)

    def body(o_ref, w_ref, h_ref, ho_ref):
        ho_ref[...] = h_ref[...] + _dot(o_ref[...].astype(MXU_DTYPE), w_ref[...])

    return pl.pallas_call(
        body,
        name="attn_out",
        grid=(t // tm,),
        in_specs=[_rows(tm, D_MODEL), _resident((D_MODEL, D_MODEL)), _rows(tm, D_MODEL)],
        out_specs=_rows(tm, D_MODEL),
        out_shape=jax.ShapeDtypeStruct((t, D_MODEL), F32),
        compiler_params=_cparams("parallel"),
    )(o, w_o, h)


def _attn_out_bwd(dout, o, w_o):
    t = dout.shape[0]
    tm = min(1024, t)

    def body(d_ref, o_ref, w_ref, do_ref, dl_ref):
        do = _dot_nt(d_ref[...].astype(MXU_DTYPE), w_ref[...])
        do_ref[...] = do.astype(MXU_DTYPE)
        prod = do * o_ref[...]
        for hd in range(N_HEADS):
            lanes = slice(hd * V_HEAD, (hd + 1) * V_HEAD)
            dl_ref[:, lanes] = jnp.broadcast_to(jnp.sum(prod[:, lanes], axis=1, keepdims=True), (tm, V_HEAD))

    return pl.pallas_call(
        body,
        name="attn_out_bwd",
        grid=(t // tm,),
        in_specs=[_rows(tm, D_MODEL), _rows(tm, D_MODEL), _resident((D_MODEL, D_MODEL))],
        out_specs=[_rows(tm, D_MODEL), _rows(tm, D_MODEL)],
        out_shape=[jax.ShapeDtypeStruct((t, D_MODEL), MXU_DTYPE), jax.ShapeDtypeStruct((t, D_MODEL), F32)],
        compiler_params=_cparams("parallel"),
    )(dout, o, w_o)


def _mla_bwd_pre(dq, dk, dv, a, h, dout, g, g_q, g_kv, w_a, w_uq, w_ukv, cs, sa, sb):
    t = h.shape[0]
    tm = min(512, t)
    qw = N_HEADS * HEAD_PAD
    vw = N_HEADS * V_HEAD

    def body(dq_ref, dk_ref, dv_ref, a_ref, h_ref, do_ref, g_ref, gq_ref, gkv_ref, wa_ref, wuq_ref, wukv_ref,
             cs_ref, sa_ref, sb_ref, dh_ref, dqp_ref, dkv_ref, da_ref, dg_ref, dgq_ref, dgkv_ref):
        @pl.when(pl.program_id(0) == 0)
        def _():
            dg_ref[...] = jnp.zeros_like(dg_ref)
            dgq_ref[...] = jnp.zeros_like(dgq_ref)
            dgkv_ref[...] = jnp.zeros_like(dgkv_ref)

        cs_, sa_, sb_ = cs_ref[...], sa_ref[...], sb_ref[...]
        slab = jnp.zeros((tm, 128), F32)
        for hd in range(N_HEADS):
            lo = hd * HEAD_PAD
            dqp_ref[:, lo:lo + QK_NOPE] = (dq_ref[:, lo:lo + QK_NOPE] * ATTN_SCALE).astype(MXU_DTYPE)
            dqp_ref[:, lo + QK_NOPE:lo + HEAD_PAD] = _unrope(dq_ref[:, lo + QK_NOPE:lo + HEAD_PAD] * ATTN_SCALE, cs_, sa_, sb_).astype(MXU_DTYPE)
            dkv_ref[:, lo:lo + QK_NOPE] = dk_ref[:, lo:lo + QK_NOPE].astype(MXU_DTYPE)
            dkv_ref[:, lo + QK_NOPE:lo + HEAD_PAD] = dv_ref[:, hd * V_HEAD:(hd + 1) * V_HEAD].astype(MXU_DTYPE)
            slab = slab + dk_ref[:, lo + QK_NOPE:lo + HEAD_PAD]
        dcq = _dot_nt(dqp_ref[...], wuq_ref[...])
        dckv = _dot_nt(dkv_ref[...], wukv_ref[...])
        av = a_ref[...]
        daq, dgq = _rms_bwd(av[:, :Q_LORA], gq_ref[...], dcq)
        dakv, dgkv = _rms_bwd(av[:, Q_LORA:Q_LORA + KV_LORA], gkv_ref[...], dckv)
        da_ref[:, :Q_LORA] = daq.astype(MXU_DTYPE)
        da_ref[:, Q_LORA:Q_LORA + KV_LORA] = dakv.astype(MXU_DTYPE)
        da_ref[:, Q_LORA + KV_LORA:] = _unrope(slab, cs_, sa_, sb_).astype(MXU_DTYPE)
        dn = _dot_nt(da_ref[...], wa_ref[...])
        dx, dg = _rms_bwd(h_ref[...], g_ref[...], dn)
        dh_ref[...] = do_ref[...] + dx
        dg_ref[...] += dg
        dgq_ref[...] += dgq
        dgkv_ref[...] += dgkv

    def const(n):
        return pl.BlockSpec((1, n), lambda i: (0, 0))

    return pl.pallas_call(
        body,
        name="mla_bwd_pre",
        grid=(t // tm,),
        in_specs=[
            _rows(tm, qw), _rows(tm, qw), _rows(tm, vw), _rows(tm, A_PAD), _rows(tm, D_MODEL), _rows(tm, D_MODEL),
            _resident((1, D_MODEL)), _resident((1, Q_LORA)), _resident((1, KV_LORA)),
            _resident((D_MODEL, A_PAD)), _resident((Q_LORA, qw)), _resident((KV_LORA, qw)),
            _rows(tm, 128), _rows(tm, 128), _rows(tm, 128),
        ],
        out_specs=[_rows(tm, D_MODEL), _rows(tm, qw), _rows(tm, qw), _rows(tm, A_PAD),
                   const(D_MODEL), const(Q_LORA), const(KV_LORA)],
        out_shape=[
            jax.ShapeDtypeStruct((t, D_MODEL), F32),
            jax.ShapeDtypeStruct((t, qw), MXU_DTYPE),
            jax.ShapeDtypeStruct((t, qw), MXU_DTYPE),
            jax.ShapeDtypeStruct((t, A_PAD), MXU_DTYPE),
            jax.ShapeDtypeStruct((1, D_MODEL), F32),
            jax.ShapeDtypeStruct((1, Q_LORA), F32),
            jax.ShapeDtypeStruct((1, KV_LORA), F32),
        ],
        compiler_params=_cparams("arbitrary"),
    )(dq, dk, dv, a, h, dout, g, g_q, g_kv, w_a, w_uq, w_ukv, cs, sa, sb)


def _loss_head(h, g, target):
    t = h.shape[0]
    tm = min(1024, t)

    def body(h_ref, g_ref, t_ref, sq_ref, dh_ref, dg_ref):
        @pl.when(pl.program_id(0) == 0)
        def _():
            sq_ref[...] = jnp.zeros_like(sq_ref)
            dg_ref[...] = jnp.zeros_like(dg_ref)

        x = h_ref[...]
        err = _rms(x, g_ref[...]) - t_ref[...]
        sq_ref[...] += jnp.sum(err * err, axis=0, keepdims=True)
        dx, dg = _rms_bwd(x, g_ref[...], err * (1.0 / D_MODEL))
        dh_ref[...] = dx
        dg_ref[...] += dg

    return pl.pallas_call(
        body,
        name="loss_head",
        grid=(t // tm,),
        in_specs=[_rows(tm, D_MODEL), _resident((1, D_MODEL)), _rows(tm, D_MODEL)],
        out_specs=[pl.BlockSpec((1, D_MODEL), lambda i: (0, 0)), _rows(tm, D_MODEL), pl.BlockSpec((1, D_MODEL), lambda i: (0, 0))],
        out_shape=[jax.ShapeDtypeStruct((1, D_MODEL), F32), jax.ShapeDtypeStruct((t, D_MODEL), F32), jax.ShapeDtypeStruct((1, D_MODEL), F32)],
        compiler_params=_cparams("arbitrary"),
    )(h, g, target)


def _adamw(parts, w, m, v, name, tr, layer=0, row_offset=0, into=None):
    layers, rows, width = w.shape
    pwidth = parts.shape[-1]
    off = row_offset // tr

    def body(p_ref, w_ref, m_ref, v_ref, *rest):
        g_ref, d_ref, mo_ref, vo_ref = rest[-4:]

        def part(q):
            return p_ref[q, :, pl.ds(0, width)].astype(F32)

        g = part(0)
        for q in range(1, N_DEV):
            g = g + part(q)
        g_ref[...] = g
        m_new = ADAM_B1 * m_ref[...] + (1.0 - ADAM_B1) * g
        v_new = ADAM_B2 * v_ref[...] + (1.0 - ADAM_B2) * (g * g)
        m_hat = m_new / (1.0 - ADAM_B1 ** ADAM_STEP)
        v_hat = v_new / (1.0 - ADAM_B2 ** ADAM_STEP)
        d_ref[...] = -ADAM_LR * (m_hat / (jnp.sqrt(v_hat) + ADAM_EPS) + ADAM_WD * w_ref[...])
        mo_ref[...] = m_new
        vo_ref[...] = v_new

    blk = pl.BlockSpec((None, tr, width), lambda i: (layer, i, 0))
    earlier = () if into is None else tuple(into)
    return pl.pallas_call(
        body,
        name=name,
        grid=(rows // tr,),
        in_specs=[pl.BlockSpec((N_DEV, tr, pwidth), lambda i: (0, off + i, 0)), blk, blk, blk] + [pl.BlockSpec(memory_space=pl.ANY)] * len(earlier),
        out_specs=[blk, blk, blk, blk],
        out_shape=[jax.ShapeDtypeStruct((layers, rows, width), F32)] * 4,
        input_output_aliases={4 + j: j for j in range(len(earlier))},
        compiler_params=_cparams("parallel"),
    )(parts, w, m, v, *earlier)


_GAINS = ("ffn_norm1", "mix_norm", "ffn_norm2", "conv_norm", "final_norm")
_TINY = ("conv_w_dw", "mla_q_norm", "mla_kv_norm")


def _small_pack(parts):
    flat = jnp.concatenate([p.reshape(-1).astype(F32) for p in parts])
    return jnp.pad(flat, (0, SMALL_ROWS * D_MODEL - flat.shape[0])).reshape(SMALL_ROWS, D_MODEL)


def _small_unpack(packed, shapes):
    flat, out, off = packed.reshape(-1), [], 0
    for s in shapes:
        size = 1
        for d in s:
            size *= d
        out.append(flat[off:off + size].reshape(s))
        off += size
    return out


def _pad_to(a, axis, size):
    pad = [(0, 0)] * a.ndim
    pad[axis] = (0, size - a.shape[axis])
    return jnp.pad(a, pad)


def kernel(x, positions, ffn_norm1, ffn1_w1, ffn1_w3, ffn1_w2, mix_norm, ffn_norm2, ffn2_w1, ffn2_w3, ffn2_w2, conv_w_pw1, conv_w_dw, conv_norm, conv_w_pw2, mla_w_a, mla_q_norm, mla_kv_norm, mla_w_uq, mla_w_ukv, mla_w_o, final_norm, loss_target, m_ffn_norm1, m_ffn1_w1, m_ffn1_w3, m_ffn1_w2, m_mix_norm, m_ffn_norm2, m_ffn2_w1, m_ffn2_w3, m_ffn2_w2, m_conv_w_pw1, m_conv_w_dw, m_conv_norm, m_conv_w_pw2, m_mla_w_a, m_mla_q_norm, m_mla_kv_norm, m_mla_w_uq, m_mla_w_ukv, m_mla_w_o, m_final_norm, v_ffn_norm1, v_ffn1_w1, v_ffn1_w3, v_ffn1_w2, v_mix_norm, v_ffn_norm2, v_ffn2_w1, v_ffn2_w3, v_ffn2_w2, v_conv_w_pw1, v_conv_w_dw, v_conv_norm, v_conv_w_pw2, v_mla_w_a, v_mla_q_norm, v_mla_kv_norm, v_mla_w_uq, v_mla_w_ukv, v_mla_w_o, v_final_norm):
    args = dict(locals())
    wire = lambda a: a.astype(WIRE_DTYPE)

    def ffn_shards(prefix, layer):
        w13_shard = _pad_to(jnp.stack([args[prefix + "_w1"][layer], args[prefix + "_w3"][layer]]), 2, FF_SHARD_PAD)
        return [(wire(w13_shard), 1), (wire(_pad_to(args[prefix + "_w2"][layer], 0, FF_SHARD_PAD)), 2)]

    def uq_pad(a):
        return _pad_to(a, 3, HEAD_PAD).reshape(1, a.shape[1], N_HEADS * HEAD_PAD)

    ukv_rows = lambda a: a.reshape(1, a.shape[1], N_HEADS * HEAD_PAD)
    tiny_shapes = [conv_w_dw.shape, mla_q_norm.shape, mla_kv_norm.shape]

    first = _gather_two_level(*_gather_streams(ffn_shards("ffn1", 0)), "weight_gather_first")
    later_groups = [
        [(wire(conv_w_pw1[0]), 1), (wire(conv_w_pw2[0]), 2), (_small_pack([conv_w_dw, mla_q_norm, mla_kv_norm]), None)],
        ffn_shards("ffn2", 0),
        ffn_shards("ffn1", 1),
        [(wire(_pad_to(mla_w_a[0], 1, A_PAD)), 2), (wire(uq_pad(mla_w_uq)[0]), 2), (wire(ukv_rows(mla_w_ukv)[0]), 2), (wire(mla_w_o[0]), 2)],
        ffn_shards("ffn2", 1),
    ]
    later_streams, later_outs = _gather_streams([sh for grp in later_groups for sh in grp])
    group_ids, at = [], 0
    for grp in later_groups:
        group_ids.append(list(range(at, at + len(grp))))
        at += len(grp)
    gather, gather_token = _exchange_start(
        later_streams, _landing(later_streams, later_outs), group_ids, (first[0],), "weight_gather_start")

    inv_freq = ROPE_THETA ** (-2.0 * jnp.arange(QK_ROPE // 2, dtype=F32) / QK_ROPE)
    ang = positions[0].astype(F32)[:, None] * inv_freq
    cos, sin, zero = jnp.cos(ang), jnp.sin(ang), jnp.zeros_like(ang)
    cs = jnp.concatenate([cos, cos, zero, zero], axis=1)
    sa = jnp.concatenate([sin, zero, zero, zero], axis=1)
    sb = jnp.concatenate([zero, sin, zero, zero], axis=1)

    ffn_w = {("ffn1", 0): first}
    h0 = x[0]
    h1, n1, a1, b1 = _ffn_fwd(h0, ffn_norm1[0:1], *ffn_w[("ffn1", 0)], after=(gather_token,))
    w_pw1, w_pw2, tiny = _exchange_wait(gather, 0, (h1,), "weight_gather_wait_conv")
    tiny = [_small_unpack(tiny[q], tiny_shapes) for q in range(N_DEV)]
    w_dw = jnp.concatenate([t_[0][0] for t_ in tiny], axis=1)
    g_q = jnp.concatenate([t_[1] for t_ in tiny], axis=1)
    g_kv = jnp.concatenate([t_[2] for t_ in tiny], axis=1)
    n_c, u, glu = _conv_pre(h1, mix_norm[0:1], w_pw1)
    h2, c, sw = _conv_main(glu, w_dw, conv_norm, w_pw2, h1)
    ffn_w[("ffn2", 0)] = _exchange_wait(gather, 1, (h2,), "weight_gather_wait_ffn2_0")
    h3, n2, a2, b2 = _ffn_fwd(h2, ffn_norm2[0:1], *ffn_w[("ffn2", 0)])
    ffn_w[("ffn1", 1)] = _exchange_wait(gather, 2, (h3,), "weight_gather_wait_ffn1_1")
    h4, n3, a3, b3 = _ffn_fwd(h3, ffn_norm1[1:2], *ffn_w[("ffn1", 1)])
    w_a, w_uq, w_ukv, w_o = _exchange_wait(gather, 3, (h4,), "weight_gather_wait_mla")
    n_a, a_lat, cq, ckv, q, k, v = _mla_pre(h4, mix_norm[1:2], w_a, g_q, g_kv, w_uq, w_ukv, cs, sa, sb)
    o, lse = _flash_fwd(q, k, v)
    h5 = _attn_out(o, w_o, h4)
    ffn_w[("ffn2", 1)] = _exchange_wait(gather, 4, (h5,), "weight_gather_wait_ffn2_1")
    h6, n4, a4, b4 = _ffn_fwd(h5, ffn_norm2[1:2], *ffn_w[("ffn2", 1)])

    sq, dh, dg_final = _loss_head(h6, final_norm[None, :], loss_target[0])
    loss = lax.psum(0.5 / D_MODEL * jnp.sum(sq), ("x", "y", "c"))

    gain = {}

    def ffn_backward(prefix, norm, layer, dh, h_in, n, a, b, after=()):
        dh, dob, s, dab, gain[(norm, layer)] = _ffn_bwd(dh, h_in, args[norm][layer:layer + 1], a, b, *ffn_w[(prefix, layer)], after=after)
        return dh, [(_wgrad(n, dab, "wgrad_ffn_in"), 1), (_wgrad(s, dob, "wgrad_ffn_out"), 2)]

    dh, g_ffn2_1 = ffn_backward("ffn2", "ffn_norm2", 1, dh, h5, n4, a4, b4)
    do, delta = _attn_out_bwd(dh, o, w_o)
    d_o = _wgrad(o, dh, "wgrad_attn_out")
    dq, dk, dv = _flash_bwd(q, k, v, do, lse, delta)
    dh, dqp, dkv, da_lat, gain[("mix_norm", 1)], d_gq, d_gkv = _mla_bwd_pre(
        dq, dk, dv, a_lat, h4, dh, mix_norm[1:2], g_q, g_kv, w_a, w_uq, w_ukv, cs, sa, sb)
    g_mla = [(_wgrad(n_a, da_lat, "wgrad_mla_a"), 2), (_wgrad(cq, dqp, "wgrad_mla_uq"), 2), (_wgrad(ckv, dkv, "wgrad_mla_ukv"), 2), (d_o, 2)]
    dh, g_ffn1_1 = ffn_backward("ffn1", "ffn_norm1", 1, dh, h3, n3, a3, b3)
    streams_a, outs_a = _scatter_streams(g_ffn2_1 + g_mla + g_ffn1_1)
    scatter_a, token_a = _exchange_start(
        streams_a, _landing(streams_a, outs_a), [list(range(len(streams_a)))], (), "grad_scatter_start_layer1")

    dh, g_ffn2_0 = ffn_backward("ffn2", "ffn_norm2", 0, dh, h2, n2, a2, b2, after=(token_a,))
    dc, gain[("conv_norm", 0)] = _conv_bwd_post(dh, c, conv_norm, w_pw2)
    d_pw2 = _wgrad(sw, dh, "wgrad_conv_pw2")
    dh, du, d_dw, gain[("mix_norm", 0)] = _conv_bwd_pre(dc, glu, u, w_dw, w_pw1, h1, mix_norm[0:1], dh)
    d_pw1 = _wgrad(n_c, du, "wgrad_conv_pw1")
    streams_b, outs_b = _scatter_streams(g_ffn2_0 + [(d_pw1, 1), (d_pw2, 2)])
    scatter_b, token_b = _exchange_start(
        streams_b, _landing(streams_b, outs_b), [list(range(len(streams_b)))], (), "grad_scatter_start_conv")

    dh, dob, s_act, dab, gain[("ffn_norm1", 0)] = _ffn_bwd(dh, h0, ffn_norm1[0:1], a1, b1, *ffn_w[("ffn1", 0)], after=(token_b,))
    grad_x = dh[None]
    streams_d, outs_d = _scatter_streams([(_wgrad(s_act, dob, "wgrad_ffn_out"), 2)])
    scatter_d, token_d = _exchange_start(streams_d, _landing(streams_d, outs_d), [[0]], (), "grad_scatter_start_last_out")
    streams_e, outs_e = _scatter_streams([(_wgrad(n1, dab, "wgrad_ffn_in_half", after=(token_d,), only=0), 1)])
    scatter_e, token_e = _exchange_start(streams_e, _landing(streams_e, outs_e), [[0]], (), "grad_scatter_start_last_w1")
    g_ffn1_0 = [(_wgrad(n1, dab, "wgrad_ffn_in_half", after=(token_e,), only=1), 1)]

    gain_rows = jnp.concatenate([
        gain[("ffn_norm1", 0)], gain[("ffn_norm1", 1)], gain[("mix_norm", 0)], gain[("mix_norm", 1)],
        gain[("ffn_norm2", 0)], gain[("ffn_norm2", 1)], gain[("conv_norm", 0)], dg_final])
    dw_by_dest = jnp.moveaxis(d_dw[:CONV_WIDTH].reshape(CONV_WIDTH, N_DEV, -1), 1, 0)
    small_by_dest = jnp.stack([
        _small_pack([gain_rows, dw_by_dest[p], d_gq.reshape(N_DEV, -1)[p], d_gkv.reshape(N_DEV, -1)[p]]) for p in range(N_DEV)])
    streams_c, outs_c = _scatter_streams(g_ffn1_0)
    streams_c.append((small_by_dest, len(outs_c), _slot, _slot, lambda dst, src, me: _place_slot(dst, lax.dynamic_index_in_dim(src, me, 0, False), me)))
    outs_c.append(jax.ShapeDtypeStruct((N_DEV, SMALL_ROWS, D_MODEL), F32))
    scatter_c, token_c = _exchange_start(streams_c, _landing(streams_c, outs_c), [list(range(len(streams_c)))], (), "grad_scatter_start_last")

    (p13_ffn2_1, p2_ffn2_1, p_a, p_uq, p_ukv, p_o, p13_ffn1_1, p2_ffn1_1) = _exchange_wait(scatter_a, 0, (token_c,), "grad_scatter_wait_layer1")
    p13_ffn2_0, p2_ffn2_0, p_pw1, p_pw2 = _exchange_wait(scatter_b, 0, (token_c,), "grad_scatter_wait_conv")
    results = {}

    def adam(name, parts, tr, view=lambda a: a, **kw):
        results[name] = _adamw(parts, view(args[name]), view(args["m_" + name]), view(args["v_" + name]), "adamw_" + name, tr,
                               into=results.get(name), **kw)

    def adam_ffn(prefix, layer, p13, p2):
        p13 = p13.reshape(N_DEV, 2 * D_MODEL, FF_SHARD_PAD)
        adam(prefix + "_w1", p13, 256, layer=layer)
        adam(prefix + "_w3", p13, 256, layer=layer, row_offset=D_MODEL)
        adam(prefix + "_w2", p2, FF_SHARD, layer=layer)

    adam_ffn("ffn2", 0, p13_ffn2_0, p2_ffn2_0)
    adam_ffn("ffn2", 1, p13_ffn2_1, p2_ffn2_1)
    adam_ffn("ffn1", 1, p13_ffn1_1, p2_ffn1_1)
    adam("conv_w_pw1", p_pw1, 256)
    adam("conv_w_pw2", p_pw2, 128)
    adam("mla_w_a", p_a, 128)
    adam("mla_w_uq", p_uq, 64, view=uq_pad)
    adam("mla_w_ukv", p_ukv, 32, view=ukv_rows)
    adam("mla_w_o", p_o, 128)
    results["mla_w_uq"] = [r.reshape(1, -1, N_HEADS, HEAD_PAD)[..., :QK_NOPE + QK_ROPE] for r in results["mla_w_uq"]]
    results["mla_w_ukv"] = [r.reshape(mla_w_ukv.shape) for r in results["mla_w_ukv"]]

    (p2_ffn1_0,) = _exchange_wait(scatter_d, 0, (results["mla_w_o"][0],), "grad_scatter_wait_last_out")
    (p1_ffn1_0,) = _exchange_wait(scatter_e, 0, (p2_ffn1_0,), "grad_scatter_wait_last_w1")
    p3_ffn1_0, p_small = _exchange_wait(scatter_c, 0, (p1_ffn1_0,), "grad_scatter_wait_last")
    adam("ffn1_w1", p1_ffn1_0, 256, layer=0)
    adam("ffn1_w3", p3_ffn1_0, 256, layer=0)
    adam("ffn1_w2", p2_ffn1_0, FF_SHARD, layer=0)
    small_names = _GAINS + _TINY
    small_shapes = [args[n].shape for n in small_names]
    pack_small = lambda prefix: _small_pack([args[prefix + n] for n in small_names])[None]
    small = _adamw(p_small, pack_small(""), pack_small("m_"), pack_small("v_"), "adamw_small", SMALL_ROWS)
    for kind in range(4):
        for n, leaf in zip(small_names, _small_unpack(small[kind][0], small_shapes)):
            results.setdefault(n, [None] * 4)[kind] = leaf

    order = ("ffn_norm1", "ffn1_w1", "ffn1_w3", "ffn1_w2", "mix_norm", "ffn_norm2", "ffn2_w1", "ffn2_w3", "ffn2_w2",
             "conv_w_pw1", "conv_w_dw", "conv_norm", "conv_w_pw2", "mla_w_a", "mla_q_norm", "mla_kv_norm",
             "mla_w_uq", "mla_w_ukv", "mla_w_o", "final_norm")
    outputs = [loss, grad_x]
    for kind in range(4):
        outputs.extend(results[n][kind] for n in order)
    return tuple(outputs)
```

```python
import jax
import jax.numpy as jnp
from jax import lax
from jax.experimental import pallas as pl
from jax.experimental.pallas import tpu as pltpu

F32 = jnp.float32
MXU_DTYPE = jnp.bfloat16
WIRE_DTYPE = jnp.bfloat16

N_DEV = 8
D_MODEL = 1024
FF_SHARD = 352
FF_SHARD_PAD = 384
D_FF_PAD = N_DEV * FF_SHARD_PAD
N_HEADS = 8
QK_NOPE = 128
QK_ROPE = 64
V_HEAD = 128
Q_LORA = 512
KV_LORA = 256
HEAD_PAD = 256
A_WIDTH = Q_LORA + KV_LORA + QK_ROPE
A_PAD = Q_LORA + KV_LORA + 128
CONV_WIDTH = 31
CONV_HALO = 32
CONV_ROWS = 16
CHUNK_SHIFT = 6
ROPE_THETA = 10000.0
RMS_EPS = 1e-6
ATTN_SCALE = (QK_NOPE + QK_ROPE) ** -0.5
FFN_RES_WEIGHT = 0.5
ADAM_LR = 0.001
ADAM_B1 = 0.9
ADAM_B2 = 0.999
ADAM_EPS = 1e-08
ADAM_WD = 0.01
ADAM_STEP = 10

SMALL_ROWS = 16
VMEM_LIMIT = 56 << 20


def _cparams(*sem):
    return pltpu.CompilerParams(dimension_semantics=sem, vmem_limit_bytes=VMEM_LIMIT)


def _dot(a, b):
    return lax.dot_general(a, b, (((1,), (0,)), ((), ())), preferred_element_type=F32)


def _dot_nt(a, b):
    return lax.dot_general(a, b, (((1,), (1,)), ((), ())), preferred_element_type=F32)


def _dot_tn(a, b):
    return lax.dot_general(a, b, (((0,), (0,)), ((), ())), preferred_element_type=F32)


def _resident(shape, index=None):
    fixed = index if index is not None else (0,) * len(shape)
    return pl.BlockSpec(shape, lambda *_: fixed, pipeline_mode=pl.Buffered(1))


def _rows(tm, n):
    return pl.BlockSpec((tm, n), lambda t, *_: (t, 0))


def _rms(x, g):
    r = lax.rsqrt(jnp.mean(x * x, axis=-1, keepdims=True) + RMS_EPS)
    return x * r * g


def _rms_bwd(x, g, dy):
    r = lax.rsqrt(jnp.mean(x * x, axis=-1, keepdims=True) + RMS_EPS)
    xr = x * r
    dg = jnp.sum(dy * xr, axis=0, keepdims=True)
    u = dy * g
    dx = r * (u - xr * jnp.mean(u * xr, axis=-1, keepdims=True))
    return dx, dg


def _sigmoid(x):
    return 1.0 / (1.0 + jnp.exp(-x))


def _me():
    return 4 * lax.axis_index("x") + 2 * lax.axis_index("y") + lax.axis_index("c")


def _peer(k):
    px, py, pc = lax.axis_index("x") ^ ((k >> 2) & 1), lax.axis_index("y") ^ ((k >> 1) & 1), lax.axis_index("c") ^ (k & 1)
    return (px, py, pc), 4 * px + 2 * py + pc


def _stream_copies(stream, src_ref, dst_ref, send_sem, recv_sem, base):
    send, land = stream[2], stream[3]
    me = _me()
    outgoing, incoming = [], []
    for k in range(1, N_DEV):
        pos, p = _peer(k)
        common = dict(send_sem=send_sem.at[base + k - 1], recv_sem=recv_sem.at[base + k - 1], device_id=pos, device_id_type=pl.DeviceIdType.MESH)
        outgoing.append(pltpu.make_async_remote_copy(src_ref=send(src_ref, p), dst_ref=land(dst_ref, me), **common))
        incoming.append(pltpu.make_async_remote_copy(src_ref=send(src_ref, p), dst_ref=land(dst_ref, p), **common))
    return outgoing, incoming


_HBM = pl.BlockSpec(memory_space=pltpu.HBM)
_SEM = pl.BlockSpec(memory_space=pltpu.SEMAPHORE)


_OTHER_CHIPS = (2, 4, 6)


def _gather_two_level(streams, outs, name):
    n = len(streams)
    slots = N_DEV - 1

    def body(*refs):
        srcs, dsts = refs[:n], refs[n:2 * n]
        send_sem, recv_sem, local_sem = refs[2 * n:]
        me = _me()
        sibling_pos, sibling = _peer(1)

        def block(e, q):
            return streams[e][3](dsts[e], q)

        def copy(e, slot, src_ref, q, to):
            return pltpu.make_async_remote_copy(src_ref=src_ref, dst_ref=block(e, q), send_sem=send_sem.at[e * slots + slot],
                                                recv_sem=recv_sem.at[e * slots + slot], device_id=to, device_id_type=pl.DeviceIdType.MESH)

        own = [pltpu.make_async_copy(srcs[e], block(e, me), local_sem.at[e]) for e in range(n)]
        for cp in own:
            cp.start()
        first = []
        for e in range(n):
            first.append(copy(e, 0, srcs[e], me, sibling_pos))
            for j, k in enumerate(_OTHER_CHIPS):
                first.append(copy(e, 1 + j, srcs[e], me, _peer(k)[0]))
        for cp in first:
            cp.start()
        passed = []
        for e in range(n):
            for j, k in enumerate(_OTHER_CHIPS):
                pos, p = _peer(k)
                copy(e, 1 + j, srcs[e], p, pos).wait_recv()
                passed.append(copy(e, 4 + j, block(e, p), p, sibling_pos))
                passed[-1].start()
        for e in range(n):
            copy(e, 0, srcs[e], sibling, sibling_pos).wait_recv()
            for j, k in enumerate(_OTHER_CHIPS):
                p = _peer(k ^ 1)[1]
                copy(e, 4 + j, block(e, p), p, sibling_pos).wait_recv()
        for cp in first + passed:
            cp.wait_send()
        for cp in own:
            cp.wait()

    return pl.pallas_call(
        body,
        name=name,
        out_shape=tuple(outs),
        in_specs=[_HBM] * n,
        out_specs=tuple([_HBM] * n),
        scratch_shapes=[pltpu.SemaphoreType.DMA((n * slots,)), pltpu.SemaphoreType.DMA((n * slots,)), pltpu.SemaphoreType.DMA((n,))],
        compiler_params=pltpu.CompilerParams(vmem_limit_bytes=VMEM_LIMIT),
    )(*[st[0] for st in streams])


def _landing(streams, outs):
    me = _me()
    lands = [lax.empty(o.shape, o.dtype) for o in outs]
    for st in streams:
        lands[st[1]] = st[4](lands[st[1]], st[0], me)
    return lands


def _exchange_start(streams, lands, groups, after, name):
    n, n_land, n_grp = len(streams), len(lands), len(groups)
    side_effects = pltpu.SideEffectType.DATAFLOW_SIDE_EFFECTING

    def body(*refs):
        srcs, dsts = refs[:n], refs[n:n + n_land]
        outs = refs[n + n_land + len(after):]
        send_sems, recv_sems, token = outs[:n_grp], outs[n_grp:2 * n_grp], outs[-1]
        for gi, group in enumerate(groups):
            for j, e in enumerate(group):
                outgoing, _ = _stream_copies(streams[e], srcs[e], dsts[streams[e][1]], send_sems[gi], recv_sems[gi], j * (N_DEV - 1))
                for cp in outgoing:
                    cp.start()
        token[...] = jnp.zeros_like(token)

    sems = tuple(pltpu.SemaphoreType.DMA((len(g) * (N_DEV - 1),)) for g in groups)
    passed = [st[0] for st in streams] + list(lands)
    res = pl.pallas_call(
        body,
        name=name,
        out_shape=sems + sems + tuple(pltpu.HBM(a.shape, a.dtype) for a in passed) + (jax.ShapeDtypeStruct((8, 128), F32),),
        in_specs=[_HBM] * (n + n_land) + [pl.BlockSpec(memory_space=pl.ANY)] * len(after),
        out_specs=tuple([_SEM] * (2 * n_grp) + [_HBM] * (n + n_land) + [pl.BlockSpec(memory_space=pltpu.VMEM)]),
        input_output_aliases={i: 2 * n_grp + i for i in range(n + n_land)},
        compiler_params=pltpu.CompilerParams(has_side_effects=side_effects),
    )(*passed, *after)
    handle = dict(streams=streams, groups=groups, send=res[:n_grp], recv=res[n_grp:2 * n_grp],
                  srcs=res[2 * n_grp:2 * n_grp + n], lands=res[2 * n_grp + n:2 * n_grp + n + n_land])
    return handle, res[-1]


def _exchange_wait(handle, gi, after, name):
    streams, group = handle["streams"], handle["groups"][gi]
    land_ids = sorted({streams[e][1] for e in group})
    srcs = [handle["srcs"][e] for e in group]
    lands = [handle["lands"][i] for i in land_ids]
    n, n_land = len(srcs), len(lands)
    side_effects = pltpu.SideEffectType.DATAFLOW_SIDE_EFFECTING

    def body(*refs):
        src_refs, land_refs = refs[:n], refs[n:n + n_land]
        send_sem, recv_sem = refs[n + n_land:n + n_land + 2]
        for j, e in enumerate(group):
            outgoing, incoming = _stream_copies(streams[e], src_refs[j], land_refs[land_ids.index(streams[e][1])], send_sem, recv_sem, j * (N_DEV - 1))
            for cp in outgoing:
                cp.wait_send()
            for cp in incoming:
                cp.wait_recv()

    res = pl.pallas_call(
        body,
        name=name,
        out_shape=tuple(pltpu.HBM(a.shape, a.dtype) for a in lands),
        in_specs=[_HBM] * (n + n_land) + [_SEM, _SEM] + [pl.BlockSpec(memory_space=pl.ANY)] * len(after),
        out_specs=tuple([_HBM] * n_land),
        input_output_aliases={n + i: i for i in range(n_land)},
        compiler_params=pltpu.CompilerParams(has_side_effects=side_effects),
    )(*srcs, *lands, handle["send"][gi], handle["recv"][gi], *after)
    return list(res)


def _whole(ref, _):
    return ref


def _slot(ref, q):
    return ref.at[q]


def _lane_block(width):
    def pick(ref, q):
        idx = (slice(None),) * (len(ref.shape) - 1) + (pl.ds(pl.multiple_of(q * width, 128), width),)
        return ref.at[idx]
    return pick


def _row_block(height):
    def pick(ref, q):
        idx = (slice(None),) * (len(ref.shape) - 2) + (pl.ds(pl.multiple_of(q * height, 16), height), slice(None))
        return ref.at[idx]
    return pick


def _block_of(axis_from_end, size):
    return _lane_block(size) if axis_from_end == 1 else _row_block(size)


def _gather_streams(shards):
    streams, outs = [], []
    for i, (shard, axis) in enumerate(shards):
        shape = list(shard.shape)
        if axis is None:
            shape, land = [N_DEV] + shape, _slot
            place = _place_slot
        else:
            size = shape[-axis]
            land = _block_of(axis, size)
            shape[-axis] *= N_DEV
            place = lambda dst, src, me, size=size, axis=axis: lax.dynamic_update_slice_in_dim(dst, src, me * size, dst.ndim - axis)
        streams.append((shard, i, _whole, land, place))
        outs.append(jax.ShapeDtypeStruct(tuple(shape), shard.dtype))
    return streams, outs


def _place_slot(dst, src, me):
    return lax.dynamic_update_index_in_dim(dst, src, me, 0)


def _scatter_streams(fulls):
    streams, outs = [], []
    for i, (full, axis) in enumerate(fulls):
        shape = list(full.shape)
        size = shape[-axis] // N_DEV
        shape[-axis] = size
        place = lambda dst, src, me, size=size, axis=axis: _place_slot(
            dst, lax.dynamic_slice_in_dim(src, me * size, size, src.ndim - axis), me)
        streams.append((full, i, _block_of(axis, size), _slot, place))
        outs.append(jax.ShapeDtypeStruct((N_DEV,) + tuple(shape), full.dtype))
    return streams, outs


FFN_TM = 512
FFN_TF = 1024
FFN_FWD_TM = 1024
FFN_FWD_TF = 1024


def _ffn_fwd(h, g, w13, w2, after=()):
    t = h.shape[0]
    tm, tf = min(FFN_FWD_TM, t), FFN_FWD_TF
    nf = D_FF_PAD // tf

    def body(h_ref, g_ref, w1_ref, w3_ref, w2_ref, *rest):
        ho_ref, n_ref, a_ref, b_ref, n_scr, acc = rest[len(after):]
        f = pl.program_id(1)

        @pl.when(f == 0)
        def _():
            n_scr[...] = _rms(h_ref[...], g_ref[...]).astype(MXU_DTYPE)
            n_ref[...] = n_scr[...]
            acc[...] = jnp.zeros_like(acc)

        n = n_scr[...]
        a = _dot(n, w1_ref[...])
        b = _dot(n, w3_ref[...])
        a_ref[...] = a.astype(MXU_DTYPE)
        b_ref[...] = b.astype(MXU_DTYPE)
        s = (a * _sigmoid(a)) * b
        acc[...] += _dot(s.astype(MXU_DTYPE), w2_ref[...])

        @pl.when(f == nf - 1)
        def _():
            ho_ref[...] = h_ref[...] + FFN_RES_WEIGHT * acc[...]

    return pl.pallas_call(
        body,
        name="ffn_fwd",
        grid=(t // tm, nf),
        in_specs=[
            _rows(tm, D_MODEL),
            _resident((1, D_MODEL)),
            pl.BlockSpec((None, D_MODEL, tf), lambda i, f: (0, 0, f)),
            pl.BlockSpec((None, D_MODEL, tf), lambda i, f: (1, 0, f)),
            pl.BlockSpec((tf, D_MODEL), lambda i, f: (f, 0)),
        ] + [pl.BlockSpec(memory_space=pl.ANY)] * len(after),
        out_specs=[
            _rows(tm, D_MODEL),
            _rows(tm, D_MODEL),
            pl.BlockSpec((tm, tf), lambda i, f: (i, f)),
            pl.BlockSpec((tm, tf), lambda i, f: (i, f)),
        ],
        out_shape=[
            jax.ShapeDtypeStruct((t, D_MODEL), F32),
            jax.ShapeDtypeStruct((t, D_MODEL), MXU_DTYPE),
            jax.ShapeDtypeStruct((t, D_FF_PAD), MXU_DTYPE),
            jax.ShapeDtypeStruct((t, D_FF_PAD), MXU_DTYPE),
        ],
        scratch_shapes=[pltpu.VMEM((tm, D_MODEL), MXU_DTYPE), pltpu.VMEM((tm, D_MODEL), F32)],
        compiler_params=_cparams("parallel", "arbitrary"),
    )(h, g, w13, w13, w2, *after)


def _ffn_bwd(dout, h, g, a, b, w13, w2, after=()):
    t = h.shape[0]
    tm, tf = min(FFN_TM, t), FFN_TF
    nf = D_FF_PAD // tf

    def body(do_ref, h_ref, g_ref, a_ref, b_ref, w1_ref, w3_ref, w2_ref, *rest):
        dh_ref, dob_ref, s_ref, dab_ref, dg_ref, dob_scr, acc = rest[len(after):]
        i, f = pl.program_id(0), pl.program_id(1)

        @pl.when(f == 0)
        def _():
            dob_scr[...] = (FFN_RES_WEIGHT * do_ref[...]).astype(MXU_DTYPE)
            dob_ref[...] = dob_scr[...]
            acc[...] = jnp.zeros_like(acc)

        @pl.when((f == 0) & (i == 0))
        def _():
            dg_ref[...] = jnp.zeros_like(dg_ref)

        ds = _dot_nt(dob_scr[...], w2_ref[...])
        av = a_ref[...].astype(F32)
        bv = b_ref[...].astype(F32)
        sig = _sigmoid(av)
        sil = av * sig
        s_ref[...] = (sil * bv).astype(MXU_DTYPE)
        da = (ds * bv * (sig * (1.0 + av * (1.0 - sig)))).astype(MXU_DTYPE)
        db = (ds * sil).astype(MXU_DTYPE)
        dab_ref[0] = da
        dab_ref[1] = db
        acc[...] += _dot_nt(da, w1_ref[...]) + _dot_nt(db, w3_ref[...])

        @pl.when(f == nf - 1)
        def _():
            dx, dg = _rms_bwd(h_ref[...], g_ref[...], acc[...])
            dh_ref[...] = do_ref[...] + dx
            dg_ref[...] += dg

    return pl.pallas_call(
        body,
        name="ffn_bwd",
        grid=(t // tm, nf),
        in_specs=[
            _rows(tm, D_MODEL),
            _rows(tm, D_MODEL),
            _resident((1, D_MODEL)),
            pl.BlockSpec((tm, tf), lambda i, f: (i, f)),
            pl.BlockSpec((tm, tf), lambda i, f: (i, f)),
            pl.BlockSpec((None, D_MODEL, tf), lambda i, f: (0, 0, f)),
            pl.BlockSpec((None, D_MODEL, tf), lambda i, f: (1, 0, f)),
            pl.BlockSpec((tf, D_MODEL), lambda i, f: (f, 0)),
        ] + [pl.BlockSpec(memory_space=pl.ANY)] * len(after),
        out_specs=[
            _rows(tm, D_MODEL),
            _rows(tm, D_MODEL),
            pl.BlockSpec((tm, tf), lambda i, f: (i, f)),
            pl.BlockSpec((2, tm, tf), lambda i, f: (0, i, f)),
            pl.BlockSpec((1, D_MODEL), lambda i, f: (0, 0)),
        ],
        out_shape=[
            jax.ShapeDtypeStruct((t, D_MODEL), F32),
            jax.ShapeDtypeStruct((t, D_MODEL), MXU_DTYPE),
            jax.ShapeDtypeStruct((t, D_FF_PAD), MXU_DTYPE),
            jax.ShapeDtypeStruct((2, t, D_FF_PAD), MXU_DTYPE),
            jax.ShapeDtypeStruct((1, D_MODEL), F32),
        ],
        scratch_shapes=[pltpu.VMEM((tm, D_MODEL), MXU_DTYPE), pltpu.VMEM((tm, D_MODEL), F32)],
        compiler_params=_cparams("arbitrary", "arbitrary"),
    )(dout, h, g, a, b, w13, w13, w2, *after)


def _wgrad(x, y, name, out_dtype=WIRE_DTYPE, after=(), only=None):
    t, m = x.shape
    grouped = y.ndim == 3 and only is None
    groups = y.shape[0] if grouped else 1
    n = y.shape[-1]
    tk = min(2048, t)
    bm = m if m <= 1536 else m // 2
    bn = n if n <= 1536 else n // 2
    nk = t // tk

    def body(x_ref, y_ref, *rest):
        o_ref, acc = rest[len(after):]
        k = pl.program_id(3)

        @pl.when(k == 0)
        def _():
            acc[...] = jnp.zeros_like(acc)

        acc[...] += _dot_tn(x_ref[...].astype(MXU_DTYPE), y_ref[...].astype(MXU_DTYPE))

        @pl.when(k == nk - 1)
        def _():
            o_ref[...] = acc[...].astype(out_dtype)

    if grouped:
        y_spec = pl.BlockSpec((None, tk, bn), lambda g, i, j, k: (g, k, j))
        o_spec = pl.BlockSpec((None, bm, bn), lambda g, i, j, k: (g, i, j))
        o_shape = jax.ShapeDtypeStruct((groups, m, n), out_dtype)
    else:
        if only is None:
            y_spec = pl.BlockSpec((tk, bn), lambda g, i, j, k: (k, j))
        else:
            y_spec = pl.BlockSpec((None, tk, bn), lambda g, i, j, k: (only, k, j))
        o_spec = pl.BlockSpec((bm, bn), lambda g, i, j, k: (i, j))
        o_shape = jax.ShapeDtypeStruct((m, n), out_dtype)
    return pl.pallas_call(
        body,
        name=name,
        grid=(groups, m // bm, n // bn, nk),
        in_specs=[pl.BlockSpec((tk, bm), lambda g, i, j, k: (k, i)), y_spec] + [pl.BlockSpec(memory_space=pl.ANY)] * len(after),
        out_specs=o_spec,
        out_shape=o_shape,
        scratch_shapes=[pltpu.VMEM((bm, bn), F32)],
        compiler_params=_cparams("parallel", "parallel", "parallel", "arbitrary"),
    )(x, y, *after)


def _conv_pre(h, g, w_pw1):
    t = h.shape[0]
    tm = min(512, t)

    def body(h_ref, g_ref, w_ref, n_ref, u_ref, glu_ref):
        n = _rms(h_ref[...], g_ref[...]).astype(MXU_DTYPE)
        n_ref[...] = n
        u = _dot(n, w_ref[...])
        u_ref[...] = u
        glu_ref[...] = u[:, :D_MODEL] * _sigmoid(u[:, D_MODEL:])

    return pl.pallas_call(
        body,
        name="conv_pre",
        grid=(t // tm,),
        in_specs=[_rows(tm, D_MODEL), _resident((1, D_MODEL)), _resident((D_MODEL, 2 * D_MODEL))],
        out_specs=[_rows(tm, D_MODEL), _rows(tm, 2 * D_MODEL), _rows(tm, D_MODEL)],
        out_shape=[
            jax.ShapeDtypeStruct((t, D_MODEL), MXU_DTYPE),
            jax.ShapeDtypeStruct((t, 2 * D_MODEL), F32),
            jax.ShapeDtypeStruct((t, D_MODEL), F32),
        ],
        compiler_params=_cparams("parallel"),
    )(h, g, w_pw1)


def _shifted_copies(dst, src, rows):
    for r in range(1, 8):
        dst[r - 1] = src[pl.ds(r, rows), :]


def _window(src, shifted, offset, base, rows):
    r = offset % 8
    start = base + offset - r
    return src[pl.ds(start, rows), :] if r == 0 else shifted[r - 1, pl.ds(start, rows), :]


def _conv_main(glu, w_dw, g, w_pw2, h):
    t = h.shape[0]
    tm = min(512, t)
    per = tm // CONV_HALO
    shifted_rows = tm + CONV_HALO - 8

    def body(glu_ref, halo_ref, w_ref, g_ref, w2_ref, h_ref, ho_ref, c_ref, sw_ref, ext, ext_sh):
        i = pl.program_id(0)
        ext[pl.ds(0, CONV_HALO), :] = jnp.where(i == 0, 0.0, halo_ref[...])
        ext[pl.ds(CONV_HALO, tm), :] = glu_ref[...]
        _shifted_copies(ext_sh, ext, shifted_rows)
        for base in range(0, tm, CONV_ROWS):
            acc = jnp.zeros((CONV_ROWS, D_MODEL), F32)
            for k in range(CONV_WIDTH):
                acc = acc + _window(ext, ext_sh, CONV_HALO - (CONV_WIDTH - 1) + k, base, CONV_ROWS) * w_ref[pl.ds(k, 1), :]
            c_ref[pl.ds(base, CONV_ROWS), :] = acc
        y = _rms(c_ref[...], g_ref[...])
        sw = (y * _sigmoid(y)).astype(MXU_DTYPE)
        sw_ref[...] = sw
        ho_ref[...] = h_ref[...] + _dot(sw, w2_ref[...])

    return pl.pallas_call(
        body,
        name="conv_main",
        grid=(t // tm,),
        in_specs=[
            _rows(tm, D_MODEL),
            pl.BlockSpec((CONV_HALO, D_MODEL), lambda i: (jnp.maximum(i * per - 1, 0), 0)),
            _resident((CONV_WIDTH, D_MODEL)),
            _resident((1, D_MODEL)),
            _resident((D_MODEL, D_MODEL)),
            _rows(tm, D_MODEL),
        ],
        out_specs=[_rows(tm, D_MODEL), _rows(tm, D_MODEL), _rows(tm, D_MODEL)],
        out_shape=[
            jax.ShapeDtypeStruct((t, D_MODEL), F32),
            jax.ShapeDtypeStruct((t, D_MODEL), F32),
            jax.ShapeDtypeStruct((t, D_MODEL), MXU_DTYPE),
        ],
        scratch_shapes=[pltpu.VMEM((tm + CONV_HALO, D_MODEL), F32), pltpu.VMEM((7, shifted_rows, D_MODEL), F32)],
        compiler_params=_cparams("parallel"),
    )(glu, glu, w_dw, g, w_pw2, h)


def _conv_bwd_post(dout, c, g, w_pw2):
    t = dout.shape[0]
    tm = min(1024, t)

    def body(do_ref, c_ref, g_ref, w2_ref, dc_ref, dg_ref):
        @pl.when(pl.program_id(0) == 0)
        def _():
            dg_ref[...] = jnp.zeros_like(dg_ref)

        dsw = _dot_nt(do_ref[...].astype(MXU_DTYPE), w2_ref[...])
        cv = c_ref[...]
        y = _rms(cv, g_ref[...])
        sig = _sigmoid(y)
        dy = dsw * (sig * (1.0 + y * (1.0 - sig)))
        dc, dg = _rms_bwd(cv, g_ref[...], dy)
        dc_ref[...] = dc
        dg_ref[...] += dg

    return pl.pallas_call(
        body,
        name="conv_bwd_post",
        grid=(t // tm,),
        in_specs=[_rows(tm, D_MODEL), _rows(tm, D_MODEL), _resident((1, D_MODEL)), _resident((D_MODEL, D_MODEL))],
        out_specs=[_rows(tm, D_MODEL), pl.BlockSpec((1, D_MODEL), lambda i: (0, 0))],
        out_shape=[jax.ShapeDtypeStruct((t, D_MODEL), F32), jax.ShapeDtypeStruct((1, D_MODEL), F32)],
        compiler_params=_cparams("arbitrary"),
    )(dout, c, g, w_pw2)


def _conv_bwd_pre(dc, glu, u, w_dw, w_pw1, h, g, dout):
    t = h.shape[0]
    tm = min(256, t)
    per = tm // CONV_HALO
    nt = t // tm
    last_halo = t // CONV_HALO - 1
    shifted_rows = tm + CONV_HALO - 8

    def body(dc_ref, dcn_ref, glu_ref, gluh_ref, u_ref, w_ref, w1_ref, h_ref, g_ref, do_ref,
             dh_ref, du_ref, dw_ref, dg_ref, dext, gext, dext_sh, gext_sh):
        i = pl.program_id(0)

        @pl.when(i == 0)
        def _():
            dw_ref[...] = jnp.zeros_like(dw_ref)
            dg_ref[...] = jnp.zeros_like(dg_ref)

        dext[pl.ds(0, tm), :] = dc_ref[...]
        dext[pl.ds(tm, CONV_HALO), :] = jnp.where(i == nt - 1, 0.0, dcn_ref[...])
        gext[pl.ds(0, CONV_HALO), :] = jnp.where(i == 0, 0.0, gluh_ref[...])
        gext[pl.ds(CONV_HALO, tm), :] = glu_ref[...]
        _shifted_copies(dext_sh, dext, shifted_rows)
        _shifted_copies(gext_sh, gext, shifted_rows)
        for base in range(0, tm, CONV_ROWS):
            rows = pl.ds(base, CONV_ROWS)
            dglu = jnp.zeros((CONV_ROWS, D_MODEL), F32)
            for k in range(CONV_WIDTH):
                dglu = dglu + _window(dext, dext_sh, CONV_WIDTH - 1 - k, base, CONV_ROWS) * w_ref[pl.ds(k, 1), :]
            av, bv = u_ref[rows, :D_MODEL], u_ref[rows, D_MODEL:]
            sig = _sigmoid(bv)
            du_ref[rows, :D_MODEL] = (dglu * sig).astype(MXU_DTYPE)
            du_ref[rows, D_MODEL:] = (dglu * av * (sig * (1.0 - sig))).astype(MXU_DTYPE)
        for k in range(CONV_WIDTH):
            part = jnp.zeros((CONV_ROWS, D_MODEL), F32)
            for base in range(0, tm, CONV_ROWS):
                part = part + dc_ref[pl.ds(base, CONV_ROWS), :] * _window(gext, gext_sh, CONV_HALO - (CONV_WIDTH - 1) + k, base, CONV_ROWS)
            dw_ref[pl.ds(k, 1), :] += jnp.sum(part, axis=0, keepdims=True)
        dn = _dot_nt(du_ref[...], w1_ref[...])
        dx, dg = _rms_bwd(h_ref[...], g_ref[...], dn)
        dh_ref[...] = do_ref[...] + dx
        dg_ref[...] += dg

    return pl.pallas_call(
        body,
        name="conv_bwd_pre",
        grid=(nt,),
        in_specs=[
            _rows(tm, D_MODEL),
            pl.BlockSpec((CONV_HALO, D_MODEL), lambda i: (jnp.minimum((i + 1) * per, last_halo), 0)),
            _rows(tm, D_MODEL),
            pl.BlockSpec((CONV_HALO, D_MODEL), lambda i: (jnp.maximum(i * per - 1, 0), 0)),
            _rows(tm, 2 * D_MODEL),
            _resident((CONV_WIDTH, D_MODEL)),
            _resident((D_MODEL, 2 * D_MODEL)),
            _rows(tm, D_MODEL),
            _resident((1, D_MODEL)),
            _rows(tm, D_MODEL),
        ],
        out_specs=[
            _rows(tm, D_MODEL),
            _rows(tm, 2 * D_MODEL),
            pl.BlockSpec((CONV_HALO, D_MODEL), lambda i: (0, 0)),
            pl.BlockSpec((1, D_MODEL), lambda i: (0, 0)),
        ],
        out_shape=[
            jax.ShapeDtypeStruct((t, D_MODEL), F32),
            jax.ShapeDtypeStruct((t, 2 * D_MODEL), MXU_DTYPE),
            jax.ShapeDtypeStruct((CONV_HALO, D_MODEL), F32),
            jax.ShapeDtypeStruct((1, D_MODEL), F32),
        ],
        scratch_shapes=[pltpu.VMEM((tm + CONV_HALO, D_MODEL), F32), pltpu.VMEM((tm + CONV_HALO, D_MODEL), F32),
                        pltpu.VMEM((7, shifted_rows, D_MODEL), F32), pltpu.VMEM((7, shifted_rows, D_MODEL), F32)],
        compiler_params=_cparams("arbitrary"),
    )(dc, dc, glu, glu, u, w_dw, w_pw1, h, g, dout)


def _rope(s, cs, sa, sb):
    return s * cs - pltpu.roll(s, 96, 1) * sa + pltpu.roll(s, 32, 1) * sb


def _unrope(d, cs, sa, sb):
    return d * cs + pltpu.roll(d, 96, 1) * sa - pltpu.roll(d, 32, 1) * sb


def _mla_pre(h, g, w_a, g_q, g_kv, w_uq, w_ukv, cs, sa, sb):
    t = h.shape[0]
    tm = min(512, t)
    qw = N_HEADS * HEAD_PAD
    vw = N_HEADS * V_HEAD

    def body(h_ref, g_ref, wa_ref, gq_ref, gkv_ref, wuq_ref, wukv_ref, cs_ref, sa_ref, sb_ref,
             n_ref, a_ref, cq_ref, ckv_ref, q_ref, k_ref, v_ref):
        n = _rms(h_ref[...], g_ref[...]).astype(MXU_DTYPE)
        n_ref[...] = n
        a = _dot(n, wa_ref[...])
        a_ref[...] = a
        cq = _rms(a[:, :Q_LORA], gq_ref[...]).astype(MXU_DTYPE)
        ckv = _rms(a[:, Q_LORA:Q_LORA + KV_LORA], gkv_ref[...]).astype(MXU_DTYPE)
        cq_ref[...] = cq
        ckv_ref[...] = ckv
        q = _dot(cq, wuq_ref[...]) * ATTN_SCALE
        kv = _dot(ckv, wukv_ref[...])
        cs_, sa_, sb_ = cs_ref[...], sa_ref[...], sb_ref[...]
        k_rot = _rope(a[:, Q_LORA + KV_LORA:], cs_, sa_, sb_).astype(MXU_DTYPE)
        for hd in range(N_HEADS):
            lo = hd * HEAD_PAD
            q_ref[:, lo:lo + QK_NOPE] = q[:, lo:lo + QK_NOPE].astype(MXU_DTYPE)
            q_ref[:, lo + QK_NOPE:lo + HEAD_PAD] = _rope(q[:, lo + QK_NOPE:lo + HEAD_PAD], cs_, sa_, sb_).astype(MXU_DTYPE)
            k_ref[:, lo:lo + QK_NOPE] = kv[:, lo:lo + QK_NOPE].astype(MXU_DTYPE)
            k_ref[:, lo + QK_NOPE:lo + HEAD_PAD] = k_rot
            v_ref[:, hd * V_HEAD:(hd + 1) * V_HEAD] = kv[:, lo + QK_NOPE:lo + HEAD_PAD].astype(MXU_DTYPE)

    return pl.pallas_call(
        body,
        name="mla_pre",
        grid=(t // tm,),
        in_specs=[
            _rows(tm, D_MODEL),
            _resident((1, D_MODEL)),
            _resident((D_MODEL, A_PAD)),
            _resident((1, Q_LORA)),
            _resident((1, KV_LORA)),
            _resident((Q_LORA, qw)),
            _resident((KV_LORA, qw)),
            _rows(tm, 128),
            _rows(tm, 128),
            _rows(tm, 128),
        ],
        out_specs=[_rows(tm, D_MODEL), _rows(tm, A_PAD), _rows(tm, Q_LORA), _rows(tm, KV_LORA), _rows(tm, qw), _rows(tm, qw), _rows(tm, vw)],
        out_shape=[
            jax.ShapeDtypeStruct((t, D_MODEL), MXU_DTYPE),
            jax.ShapeDtypeStruct((t, A_PAD), F32),
            jax.ShapeDtypeStruct((t, Q_LORA), MXU_DTYPE),
            jax.ShapeDtypeStruct((t, KV_LORA), MXU_DTYPE),
            jax.ShapeDtypeStruct((t, qw), MXU_DTYPE),
            jax.ShapeDtypeStruct((t, qw), MXU_DTYPE),
            jax.ShapeDtypeStruct((t, vw), MXU_DTYPE),
        ],
        compiler_params=_cparams("parallel"),
    )(h, g, w_a, g_q, g_kv, w_uq, w_ukv, cs, sa, sb)


def _attn_tile(t):
    return min(1024, t)


def _chunk_mask(rows, cols):
    r = lax.broadcasted_iota(jnp.int32, (rows, cols), 0)
    c = lax.broadcasted_iota(jnp.int32, (rows, cols), 1)
    return (r >> CHUNK_SHIFT) >= (c >> CHUNK_SHIFT)


def _causal_pairs(n, by_column=False):
    if by_column:
        pairs = [(i, j) for j in range(n) for i in range(j, n)]
    else:
        pairs = [(i, j) for i in range(n) for j in range(i + 1)]
    return jnp.asarray([p[0] for p in pairs], jnp.int32), jnp.asarray([p[1] for p in pairs], jnp.int32)


def _flash_fwd(q, k, v):
    t = q.shape[0]
    tq = tk = _attn_tile(t)
    rows_of, cols_of = _causal_pairs(t // tq)

    def body(qi_ref, ki_ref, q_ref, k_ref, v_ref, o_ref, lse_ref, m_s, l_s, acc):
        step = pl.program_id(1)
        qi, ki = qi_ref[step], ki_ref[step]

        @pl.when(ki == 0)
        def _():
            m_s[...] = jnp.full_like(m_s, -jnp.inf)
            l_s[...] = jnp.zeros_like(l_s)
            acc[...] = jnp.zeros_like(acc)

        def block(on_diagonal):
            s = _dot_nt(q_ref[...], k_ref[...])
            if on_diagonal:
                s = jnp.where(_chunk_mask(tq, tk), s, -jnp.inf)
            m_prev = m_s[...]
            m_new = jnp.maximum(m_prev, jnp.max(s, axis=1, keepdims=True))
            alpha = jnp.exp(m_prev - m_new)
            p = jnp.exp(s - m_new)
            l_s[...] = alpha * l_s[...] + jnp.sum(p, axis=1, keepdims=True)
            acc[...] = alpha * acc[...] + _dot(p.astype(MXU_DTYPE), v_ref[...])
            m_s[...] = m_new

        pl.when(ki < qi)(lambda: block(False))

        @pl.when(ki == qi)
        def _():
            block(True)
            o_ref[...] = acc[...] / l_s[...]
            lse_ref[...] = jnp.broadcast_to(m_s[...] + jnp.log(l_s[...]), (tq, V_HEAD))

    grid_spec = pltpu.PrefetchScalarGridSpec(
        num_scalar_prefetch=2,
        grid=(N_HEADS, rows_of.shape[0]),
        in_specs=[
            pl.BlockSpec((tq, HEAD_PAD), lambda h, s, qi, ki: (qi[s], h)),
            pl.BlockSpec((tk, HEAD_PAD), lambda h, s, qi, ki: (ki[s], h)),
            pl.BlockSpec((tk, V_HEAD), lambda h, s, qi, ki: (ki[s], h)),
        ],
        out_specs=[pl.BlockSpec((tq, V_HEAD), lambda h, s, qi, ki: (qi[s], h)), pl.BlockSpec((tq, V_HEAD), lambda h, s, qi, ki: (qi[s], h))],
        scratch_shapes=[pltpu.VMEM((tq, 1), F32), pltpu.VMEM((tq, 1), F32), pltpu.VMEM((tq, V_HEAD), F32)],
    )
    return pl.pallas_call(
        body,
        name="flash_fwd",
        grid_spec=grid_spec,
        out_shape=[jax.ShapeDtypeStruct((t, N_HEADS * V_HEAD), F32), jax.ShapeDtypeStruct((t, N_HEADS * V_HEAD), F32)],
        compiler_params=_cparams("parallel", "arbitrary"),
    )(rows_of, cols_of, q, k, v)


def _flash_bwd(q, k, v, do, lse, delta):
    t = q.shape[0]
    tq = tk = _attn_tile(t)
    nq = t // tq
    rows_of, cols_of = _causal_pairs(nq, by_column=True)

    def body(qi_ref, kj_ref, q_ref, k_ref, v_ref, do_ref, lse_ref, dl_ref, dq_ref, dk_ref, dv_ref, dk_acc, dv_acc):
        step = pl.program_id(1)
        qi, kj = qi_ref[step], kj_ref[step]

        @pl.when(step == 0)
        def _():
            dq_ref[...] = jnp.zeros_like(dq_ref)

        def block(on_diagonal):
            qv, kv, dov = q_ref[...], k_ref[...], do_ref[...]
            s = _dot_nt(qv, kv)
            if on_diagonal:
                s = jnp.where(_chunk_mask(tq, tk), s, -jnp.inf)
            p = jnp.exp(s - lse_ref[:, :1])
            dp = _dot_nt(dov, v_ref[...])
            ds = (p * (dp - dl_ref[:, :1])).astype(MXU_DTYPE)
            dv_new = _dot_tn(p.astype(MXU_DTYPE), dov)
            dk_new = _dot_tn(ds, qv)
            if on_diagonal:
                dv_acc[...] = dv_new
                dk_acc[...] = dk_new
            else:
                dv_acc[...] += dv_new
                dk_acc[...] += dk_new
            rows = pl.ds(pl.multiple_of(qi * tq, tq), tq)
            dq_ref[rows, :] += _dot(ds, kv)

        pl.when(qi > kj)(lambda: block(False))
        pl.when(qi == kj)(lambda: block(True))

        @pl.when(qi == nq - 1)
        def _():
            dk_ref[...] = dk_acc[...]
            dv_ref[...] = dv_acc[...]

    grid_spec = pltpu.PrefetchScalarGridSpec(
        num_scalar_prefetch=2,
        grid=(N_HEADS, rows_of.shape[0]),
        in_specs=[
            pl.BlockSpec((tq, HEAD_PAD), lambda h, s, qi, kj: (qi[s], h)),
            pl.BlockSpec((tk, HEAD_PAD), lambda h, s, qi, kj: (kj[s], h)),
            pl.BlockSpec((tk, V_HEAD), lambda h, s, qi, kj: (kj[s], h)),
            pl.BlockSpec((tq, V_HEAD), lambda h, s, qi, kj: (qi[s], h)),
            pl.BlockSpec((tq, V_HEAD), lambda h, s, qi, kj: (qi[s], h)),
            pl.BlockSpec((tq, V_HEAD), lambda h, s, qi, kj: (qi[s], h)),
        ],
        out_specs=[
            pl.BlockSpec((t, HEAD_PAD), lambda h, s, qi, kj: (0, h)),
            pl.BlockSpec((tk, HEAD_PAD), lambda h, s, qi, kj: (kj[s], h)),
            pl.BlockSpec((tk, V_HEAD), lambda h, s, qi, kj: (kj[s], h)),
        ],
        scratch_shapes=[pltpu.VMEM((tk, HEAD_PAD), F32), pltpu.VMEM((tk, V_HEAD), F32)],
    )
    return pl.pallas_call(
        body,
        name="flash_bwd",
        grid_spec=grid_spec,
        out_shape=[
            jax.ShapeDtypeStruct((t, N_HEADS * HEAD_PAD), F32),
            jax.ShapeDtypeStruct((t, N_HEADS * HEAD_PAD), F32),
            jax.ShapeDtypeStruct((t, N_HEADS * V_HEAD), F32),
        ],
        compiler_params=_cparams("arbitrary", "arbitrary"),
    )(rows_of, cols_of, q, k, v, do, lse, delta)


def _attn_out(o, w_o, h):
    t = h.shape[0]
    tm = min(1024, t)

    def body(o_ref, w_ref, h_ref, ho_ref):
        ho_ref[...] = h_ref[...] + _dot(o_ref[...].astype(MXU_DTYPE), w_ref[...])

    return pl.pallas_call(
        body,
        name="attn_out",
        grid=(t // tm,),
        in_specs=[_rows(tm, D_MODEL), _resident((D_MODEL, D_MODEL)), _rows(tm, D_MODEL)],
        out_specs=_rows(tm, D_MODEL),
        out_shape=jax.ShapeDtypeStruct((t, D_MODEL), F32),
        compiler_params=_cparams("parallel"),
    )(o, w_o, h)


def _attn_out_bwd(dout, o, w_o):
    t = dout.shape[0]
    tm = min(1024, t)

    def body(d_ref, o_ref, w_ref, do_ref, dl_ref):
        do = _dot_nt(d_ref[...].astype(MXU_DTYPE), w_ref[...])
        do_ref[...] = do.astype(MXU_DTYPE)
        prod = do * o_ref[...]
        for hd in range(N_HEADS):
            lanes = slice(hd * V_HEAD, (hd + 1) * V_HEAD)
            dl_ref[:, lanes] = jnp.broadcast_to(jnp.sum(prod[:, lanes], axis=1, keepdims=True), (tm, V_HEAD))

    return pl.pallas_call(
        body,
        name="attn_out_bwd",
        grid=(t // tm,),
        in_specs=[_rows(tm, D_MODEL), _rows(tm, D_MODEL), _resident((D_MODEL, D_MODEL))],
        out_specs=[_rows(tm, D_MODEL), _rows(tm, D_MODEL)],
        out_shape=[jax.ShapeDtypeStruct((t, D_MODEL), MXU_DTYPE), jax.ShapeDtypeStruct((t, D_MODEL), F32)],
        compiler_params=_cparams("parallel"),
    )(dout, o, w_o)


def _mla_bwd_pre(dq, dk, dv, a, h, dout, g, g_q, g_kv, w_a, w_uq, w_ukv, cs, sa, sb):
    t = h.shape[0]
    tm = min(512, t)
    qw = N_HEADS * HEAD_PAD
    vw = N_HEADS * V_HEAD

    def body(dq_ref, dk_ref, dv_ref, a_ref, h_ref, do_ref, g_ref, gq_ref, gkv_ref, wa_ref, wuq_ref, wukv_ref,
             cs_ref, sa_ref, sb_ref, dh_ref, dqp_ref, dkv_ref, da_ref, dg_ref, dgq_ref, dgkv_ref):
        @pl.when(pl.program_id(0) == 0)
        def _():
            dg_ref[...] = jnp.zeros_like(dg_ref)
            dgq_ref[...] = jnp.zeros_like(dgq_ref)
            dgkv_ref[...] = jnp.zeros_like(dgkv_ref)

        cs_, sa_, sb_ = cs_ref[...], sa_ref[...], sb_ref[...]
        slab = jnp.zeros((tm, 128), F32)
        for hd in range(N_HEADS):
            lo = hd * HEAD_PAD
            dqp_ref[:, lo:lo + QK_NOPE] = (dq_ref[:, lo:lo + QK_NOPE] * ATTN_SCALE).astype(MXU_DTYPE)
            dqp_ref[:, lo + QK_NOPE:lo + HEAD_PAD] = _unrope(dq_ref[:, lo + QK_NOPE:lo + HEAD_PAD] * ATTN_SCALE, cs_, sa_, sb_).astype(MXU_DTYPE)
            dkv_ref[:, lo:lo + QK_NOPE] = dk_ref[:, lo:lo + QK_NOPE].astype(MXU_DTYPE)
            dkv_ref[:, lo + QK_NOPE:lo + HEAD_PAD] = dv_ref[:, hd * V_HEAD:(hd + 1) * V_HEAD].astype(MXU_DTYPE)
            slab = slab + dk_ref[:, lo + QK_NOPE:lo + HEAD_PAD]
        dcq = _dot_nt(dqp_ref[...], wuq_ref[...])
        dckv = _dot_nt(dkv_ref[...], wukv_ref[...])
        av = a_ref[...]
        daq, dgq = _rms_bwd(av[:, :Q_LORA], gq_ref[...], dcq)
        dakv, dgkv = _rms_bwd(av[:, Q_LORA:Q_LORA + KV_LORA], gkv_ref[...], dckv)
        da_ref[:, :Q_LORA] = daq.astype(MXU_DTYPE)
        da_ref[:, Q_LORA:Q_LORA + KV_LORA] = dakv.astype(MXU_DTYPE)
        da_ref[:, Q_LORA + KV_LORA:] = _unrope(slab, cs_, sa_, sb_).astype(MXU_DTYPE)
        dn = _dot_nt(da_ref[...], wa_ref[...])
        dx, dg = _rms_bwd(h_ref[...], g_ref[...], dn)
        dh_ref[...] = do_ref[...] + dx
        dg_ref[...] += dg
        dgq_ref[...] += dgq
        dgkv_ref[...] += dgkv

    def const(n):
        return pl.BlockSpec((1, n), lambda i: (0, 0))

    return pl.pallas_call(
        body,
        name="mla_bwd_pre",
        grid=(t // tm,),
        in_specs=[
            _rows(tm, qw), _rows(tm, qw), _rows(tm, vw), _rows(tm, A_PAD), _rows(tm, D_MODEL), _rows(tm, D_MODEL),
            _resident((1, D_MODEL)), _resident((1, Q_LORA)), _resident((1, KV_LORA)),
            _resident((D_MODEL, A_PAD)), _resident((Q_LORA, qw)), _resident((KV_LORA, qw)),
            _rows(tm, 128), _rows(tm, 128), _rows(tm, 128),
        ],
        out_specs=[_rows(tm, D_MODEL), _rows(tm, qw), _rows(tm, qw), _rows(tm, A_PAD),
                   const(D_MODEL), const(Q_LORA), const(KV_LORA)],
        out_shape=[
            jax.ShapeDtypeStruct((t, D_MODEL), F32),
            jax.ShapeDtypeStruct((t, qw), MXU_DTYPE),
            jax.ShapeDtypeStruct((t, qw), MXU_DTYPE),
            jax.ShapeDtypeStruct((t, A_PAD), MXU_DTYPE),
            jax.ShapeDtypeStruct((1, D_MODEL), F32),
            jax.ShapeDtypeStruct((1, Q_LORA), F32),
            jax.ShapeDtypeStruct((1, KV_LORA), F32),
        ],
        compiler_params=_cparams("arbitrary"),
    )(dq, dk, dv, a, h, dout, g, g_q, g_kv, w_a, w_uq, w_ukv, cs, sa, sb)


def _loss_head(h, g, target):
    t = h.shape[0]
    tm = min(1024, t)

    def body(h_ref, g_ref, t_ref, sq_ref, dh_ref, dg_ref):
        @pl.when(pl.program_id(0) == 0)
        def _():
            sq_ref[...] = jnp.zeros_like(sq_ref)
            dg_ref[...] = jnp.zeros_like(dg_ref)

        x = h_ref[...]
        err = _rms(x, g_ref[...]) - t_ref[...]
        sq_ref[...] += jnp.sum(err * err, axis=0, keepdims=True)
        dx, dg = _rms_bwd(x, g_ref[...], err * (1.0 / D_MODEL))
        dh_ref[...] = dx
        dg_ref[...] += dg

    return pl.pallas_call(
        body,
        name="loss_head",
        grid=(t // tm,),
        in_specs=[_rows(tm, D_MODEL), _resident((1, D_MODEL)), _rows(tm, D_MODEL)],
        out_specs=[pl.BlockSpec((1, D_MODEL), lambda i: (0, 0)), _rows(tm, D_MODEL), pl.BlockSpec((1, D_MODEL), lambda i: (0, 0))],
        out_shape=[jax.ShapeDtypeStruct((1, D_MODEL), F32), jax.ShapeDtypeStruct((t, D_MODEL), F32), jax.ShapeDtypeStruct((1, D_MODEL), F32)],
        compiler_params=_cparams("arbitrary"),
    )(h, g, target)


def _adamw(parts, w, m, v, name, tr, layer=0, row_offset=0, into=None):
    layers, rows, width = w.shape
    pwidth = parts.shape[-1]
    off = row_offset // tr

    def body(p_ref, w_ref, m_ref, v_ref, *rest):
        g_ref, d_ref, mo_ref, vo_ref = rest[-4:]

        def part(q):
            return p_ref[q, :, pl.ds(0, width)].astype(F32)

        g = part(0)
        for q in range(1, N_DEV):
            g = g + part(q)
        g_ref[...] = g
        m_new = ADAM_B1 * m_ref[...] + (1.0 - ADAM_B1) * g
        v_new = ADAM_B2 * v_ref[...] + (1.0 - ADAM_B2) * (g * g)
        m_hat = m_new / (1.0 - ADAM_B1 ** ADAM_STEP)
        v_hat = v_new / (1.0 - ADAM_B2 ** ADAM_STEP)
        d_ref[...] = -ADAM_LR * (m_hat / (jnp.sqrt(v_hat) + ADAM_EPS) + ADAM_WD * w_ref[...])
        mo_ref[...] = m_new
        vo_ref[...] = v_new

    blk = pl.BlockSpec((None, tr, width), lambda i: (layer, i, 0))
    earlier = () if into is None else tuple(into)
    return pl.pallas_call(
        body,
        name=name,
        grid=(rows // tr,),
        in_specs=[pl.BlockSpec((N_DEV, tr, pwidth), lambda i: (0, off + i, 0)), blk, blk, blk] + [pl.BlockSpec(memory_space=pl.ANY)] * len(earlier),
        out_specs=[blk, blk, blk, blk],
        out_shape=[jax.ShapeDtypeStruct((layers, rows, width), F32)] * 4,
        input_output_aliases={4 + j: j for j in range(len(earlier))},
        compiler_params=_cparams("parallel"),
    )(parts, w, m, v, *earlier)


_GAINS = ("ffn_norm1", "mix_norm", "ffn_norm2", "conv_norm", "final_norm")
_TINY = ("conv_w_dw", "mla_q_norm", "mla_kv_norm")


def _small_pack(parts):
    flat = jnp.concatenate([p.reshape(-1).astype(F32) for p in parts])
    return jnp.pad(flat, (0, SMALL_ROWS * D_MODEL - flat.shape[0])).reshape(SMALL_ROWS, D_MODEL)


def _small_unpack(packed, shapes):
    flat, out, off = packed.reshape(-1), [], 0
    for s in shapes:
        size = 1
        for d in s:
            size *= d
        out.append(flat[off:off + size].reshape(s))
        off += size
    return out


def _pad_to(a, axis, size):
    pad = [(0, 0)] * a.ndim
    pad[axis] = (0, size - a.shape[axis])
    return jnp.pad(a, pad)


def kernel(x, positions, ffn_norm1, ffn1_w1, ffn1_w3, ffn1_w2, mix_norm, ffn_norm2, ffn2_w1, ffn2_w3, ffn2_w2, conv_w_pw1, conv_w_dw, conv_norm, conv_w_pw2, mla_w_a, mla_q_norm, mla_kv_norm, mla_w_uq, mla_w_ukv, mla_w_o, final_norm, loss_target, m_ffn_norm1, m_ffn1_w1, m_ffn1_w3, m_ffn1_w2, m_mix_norm, m_ffn_norm2, m_ffn2_w1, m_ffn2_w3, m_ffn2_w2, m_conv_w_pw1, m_conv_w_dw, m_conv_norm, m_conv_w_pw2, m_mla_w_a, m_mla_q_norm, m_mla_kv_norm, m_mla_w_uq, m_mla_w_ukv, m_mla_w_o, m_final_norm, v_ffn_norm1, v_ffn1_w1, v_ffn1_w3, v_ffn1_w2, v_mix_norm, v_ffn_norm2, v_ffn2_w1, v_ffn2_w3, v_ffn2_w2, v_conv_w_pw1, v_conv_w_dw, v_conv_norm, v_conv_w_pw2, v_mla_w_a, v_mla_q_norm, v_mla_kv_norm, v_mla_w_uq, v_mla_w_ukv, v_mla_w_o, v_final_norm):
    args = dict(locals())
    wire = lambda a: a.astype(WIRE_DTYPE)

    def ffn_shards(prefix, layer):
        w13_shard = _pad_to(jnp.stack([args[prefix + "_w1"][layer], args[prefix + "_w3"][layer]]), 2, FF_SHARD_PAD)
        return [(wire(w13_shard), 1), (wire(_pad_to(args[prefix + "_w2"][layer], 0, FF_SHARD_PAD)), 2)]

    def uq_pad(a):
        return _pad_to(a, 3, HEAD_PAD).reshape(1, a.shape[1], N_HEADS * HEAD_PAD)

    ukv_rows = lambda a: a.reshape(1, a.shape[1], N_HEADS * HEAD_PAD)
    tiny_shapes = [conv_w_dw.shape, mla_q_norm.shape, mla_kv_norm.shape]

    first = _gather_two_level(*_gather_streams(ffn_shards("ffn1", 0)), "weight_gather_first")
    later_groups = [
        [(wire(conv_w_pw1[0]), 1), (wire(conv_w_pw2[0]), 2), (_small_pack([conv_w_dw, mla_q_norm, mla_kv_norm]), None)],
        ffn_shards("ffn2", 0),
        ffn_shards("ffn1", 1),
        [(wire(_pad_to(mla_w_a[0], 1, A_PAD)), 2), (wire(uq_pad(mla_w_uq)[0]), 2), (wire(ukv_rows(mla_w_ukv)[0]), 2), (wire(mla_w_o[0]), 2)],
        ffn_shards("ffn2", 1),
    ]
    later_streams, later_outs = _gather_streams([sh for grp in later_groups for sh in grp])
    group_ids, at = [], 0
    for grp in later_groups:
        group_ids.append(list(range(at, at + len(grp))))
        at += len(grp)
    gather, gather_token = _exchange_start(
        later_streams, _landing(later_streams, later_outs), group_ids, (first[0],), "weight_gather_start")

    inv_freq = ROPE_THETA ** (-2.0 * jnp.arange(QK_ROPE // 2, dtype=F32) / QK_ROPE)
    ang = positions[0].astype(F32)[:, None] * inv_freq
    cos, sin, zero = jnp.cos(ang), jnp.sin(ang), jnp.zeros_like(ang)
    cs = jnp.concatenate([cos, cos, zero, zero], axis=1)
    sa = jnp.concatenate([sin, zero, zero, zero], axis=1)
    sb = jnp.concatenate([zero, sin, zero, zero], axis=1)

    ffn_w = {("ffn1", 0): first}
    h0 = x[0]
    h1, n1, a1, b1 = _ffn_fwd(h0, ffn_norm1[0:1], *ffn_w[("ffn1", 0)], after=(gather_token,))
    w_pw1, w_pw2, tiny = _exchange_wait(gather, 0, (h1,), "weight_gather_wait_conv")
    tiny = [_small_unpack(tiny[q], tiny_shapes) for q in range(N_DEV)]
    w_dw = jnp.concatenate([t_[0][0] for t_ in tiny], axis=1)
    g_q = jnp.concatenate([t_[1] for t_ in tiny], axis=1)
    g_kv = jnp.concatenate([t_[2] for t_ in tiny], axis=1)
    n_c, u, glu = _conv_pre(h1, mix_norm[0:1], w_pw1)
    h2, c, sw = _conv_main(glu, w_dw, conv_norm, w_pw2, h1)
    ffn_w[("ffn2", 0)] = _exchange_wait(gather, 1, (h2,), "weight_gather_wait_ffn2_0")
    h3, n2, a2, b2 = _ffn_fwd(h2, ffn_norm2[0:1], *ffn_w[("ffn2", 0)])
    ffn_w[("ffn1", 1)] = _exchange_wait(gather, 2, (h3,), "weight_gather_wait_ffn1_1")
    h4, n3, a3, b3 = _ffn_fwd(h3, ffn_norm1[1:2], *ffn_w[("ffn1", 1)])
    w_a, w_uq, w_ukv, w_o = _exchange_wait(gather, 3, (h4,), "weight_gather_wait_mla")
    n_a, a_lat, cq, ckv, q, k, v = _mla_pre(h4, mix_norm[1:2], w_a, g_q, g_kv, w_uq, w_ukv, cs, sa, sb)
    o, lse = _flash_fwd(q, k, v)
    h5 = _attn_out(o, w_o, h4)
    ffn_w[("ffn2", 1)] = _exchange_wait(gather, 4, (h5,), "weight_gather_wait_ffn2_1")
    h6, n4, a4, b4 = _ffn_fwd(h5, ffn_norm2[1:2], *ffn_w[("ffn2", 1)])

    sq, dh, dg_final = _loss_head(h6, final_norm[None, :], loss_target[0])
    loss = lax.psum(0.5 / D_MODEL * jnp.sum(sq), ("x", "y", "c"))

    gain = {}

    def ffn_backward(prefix, norm, layer, dh, h_in, n, a, b, after=()):
        dh, dob, s, dab, gain[(norm, layer)] = _ffn_bwd(dh, h_in, args[norm][layer:layer + 1], a, b, *ffn_w[(prefix, layer)], after=after)
        return dh, [(_wgrad(n, dab, "wgrad_ffn_in"), 1), (_wgrad(s, dob, "wgrad_ffn_out"), 2)]

    dh, g_ffn2_1 = ffn_backward("ffn2", "ffn_norm2", 1, dh, h5, n4, a4, b4)
    do, delta = _attn_out_bwd(dh, o, w_o)
    d_o = _wgrad(o, dh, "wgrad_attn_out")
    dq, dk, dv = _flash_bwd(q, k, v, do, lse, delta)
    dh, dqp, dkv, da_lat, gain[("mix_norm", 1)], d_gq, d_gkv = _mla_bwd_pre(
        dq, dk, dv, a_lat, h4, dh, mix_norm[1:2], g_q, g_kv, w_a, w_uq, w_ukv, cs, sa, sb)
    g_mla = [(_wgrad(n_a, da_lat, "wgrad_mla_a"), 2), (_wgrad(cq, dqp, "wgrad_mla_uq"), 2), (_wgrad(ckv, dkv, "wgrad_mla_ukv"), 2), (d_o, 2)]
    dh, g_ffn1_1 = ffn_backward("ffn1", "ffn_norm1", 1, dh, h3, n3, a3, b3)
    streams_a, outs_a = _scatter_streams(g_ffn2_1 + g_mla + g_ffn1_1)
    scatter_a, token_a = _exchange_start(
        streams_a, _landing(streams_a, outs_a), [list(range(len(streams_a)))], (), "grad_scatter_start_layer1")

    dh, g_ffn2_0 = ffn_backward("ffn2", "ffn_norm2", 0, dh, h2, n2, a2, b2, after=(token_a,))
    dc, gain[("conv_norm", 0)] = _conv_bwd_post(dh, c, conv_norm, w_pw2)
    d_pw2 = _wgrad(sw, dh, "wgrad_conv_pw2")
    dh, du, d_dw, gain[("mix_norm", 0)] = _conv_bwd_pre(dc, glu, u, w_dw, w_pw1, h1, mix_norm[0:1], dh)
    d_pw1 = _wgrad(n_c, du, "wgrad_conv_pw1")
    streams_b, outs_b = _scatter_streams(g_ffn2_0 + [(d_pw1, 1), (d_pw2, 2)])
    scatter_b, token_b = _exchange_start(
        streams_b, _landing(streams_b, outs_b), [list(range(len(streams_b)))], (), "grad_scatter_start_conv")

    dh, dob, s_act, dab, gain[("ffn_norm1", 0)] = _ffn_bwd(dh, h0, ffn_norm1[0:1], a1, b1, *ffn_w[("ffn1", 0)], after=(token_b,))
    grad_x = dh[None]
    streams_d, outs_d = _scatter_streams([(_wgrad(s_act, dob, "wgrad_ffn_out"), 2)])
    scatter_d, token_d = _exchange_start(streams_d, _landing(streams_d, outs_d), [[0]], (), "grad_scatter_start_last_out")
    streams_e, outs_e = _scatter_streams([(_wgrad(n1, dab, "wgrad_ffn_in_half", after=(token_d,), only=0), 1)])
    scatter_e, token_e = _exchange_start(streams_e, _landing(streams_e, outs_e), [[0]], (), "grad_scatter_start_last_w1")
    g_ffn1_0 = [(_wgrad(n1, dab, "wgrad_ffn_in_half", after=(token_e,), only=1), 1)]

    gain_rows = jnp.concatenate([
        gain[("ffn_norm1", 0)], gain[("ffn_norm1", 1)], gain[("mix_norm", 0)], gain[("mix_norm", 1)],
        gain[("ffn_norm2", 0)], gain[("ffn_norm2", 1)], gain[("conv_norm", 0)], dg_final])
    dw_by_dest = jnp.moveaxis(d_dw[:CONV_WIDTH].reshape(CONV_WIDTH, N_DEV, -1), 1, 0)
    small_by_dest = jnp.stack([
        _small_pack([gain_rows, dw_by_dest[p], d_gq.reshape(N_DEV, -1)[p], d_gkv.reshape(N_DEV, -1)[p]]) for p in range(N_DEV)])
    streams_c, outs_c = _scatter_streams(g_ffn1_0)
    streams_c.append((small_by_dest, len(outs_c), _slot, _slot, lambda dst, src, me: _place_slot(dst, lax.dynamic_index_in_dim(src, me, 0, False), me)))
    outs_c.append(jax.ShapeDtypeStruct((N_DEV, SMALL_ROWS, D_MODEL), F32))
    scatter_c, token_c = _exchange_start(streams_c, _landing(streams_c, outs_c), [list(range(len(streams_c)))], (), "grad_scatter_start_last")

    (p13_ffn2_1, p2_ffn2_1, p_a, p_uq, p_ukv, p_o, p13_ffn1_1, p2_ffn1_1) = _exchange_wait(scatter_a, 0, (token_c,), "grad_scatter_wait_layer1")
    p13_ffn2_0, p2_ffn2_0, p_pw1, p_pw2 = _exchange_wait(scatter_b, 0, (token_c,), "grad_scatter_wait_conv")
    results = {}

    def adam(name, parts, tr, view=lambda a: a, **kw):
        results[name] = _adamw(parts, view(args[name]), view(args["m_" + name]), view(args["v_" + name]), "adamw_" + name, tr,
                               into=results.get(name), **kw)

    def adam_ffn(prefix, layer, p13, p2):
        p13 = p13.reshape(N_DEV, 2 * D_MODEL, FF_SHARD_PAD)
        adam(prefix + "_w1", p13, 256, layer=layer)
        adam(prefix + "_w3", p13, 256, layer=layer, row_offset=D_MODEL)
        adam(prefix + "_w2", p2, FF_SHARD, layer=layer)

    adam_ffn("ffn2", 0, p13_ffn2_0, p2_ffn2_0)
    adam_ffn("ffn2", 1, p13_ffn2_1, p2_ffn2_1)
    adam_ffn("ffn1", 1, p13_ffn1_1, p2_ffn1_1)
    adam("conv_w_pw1", p_pw1, 256)
    adam("conv_w_pw2", p_pw2, 128)
    adam("mla_w_a", p_a, 128)
    adam("mla_w_uq", p_uq, 64, view=uq_pad)
    adam("mla_w_ukv", p_ukv, 32, view=ukv_rows)
    adam("mla_w_o", p_o, 128)
    results["mla_w_uq"] = [r.reshape(1, -1, N_HEADS, HEAD_PAD)[..., :QK_NOPE + QK_ROPE] for r in results["mla_w_uq"]]
    results["mla_w_ukv"] = [r.reshape(mla_w_ukv.shape) for r in results["mla_w_ukv"]]

    (p2_ffn1_0,) = _exchange_wait(scatter_d, 0, (results["mla_w_o"][0],), "grad_scatter_wait_last_out")
    adam("ffn1_w2", p2_ffn1_0, FF_SHARD, layer=0)
    (p1_ffn1_0,) = _exchange_wait(scatter_e, 0, (results["ffn1_w2"][0],), "grad_scatter_wait_last_w1")
    adam("ffn1_w1", p1_ffn1_0, 256, layer=0)
    p3_ffn1_0, p_small = _exchange_wait(scatter_c, 0, (results["ffn1_w1"][0],), "grad_scatter_wait_last")
    adam("ffn1_w3", p3_ffn1_0, 256, layer=0)
    small_names = _GAINS + _TINY
    small_shapes = [args[n].shape for n in small_names]
    pack_small = lambda prefix: _small_pack([args[prefix + n] for n in small_names])[None]
    small = _adamw(p_small, pack_small(""), pack_small("m_"), pack_small("v_"), "adamw_small", SMALL_ROWS)
    for kind in range(4):
        for n, leaf in zip(small_names, _small_unpack(small[kind][0], small_shapes)):
            results.setdefault(n, [None] * 4)[kind] = leaf

    order = ("ffn_norm1", "ffn1_w1", "ffn1_w3", "ffn1_w2", "mix_norm", "ffn_norm2", "ffn2_w1", "ffn2_w3", "ffn2_w2",
             "conv_w_pw1", "conv_w_dw", "conv_norm", "conv_w_pw2", "mla_w_a", "mla_q_norm", "mla_kv_norm",
             "mla_w_uq", "mla_w_ukv", "mla_w_o", "final_norm")
    outputs = [loss, grad_x]
    for kind in range(4):
        outputs.extend(results[n][kind] for n in order)
    return tuple(outputs)
```

```python
import jax
import jax.numpy as jnp
from jax import lax
from jax.experimental import pallas as pl
from jax.experimental.pallas import tpu as pltpu

F32 = jnp.float32
MXU_DTYPE = jnp.bfloat16
WIRE_DTYPE = jnp.bfloat16

N_DEV = 8
D_MODEL = 1024
FF_SHARD = 352
FF_SHARD_PAD = 384
D_FF_PAD = N_DEV * FF_SHARD_PAD
N_HEADS = 8
QK_NOPE = 128
QK_ROPE = 64
V_HEAD = 128
Q_LORA = 512
KV_LORA = 256
HEAD_PAD = 256
A_WIDTH = Q_LORA + KV_LORA + QK_ROPE
A_PAD = Q_LORA + KV_LORA + 128
CONV_WIDTH = 31
CONV_HALO = 32
CONV_ROWS = 16
CHUNK_SHIFT = 6
ROPE_THETA = 10000.0
RMS_EPS = 1e-6
ATTN_SCALE = (QK_NOPE + QK_ROPE) ** -0.5
FFN_RES_WEIGHT = 0.5
ADAM_LR = 0.001
ADAM_B1 = 0.9
ADAM_B2 = 0.999
ADAM_EPS = 1e-08
ADAM_WD = 0.01
ADAM_STEP = 10

SMALL_ROWS = 16
VMEM_LIMIT = 56 << 20


def _cparams(*sem):
    return pltpu.CompilerParams(dimension_semantics=sem, vmem_limit_bytes=VMEM_LIMIT)


def _dot(a, b):
    return lax.dot_general(a, b, (((1,), (0,)), ((), ())), preferred_element_type=F32)


def _dot_nt(a, b):
    return lax.dot_general(a, b, (((1,), (1,)), ((), ())), preferred_element_type=F32)


def _dot_tn(a, b):
    return lax.dot_general(a, b, (((0,), (0,)), ((), ())), preferred_element_type=F32)


def _resident(shape, index=None):
    fixed = index if index is not None else (0,) * len(shape)
    return pl.BlockSpec(shape, lambda *_: fixed, pipeline_mode=pl.Buffered(1))


def _rows(tm, n):
    return pl.BlockSpec((tm, n), lambda t, *_: (t, 0))


def _rms(x, g):
    r = lax.rsqrt(jnp.mean(x * x, axis=-1, keepdims=True) + RMS_EPS)
    return x * r * g


def _rms_bwd(x, g, dy):
    r = lax.rsqrt(jnp.mean(x * x, axis=-1, keepdims=True) + RMS_EPS)
    xr = x * r
    dg = jnp.sum(dy * xr, axis=0, keepdims=True)
    u = dy * g
    dx = r * (u - xr * jnp.mean(u * xr, axis=-1, keepdims=True))
    return dx, dg


def _sigmoid(x):
    return 1.0 / (1.0 + jnp.exp(-x))


def _me():
    return 4 * lax.axis_index("x") + 2 * lax.axis_index("y") + lax.axis_index("c")


def _peer(k):
    px, py, pc = lax.axis_index("x") ^ ((k >> 2) & 1), lax.axis_index("y") ^ ((k >> 1) & 1), lax.axis_index("c") ^ (k & 1)
    return (px, py, pc), 4 * px + 2 * py + pc


def _stream_copies(stream, src_ref, dst_ref, send_sem, recv_sem, base):
    send, land = stream[2], stream[3]
    me = _me()
    outgoing, incoming = [], []
    for k in range(1, N_DEV):
        pos, p = _peer(k)
        common = dict(send_sem=send_sem.at[base + k - 1], recv_sem=recv_sem.at[base + k - 1], device_id=pos, device_id_type=pl.DeviceIdType.MESH)
        outgoing.append(pltpu.make_async_remote_copy(src_ref=send(src_ref, p), dst_ref=land(dst_ref, me), **common))
        incoming.append(pltpu.make_async_remote_copy(src_ref=send(src_ref, p), dst_ref=land(dst_ref, p), **common))
    return outgoing, incoming


_HBM = pl.BlockSpec(memory_space=pltpu.HBM)
_SEM = pl.BlockSpec(memory_space=pltpu.SEMAPHORE)


_OTHER_CHIPS = (2, 4, 6)


def _gather_two_level(streams, outs, name):
    n = len(streams)
    slots = N_DEV - 1

    def body(*refs):
        srcs, dsts = refs[:n], refs[n:2 * n]
        send_sem, recv_sem, local_sem = refs[2 * n:]
        me = _me()
        sibling_pos, sibling = _peer(1)

        def block(e, q):
            return streams[e][3](dsts[e], q)

        def copy(e, slot, src_ref, q, to):
            return pltpu.make_async_remote_copy(src_ref=src_ref, dst_ref=block(e, q), send_sem=send_sem.at[e * slots + slot],
                                                recv_sem=recv_sem.at[e * slots + slot], device_id=to, device_id_type=pl.DeviceIdType.MESH)

        own = [pltpu.make_async_copy(srcs[e], block(e, me), local_sem.at[e]) for e in range(n)]
        for cp in own:
            cp.start()
        first = []
        for e in range(n):
            first.append(copy(e, 0, srcs[e], me, sibling_pos))
            for j, k in enumerate(_OTHER_CHIPS):
                first.append(copy(e, 1 + j, srcs[e], me, _peer(k)[0]))
        for cp in first:
            cp.start()
        passed = []
        for e in range(n):
            for j, k in enumerate(_OTHER_CHIPS):
                pos, p = _peer(k)
                copy(e, 1 + j, srcs[e], p, pos).wait_recv()
                passed.append(copy(e, 4 + j, block(e, p), p, sibling_pos))
                passed[-1].start()
        for e in range(n):
            copy(e, 0, srcs[e], sibling, sibling_pos).wait_recv()
            for j, k in enumerate(_OTHER_CHIPS):
                p = _peer(k ^ 1)[1]
                copy(e, 4 + j, block(e, p), p, sibling_pos).wait_recv()
        for cp in first + passed:
            cp.wait_send()
        for cp in own:
            cp.wait()

    return pl.pallas_call(
        body,
        name=name,
        out_shape=tuple(outs),
        in_specs=[_HBM] * n,
        out_specs=tuple([_HBM] * n),
        scratch_shapes=[pltpu.SemaphoreType.DMA((n * slots,)), pltpu.SemaphoreType.DMA((n * slots,)), pltpu.SemaphoreType.DMA((n,))],
        compiler_params=pltpu.CompilerParams(vmem_limit_bytes=VMEM_LIMIT),
    )(*[st[0] for st in streams])


def _landing(streams, outs):
    me = _me()
    lands = [lax.empty(o.shape, o.dtype) for o in outs]
    for st in streams:
        lands[st[1]] = st[4](lands[st[1]], st[0], me)
    return lands


def _exchange_start(streams, lands, groups, after, name):
    n, n_land, n_grp = len(streams), len(lands), len(groups)
    side_effects = pltpu.SideEffectType.DATAFLOW_SIDE_EFFECTING

    def body(*refs):
        srcs, dsts = refs[:n], refs[n:n + n_land]
        outs = refs[n + n_land + len(after):]
        send_sems, recv_sems, token = outs[:n_grp], outs[n_grp:2 * n_grp], outs[-1]
        for gi, group in enumerate(groups):
            for j, e in enumerate(group):
                outgoing, _ = _stream_copies(streams[e], srcs[e], dsts[streams[e][1]], send_sems[gi], recv_sems[gi], j * (N_DEV - 1))
                for cp in outgoing:
                    cp.start()
        token[...] = jnp.zeros_like(token)

    sems = tuple(pltpu.SemaphoreType.DMA((len(g) * (N_DEV - 1),)) for g in groups)
    passed = [st[0] for st in streams] + list(lands)
    res = pl.pallas_call(
        body,
        name=name,
        out_shape=sems + sems + tuple(pltpu.HBM(a.shape, a.dtype) for a in passed) + (jax.ShapeDtypeStruct((8, 128), F32),),
        in_specs=[_HBM] * (n + n_land) + [pl.BlockSpec(memory_space=pl.ANY)] * len(after),
        out_specs=tuple([_SEM] * (2 * n_grp) + [_HBM] * (n + n_land) + [pl.BlockSpec(memory_space=pltpu.VMEM)]),
        input_output_aliases={i: 2 * n_grp + i for i in range(n + n_land)},
        compiler_params=pltpu.CompilerParams(has_side_effects=side_effects),
    )(*passed, *after)
    handle = dict(streams=streams, groups=groups, send=res[:n_grp], recv=res[n_grp:2 * n_grp],
                  srcs=res[2 * n_grp:2 * n_grp + n], lands=res[2 * n_grp + n:2 * n_grp + n + n_land])
    return handle, res[-1]


def _exchange_wait(handle, gi, after, name):
    streams, group = handle["streams"], handle["groups"][gi]
    land_ids = sorted({streams[e][1] for e in group})
    srcs = [handle["srcs"][e] for e in group]
    lands = [handle["lands"][i] for i in land_ids]
    n, n_land = len(srcs), len(lands)
    side_effects = pltpu.SideEffectType.DATAFLOW_SIDE_EFFECTING

    def body(*refs):
        src_refs, land_refs = refs[:n], refs[n:n + n_land]
        send_sem, recv_sem = refs[n + n_land:n + n_land + 2]
        for j, e in enumerate(group):
            outgoing, incoming = _stream_copies(streams[e], src_refs[j], land_refs[land_ids.index(streams[e][1])], send_sem, recv_sem, j * (N_DEV - 1))
            for cp in outgoing:
                cp.wait_send()
            for cp in incoming:
                cp.wait_recv()

    res = pl.pallas_call(
        body,
        name=name,
        out_shape=tuple(pltpu.HBM(a.shape, a.dtype) for a in lands),
        in_specs=[_HBM] * (n + n_land) + [_SEM, _SEM] + [pl.BlockSpec(memory_space=pl.ANY)] * len(after),
        out_specs=tuple([_HBM] * n_land),
        input_output_aliases={n + i: i for i in range(n_land)},
        compiler_params=pltpu.CompilerParams(has_side_effects=side_effects),
    )(*srcs, *lands, handle["send"][gi], handle["recv"][gi], *after)
    return list(res)


def _whole(ref, _):
    return ref


def _slot(ref, q):
    return ref.at[q]


def _lane_block(width):
    def pick(ref, q):
        idx = (slice(None),) * (len(ref.shape) - 1) + (pl.ds(pl.multiple_of(q * width, 128), width),)
        return ref.at[idx]
    return pick


def _row_block(height):
    def pick(ref, q):
        idx = (slice(None),) * (len(ref.shape) - 2) + (pl.ds(pl.multiple_of(q * height, 16), height), slice(None))
        return ref.at[idx]
    return pick


def _block_of(axis_from_end, size):
    return _lane_block(size) if axis_from_end == 1 else _row_block(size)


def _gather_streams(shards):
    streams, outs = [], []
    for i, (shard, axis) in enumerate(shards):
        shape = list(shard.shape)
        if axis is None:
            shape, land = [N_DEV] + shape, _slot
            place = _place_slot
        else:
            size = shape[-axis]
            land = _block_of(axis, size)
            shape[-axis] *= N_DEV
            place = lambda dst, src, me, size=size, axis=axis: lax.dynamic_update_slice_in_dim(dst, src, me * size, dst.ndim - axis)
        streams.append((shard, i, _whole, land, place))
        outs.append(jax.ShapeDtypeStruct(tuple(shape), shard.dtype))
    return streams, outs


def _place_slot(dst, src, me):
    return lax.dynamic_update_index_in_dim(dst, src, me, 0)


def _scatter_streams(fulls):
    streams, outs = [], []
    for i, (full, axis) in enumerate(fulls):
        shape = list(full.shape)
        size = shape[-axis] // N_DEV
        shape[-axis] = size
        place = lambda dst, src, me, size=size, axis=axis: _place_slot(
            dst, lax.dynamic_slice_in_dim(src, me * size, size, src.ndim - axis), me)
        streams.append((full, i, _block_of(axis, size), _slot, place))
        outs.append(jax.ShapeDtypeStruct((N_DEV,) + tuple(shape), full.dtype))
    return streams, outs


FFN_TM = 512
FFN_TF = 1024
FFN_FWD_TM = 1024
FFN_FWD_TF = 1024


def _ffn_fwd(h, g, w13, w2, after=()):
    t = h.shape[0]
    tm, tf = min(FFN_FWD_TM, t), FFN_FWD_TF
    nf = D_FF_PAD // tf

    def body(h_ref, g_ref, w1_ref, w3_ref, w2_ref, *rest):
        ho_ref, n_ref, a_ref, b_ref, n_scr, acc = rest[len(after):]
        f = pl.program_id(1)

        @pl.when(f == 0)
        def _():
            n_scr[...] = _rms(h_ref[...], g_ref[...]).astype(MXU_DTYPE)
            n_ref[...] = n_scr[...]
            acc[...] = jnp.zeros_like(acc)

        n = n_scr[...]
        a = _dot(n, w1_ref[...])
        b = _dot(n, w3_ref[...])
        a_ref[...] = a.astype(MXU_DTYPE)
        b_ref[...] = b.astype(MXU_DTYPE)
        s = (a * _sigmoid(a)) * b
        acc[...] += _dot(s.astype(MXU_DTYPE), w2_ref[...])

        @pl.when(f == nf - 1)
        def _():
            ho_ref[...] = h_ref[...] + FFN_RES_WEIGHT * acc[...]

    return pl.pallas_call(
        body,
        name="ffn_fwd",
        grid=(t // tm, nf),
        in_specs=[
            _rows(tm, D_MODEL),
            _resident((1, D_MODEL)),
            pl.BlockSpec((None, D_MODEL, tf), lambda i, f: (0, 0, f)),
            pl.BlockSpec((None, D_MODEL, tf), lambda i, f: (1, 0, f)),
            pl.BlockSpec((tf, D_MODEL), lambda i, f: (f, 0)),
        ] + [pl.BlockSpec(memory_space=pl.ANY)] * len(after),
        out_specs=[
            _rows(tm, D_MODEL),
            _rows(tm, D_MODEL),
            pl.BlockSpec((tm, tf), lambda i, f: (i, f)),
            pl.BlockSpec((tm, tf), lambda i, f: (i, f)),
        ],
        out_shape=[
            jax.ShapeDtypeStruct((t, D_MODEL), F32),
            jax.ShapeDtypeStruct((t, D_MODEL), MXU_DTYPE),
            jax.ShapeDtypeStruct((t, D_FF_PAD), MXU_DTYPE),
            jax.ShapeDtypeStruct((t, D_FF_PAD), MXU_DTYPE),
        ],
        scratch_shapes=[pltpu.VMEM((tm, D_MODEL), MXU_DTYPE), pltpu.VMEM((tm, D_MODEL), F32)],
        compiler_params=_cparams("parallel", "arbitrary"),
    )(h, g, w13, w13, w2, *after)


def _ffn_bwd(dout, h, g, a, b, w13, w2, after=()):
    t = h.shape[0]
    tm, tf = min(FFN_TM, t), FFN_TF
    nf = D_FF_PAD // tf

    def body(do_ref, h_ref, g_ref, a_ref, b_ref, w1_ref, w3_ref, w2_ref, *rest):
        dh_ref, dob_ref, s_ref, dab_ref, dg_ref, dob_scr, acc = rest[len(after):]
        i, f = pl.program_id(0), pl.program_id(1)

        @pl.when(f == 0)
        def _():
            dob_scr[...] = (FFN_RES_WEIGHT * do_ref[...]).astype(MXU_DTYPE)
            dob_ref[...] = dob_scr[...]
            acc[...] = jnp.zeros_like(acc)

        @pl.when((f == 0) & (i == 0))
        def _():
            dg_ref[...] = jnp.zeros_like(dg_ref)

        ds = _dot_nt(dob_scr[...], w2_ref[...])
        av = a_ref[...].astype(F32)
        bv = b_ref[...].astype(F32)
        sig = _sigmoid(av)
        sil = av * sig
        s_ref[...] = (sil * bv).astype(MXU_DTYPE)
        da = (ds * bv * (sig * (1.0 + av * (1.0 - sig)))).astype(MXU_DTYPE)
        db = (ds * sil).astype(MXU_DTYPE)
        dab_ref[0] = da
        dab_ref[1] = db
        acc[...] += _dot_nt(da, w1_ref[...]) + _dot_nt(db, w3_ref[...])

        @pl.when(f == nf - 1)
        def _():
            dx, dg = _rms_bwd(h_ref[...], g_ref[...], acc[...])
            dh_ref[...] = do_ref[...] + dx
            dg_ref[...] += dg

    return pl.pallas_call(
        body,
        name="ffn_bwd",
        grid=(t // tm, nf),
        in_specs=[
            _rows(tm, D_MODEL),
            _rows(tm, D_MODEL),
            _resident((1, D_MODEL)),
            pl.BlockSpec((tm, tf), lambda i, f: (i, f)),
            pl.BlockSpec((tm, tf), lambda i, f: (i, f)),
            pl.BlockSpec((None, D_MODEL, tf), lambda i, f: (0, 0, f)),
            pl.BlockSpec((None, D_MODEL, tf), lambda i, f: (1, 0, f)),
            pl.BlockSpec((tf, D_MODEL), lambda i, f: (f, 0)),
        ] + [pl.BlockSpec(memory_space=pl.ANY)] * len(after),
        out_specs=[
            _rows(tm, D_MODEL),
            _rows(tm, D_MODEL),
            pl.BlockSpec((tm, tf), lambda i, f: (i, f)),
            pl.BlockSpec((2, tm, tf), lambda i, f: (0, i, f)),
            pl.BlockSpec((1, D_MODEL), lambda i, f: (0, 0)),
        ],
        out_shape=[
            jax.ShapeDtypeStruct((t, D_MODEL), F32),
            jax.ShapeDtypeStruct((t, D_MODEL), MXU_DTYPE),
            jax.ShapeDtypeStruct((t, D_FF_PAD), MXU_DTYPE),
            jax.ShapeDtypeStruct((2, t, D_FF_PAD), MXU_DTYPE),
            jax.ShapeDtypeStruct((1, D_MODEL), F32),
        ],
        scratch_shapes=[pltpu.VMEM((tm, D_MODEL), MXU_DTYPE), pltpu.VMEM((tm, D_MODEL), F32)],
        compiler_params=_cparams("arbitrary", "arbitrary"),
    )(dout, h, g, a, b, w13, w13, w2, *after)


def _wgrad(x, y, name, out_dtype=WIRE_DTYPE, after=(), only=None):
    t, m = x.shape
    grouped = y.ndim == 3 and only is None
    groups = y.shape[0] if grouped else 1
    n = y.shape[-1]
    tk = min(2048, t)
    bm = m if m <= 1536 else m // 2
    bn = n if n <= 1536 else n // 2
    nk = t // tk

    def body(x_ref, y_ref, *rest):
        o_ref, acc = rest[len(after):]
        k = pl.program_id(3)

        @pl.when(k == 0)
        def _():
            acc[...] = jnp.zeros_like(acc)

        acc[...] += _dot_tn(x_ref[...].astype(MXU_DTYPE), y_ref[...].astype(MXU_DTYPE))

        @pl.when(k == nk - 1)
        def _():
            o_ref[...] = acc[...].astype(out_dtype)

    if grouped:
        y_spec = pl.BlockSpec((None, tk, bn), lambda g, i, j, k: (g, k, j))
        o_spec = pl.BlockSpec((None, bm, bn), lambda g, i, j, k: (g, i, j))
        o_shape = jax.ShapeDtypeStruct((groups, m, n), out_dtype)
    else:
        if only is None:
            y_spec = pl.BlockSpec((tk, bn), lambda g, i, j, k: (k, j))
        else:
            y_spec = pl.BlockSpec((None, tk, bn), lambda g, i, j, k: (only, k, j))
        o_spec = pl.BlockSpec((bm, bn), lambda g, i, j, k: (i, j))
        o_shape = jax.ShapeDtypeStruct((m, n), out_dtype)
    return pl.pallas_call(
        body,
        name=name,
        grid=(groups, m // bm, n // bn, nk),
        in_specs=[pl.BlockSpec((tk, bm), lambda g, i, j, k: (k, i)), y_spec] + [pl.BlockSpec(memory_space=pl.ANY)] * len(after),
        out_specs=o_spec,
        out_shape=o_shape,
        scratch_shapes=[pltpu.VMEM((bm, bn), F32)],
        compiler_params=_cparams("parallel", "parallel", "parallel", "arbitrary"),
    )(x, y, *after)


def _conv_pre(h, g, w_pw1):
    t = h.shape[0]
    tm = min(512, t)

    def body(h_ref, g_ref, w_ref, n_ref, u_ref, glu_ref):
        n = _rms(h_ref[...], g_ref[...]).astype(MXU_DTYPE)
        n_ref[...] = n
        u = _dot(n, w_ref[...])
        u_ref[...] = u
        glu_ref[...] = u[:, :D_MODEL] * _sigmoid(u[:, D_MODEL:])

    return pl.pallas_call(
        body,
        name="conv_pre",
        grid=(t // tm,),
        in_specs=[_rows(tm, D_MODEL), _resident((1, D_MODEL)), _resident((D_MODEL, 2 * D_MODEL))],
        out_specs=[_rows(tm, D_MODEL), _rows(tm, 2 * D_MODEL), _rows(tm, D_MODEL)],
        out_shape=[
            jax.ShapeDtypeStruct((t, D_MODEL), MXU_DTYPE),
            jax.ShapeDtypeStruct((t, 2 * D_MODEL), F32),
            jax.ShapeDtypeStruct((t, D_MODEL), F32),
        ],
        compiler_params=_cparams("parallel"),
    )(h, g, w_pw1)


def _shifted_copies(dst, src, rows):
    for r in range(1, 8):
        dst[r - 1] = src[pl.ds(r, rows), :]


def _window(src, shifted, offset, base, rows):
    r = offset % 8
    start = base + offset - r
    return src[pl.ds(start, rows), :] if r == 0 else shifted[r - 1, pl.ds(start, rows), :]


def _conv_main(glu, w_dw, g, w_pw2, h):
    t = h.shape[0]
    tm = min(512, t)
    per = tm // CONV_HALO
    shifted_rows = tm + CONV_HALO - 8

    def body(glu_ref, halo_ref, w_ref, g_ref, w2_ref, h_ref, ho_ref, c_ref, sw_ref, ext, ext_sh):
        i = pl.program_id(0)
        ext[pl.ds(0, CONV_HALO), :] = jnp.where(i == 0, 0.0, halo_ref[...])
        ext[pl.ds(CONV_HALO, tm), :] = glu_ref[...]
        _shifted_copies(ext_sh, ext, shifted_rows)
        for base in range(0, tm, CONV_ROWS):
            acc = jnp.zeros((CONV_ROWS, D_MODEL), F32)
            for k in range(CONV_WIDTH):
                acc = acc + _window(ext, ext_sh, CONV_HALO - (CONV_WIDTH - 1) + k, base, CONV_ROWS) * w_ref[pl.ds(k, 1), :]
            c_ref[pl.ds(base, CONV_ROWS), :] = acc
        y = _rms(c_ref[...], g_ref[...])
        sw = (y * _sigmoid(y)).astype(MXU_DTYPE)
        sw_ref[...] = sw
        ho_ref[...] = h_ref[...] + _dot(sw, w2_ref[...])

    return pl.pallas_call(
        body,
        name="conv_main",
        grid=(t // tm,),
        in_specs=[
            _rows(tm, D_MODEL),
            pl.BlockSpec((CONV_HALO, D_MODEL), lambda i: (jnp.maximum(i * per - 1, 0), 0)),
            _resident((CONV_WIDTH, D_MODEL)),
            _resident((1, D_MODEL)),
            _resident((D_MODEL, D_MODEL)),
            _rows(tm, D_MODEL),
        ],
        out_specs=[_rows(tm, D_MODEL), _rows(tm, D_MODEL), _rows(tm, D_MODEL)],
        out_shape=[
            jax.ShapeDtypeStruct((t, D_MODEL), F32),
            jax.ShapeDtypeStruct((t, D_MODEL), F32),
            jax.ShapeDtypeStruct((t, D_MODEL), MXU_DTYPE),
        ],
        scratch_shapes=[pltpu.VMEM((tm + CONV_HALO, D_MODEL), F32), pltpu.VMEM((7, shifted_rows, D_MODEL), F32)],
        compiler_params=_cparams("parallel"),
    )(glu, glu, w_dw, g, w_pw2, h)


def _conv_bwd_post(dout, c, g, w_pw2):
    t = dout.shape[0]
    tm = min(1024, t)

    def body(do_ref, c_ref, g_ref, w2_ref, dc_ref, dg_ref):
        @pl.when(pl.program_id(0) == 0)
        def _():
            dg_ref[...] = jnp.zeros_like(dg_ref)

        dsw = _dot_nt(do_ref[...].astype(MXU_DTYPE), w2_ref[...])
        cv = c_ref[...]
        y = _rms(cv, g_ref[...])
        sig = _sigmoid(y)
        dy = dsw * (sig * (1.0 + y * (1.0 - sig)))
        dc, dg = _rms_bwd(cv, g_ref[...], dy)
        dc_ref[...] = dc
        dg_ref[...] += dg

    return pl.pallas_call(
        body,
        name="conv_bwd_post",
        grid=(t // tm,),
        in_specs=[_rows(tm, D_MODEL), _rows(tm, D_MODEL), _resident((1, D_MODEL)), _resident((D_MODEL, D_MODEL))],
        out_specs=[_rows(tm, D_MODEL), pl.BlockSpec((1, D_MODEL), lambda i: (0, 0))],
        out_shape=[jax.ShapeDtypeStruct((t, D_MODEL), F32), jax.ShapeDtypeStruct((1, D_MODEL), F32)],
        compiler_params=_cparams("arbitrary"),
    )(dout, c, g, w_pw2)


def _conv_bwd_pre(dc, glu, u, w_dw, w_pw1, h, g, dout):
    t = h.shape[0]
    tm = min(256, t)
    per = tm // CONV_HALO
    nt = t // tm
    last_halo = t // CONV_HALO - 1
    shifted_rows = tm + CONV_HALO - 8

    def body(dc_ref, dcn_ref, glu_ref, gluh_ref, u_ref, w_ref, w1_ref, h_ref, g_ref, do_ref,
             dh_ref, du_ref, dw_ref, dg_ref, dext, gext, dext_sh, gext_sh):
        i = pl.program_id(0)

        @pl.when(i == 0)
        def _():
            dw_ref[...] = jnp.zeros_like(dw_ref)
            dg_ref[...] = jnp.zeros_like(dg_ref)

        dext[pl.ds(0, tm), :] = dc_ref[...]
        dext[pl.ds(tm, CONV_HALO), :] = jnp.where(i == nt - 1, 0.0, dcn_ref[...])
        gext[pl.ds(0, CONV_HALO), :] = jnp.where(i == 0, 0.0, gluh_ref[...])
        gext[pl.ds(CONV_HALO, tm), :] = glu_ref[...]
        _shifted_copies(dext_sh, dext, shifted_rows)
        _shifted_copies(gext_sh, gext, shifted_rows)
        for base in range(0, tm, CONV_ROWS):
            rows = pl.ds(base, CONV_ROWS)
            dglu = jnp.zeros((CONV_ROWS, D_MODEL), F32)
            for k in range(CONV_WIDTH):
                dglu = dglu + _window(dext, dext_sh, CONV_WIDTH - 1 - k, base, CONV_ROWS) * w_ref[pl.ds(k, 1), :]
            av, bv = u_ref[rows, :D_MODEL], u_ref[rows, D_MODEL:]
            sig = _sigmoid(bv)
            du_ref[rows, :D_MODEL] = (dglu * sig).astype(MXU_DTYPE)
            du_ref[rows, D_MODEL:] = (dglu * av * (sig * (1.0 - sig))).astype(MXU_DTYPE)
        for k in range(CONV_WIDTH):
            part = jnp.zeros((CONV_ROWS, D_MODEL), F32)
            for base in range(0, tm, CONV_ROWS):
                part = part + dc_ref[pl.ds(base, CONV_ROWS), :] * _window(gext, gext_sh, CONV_HALO - (CONV_WIDTH - 1) + k, base, CONV_ROWS)
            dw_ref[pl.ds(k, 1), :] += jnp.sum(part, axis=0, keepdims=True)
        dn = _dot_nt(du_ref[...], w1_ref[...])
        dx, dg = _rms_bwd(h_ref[...], g_ref[...], dn)
        dh_ref[...] = do_ref[...] + dx
        dg_ref[...] += dg

    return pl.pallas_call(
        body,
        name="conv_bwd_pre",
        grid=(nt,),
        in_specs=[
            _rows(tm, D_MODEL),
            pl.BlockSpec((CONV_HALO, D_MODEL), lambda i: (jnp.minimum((i + 1) * per, last_halo), 0)),
            _rows(tm, D_MODEL),
            pl.BlockSpec((CONV_HALO, D_MODEL), lambda i: (jnp.maximum(i * per - 1, 0), 0)),
            _rows(tm, 2 * D_MODEL),
            _resident((CONV_WIDTH, D_MODEL)),
            _resident((D_MODEL, 2 * D_MODEL)),
            _rows(tm, D_MODEL),
            _resident((1, D_MODEL)),
            _rows(tm, D_MODEL),
        ],
        out_specs=[
            _rows(tm, D_MODEL),
            _rows(tm, 2 * D_MODEL),
            pl.BlockSpec((CONV_HALO, D_MODEL), lambda i: (0, 0)),
            pl.BlockSpec((1, D_MODEL), lambda i: (0, 0)),
        ],
        out_shape=[
            jax.ShapeDtypeStruct((t, D_MODEL), F32),
            jax.ShapeDtypeStruct((t, 2 * D_MODEL), MXU_DTYPE),
            jax.ShapeDtypeStruct((CONV_HALO, D_MODEL), F32),
            jax.ShapeDtypeStruct((1, D_MODEL), F32),
        ],
        scratch_shapes=[pltpu.VMEM((tm + CONV_HALO, D_MODEL), F32), pltpu.VMEM((tm + CONV_HALO, D_MODEL), F32),
                        pltpu.VMEM((7, shifted_rows, D_MODEL), F32), pltpu.VMEM((7, shifted_rows, D_MODEL), F32)],
        compiler_params=_cparams("arbitrary"),
    )(dc, dc, glu, glu, u, w_dw, w_pw1, h, g, dout)


def _rope(s, cs, sa, sb):
    return s * cs - pltpu.roll(s, 96, 1) * sa + pltpu.roll(s, 32, 1) * sb


def _unrope(d, cs, sa, sb):
    return d * cs + pltpu.roll(d, 96, 1) * sa - pltpu.roll(d, 32, 1) * sb


def _mla_pre(h, g, w_a, g_q, g_kv, w_uq, w_ukv, cs, sa, sb):
    t = h.shape[0]
    tm = min(512, t)
    qw = N_HEADS * HEAD_PAD
    vw = N_HEADS * V_HEAD

    def body(h_ref, g_ref, wa_ref, gq_ref, gkv_ref, wuq_ref, wukv_ref, cs_ref, sa_ref, sb_ref,
             n_ref, a_ref, cq_ref, ckv_ref, q_ref, k_ref, v_ref):
        n = _rms(h_ref[...], g_ref[...]).astype(MXU_DTYPE)
        n_ref[...] = n
        a = _dot(n, wa_ref[...])
        a_ref[...] = a
        cq = _rms(a[:, :Q_LORA], gq_ref[...]).astype(MXU_DTYPE)
        ckv = _rms(a[:, Q_LORA:Q_LORA + KV_LORA], gkv_ref[...]).astype(MXU_DTYPE)
        cq_ref[...] = cq
        ckv_ref[...] = ckv
        q = _dot(cq, wuq_ref[...]) * ATTN_SCALE
        kv = _dot(ckv, wukv_ref[...])
        cs_, sa_, sb_ = cs_ref[...], sa_ref[...], sb_ref[...]
        k_rot = _rope(a[:, Q_LORA + KV_LORA:], cs_, sa_, sb_).astype(MXU_DTYPE)
        for hd in range(N_HEADS):
            lo = hd * HEAD_PAD
            q_ref[:, lo:lo + QK_NOPE] = q[:, lo:lo + QK_NOPE].astype(MXU_DTYPE)
            q_ref[:, lo + QK_NOPE:lo + HEAD_PAD] = _rope(q[:, lo + QK_NOPE:lo + HEAD_PAD], cs_, sa_, sb_).astype(MXU_DTYPE)
            k_ref[:, lo:lo + QK_NOPE] = kv[:, lo:lo + QK_NOPE].astype(MXU_DTYPE)
            k_ref[:, lo + QK_NOPE:lo + HEAD_PAD] = k_rot
            v_ref[:, hd * V_HEAD:(hd + 1) * V_HEAD] = kv[:, lo + QK_NOPE:lo + HEAD_PAD].astype(MXU_DTYPE)

    return pl.pallas_call(
        body,
        name="mla_pre",
        grid=(t // tm,),
        in_specs=[
            _rows(tm, D_MODEL),
            _resident((1, D_MODEL)),
            _resident((D_MODEL, A_PAD)),
            _resident((1, Q_LORA)),
            _resident((1, KV_LORA)),
            _resident((Q_LORA, qw)),
            _resident((KV_LORA, qw)),
            _rows(tm, 128),
            _rows(tm, 128),
            _rows(tm, 128),
        ],
        out_specs=[_rows(tm, D_MODEL), _rows(tm, A_PAD), _rows(tm, Q_LORA), _rows(tm, KV_LORA), _rows(tm, qw), _rows(tm, qw), _rows(tm, vw)],
        out_shape=[
            jax.ShapeDtypeStruct((t, D_MODEL), MXU_DTYPE),
            jax.ShapeDtypeStruct((t, A_PAD), F32),
            jax.ShapeDtypeStruct((t, Q_LORA), MXU_DTYPE),
            jax.ShapeDtypeStruct((t, KV_LORA), MXU_DTYPE),
            jax.ShapeDtypeStruct((t, qw), MXU_DTYPE),
            jax.ShapeDtypeStruct((t, qw), MXU_DTYPE),
            jax.ShapeDtypeStruct((t, vw), MXU_DTYPE),
        ],
        compiler_params=_cparams("parallel"),
    )(h, g, w_a, g_q, g_kv, w_uq, w_ukv, cs, sa, sb)


def _attn_tile(t):
    return min(1024, t)


ATTN_KEY_SUB = 512


def _chunk_mask(rows, cols, col0=0):
    r = lax.broadcasted_iota(jnp.int32, (rows, cols), 0)
    c = col0 + lax.broadcasted_iota(jnp.int32, (rows, cols), 1)
    return (r >> CHUNK_SHIFT) >= (c >> CHUNK_SHIFT)


def _causal_pairs(n, by_column=False):
    if by_column:
        pairs = [(i, j) for j in range(n) for i in range(j, n)]
    else:
        pairs = [(i, j) for i in range(n) for j in range(i + 1)]
    return jnp.asarray([p[0] for p in pairs], jnp.int32), jnp.asarray([p[1] for p in pairs], jnp.int32)


def _flash_fwd(q, k, v):
    t = q.shape[0]
    tq = tk = _attn_tile(t)
    rows_of, cols_of = _causal_pairs(t // tq)

    def body(qi_ref, ki_ref, q_ref, k_ref, v_ref, o_ref, lse_ref, m_s, l_s, acc):
        step = pl.program_id(1)
        qi, ki = qi_ref[step], ki_ref[step]

        @pl.when(ki == 0)
        def _():
            m_s[...] = jnp.full_like(m_s, -jnp.inf)
            l_s[...] = jnp.zeros_like(l_s)
            acc[...] = jnp.zeros_like(acc)

        def block(on_diagonal):
            s = _dot_nt(q_ref[...], k_ref[...])
            if on_diagonal:
                s = jnp.where(_chunk_mask(tq, tk), s, -jnp.inf)
            m_prev = m_s[...]
            m_new = jnp.maximum(m_prev, jnp.max(s, axis=1, keepdims=True))
            alpha = jnp.exp(m_prev - m_new)
            p = jnp.exp(s - m_new)
            l_s[...] = alpha * l_s[...] + jnp.sum(p, axis=1, keepdims=True)
            acc[...] = alpha * acc[...] + _dot(p.astype(MXU_DTYPE), v_ref[...])
            m_s[...] = m_new

        pl.when(ki < qi)(lambda: block(False))

        @pl.when(ki == qi)
        def _():
            block(True)
            o_ref[...] = acc[...] / l_s[...]
            lse_ref[...] = jnp.broadcast_to(m_s[...] + jnp.log(l_s[...]), (tq, V_HEAD))

    grid_spec = pltpu.PrefetchScalarGridSpec(
        num_scalar_prefetch=2,
        grid=(N_HEADS, rows_of.shape[0]),
        in_specs=[
            pl.BlockSpec((tq, HEAD_PAD), lambda h, s, qi, ki: (qi[s], h)),
            pl.BlockSpec((tk, HEAD_PAD), lambda h, s, qi, ki: (ki[s], h)),
            pl.BlockSpec((tk, V_HEAD), lambda h, s, qi, ki: (ki[s], h)),
        ],
        out_specs=[pl.BlockSpec((tq, V_HEAD), lambda h, s, qi, ki: (qi[s], h)), pl.BlockSpec((tq, V_HEAD), lambda h, s, qi, ki: (qi[s], h))],
        scratch_shapes=[pltpu.VMEM((tq, 1), F32), pltpu.VMEM((tq, 1), F32), pltpu.VMEM((tq, V_HEAD), F32)],
    )
    return pl.pallas_call(
        body,
        name="flash_fwd",
        grid_spec=grid_spec,
        out_shape=[jax.ShapeDtypeStruct((t, N_HEADS * V_HEAD), F32), jax.ShapeDtypeStruct((t, N_HEADS * V_HEAD), F32)],
        compiler_params=_cparams("parallel", "arbitrary"),
    )(rows_of, cols_of, q, k, v)


def _flash_bwd(q, k, v, do, lse, delta):
    t = q.shape[0]
    tq = tk = _attn_tile(t)
    nq = t // tq
    rows_of, cols_of = _causal_pairs(nq, by_column=True)

    def body(qi_ref, kj_ref, q_ref, k_ref, v_ref, do_ref, lse_ref, dl_ref, dq_ref, dk_ref, dv_ref, dk_acc, dv_acc):
        step = pl.program_id(1)
        qi, kj = qi_ref[step], kj_ref[step]

        @pl.when(step == 0)
        def _():
            dq_ref[...] = jnp.zeros_like(dq_ref)

        def block(on_diagonal):
            qv, dov = q_ref[...], do_ref[...]
            lse, dl = lse_ref[:, :1], dl_ref[:, :1]
            dq_new = None
            for lo in range(0, tk, min(ATTN_KEY_SUB, tk)):
                keys = pl.ds(lo, min(ATTN_KEY_SUB, tk))
                kv = k_ref[keys, :]
                s = _dot_nt(qv, kv)
                if on_diagonal:
                    s = jnp.where(_chunk_mask(tq, min(ATTN_KEY_SUB, tk), lo), s, -jnp.inf)
                p = jnp.exp(s - lse)
                dp = _dot_nt(dov, v_ref[keys, :])
                ds = (p * (dp - dl)).astype(MXU_DTYPE)
                dv_new = _dot_tn(p.astype(MXU_DTYPE), dov)
                dk_new = _dot_tn(ds, qv)
                if on_diagonal:
                    dv_acc[keys, :] = dv_new
                    dk_acc[keys, :] = dk_new
                else:
                    dv_acc[keys, :] += dv_new
                    dk_acc[keys, :] += dk_new
                part = _dot(ds, kv)
                dq_new = part if dq_new is None else dq_new + part
            rows = pl.ds(pl.multiple_of(qi * tq, tq), tq)
            dq_ref[rows, :] += dq_new

        pl.when(qi > kj)(lambda: block(False))
        pl.when(qi == kj)(lambda: block(True))

        @pl.when(qi == nq - 1)
        def _():
            dk_ref[...] = dk_acc[...]
            dv_ref[...] = dv_acc[...]

    grid_spec = pltpu.PrefetchScalarGridSpec(
        num_scalar_prefetch=2,
        grid=(N_HEADS, rows_of.shape[0]),
        in_specs=[
            pl.BlockSpec((tq, HEAD_PAD), lambda h, s, qi, kj: (qi[s], h)),
            pl.BlockSpec((tk, HEAD_PAD), lambda h, s, qi, kj: (kj[s], h)),
            pl.BlockSpec((tk, V_HEAD), lambda h, s, qi, kj: (kj[s], h)),
            pl.BlockSpec((tq, V_HEAD), lambda h, s, qi, kj: (qi[s], h)),
            pl.BlockSpec((tq, V_HEAD), lambda h, s, qi, kj: (qi[s], h)),
            pl.BlockSpec((tq, V_HEAD), lambda h, s, qi, kj: (qi[s], h)),
        ],
        out_specs=[
            pl.BlockSpec((t, HEAD_PAD), lambda h, s, qi, kj: (0, h)),
            pl.BlockSpec((tk, HEAD_PAD), lambda h, s, qi, kj: (kj[s], h)),
            pl.BlockSpec((tk, V_HEAD), lambda h, s, qi, kj: (kj[s], h)),
        ],
        scratch_shapes=[pltpu.VMEM((tk, HEAD_PAD), F32), pltpu.VMEM((tk, V_HEAD), F32)],
    )
    return pl.pallas_call(
        body,
        name="flash_bwd",
        grid_spec=grid_spec,
        out_shape=[
            jax.ShapeDtypeStruct((t, N_HEADS * HEAD_PAD), F32),
            jax.ShapeDtypeStruct((t, N_HEADS * HEAD_PAD), F32),
            jax.ShapeDtypeStruct((t, N_HEADS * V_HEAD), F32),
        ],
        compiler_params=_cparams("arbitrary", "arbitrary"),
    )(rows_of, cols_of, q, k, v, do, lse, delta)


def _attn_out(o, w_o, h):
    t = h.shape[0]
    tm = min(1024, t)

    def body(o_ref, w_ref, h_ref, ho_ref):
        ho_ref[...] = h_ref[...] + _dot(o_ref[...].astype(MXU_DTYPE), w_ref[...])

    return pl.pallas_call(
        body,
        name="attn_out",
        grid=(t // tm,),
        in_specs=[_rows(tm, D_MODEL), _resident((D_MODEL, D_MODEL)), _rows(tm, D_MODEL)],
        out_specs=_rows(tm, D_MODEL),
        out_shape=jax.ShapeDtypeStruct((t, D_MODEL), F32),
        compiler_params=_cparams("parallel"),
    )(o, w_o, h)


def _attn_out_bwd(dout, o, w_o):
    t = dout.shape[0]
    tm = min(1024, t)

    def body(d_ref, o_ref, w_ref, do_ref, dl_ref):
        do = _dot_nt(d_ref[...].astype(MXU_DTYPE), w_ref[...])
        do_ref[...] = do.astype(MXU_DTYPE)
        prod = do * o_ref[...]
        for hd in range(N_HEADS):
            lanes = slice(hd * V_HEAD, (hd + 1) * V_HEAD)
            dl_ref[:, lanes] = jnp.broadcast_to(jnp.sum(prod[:, lanes], axis=1, keepdims=True), (tm, V_HEAD))

    return pl.pallas_call(
        body,
        name="attn_out_bwd",
        grid=(t // tm,),
        in_specs=[_rows(tm, D_MODEL), _rows(tm, D_MODEL), _resident((D_MODEL, D_MODEL))],
        out_specs=[_rows(tm, D_MODEL), _rows(tm, D_MODEL)],
        out_shape=[jax.ShapeDtypeStruct((t, D_MODEL), MXU_DTYPE), jax.ShapeDtypeStruct((t, D_MODEL), F32)],
        compiler_params=_cparams("parallel"),
    )(dout, o, w_o)


def _mla_bwd_pre(dq, dk, dv, a, h, dout, g, g_q, g_kv, w_a, w_uq, w_ukv, cs, sa, sb):
    t = h.shape[0]
    tm = min(512, t)
    qw = N_HEADS * HEAD_PAD
    vw = N_HEADS * V_HEAD

    def body(dq_ref, dk_ref, dv_ref, a_ref, h_ref, do_ref, g_ref, gq_ref, gkv_ref, wa_ref, wuq_ref, wukv_ref,
             cs_ref, sa_ref, sb_ref, dh_ref, dqp_ref, dkv_ref, da_ref, dg_ref, dgq_ref, dgkv_ref):
        @pl.when(pl.program_id(0) == 0)
        def _():
            dg_ref[...] = jnp.zeros_like(dg_ref)
            dgq_ref[...] = jnp.zeros_like(dgq_ref)
            dgkv_ref[...] = jnp.zeros_like(dgkv_ref)

        cs_, sa_, sb_ = cs_ref[...], sa_ref[...], sb_ref[...]
        slab = jnp.zeros((tm, 128), F32)
        for hd in range(N_HEADS):
            lo = hd * HEAD_PAD
            dqp_ref[:, lo:lo + QK_NOPE] = (dq_ref[:, lo:lo + QK_NOPE] * ATTN_SCALE).astype(MXU_DTYPE)
            dqp_ref[:, lo + QK_NOPE:lo + HEAD_PAD] = _unrope(dq_ref[:, lo + QK_NOPE:lo + HEAD_PAD] * ATTN_SCALE, cs_, sa_, sb_).astype(MXU_DTYPE)
            dkv_ref[:, lo:lo + QK_NOPE] = dk_ref[:, lo:lo + QK_NOPE].astype(MXU_DTYPE)
            dkv_ref[:, lo + QK_NOPE:lo + HEAD_PAD] = dv_ref[:, hd * V_HEAD:(hd + 1) * V_HEAD].astype(MXU_DTYPE)
            slab = slab + dk_ref[:, lo + QK_NOPE:lo + HEAD_PAD]
        dcq = _dot_nt(dqp_ref[...], wuq_ref[...])
        dckv = _dot_nt(dkv_ref[...], wukv_ref[...])
        av = a_ref[...]
        daq, dgq = _rms_bwd(av[:, :Q_LORA], gq_ref[...], dcq)
        dakv, dgkv = _rms_bwd(av[:, Q_LORA:Q_LORA + KV_LORA], gkv_ref[...], dckv)
        da_ref[:, :Q_LORA] = daq.astype(MXU_DTYPE)
        da_ref[:, Q_LORA:Q_LORA + KV_LORA] = dakv.astype(MXU_DTYPE)
        da_ref[:, Q_LORA + KV_LORA:] = _unrope(slab, cs_, sa_, sb_).astype(MXU_DTYPE)
        dn = _dot_nt(da_ref[...], wa_ref[...])
        dx, dg = _rms_bwd(h_ref[...], g_ref[...], dn)
        dh_ref[...] = do_ref[...] + dx
        dg_ref[...] += dg
        dgq_ref[...] += dgq
        dgkv_ref[...] += dgkv

    def const(n):
        return pl.BlockSpec((1, n), lambda i: (0, 0))

    return pl.pallas_call(
        body,
        name="mla_bwd_pre",
        grid=(t // tm,),
        in_specs=[
            _rows(tm, qw), _rows(tm, qw), _rows(tm, vw), _rows(tm, A_PAD), _rows(tm, D_MODEL), _rows(tm, D_MODEL),
            _resident((1, D_MODEL)), _resident((1, Q_LORA)), _resident((1, KV_LORA)),
            _resident((D_MODEL, A_PAD)), _resident((Q_LORA, qw)), _resident((KV_LORA, qw)),
            _rows(tm, 128), _rows(tm, 128), _rows(tm, 128),
        ],
        out_specs=[_rows(tm, D_MODEL), _rows(tm, qw), _rows(tm, qw), _rows(tm, A_PAD),
                   const(D_MODEL), const(Q_LORA), const(KV_LORA)],
        out_shape=[
            jax.ShapeDtypeStruct((t, D_MODEL), F32),
            jax.ShapeDtypeStruct((t, qw), MXU_DTYPE),
            jax.ShapeDtypeStruct((t, qw), MXU_DTYPE),
            jax.ShapeDtypeStruct((t, A_PAD), MXU_DTYPE),
            jax.ShapeDtypeStruct((1, D_MODEL), F32),
            jax.ShapeDtypeStruct((1, Q_LORA), F32),
            jax.ShapeDtypeStruct((1, KV_LORA), F32),
        ],
        compiler_params=_cparams("arbitrary"),
    )(dq, dk, dv, a, h, dout, g, g_q, g_kv, w_a, w_uq, w_ukv, cs, sa, sb)


def _loss_head(h, g, target):
    t = h.shape[0]
    tm = min(1024, t)

    def body(h_ref, g_ref, t_ref, sq_ref, dh_ref, dg_ref):
        @pl.when(pl.program_id(0) == 0)
        def _():
            sq_ref[...] = jnp.zeros_like(sq_ref)
            dg_ref[...] = jnp.zeros_like(dg_ref)

        x = h_ref[...]
        err = _rms(x, g_ref[...]) - t_ref[...]
        sq_ref[...] += jnp.sum(err * err, axis=0, keepdims=True)
        dx, dg = _rms_bwd(x, g_ref[...], err * (1.0 / D_MODEL))
        dh_ref[...] = dx
        dg_ref[...] += dg

    return pl.pallas_call(
        body,
        name="loss_head",
        grid=(t // tm,),
        in_specs=[_rows(tm, D_MODEL), _resident((1, D_MODEL)), _rows(tm, D_MODEL)],
        out_specs=[pl.BlockSpec((1, D_MODEL), lambda i: (0, 0)), _rows(tm, D_MODEL), pl.BlockSpec((1, D_MODEL), lambda i: (0, 0))],
        out_shape=[jax.ShapeDtypeStruct((1, D_MODEL), F32), jax.ShapeDtypeStruct((t, D_MODEL), F32), jax.ShapeDtypeStruct((1, D_MODEL), F32)],
        compiler_params=_cparams("arbitrary"),
    )(h, g, target)


def _adamw(parts, w, m, v, name, tr, layer=0, row_offset=0, into=None):
    layers, rows, width = w.shape
    pwidth = parts.shape[-1]
    off = row_offset // tr

    def body(p_ref, w_ref, m_ref, v_ref, *rest):
        g_ref, d_ref, mo_ref, vo_ref = rest[-4:]

        def part(q):
            return p_ref[q, :, pl.ds(0, width)].astype(F32)

        g = part(0)
        for q in range(1, N_DEV):
            g = g + part(q)
        g_ref[...] = g
        m_new = ADAM_B1 * m_ref[...] + (1.0 - ADAM_B1) * g
        v_new = ADAM_B2 * v_ref[...] + (1.0 - ADAM_B2) * (g * g)
        m_hat = m_new / (1.0 - ADAM_B1 ** ADAM_STEP)
        v_hat = v_new / (1.0 - ADAM_B2 ** ADAM_STEP)
        d_ref[...] = -ADAM_LR * (m_hat / (jnp.sqrt(v_hat) + ADAM_EPS) + ADAM_WD * w_ref[...])
        mo_ref[...] = m_new
        vo_ref[...] = v_new

    blk = pl.BlockSpec((None, tr, width), lambda i: (layer, i, 0))
    earlier = () if into is None else tuple(into)
    return pl.pallas_call(
        body,
        name=name,
        grid=(rows // tr,),
        in_specs=[pl.BlockSpec((N_DEV, tr, pwidth), lambda i: (0, off + i, 0)), blk, blk, blk] + [pl.BlockSpec(memory_space=pl.ANY)] * len(earlier),
        out_specs=[blk, blk, blk, blk],
        out_shape=[jax.ShapeDtypeStruct((layers, rows, width), F32)] * 4,
        input_output_aliases={4 + j: j for j in range(len(earlier))},
        compiler_params=_cparams("parallel"),
    )(parts, w, m, v, *earlier)


_GAINS = ("ffn_norm1", "mix_norm", "ffn_norm2", "conv_norm", "final_norm")
_TINY = ("conv_w_dw", "mla_q_norm", "mla_kv_norm")


def _small_pack(parts):
    flat = jnp.concatenate([p.reshape(-1).astype(F32) for p in parts])
    return jnp.pad(flat, (0, SMALL_ROWS * D_MODEL - flat.shape[0])).reshape(SMALL_ROWS, D_MODEL)


def _small_unpack(packed, shapes):
    flat, out, off = packed.reshape(-1), [], 0
    for s in shapes:
        size = 1
        for d in s:
            size *= d
        out.append(flat[off:off + size].reshape(s))
        off += size
    return out


def _pad_to(a, axis, size):
    pad = [(0, 0)] * a.ndim
    pad[axis] = (0, size - a.shape[axis])
    return jnp.pad(a, pad)


def kernel(x, positions, ffn_norm1, ffn1_w1, ffn1_w3, ffn1_w2, mix_norm, ffn_norm2, ffn2_w1, ffn2_w3, ffn2_w2, conv_w_pw1, conv_w_dw, conv_norm, conv_w_pw2, mla_w_a, mla_q_norm, mla_kv_norm, mla_w_uq, mla_w_ukv, mla_w_o, final_norm, loss_target, m_ffn_norm1, m_ffn1_w1, m_ffn1_w3, m_ffn1_w2, m_mix_norm, m_ffn_norm2, m_ffn2_w1, m_ffn2_w3, m_ffn2_w2, m_conv_w_pw1, m_conv_w_dw, m_conv_norm, m_conv_w_pw2, m_mla_w_a, m_mla_q_norm, m_mla_kv_norm, m_mla_w_uq, m_mla_w_ukv, m_mla_w_o, m_final_norm, v_ffn_norm1, v_ffn1_w1, v_ffn1_w3, v_ffn1_w2, v_mix_norm, v_ffn_norm2, v_ffn2_w1, v_ffn2_w3, v_ffn2_w2, v_conv_w_pw1, v_conv_w_dw, v_conv_norm, v_conv_w_pw2, v_mla_w_a, v_mla_q_norm, v_mla_kv_norm, v_mla_w_uq, v_mla_w_ukv, v_mla_w_o, v_final_norm):
    args = dict(locals())
    wire = lambda a: a.astype(WIRE_DTYPE)

    def ffn_shards(prefix, layer):
        w13_shard = _pad_to(jnp.stack([args[prefix + "_w1"][layer], args[prefix + "_w3"][layer]]), 2, FF_SHARD_PAD)
        return [(wire(w13_shard), 1), (wire(_pad_to(args[prefix + "_w2"][layer], 0, FF_SHARD_PAD)), 2)]

    def uq_pad(a):
        return _pad_to(a, 3, HEAD_PAD).reshape(1, a.shape[1], N_HEADS * HEAD_PAD)

    ukv_rows = lambda a: a.reshape(1, a.shape[1], N_HEADS * HEAD_PAD)
    tiny_shapes = [conv_w_dw.shape, mla_q_norm.shape, mla_kv_norm.shape]

    first = _gather_two_level(*_gather_streams(ffn_shards("ffn1", 0)), "weight_gather_first")
    later_groups = [
        [(wire(conv_w_pw1[0]), 1), (wire(conv_w_pw2[0]), 2), (_small_pack([conv_w_dw, mla_q_norm, mla_kv_norm]), None)],
        ffn_shards("ffn2", 0),
        ffn_shards("ffn1", 1),
        [(wire(_pad_to(mla_w_a[0], 1, A_PAD)), 2), (wire(uq_pad(mla_w_uq)[0]), 2), (wire(ukv_rows(mla_w_ukv)[0]), 2), (wire(mla_w_o[0]), 2)],
        ffn_shards("ffn2", 1),
    ]
    later_streams, later_outs = _gather_streams([sh for grp in later_groups for sh in grp])
    group_ids, at = [], 0
    for grp in later_groups:
        group_ids.append(list(range(at, at + len(grp))))
        at += len(grp)
    gather, gather_token = _exchange_start(
        later_streams, _landing(later_streams, later_outs), group_ids, (first[0],), "weight_gather_start")

    inv_freq = ROPE_THETA ** (-2.0 * jnp.arange(QK_ROPE // 2, dtype=F32) / QK_ROPE)
    ang = positions[0].astype(F32)[:, None] * inv_freq
    cos, sin, zero = jnp.cos(ang), jnp.sin(ang), jnp.zeros_like(ang)
    cs = jnp.concatenate([cos, cos, zero, zero], axis=1)
    sa = jnp.concatenate([sin, zero, zero, zero], axis=1)
    sb = jnp.concatenate([zero, sin, zero, zero], axis=1)

    ffn_w = {("ffn1", 0): first}
    h0 = x[0]
    h1, n1, a1, b1 = _ffn_fwd(h0, ffn_norm1[0:1], *ffn_w[("ffn1", 0)], after=(gather_token,))
    w_pw1, w_pw2, tiny = _exchange_wait(gather, 0, (h1,), "weight_gather_wait_conv")
    tiny = [_small_unpack(tiny[q], tiny_shapes) for q in range(N_DEV)]
    w_dw = jnp.concatenate([t_[0][0] for t_ in tiny], axis=1)
    g_q = jnp.concatenate([t_[1] for t_ in tiny], axis=1)
    g_kv = jnp.concatenate([t_[2] for t_ in tiny], axis=1)
    n_c, u, glu = _conv_pre(h1, mix_norm[0:1], w_pw1)
    h2, c, sw = _conv_main(glu, w_dw, conv_norm, w_pw2, h1)
    ffn_w[("ffn2", 0)] = _exchange_wait(gather, 1, (h2,), "weight_gather_wait_ffn2_0")
    h3, n2, a2, b2 = _ffn_fwd(h2, ffn_norm2[0:1], *ffn_w[("ffn2", 0)])
    ffn_w[("ffn1", 1)] = _exchange_wait(gather, 2, (h3,), "weight_gather_wait_ffn1_1")
    h4, n3, a3, b3 = _ffn_fwd(h3, ffn_norm1[1:2], *ffn_w[("ffn1", 1)])
    w_a, w_uq, w_ukv, w_o = _exchange_wait(gather, 3, (h4,), "weight_gather_wait_mla")
    n_a, a_lat, cq, ckv, q, k, v = _mla_pre(h4, mix_norm[1:2], w_a, g_q, g_kv, w_uq, w_ukv, cs, sa, sb)
    o, lse = _flash_fwd(q, k, v)
    h5 = _attn_out(o, w_o, h4)
    ffn_w[("ffn2", 1)] = _exchange_wait(gather, 4, (h5,), "weight_gather_wait_ffn2_1")
    h6, n4, a4, b4 = _ffn_fwd(h5, ffn_norm2[1:2], *ffn_w[("ffn2", 1)])

    sq, dh, dg_final = _loss_head(h6, final_norm[None, :], loss_target[0])
    loss = lax.psum(0.5 / D_MODEL * jnp.sum(sq), ("x", "y", "c"))

    gain = {}

    def ffn_backward(prefix, norm, layer, dh, h_in, n, a, b, after=()):
        dh, dob, s, dab, gain[(norm, layer)] = _ffn_bwd(dh, h_in, args[norm][layer:layer + 1], a, b, *ffn_w[(prefix, layer)], after=after)
        return dh, [(_wgrad(n, dab, "wgrad_ffn_in"), 1), (_wgrad(s, dob, "wgrad_ffn_out"), 2)]

    dh, g_ffn2_1 = ffn_backward("ffn2", "ffn_norm2", 1, dh, h5, n4, a4, b4)
    do, delta = _attn_out_bwd(dh, o, w_o)
    d_o = _wgrad(o, dh, "wgrad_attn_out")
    dq, dk, dv = _flash_bwd(q, k, v, do, lse, delta)
    dh, dqp, dkv, da_lat, gain[("mix_norm", 1)], d_gq, d_gkv = _mla_bwd_pre(
        dq, dk, dv, a_lat, h4, dh, mix_norm[1:2], g_q, g_kv, w_a, w_uq, w_ukv, cs, sa, sb)
    g_mla = [(_wgrad(n_a, da_lat, "wgrad_mla_a"), 2), (_wgrad(cq, dqp, "wgrad_mla_uq"), 2), (_wgrad(ckv, dkv, "wgrad_mla_ukv"), 2), (d_o, 2)]
    dh, g_ffn1_1 = ffn_backward("ffn1", "ffn_norm1", 1, dh, h3, n3, a3, b3)
    streams_a, outs_a = _scatter_streams(g_ffn2_1 + g_mla + g_ffn1_1)
    scatter_a, token_a = _exchange_start(
        streams_a, _landing(streams_a, outs_a), [list(range(len(streams_a)))], (), "grad_scatter_start_layer1")

    dh, g_ffn2_0 = ffn_backward("ffn2", "ffn_norm2", 0, dh, h2, n2, a2, b2, after=(token_a,))
    dc, gain[("conv_norm", 0)] = _conv_bwd_post(dh, c, conv_norm, w_pw2)
    d_pw2 = _wgrad(sw, dh, "wgrad_conv_pw2")
    dh, du, d_dw, gain[("mix_norm", 0)] = _conv_bwd_pre(dc, glu, u, w_dw, w_pw1, h1, mix_norm[0:1], dh)
    d_pw1 = _wgrad(n_c, du, "wgrad_conv_pw1")
    streams_b, outs_b = _scatter_streams(g_ffn2_0 + [(d_pw1, 1), (d_pw2, 2)])
    scatter_b, token_b = _exchange_start(
        streams_b, _landing(streams_b, outs_b), [list(range(len(streams_b)))], (), "grad_scatter_start_conv")

    dh, dob, s_act, dab, gain[("ffn_norm1", 0)] = _ffn_bwd(dh, h0, ffn_norm1[0:1], a1, b1, *ffn_w[("ffn1", 0)], after=(token_b,))
    grad_x = dh[None]
    streams_d, outs_d = _scatter_streams([(_wgrad(s_act, dob, "wgrad_ffn_out"), 2)])
    scatter_d, token_d = _exchange_start(streams_d, _landing(streams_d, outs_d), [[0]], (), "grad_scatter_start_last_out")
    streams_e, outs_e = _scatter_streams([(_wgrad(n1, dab, "wgrad_ffn_in_half", after=(token_d,), only=0), 1)])
    scatter_e, token_e = _exchange_start(streams_e, _landing(streams_e, outs_e), [[0]], (), "grad_scatter_start_last_w1")
    g_ffn1_0 = [(_wgrad(n1, dab, "wgrad_ffn_in_half", after=(token_e,), only=1), 1)]

    gain_rows = jnp.concatenate([
        gain[("ffn_norm1", 0)], gain[("ffn_norm1", 1)], gain[("mix_norm", 0)], gain[("mix_norm", 1)],
        gain[("ffn_norm2", 0)], gain[("ffn_norm2", 1)], gain[("conv_norm", 0)], dg_final])
    dw_by_dest = jnp.moveaxis(d_dw[:CONV_WIDTH].reshape(CONV_WIDTH, N_DEV, -1), 1, 0)
    small_by_dest = jnp.stack([
        _small_pack([gain_rows, dw_by_dest[p], d_gq.reshape(N_DEV, -1)[p], d_gkv.reshape(N_DEV, -1)[p]]) for p in range(N_DEV)])
    streams_c, outs_c = _scatter_streams(g_ffn1_0)
    streams_c.append((small_by_dest, len(outs_c), _slot, _slot, lambda dst, src, me: _place_slot(dst, lax.dynamic_index_in_dim(src, me, 0, False), me)))
    outs_c.append(jax.ShapeDtypeStruct((N_DEV, SMALL_ROWS, D_MODEL), F32))
    scatter_c, token_c = _exchange_start(streams_c, _landing(streams_c, outs_c), [list(range(len(streams_c)))], (), "grad_scatter_start_last")

    (p13_ffn2_1, p2_ffn2_1, p_a, p_uq, p_ukv, p_o, p13_ffn1_1, p2_ffn1_1) = _exchange_wait(scatter_a, 0, (token_c,), "grad_scatter_wait_layer1")
    p13_ffn2_0, p2_ffn2_0, p_pw1, p_pw2 = _exchange_wait(scatter_b, 0, (token_c,), "grad_scatter_wait_conv")
    results = {}

    def adam(name, parts, tr, view=lambda a: a, **kw):
        results[name] = _adamw(parts, view(args[name]), view(args["m_" + name]), view(args["v_" + name]), "adamw_" + name, tr,
                               into=results.get(name), **kw)

    def adam_ffn(prefix, layer, p13, p2):
        p13 = p13.reshape(N_DEV, 2 * D_MODEL, FF_SHARD_PAD)
        adam(prefix + "_w1", p13, 256, layer=layer)
        adam(prefix + "_w3", p13, 256, layer=layer, row_offset=D_MODEL)
        adam(prefix + "_w2", p2, FF_SHARD, layer=layer)

    adam_ffn("ffn2", 0, p13_ffn2_0, p2_ffn2_0)
    adam_ffn("ffn2", 1, p13_ffn2_1, p2_ffn2_1)
    adam_ffn("ffn1", 1, p13_ffn1_1, p2_ffn1_1)
    adam("conv_w_pw1", p_pw1, 256)
    adam("conv_w_pw2", p_pw2, 128)
    adam("mla_w_a", p_a, 128)
    adam("mla_w_uq", p_uq, 64, view=uq_pad)
    adam("mla_w_ukv", p_ukv, 32, view=ukv_rows)
    adam("mla_w_o", p_o, 128)
    results["mla_w_uq"] = [r.reshape(1, -1, N_HEADS, HEAD_PAD)[..., :QK_NOPE + QK_ROPE] for r in results["mla_w_uq"]]
    results["mla_w_ukv"] = [r.reshape(mla_w_ukv.shape) for r in results["mla_w_ukv"]]

    (p2_ffn1_0,) = _exchange_wait(scatter_d, 0, (results["mla_w_o"][0],), "grad_scatter_wait_last_out")
    adam("ffn1_w2", p2_ffn1_0, FF_SHARD, layer=0)
    (p1_ffn1_0,) = _exchange_wait(scatter_e, 0, (results["ffn1_w2"][0],), "grad_scatter_wait_last_w1")
    adam("ffn1_w1", p1_ffn1_0, 256, layer=0)
    p3_ffn1_0, p_small = _exchange_wait(scatter_c, 0, (results["ffn1_w1"][0],), "grad_scatter_wait_last")
    adam("ffn1_w3", p3_ffn1_0, 256, layer=0)
    small_names = _GAINS + _TINY
    small_shapes = [args[n].shape for n in small_names]
    pack_small = lambda prefix: _small_pack([args[prefix + n] for n in small_names])[None]
    small = _adamw(p_small, pack_small(""), pack_small("m_"), pack_small("v_"), "adamw_small", SMALL_ROWS)
    for kind in range(4):
        for n, leaf in zip(small_names, _small_unpack(small[kind][0], small_shapes)):
            results.setdefault(n, [None] * 4)[kind] = leaf

    order = ("ffn_norm1", "ffn1_w1", "ffn1_w3", "ffn1_w2", "mix_norm", "ffn_norm2", "ffn2_w1", "ffn2_w3", "ffn2_w2",
             "conv_w_pw1", "conv_w_dw", "conv_norm", "conv_w_pw2", "mla_w_a", "mla_q_norm", "mla_kv_norm",
             "mla_w_uq", "mla_w_ukv", "mla_w_o", "final_norm")
    outputs = [loss, grad_x]
    for kind in range(4):
        outputs.extend(results[n][kind] for n in order)
    return tuple(outputs)
```
